```python
import jax
import jax.numpy as jnp
from jax import lax
import numpy as np

D_MODEL = 1024
BATCH = 16
SEQ = 2048
DEPTH = 2

D_FF = 2816
NORM_EPS = 1e-6
N_BRANCH = 3

GLA_HEADS = 4
GLA_DK = 64
GLA_DV = 128
GLA_QK = GLA_HEADS * GLA_DK
GLA_V = GLA_HEADS * GLA_DV
GLA_RANK = 16
GLA_TAU = 16.0
GLA_CHUNK = 64

FNET_GROUPS = 4
FNET_GC = 128
FNET_W = FNET_GROUPS * FNET_GC

RWKV_HEADS = 8
RWKV_N = 64
RWKV_W = RWKV_HEADS * RWKV_N
RWKV_DECAY_RANK = 32
RWKV_A_RANK = 32
RWKV_GATE_RANK = 96
RWKV_LN_EPS = 64e-5

RWKV_SPLITS = (RWKV_W, RWKV_W, RWKV_W, RWKV_DECAY_RANK, RWKV_DECAY_RANK, RWKV_A_RANK, RWKV_A_RANK, RWKV_GATE_RANK)
RWKV_COLS = 3 * RWKV_W + 2 * RWKV_DECAY_RANK + 2 * RWKV_A_RANK + RWKV_GATE_RANK
IN_SPLITS = (GLA_QK, GLA_QK, GLA_V, GLA_V, GLA_RANK, GLA_RANK, FNET_W, RWKV_COLS, N_BRANCH * D_MODEL)
IN_COLS = 2 * GLA_QK + 2 * GLA_V + 2 * GLA_RANK + FNET_W + RWKV_COLS + N_BRANCH * D_MODEL

kernel_name = 'hybrid_gla_fnet_rwkv7_macaron_encoder'


def split_cols(t, sizes):
    return jnp.split(t, [int(i) for i in np.cumsum(sizes)[:-1]], axis=-1)


def flip_seq(t):
    return jnp.flip(t, axis=1)


def rms_norm(x, g):
    xf = x.astype(jnp.float32)
    y = xf * lax.rsqrt(jnp.mean(xf * xf, axis=-1, keepdims=True) + NORM_EPS)
    return (y * g.astype(jnp.float32)).astype(x.dtype)


def swiglu(h, w_gate, w_up, w_down):
    return (jax.nn.silu(h @ w_gate) * (h @ w_up)) @ w_down


def gla_chunked(q, k, v, log_a, strict):
    B, S, H, dk = q.shape
    dv = v.shape[-1]
    L = GLA_CHUNK
    nc = S // L

    def chunk(t):
        return t.reshape(B, nc, L, H, t.shape[-1])

    q, k, v, log_a = chunk(q), chunk(k), chunk(v), chunk(log_a)
    b = jnp.cumsum(log_a, axis=2)
    b_last = b[:, :, -1:]
    q_dec = q * jnp.exp(b)
    k_dec = k * jnp.exp(-b)
    k_tail = k * jnp.exp(b_last - b)
    scores = jnp.einsum('bclhk,bcmhk->bchlm', q_dec, k_dec)
    idx = jnp.arange(L)
    mask = (idx[:, None] > idx[None, :]) if strict else (idx[:, None] >= idx[None, :])
    scores = jnp.where(mask, scores, 0.0)
    o_intra = jnp.einsum('bchlm,bcmhv->bclhv', scores, v)
    u = jnp.einsum('bclhk,bclhv->bchkv', k_tail, v)
    chunk_decay = jnp.exp(b_last[:, :, 0])

    def step(state, inp):
        u_c, d_c = inp
        return state * d_c[..., None] + u_c, state

    init = jnp.zeros((B, H, dk, dv), jnp.float32)
    _, prev = lax.scan(step, init, (jnp.moveaxis(u, 1, 0), jnp.moveaxis(chunk_decay, 1, 0)))
    prev = jnp.moveaxis(prev, 0, 1)
    o_inter = jnp.einsum('bclhk,bchkv->bclhv', q_dec, prev)
    return (o_intra + o_inter).reshape(B, S, H, dv)


def gla_branch(q, k, v, r, dn_f, dn_b, up_f, bias_f, up_b, bias_b, norm_g):
    B, S, _ = q.shape
    f32 = jnp.float32

    def heads(t, d):
        return t.astype(f32).reshape(B, S, GLA_HEADS, d)

    q = heads(q, GLA_DK) * (GLA_DK ** -0.5)
    k = heads(k, GLA_DK)
    v = heads(v, GLA_DV)

    def log_decay(dn, up, bias):
        z = dn.astype(f32) @ up.astype(f32) + bias.astype(f32)
        return heads(jax.nn.log_sigmoid(z) / GLA_TAU, GLA_DK)

    la_f = log_decay(dn_f, up_f, bias_f)
    la_b = log_decay(dn_b, up_b, bias_b)
    o_f = gla_chunked(q, k, v, la_f, strict=False)
    o_b = flip_seq(gla_chunked(flip_seq(q), flip_seq(k), flip_seq(v), flip_seq(la_b), strict=True))
    o = o_f + o_b
    o = o * lax.rsqrt(jnp.mean(o * o, axis=-1, keepdims=True) + NORM_EPS)
    o = o.reshape(B, S, GLA_V) * norm_g.astype(f32)
    return o * jax.nn.silu(r.astype(f32))


def fnet_branch(u):
    B, S, _ = u.shape
    z = u.astype(jnp.float32).reshape(B, S, FNET_GROUPS, FNET_GC)
    y = jnp.fft.fft2(z, axes=(1, 3), norm='ortho').real
    return y.astype(jnp.float32).reshape(B, S, FNET_W)


def centred_shift(u, mu):
    prev = jnp.pad(u[:, :-1], ((0, 0), (1, 0), (0, 0)))
    nxt = jnp.pad(u[:, 1:], ((0, 0), (0, 1), (0, 0)))
    return u + mu * (0.5 * (prev + nxt) - u)


def rwkv_scan(r, w, k, v, kk, a, strict):
    B, S, H, N = r.shape

    def step(state, inp):
        r_t, w_t, k_t, v_t, kk_t, a_t = inp
        sa = jnp.einsum('bhvk,bhk->bhv', state, -kk_t)
        new = (state * w_t[:, :, None, :]
               + sa[..., None] * (kk_t * a_t)[:, :, None, :]
               + v_t[..., None] * k_t[:, :, None, :])
        read = state if strict else new
        return new, jnp.einsum('bhvk,bhk->bhv', read, r_t)

    xs = tuple(jnp.moveaxis(t, 1, 0) for t in (r, w, k, v, kk, a))
    _, y = lax.scan(step, jnp.zeros((B, H, N, N), jnp.float32), xs)
    return jnp.moveaxis(y, 0, 1)


def rwkv7_branch(u, mu, w0_f, w2_f, w0_b, w2_b, a0_f, a2_f, a0_b, a2_b, g2, k_k, k_a, r_k, ln_g, ln_b):
    B, S, _ = u.shape
    f32 = jnp.float32
    u = centred_shift(u.astype(f32), mu.astype(f32))
    r, k, v, wd_f, wd_b, ad_f, ad_b, gd = split_cols(u, RWKV_SPLITS)

    def heads(t):
        return t.reshape(B, S, RWKV_HEADS, RWKV_N)

    def decay(wd, w0, w2):
        w = -jax.nn.softplus(-(w0.astype(f32) + jnp.tanh(wd) @ w2.astype(f32))) - 0.5
        return heads(jnp.exp(-jnp.exp(w)))

    def icl_rate(ad, a0, a2):
        return heads(jax.nn.sigmoid(a0.astype(f32) + ad @ a2.astype(f32)))

    w_f, w_b = decay(wd_f, w0_f, w2_f), decay(wd_b, w0_b, w2_b)
    a_f, a_b = icl_rate(ad_f, a0_f, a2_f), icl_rate(ad_b, a0_b, a2_b)
    g = jax.nn.sigmoid(gd) @ g2.astype(f32)
    r, k, v = heads(r), heads(k), heads(v)
    k_k = k_k.astype(f32).reshape(RWKV_HEADS, RWKV_N)
    k_a = k_a.astype(f32).reshape(RWKV_HEADS, RWKV_N)
    r_k = r_k.astype(f32).reshape(RWKV_HEADS, RWKV_N)
    kk = k * k_k
    kk = kk * lax.rsqrt(jnp.sum(kk * kk, axis=-1, keepdims=True) + 1e-12)
    k_f = k * (1.0 + (a_f - 1.0) * k_a)
    k_b = k * (1.0 + (a_b - 1.0) * k_a)
    y_f = rwkv_scan(r, w_f, k_f, v, kk, a_f, strict=False)
    y_b = flip_seq(rwkv_scan(flip_seq(r), flip_seq(w_b), flip_seq(k_b), flip_seq(v),
                             flip_seq(kk), flip_seq(a_b), strict=True))
    y = y_f + y_b
    mean = jnp.mean(y, axis=-1, keepdims=True)
    var = jnp.mean((y - mean) ** 2, axis=-1, keepdims=True)
    y = ((y - mean) * lax.rsqrt(var + RWKV_LN_EPS)).reshape(B, S, RWKV_W)
    y = y * ln_g.astype(f32) + ln_b.astype(f32)
    bonus = (jnp.sum(r * k_f * r_k, axis=-1, keepdims=True) * v).reshape(B, S, RWKV_W)
    return (y + bonus) * g


def hybrid_layer(x, ffn1_norm, ffn1_gate, ffn1_up, ffn1_down, mix_norm, w_in,
                 gla_up_f, gla_bias_f, gla_up_b, gla_bias_b, gla_norm,
                 rwkv_mu, rwkv_w0_f, rwkv_w2_f, rwkv_w0_b, rwkv_w2_b,
                 rwkv_a0_f, rwkv_a2_f, rwkv_a0_b, rwkv_a2_b, rwkv_g2,
                 rwkv_k_k, rwkv_k_a, rwkv_r_k, rwkv_ln_g, rwkv_ln_b,
                 proj_gla, proj_fnet, proj_rwkv, w_out,
                 ffn2_norm, ffn2_gate, ffn2_up, ffn2_down):
    B, S, D = x.shape
    x = x + 0.5 * swiglu(rms_norm(x, ffn1_norm), ffn1_gate, ffn1_up, ffn1_down)
    h = rms_norm(x, mix_norm)
    z = h @ w_in
    q, k, v, r, dn_f, dn_b, u_fnet, u_rwkv, gate = split_cols(z, IN_SPLITS)
    y_a = gla_branch(q, k, v, r, dn_f, dn_b, gla_up_f, gla_bias_f, gla_up_b, gla_bias_b, gla_norm).astype(x.dtype)
    y_b = fnet_branch(u_fnet).astype(x.dtype)
    y_c = rwkv7_branch(u_rwkv, rwkv_mu, rwkv_w0_f, rwkv_w2_f, rwkv_w0_b, rwkv_w2_b,
                       rwkv_a0_f, rwkv_a2_f, rwkv_a0_b, rwkv_a2_b, rwkv_g2,
                       rwkv_k_k, rwkv_k_a, rwkv_r_k, rwkv_ln_g, rwkv_ln_b).astype(x.dtype)
    gates = jax.nn.sigmoid(gate).reshape(B, S, N_BRANCH, D)
    merged = (gates[:, :, 0] * (y_a @ proj_gla)
              + gates[:, :, 1] * (y_b @ proj_fnet)
              + gates[:, :, 2] * (y_c @ proj_rwkv))
    x = x + merged @ w_out
    x = x + 0.5 * swiglu(rms_norm(x, ffn2_norm), ffn2_gate, ffn2_up, ffn2_down)
    return x


def setup_inputs(seed: int = 0) -> dict:
    key = jax.random.key(seed)
    ks = iter(jax.random.split(key, 64))

    def nrm(shape, scale=1.0):
        return scale * jax.random.normal(next(ks), shape, jnp.float32)

    def uni(shape, lo, hi):
        return jax.random.uniform(next(ks), shape, jnp.float32, lo, hi)

    L, D, F = DEPTH, D_MODEL, D_FF
    return {
        'x': nrm((BATCH, SEQ, D)),
        'ffn1_norm': 1.0 + nrm((L, D), 0.02),
        'ffn1_gate': nrm((L, D, F), D ** -0.5),
        'ffn1_up': nrm((L, D, F), D ** -0.5),
        'ffn1_down': nrm((L, F, D), F ** -0.5),
        'mix_norm': 1.0 + nrm((L, D), 0.02),
        'w_in': nrm((L, D, IN_COLS), D ** -0.5),
        'gla_up_f': nrm((L, GLA_RANK, GLA_QK), GLA_RANK ** -0.5),
        'gla_bias_f': nrm((L, GLA_QK), 0.1),
        'gla_up_b': nrm((L, GLA_RANK, GLA_QK), GLA_RANK ** -0.5),
        'gla_bias_b': nrm((L, GLA_QK), 0.1),
        'gla_norm': 1.0 + nrm((L, GLA_V), 0.02),
        'rwkv_mu': uni((L, RWKV_COLS), 0.0, 1.0),
        'rwkv_w0_f': -1.0 + nrm((L, RWKV_W), 0.5),
        'rwkv_w2_f': nrm((L, RWKV_DECAY_RANK, RWKV_W), RWKV_DECAY_RANK ** -0.5),
        'rwkv_w0_b': -1.0 + nrm((L, RWKV_W), 0.5),
        'rwkv_w2_b': nrm((L, RWKV_DECAY_RANK, RWKV_W), RWKV_DECAY_RANK ** -0.5),
        'rwkv_a0_f': nrm((L, RWKV_W), 0.1),
        'rwkv_a2_f': nrm((L, RWKV_A_RANK, RWKV_W), RWKV_A_RANK ** -0.5),
        'rwkv_a0_b': nrm((L, RWKV_W), 0.1),
        'rwkv_a2_b': nrm((L, RWKV_A_RANK, RWKV_W), RWKV_A_RANK ** -0.5),
        'rwkv_g2': nrm((L, RWKV_GATE_RANK, RWKV_W), RWKV_GATE_RANK ** -0.5),
        'rwkv_k_k': 0.85 + nrm((L, RWKV_W), 0.02),
        'rwkv_k_a': 1.0 + nrm((L, RWKV_W), 0.02),
        'rwkv_r_k': nrm((L, RWKV_W), 0.1),
        'rwkv_ln_g': 1.0 + nrm((L, RWKV_W), 0.02),
        'rwkv_ln_b': nrm((L, RWKV_W), 0.02),
        'proj_gla': nrm((L, GLA_V, D), GLA_V ** -0.5),
        'proj_fnet': nrm((L, FNET_W, D), FNET_W ** -0.5),
        'proj_rwkv': nrm((L, RWKV_W, D), RWKV_W ** -0.5),
        'w_out': nrm((L, D, D), D ** -0.5),
        'ffn2_norm': 1.0 + nrm((L, D), 0.02),
        'ffn2_gate': nrm((L, D, F), D ** -0.5),
        'ffn2_up': nrm((L, D, F), D ** -0.5),
        'ffn2_down': nrm((L, F, D), F ** -0.5),
        'final_norm': 1.0 + nrm((D,), 0.02),
    }


def reference(x, ffn1_norm, ffn1_gate, ffn1_up, ffn1_down, mix_norm, w_in,
              gla_up_f, gla_bias_f, gla_up_b, gla_bias_b, gla_norm,
              rwkv_mu, rwkv_w0_f, rwkv_w2_f, rwkv_w0_b, rwkv_w2_b,
              rwkv_a0_f, rwkv_a2_f, rwkv_a0_b, rwkv_a2_b, rwkv_g2,
              rwkv_k_k, rwkv_k_a, rwkv_r_k, rwkv_ln_g, rwkv_ln_b,
              proj_gla, proj_fnet, proj_rwkv, w_out,
              ffn2_norm, ffn2_gate, ffn2_up, ffn2_down, final_norm):
    for l in range(DEPTH):
        x = hybrid_layer(x, ffn1_norm[l], ffn1_gate[l], ffn1_up[l], ffn1_down[l], mix_norm[l], w_in[l],
                         gla_up_f[l], gla_bias_f[l], gla_up_b[l], gla_bias_b[l], gla_norm[l],
                         rwkv_mu[l], rwkv_w0_f[l], rwkv_w2_f[l], rwkv_w0_b[l], rwkv_w2_b[l],
                         rwkv_a0_f[l], rwkv_a2_f[l], rwkv_a0_b[l], rwkv_a2_b[l], rwkv_g2[l],
                         rwkv_k_k[l], rwkv_k_a[l], rwkv_r_k[l], rwkv_ln_g[l], rwkv_ln_b[l],
                         proj_gla[l], proj_fnet[l], proj_rwkv[l], w_out[l],
                         ffn2_norm[l], ffn2_gate[l], ffn2_up[l], ffn2_down[l])
    return rms_norm(x, final_norm)
```

```python
import functools

import jax
import jax.numpy as jnp
from jax import lax
from jax.experimental import pallas as pl
from jax.experimental.pallas import tpu as pltpu

F32 = jnp.float32
BF16 = jnp.bfloat16
HIGHEST = lax.Precision.HIGHEST

NORM_EPS = 1e-6
RWKV_LN_EPS = 64e-5
GLA_TAU = 16.0

GLA_HEADS, GLA_DK, GLA_DV, GLA_RANK = 4, 64, 128, 16
GLA_QK, GLA_V = GLA_HEADS * GLA_DK, GLA_HEADS * GLA_DV
FNET_GROUPS, FNET_GC = 4, 128
FNET_W = FNET_GROUPS * FNET_GC
RWKV_HEADS, RWKV_N = 8, 64
RWKV_W = RWKV_HEADS * RWKV_N
RWKV_LOWRANK = 128
RWKV_GATE_RANK = 96
RWKV_COLS = 3 * RWKV_W + RWKV_LOWRANK + RWKV_GATE_RANK

LANES = 128
CHUNK = 64

GLA_COLS_RAW = 2 * GLA_QK + 2 * GLA_V + 2 * GLA_RANK
GLA_COLS = 13 * LANES
RWKV_COLS_PAD = 14 * LANES
GATE_COLS = 3 * 1024

VMEM_LIMIT = 56 * 1024 * 1024


def _mm(a, b):
    return jnp.dot(a.astype(BF16), b.astype(BF16), preferred_element_type=F32)


def _mm_nt(a, b):
    return lax.dot_general(a.astype(BF16), b.astype(BF16), (((1,), (1,)), ((), ())),
                           preferred_element_type=F32)


def _mm_tn(a, b):
    return lax.dot_general(a.astype(BF16), b.astype(BF16), (((0,), (0,)), ((), ())),
                           preferred_element_type=F32)


def _mm_f32(a, b):
    return jnp.dot(a, b, preferred_element_type=F32, precision=HIGHEST)


def _sigmoid(x):
    return 1.0 / (1.0 + jnp.exp(-x))


def _log_sigmoid(x):
    return jnp.minimum(x, 0.0) - jnp.log(1.0 + jnp.exp(-jnp.abs(x)))


def _rms(x, g):
    return x * lax.rsqrt(jnp.mean(x * x, axis=-1, keepdims=True) + NORM_EPS) * g


def _const_spec(shape):
    nd = len(shape)
    return pl.BlockSpec(shape, lambda *_: (0,) * nd, pipeline_mode=pl.Buffered(1))


def _params(sem):
    return pltpu.CompilerParams(dimension_semantics=sem, vmem_limit_bytes=VMEM_LIMIT)


def _ffn_body(x_ref, g_ref, wg_ref, wu_ref, wd_ref, fg_ref, o_ref, *, final):
    x = x_ref[...]
    h = _rms(x, g_ref[...]).astype(BF16)
    gate = jnp.dot(h, wg_ref[...], preferred_element_type=F32)
    up = jnp.dot(h, wu_ref[...], preferred_element_type=F32)
    act = (gate * _sigmoid(gate) * up).astype(BF16)
    y = x + 0.5 * jnp.dot(act, wd_ref[...], preferred_element_type=F32)
    if final:
        y = _rms(y, fg_ref[...])
    o_ref[...] = y


def _ffn(x, g, wg, wu, wd, fg, *, final, tm):
    t, d = x.shape
    f = wg.shape[1]
    row = pl.BlockSpec((tm, d), lambda i: (i, 0))
    return pl.pallas_call(
        functools.partial(_ffn_body, final=final),
        grid=(t // tm,),
        in_specs=[row, _const_spec((1, d)), _const_spec((d, f)), _const_spec((d, f)),
                  _const_spec((f, d)), _const_spec((1, d))],
        out_specs=row,
        out_shape=jax.ShapeDtypeStruct((t, d), F32),
        compiler_params=_params(("parallel",)),
        name="ffn",
    )(x, g, wg, wu, wd, fg)


def _inproj_body(x_ref, g_ref, w_ref, gla_ref, fnet_ref, rwkv_ref, gate_ref):
    h = _rms(x_ref[...], g_ref[...]).astype(BF16)
    c0, c1, c2 = GLA_COLS, GLA_COLS + FNET_W, GLA_COLS + FNET_W + RWKV_COLS_PAD
    gla_ref[...] = jnp.dot(h, w_ref[:, 0:c0], preferred_element_type=F32)
    fnet_ref[...] = jnp.dot(h, w_ref[:, c0:c1], preferred_element_type=F32).astype(BF16)
    rwkv_ref[...] = jnp.dot(h, w_ref[:, c1:c2], preferred_element_type=F32)
    gate_ref[...] = jnp.dot(h, w_ref[:, c2:c2 + GATE_COLS], preferred_element_type=F32)


def _inproj(x, g, w, *, tm):
    t, d = x.shape
    n = w.shape[1]
    row = lambda c: pl.BlockSpec((tm, c), lambda i: (i, 0))
    return pl.pallas_call(
        _inproj_body,
        grid=(t // tm,),
        in_specs=[row(d), _const_spec((1, d)), _const_spec((d, n))],
        out_specs=[row(GLA_COLS), row(FNET_W), row(RWKV_COLS_PAD), row(GATE_COLS)],
        out_shape=[jax.ShapeDtypeStruct((t, GLA_COLS), F32),
                   jax.ShapeDtypeStruct((t, FNET_W), BF16),
                   jax.ShapeDtypeStruct((t, RWKV_COLS_PAD), F32),
                   jax.ShapeDtypeStruct((t, GATE_COLS), F32)],
        compiler_params=_params(("parallel",)),
        name="inproj",
    )(x, g, w)


def _lane_masks():
    lane = lax.broadcasted_iota(jnp.int32, (CHUNK, LANES), 1)
    return lane < (LANES // 2), lane >= (LANES // 2)


def _masked_stack(x, lo, hi):
    return jnp.concatenate([jnp.where(lo, x, 0.0), jnp.where(hi, x, 0.0)], axis=0)


def _pair_masks(fwd, inclusive):
    n = 2 * CHUNK
    i = lax.broadcasted_iota(jnp.int32, (n, n), 0)
    j = lax.broadcasted_iota(jnp.int32, (n, n), 1)
    same = (i < CHUNK) == (j < CHUNK)
    strict = (j < i) if fwd else (j > i)
    if inclusive:
        return same & (strict | (i == j))
    return same & strict


def _gla_direction(z, up, bias, tri, st_ref, o_ref, fwd):
    L = CHUNK
    q = z[:, 0:GLA_QK] * (GLA_DK ** -0.5)
    k = z[:, GLA_QK:2 * GLA_QK]
    v = z[:, 2 * GLA_QK:2 * GLA_QK + GLA_V]
    dn = z[:, GLA_COLS - LANES:GLA_COLS]
    la = _log_sigmoid(_mm_f32(dn, up) + bias) * (1.0 / GLA_TAU)
    b = _mm_f32(tri, la)
    btot = b[L - 1:L, :] if fwd else b[0:1, :]
    qd = q * jnp.exp(b)
    kd = k * jnp.exp(-b)
    kt = k * jnp.exp(btot - b)
    dec = jnp.exp(btot)
    lo, hi = _lane_masks()
    mask = _pair_masks(fwd, inclusive=fwd)
    for p in range(GLA_HEADS // 2):
        sl = slice(p * LANES, (p + 1) * LANES)
        q_ms = _masked_stack(qd[:, sl], lo, hi)
        k_st = jnp.concatenate([kd[:, sl], kd[:, sl]], axis=0)
        kt_ms = _masked_stack(kt[:, sl], lo, hi)
        v_st = jnp.concatenate([v[:, (2 * p) * GLA_DV:(2 * p + 1) * GLA_DV],
                                v[:, (2 * p + 1) * GLA_DV:(2 * p + 2) * GLA_DV]], axis=0)
        st = st_ref[p]
        sc = jnp.where(mask, _mm_nt(q_ms, k_st), 0.0)
        o = _mm(sc, v_st) + _mm_nt(q_ms, st)
        o_ref[:, (2 * p) * GLA_DV:(2 * p + 1) * GLA_DV] = o[0:L]
        o_ref[:, (2 * p + 1) * GLA_DV:(2 * p + 2) * GLA_DV] = o[L:2 * L]
        st_ref[p] = st * dec[:, sl] + _mm_tn(v_st, kt_ms)


def _gla_body(zf_ref, zb_ref, upf_ref, bf_ref, upb_ref, bb_ref, trif_ref, trib_ref,
              of_ref, ob_ref, stf_ref, stb_ref):
    @pl.when(pl.program_id(1) == 0)
    def _():
        stf_ref[...] = jnp.zeros_like(stf_ref)
        stb_ref[...] = jnp.zeros_like(stb_ref)

    _gla_direction(zf_ref[...], upf_ref[...], bf_ref[...], trif_ref[...], stf_ref, of_ref, True)
    _gla_direction(zb_ref[...], upb_ref[...], bb_ref[...], trib_ref[...], stb_ref, ob_ref, False)


def _gla(z, upf, bf, upb, bb, trif, trib):
    bsz, s, _ = z.shape
    nc = s // CHUNK
    fmap = lambda b, c: (b, c, 0)
    bmap = lambda b, c: (b, nc - 1 - c, 0)
    zspec = lambda m: pl.BlockSpec((None, CHUNK, GLA_COLS), m)
    ospec = lambda m: pl.BlockSpec((None, CHUNK, GLA_V), m)
    state = pltpu.VMEM((GLA_HEADS // 2, GLA_DV, LANES), F32)
    return pl.pallas_call(
        _gla_body,
        grid=(bsz, nc),
        in_specs=[zspec(fmap), zspec(bmap), _const_spec(upf.shape), _const_spec(bf.shape),
                  _const_spec(upb.shape), _const_spec(bb.shape),
                  _const_spec(trif.shape), _const_spec(trib.shape)],
        out_specs=[ospec(fmap), ospec(bmap)],
        out_shape=[jax.ShapeDtypeStruct((bsz, s, GLA_V), F32)] * 2,
        scratch_shapes=[state, state],
        compiler_params=_params(("parallel", "arbitrary")),
        name="gla",
    )(z, z, upf, bf, upb, bb, trif, trib)


def _fnet_body(z_ref, cc_ref, sc_ref, cs_ref, o_ref, ab_ref, *, rows):
    s = z_ref.shape[0]
    z = z_ref[...]
    ab_ref[0:s, :] = jnp.dot(z, cc_ref[...], preferred_element_type=F32).astype(BF16)
    ab_ref[s:2 * s, :] = jnp.dot(z, sc_ref[...], preferred_element_type=F32).astype(BF16)
    for i in range(s // rows):
        o_ref[i * rows:(i + 1) * rows, :] = jnp.dot(
            cs_ref[i * rows:(i + 1) * rows, :], ab_ref[...], preferred_element_type=F32)


def _fnet(z, cc, sc, cs):
    bsz, s, w = z.shape
    rows = min(s, 512)
    blk = lambda: pl.BlockSpec((None, s, w), lambda b: (b, 0, 0))
    return pl.pallas_call(
        functools.partial(_fnet_body, rows=rows),
        grid=(bsz,),
        in_specs=[blk(), _const_spec(cc.shape), _const_spec(sc.shape), _const_spec(cs.shape)],
        out_specs=blk(),
        out_shape=jax.ShapeDtypeStruct((bsz, s, w), F32),
        scratch_shapes=[pltpu.VMEM((2 * s, w), BF16)],
        compiler_params=_params(("parallel",)),
        name="fnet",
    )(z, cc, sc, cs)


def _rwkv_direction(u_ref, up_ref, un_ref, c, nc, prm, st_ref, y_ref, extra_refs, fwd):
    L = CHUNK
    (mu, w0, a0, w2, a2, g2, k_k, k_a, r_k, tri, gsum) = prm
    u = u_ref[...]
    row = lax.broadcasted_iota(jnp.int32, (L, 1), 0)
    prev_row = jnp.where(c > 0, up_ref[7:8, :], 0.0)
    next_row = jnp.where(c < nc - 1, un_ref[0:1, :], 0.0)
    prev = jnp.where(row == 0, prev_row, pltpu.roll(u, 1, 0))
    nxt = jnp.where(row == L - 1, next_row, pltpu.roll(u, L - 1, 0))
    u = u + mu * (0.5 * (prev + nxt) - u)

    W = RWKV_W
    r, k, v = u[:, 0:W], u[:, W:2 * W], u[:, 2 * W:3 * W]
    sm = u[:, 3 * W:3 * W + LANES]
    lw = -jnp.exp(_log_sigmoid(w0 + _mm_f32(jnp.tanh(sm), w2)) - 0.5)
    a = _sigmoid(a0 + _mm_f32(sm, a2))
    kk = k * k_k
    kap = kk * lax.rsqrt(_mm_f32(kk * kk, gsum) + 1e-12)
    kd = k * (1.0 + (a - 1.0) * k_a)
    beta = kap * a
    inc = _mm_f32(tri, lw)
    exc = inc - lw
    tot = inc[L - 1:L, :] if fwd else inc[0:1, :]
    e_exc = jnp.exp(exc)
    e_ninc = jnp.exp(-inc)
    e_tail = jnp.exp(tot - inc)
    dec = jnp.exp(tot)
    kap_h = kap * e_exc
    r_h = r * (jnp.exp(inc) if fwd else e_exc)
    k_t = kd * e_ninc
    b_t = beta * e_ninc
    k_tail = kd * e_tail
    b_tail = beta * e_tail

    if extra_refs is not None:
        bonus_ref, g_ref = extra_refs
        gd = u[:, 3 * W + LANES:3 * W + 2 * LANES]
        g_ref[...] = _mm_f32(_sigmoid(gd), g2)
        bonus_ref[...] = _mm_f32(r * kd * r_k, gsum) * v

    lo, hi = _lane_masks()
    m_strict = _pair_masks(fwd, inclusive=False)
    m_read = _pair_masks(fwd, inclusive=fwd)
    n2 = 2 * L
    ii = lax.broadcasted_iota(jnp.int32, (n2, n2), 0)
    jj = lax.broadcasted_iota(jnp.int32, (n2, n2), 1)
    eye = (ii == jj).astype(F32)
    for p in range(RWKV_HEADS // 2):
        sl = slice(p * LANES, (p + 1) * LANES)
        xs = jnp.concatenate([_masked_stack(kap_h[:, sl], lo, hi), _masked_stack(r_h[:, sl], lo, hi)], axis=0)
        ys = jnp.concatenate([k_t[:, sl], k_t[:, sl], b_t[:, sl], b_t[:, sl]], axis=0)
        s_all = _mm_nt(xs, ys)
        m_k = jnp.where(m_strict, s_all[0:n2, 0:n2], 0.0)
        nmat = jnp.where(m_strict, -s_all[0:n2, n2:2 * n2], 0.0)
        p_k = jnp.where(m_read, s_all[n2:2 * n2, 0:n2], 0.0)
        p_b = jnp.where(m_read, -s_all[n2:2 * n2, n2:2 * n2], 0.0)
        t_inv = eye + nmat
        pw = _mm_f32(nmat, nmat)
        for _ in range(4):
            both = _mm_f32(jnp.concatenate([t_inv, pw], axis=0), pw)
            t_inv = t_inv + both[0:n2]
            pw = both[n2:2 * n2]
        t_inv = t_inv + _mm_f32(t_inv, pw)

        st = st_ref[p]
        v_ms = _masked_stack(v[:, sl], lo, hi)
        xh = _mm_nt(xs, st)
        uu = _mm_f32(t_inv, xh[0:n2] + _mm(m_k, v_ms))
        y = xh[n2:2 * n2] + _mm(jnp.concatenate([p_k, p_b], axis=1), jnp.concatenate([v_ms, uu], axis=0))
        y_ref[:, sl] = y[0:L] + y[L:n2]
        tails = jnp.concatenate([_masked_stack(k_tail[:, sl], lo, hi), _masked_stack(b_tail[:, sl], lo, hi)], axis=0)
        st_ref[p] = st * dec[:, sl] + _mm_tn(jnp.concatenate([v_ms, -uu], axis=0), tails)


def _rwkv_body(uf_ref, ufp_ref, ufn_ref, ub_ref, ubp_ref, ubn_ref,
               mu_ref, w0f_ref, w0b_ref, a0f_ref, a0b_ref, w2f_ref, w2b_ref, a2f_ref, a2b_ref, g2_ref,
               kk_ref, ka_ref, rk_ref, trif_ref, trib_ref, gsum_ref,
               yf_ref, yb_ref, bonus_ref, g_ref, stf_ref, stb_ref):
    c = pl.program_id(1)
    nc = pl.num_programs(1)

    @pl.when(c == 0)
    def _():
        stf_ref[...] = jnp.zeros_like(stf_ref)
        stb_ref[...] = jnp.zeros_like(stb_ref)

    shared = (g2_ref[...], kk_ref[...], ka_ref[...], rk_ref[...])
    prm_f = (mu_ref[...], w0f_ref[...], a0f_ref[...], w2f_ref[...], a2f_ref[...]) + shared + (trif_ref[...], gsum_ref[...])
    prm_b = (mu_ref[...], w0b_ref[...], a0b_ref[...], w2b_ref[...], a2b_ref[...]) + shared + (trib_ref[...], gsum_ref[...])
    _rwkv_direction(uf_ref, ufp_ref, ufn_ref, c, nc, prm_f, stf_ref, yf_ref, (bonus_ref, g_ref), True)
    _rwkv_direction(ub_ref, ubp_ref, ubn_ref, nc - 1 - c, nc, prm_b, stb_ref, yb_ref, None, False)


def _rwkv(u, consts):
    bsz, s, cols = u.shape
    nc = s // CHUNK
    sub = 8
    per = CHUNK // sub
    nblk = s // sub
    fc = lambda c: c
    bc = lambda c: nc - 1 - c
    main = lambda cm: pl.BlockSpec((None, CHUNK, cols), lambda b, c: (b, cm(c), 0))
    prev = lambda cm: pl.BlockSpec((None, sub, cols), lambda b, c: (b, jnp.maximum(cm(c) * per - 1, 0), 0))
    nxt = lambda cm: pl.BlockSpec((None, sub, cols), lambda b, c: (b, jnp.minimum((cm(c) + 1) * per, nblk - 1), 0))
    ospec = lambda cm: pl.BlockSpec((None, CHUNK, RWKV_W), lambda b, c: (b, cm(c), 0))
    state = pltpu.VMEM((RWKV_HEADS // 2, LANES, LANES), F32)
    out = jax.ShapeDtypeStruct((bsz, s, RWKV_W), F32)
    return pl.pallas_call(
        _rwkv_body,
        grid=(bsz, nc),
        in_specs=[main(fc), prev(fc), nxt(fc), main(bc), prev(bc), nxt(bc)] + [_const_spec(a.shape) for a in consts],
        out_specs=[ospec(fc), ospec(bc), ospec(fc), ospec(fc)],
        out_shape=[out] * 4,
        scratch_shapes=[state, state],
        compiler_params=_params(("parallel", "arbitrary")),
        name="rwkv",
    )(u, u, u, u, u, u, *consts)


def _merge_body(x_ref, gof_ref, gob_ref, gr_ref, fn_ref, ryf_ref, ryb_ref, rbon_ref, rg_ref, gate_ref,
                gn_ref, lng_ref, lnb_ref, gavg_ref, pg_ref, pf_ref, pr_ref, wo_ref, o_ref):
    o = gof_ref[...] + gob_ref[...]
    parts = []
    for h in range(GLA_HEADS):
        oh = o[:, h * GLA_DV:(h + 1) * GLA_DV]
        parts.append(oh * lax.rsqrt(jnp.mean(oh * oh, axis=-1, keepdims=True) + NORM_EPS))
    rg = gr_ref[...]
    y_a = jnp.concatenate(parts, axis=1) * gn_ref[...] * (rg * _sigmoid(rg))
    y = ryf_ref[...] + ryb_ref[...]
    mean = _mm_f32(y, gavg_ref[...])
    yc = y - mean
    var = _mm_f32(yc * yc, gavg_ref[...])
    y_c = (yc * lax.rsqrt(var + RWKV_LN_EPS) * lng_ref[...] + lnb_ref[...] + rbon_ref[...]) * rg_ref[...]
    d = x_ref.shape[1]
    gate = gate_ref[...]
    merged = (_sigmoid(gate[:, 0:d]) * _mm(y_a, pg_ref[...])
              + _sigmoid(gate[:, d:2 * d]) * _mm(fn_ref[...], pf_ref[...])
              + _sigmoid(gate[:, 2 * d:3 * d]) * _mm(y_c, pr_ref[...]))
    o_ref[...] = x_ref[...] + _mm(merged, wo_ref[...])


def _merge(x, gla_of, gla_ob, z_gla, y_fnet, r_yf, r_yb, r_bonus, r_g, z_gate,
           gn, lng, lnb, gavg, pg, pf, pr, wo, *, tm):
    t, d = x.shape
    row = lambda c: pl.BlockSpec((tm, c), lambda i: (i, 0))
    r_col = (2 * GLA_QK + GLA_V) // GLA_V
    gr_spec = pl.BlockSpec((tm, GLA_V), lambda i: (i, r_col))
    consts = (gn, lng, lnb, gavg, pg, pf, pr, wo)
    return pl.pallas_call(
        _merge_body,
        grid=(t // tm,),
        in_specs=[row(d), row(GLA_V), row(GLA_V), gr_spec, row(FNET_W), row(RWKV_W), row(RWKV_W),
                  row(RWKV_W), row(RWKV_W), row(GATE_COLS)] + [_const_spec(a.shape) for a in consts],
        out_specs=row(d),
        out_shape=jax.ShapeDtypeStruct((t, d), F32),
        compiler_params=_params(("parallel",)),
        name="merge",
    )(x, gla_of, gla_ob, z_gla, y_fnet, r_yf, r_yb, r_bonus, r_g, z_gate, *consts)


def _dft_tables(s):
    def cos_sin(n):
        idx = jnp.arange(n, dtype=jnp.int32)
        ang = (2.0 * jnp.pi / n) * ((idx[:, None] * idx[None, :]) % n).astype(F32)
        return jnp.cos(ang) * (n ** -0.5), jnp.sin(ang) * (n ** -0.5)

    cg, sg = cos_sin(FNET_GC)
    eye = jnp.eye(FNET_GROUPS, dtype=F32)
    cc = jnp.kron(eye, cg).astype(BF16)
    sc = jnp.kron(eye, sg).astype(BF16)
    cp, sp = cos_sin(s)
    cs = jnp.concatenate([cp, -sp], axis=1).astype(BF16)
    return cc, sc, cs


def _tri(fwd):
    i = jnp.arange(CHUNK)
    m = (i[None, :] <= i[:, None]) if fwd else (i[None, :] >= i[:, None])
    return m.astype(F32)


def _pad_rows(w, offset, total):
    return jnp.zeros((total, w.shape[1]), F32).at[offset:offset + w.shape[0]].set(w.astype(F32))


def kernel(x, ffn1_norm, ffn1_gate, ffn1_up, ffn1_down, mix_norm, w_in, gla_up_f, gla_bias_f, gla_up_b, gla_bias_b, gla_norm, rwkv_mu, rwkv_w0_f, rwkv_w2_f, rwkv_w0_b, rwkv_w2_b, rwkv_a0_f, rwkv_a2_f, rwkv_a0_b, rwkv_a2_b, rwkv_g2, rwkv_k_k, rwkv_k_a, rwkv_r_k, rwkv_ln_g, rwkv_ln_b, proj_gla, proj_fnet, proj_rwkv, w_out, ffn2_norm, ffn2_gate, ffn2_up, ffn2_down, final_norm):
    bsz, s, d = x.shape
    depth = w_in.shape[0]
    t = bsz * s
    tm = 256
    assert s % CHUNK == 0 and t % tm == 0 and d * 3 == GATE_COLS

    cc, sc, cs = _dft_tables(s)
    trif, trib = _tri(True), _tri(False)
    head_of = jnp.arange(RWKV_W) // RWKV_N
    gsum = (head_of[:, None] == head_of[None, :]).astype(F32)
    gavg = gsum * (1.0 / RWKV_N)
    row = lambda a: a.astype(F32).reshape(1, -1)
    fnorm = row(final_norm)

    x2 = x.reshape(t, d)
    for l in range(depth):
        x2 = _ffn(x2, row(ffn1_norm[l]), ffn1_gate[l].astype(BF16), ffn1_up[l].astype(BF16),
                  ffn1_down[l].astype(BF16), fnorm, final=False, tm=tm)

        w = w_in[l]
        o_f, o_r, o_g = GLA_COLS_RAW, GLA_COLS_RAW + FNET_W, GLA_COLS_RAW + FNET_W + RWKV_COLS
        w_pad = jnp.concatenate([
            w[:, :o_f], jnp.zeros((d, GLA_COLS - GLA_COLS_RAW), w.dtype),
            w[:, o_f:o_r],
            w[:, o_r:o_g], jnp.zeros((d, RWKV_COLS_PAD - RWKV_COLS), w.dtype),
            w[:, o_g:]], axis=1).astype(BF16)
        z_gla, z_fnet, z_rwkv, z_gate = _inproj(x2, row(mix_norm[l]), w_pad, tm=tm)

        gla_of, gla_ob = _gla(
            z_gla.reshape(bsz, s, GLA_COLS),
            _pad_rows(gla_up_f[l], 0, LANES), row(gla_bias_f[l]),
            _pad_rows(gla_up_b[l], GLA_RANK, LANES), row(gla_bias_b[l]), trif, trib)

        y_fnet = _fnet(z_fnet.reshape(bsz, s, FNET_W), cc, sc, cs)

        mu = jnp.concatenate([rwkv_mu[l].astype(F32), jnp.zeros((RWKV_COLS_PAD - RWKV_COLS,), F32)]).reshape(1, -1)
        consts = (mu, row(rwkv_w0_f[l]), row(rwkv_w0_b[l]), row(rwkv_a0_f[l]), row(rwkv_a0_b[l]),
                  _pad_rows(rwkv_w2_f[l], 0, LANES), _pad_rows(rwkv_w2_b[l], 32, LANES),
                  _pad_rows(rwkv_a2_f[l], 64, LANES), _pad_rows(rwkv_a2_b[l], 96, LANES),
                  _pad_rows(rwkv_g2[l], 0, LANES),
                  row(rwkv_k_k[l]), row(rwkv_k_a[l]), row(rwkv_r_k[l]), trif, trib, gsum)
        r_yf, r_yb, r_bonus, r_g = _rwkv(z_rwkv.reshape(bsz, s, RWKV_COLS_PAD), consts)

        flat = lambda a: a.reshape(t, a.shape[-1])
        x2 = _merge(x2, flat(gla_of), flat(gla_ob), z_gla, flat(y_fnet), flat(r_yf), flat(r_yb),
                    flat(r_bonus), flat(r_g), z_gate,
                    row(gla_norm[l]), row(rwkv_ln_g[l]), row(rwkv_ln_b[l]), gavg,
                    proj_gla[l].astype(BF16), proj_fnet[l].astype(BF16), proj_rwkv[l].astype(BF16),
                    w_out[l].astype(BF16), tm=tm)

        x2 = _ffn(x2, row(ffn2_norm[l]), ffn2_gate[l].astype(BF16), ffn2_up[l].astype(BF16),
                  ffn2_down[l].astype(BF16), fnorm, final=(l == depth - 1), tm=tm)
    return x2.reshape(bsz, s, d)
```

```python
import functools

import jax
import jax.numpy as jnp
from jax import lax
from jax.experimental import pallas as pl
from jax.experimental.pallas import tpu as pltpu

F32 = jnp.float32
BF16 = jnp.bfloat16

NORM_EPS = 1e-6
RWKV_LN_EPS = 64e-5
GLA_TAU = 16.0

GLA_HEADS, GLA_DK, GLA_DV, GLA_RANK = 4, 64, 128, 16
GLA_QK, GLA_V = GLA_HEADS * GLA_DK, GLA_HEADS * GLA_DV
FNET_GROUPS, FNET_GC = 4, 128
FNET_W = FNET_GROUPS * FNET_GC
RWKV_HEADS, RWKV_N = 8, 64
RWKV_W = RWKV_HEADS * RWKV_N
RWKV_LOWRANK = 128
RWKV_GATE_RANK = 96
RWKV_COLS = 3 * RWKV_W + RWKV_LOWRANK + RWKV_GATE_RANK

LANES = 128
CHUNK = 64

GLA_COLS_RAW = 2 * GLA_QK + 2 * GLA_V + 2 * GLA_RANK
GLA_COLS = 13 * LANES
RWKV_COLS_PAD = 14 * LANES
GATE_COLS = 3 * 1024

VMEM_LIMIT = 56 * 1024 * 1024


def _mm(a, b):
    return jnp.dot(a.astype(BF16), b.astype(BF16), preferred_element_type=F32)


def _mm_nt(a, b):
    return lax.dot_general(a.astype(BF16), b.astype(BF16), (((1,), (1,)), ((), ())),
                           preferred_element_type=F32)


def _mm_tn(a, b):
    return lax.dot_general(a.astype(BF16), b.astype(BF16), (((0,), (0,)), ((), ())),
                           preferred_element_type=F32)


def _split(x, n):
    pieces = []
    for _ in range(n - 1):
        p = x.astype(BF16)
        pieces.append(p)
        x = x - p.astype(F32)
    pieces.append(x.astype(BF16))
    return pieces


def _mm_xw(a, w, n):
    out = None
    for p in _split(a, n):
        d = jnp.dot(p, w, preferred_element_type=F32)
        out = d if out is None else out + d
    return out


def _mm_wx(w, a, n):
    out = None
    for p in _split(a, n):
        d = jnp.dot(w, p, preferred_element_type=F32)
        out = d if out is None else out + d
    return out


def _mm_3x(a, b):
    a_hi, a_lo = _split(a, 2)
    b_hi, b_lo = _split(b, 2)
    return (jnp.dot(a_hi, b_hi, preferred_element_type=F32) + jnp.dot(a_lo, b_hi, preferred_element_type=F32)
            + jnp.dot(a_hi, b_lo, preferred_element_type=F32))


_mm_inv = _mm


def _sigmoid(x):
    return 1.0 / (1.0 + jnp.exp(-x))


def _log_sigmoid(x):
    return jnp.minimum(x, 0.0) - jnp.log(1.0 + jnp.exp(-jnp.abs(x)))


def _rms(x, g):
    return x * lax.rsqrt(jnp.mean(x * x, axis=-1, keepdims=True) + NORM_EPS) * g


def _const_spec(shape):
    nd = len(shape)
    return pl.BlockSpec(shape, lambda *_: (0,) * nd, pipeline_mode=pl.Buffered(1))


def _params(sem):
    return pltpu.CompilerParams(dimension_semantics=sem, vmem_limit_bytes=VMEM_LIMIT)


def _ffn_body(x_ref, g_ref, wg_ref, wu_ref, wd_ref, fg_ref, o_ref, *, final):
    x = x_ref[...]
    h = _rms(x, g_ref[...]).astype(BF16)
    gate = jnp.dot(h, wg_ref[...], preferred_element_type=F32)
    up = jnp.dot(h, wu_ref[...], preferred_element_type=F32)
    act = (gate * _sigmoid(gate) * up).astype(BF16)
    y = x + 0.5 * jnp.dot(act, wd_ref[...], preferred_element_type=F32)
    if final:
        y = _rms(y, fg_ref[...])
    o_ref[...] = y


def _ffn(x, g, wg, wu, wd, fg, *, final, tm):
    t, d = x.shape
    f = wg.shape[1]
    row = pl.BlockSpec((tm, d), lambda i: (i, 0))
    return pl.pallas_call(
        functools.partial(_ffn_body, final=final),
        grid=(t // tm,),
        in_specs=[row, _const_spec((1, d)), _const_spec((d, f)), _const_spec((d, f)),
                  _const_spec((f, d)), _const_spec((1, d))],
        out_specs=row,
        out_shape=jax.ShapeDtypeStruct((t, d), F32),
        compiler_params=_params(("parallel",)),
        name="ffn",
    )(x, g, wg, wu, wd, fg)


def _inproj_body(x_ref, g_ref, w_ref, gla_ref, fnet_ref, rwkv_ref, gate_ref):
    h = _rms(x_ref[...], g_ref[...]).astype(BF16)
    c0, c1, c2 = GLA_COLS, GLA_COLS + FNET_W, GLA_COLS + FNET_W + RWKV_COLS_PAD
    gla_ref[...] = jnp.dot(h, w_ref[:, 0:c0], preferred_element_type=F32)
    fnet_ref[...] = jnp.dot(h, w_ref[:, c0:c1], preferred_element_type=F32).astype(BF16)
    rwkv_ref[...] = jnp.dot(h, w_ref[:, c1:c2], preferred_element_type=F32)
    gate_ref[...] = jnp.dot(h, w_ref[:, c2:c2 + GATE_COLS], preferred_element_type=F32)


def _inproj(x, g, w, *, tm):
    t, d = x.shape
    n = w.shape[1]
    row = lambda c: pl.BlockSpec((tm, c), lambda i: (i, 0))
    return pl.pallas_call(
        _inproj_body,
        grid=(t // tm,),
        in_specs=[row(d), _const_spec((1, d)), _const_spec((d, n))],
        out_specs=[row(GLA_COLS), row(FNET_W), row(RWKV_COLS_PAD), row(GATE_COLS)],
        out_shape=[jax.ShapeDtypeStruct((t, GLA_COLS), F32),
                   jax.ShapeDtypeStruct((t, FNET_W), BF16),
                   jax.ShapeDtypeStruct((t, RWKV_COLS_PAD), F32),
                   jax.ShapeDtypeStruct((t, GATE_COLS), F32)],
        compiler_params=_params(("parallel",)),
        name="inproj",
    )(x, g, w)


def _lane_masks():
    lane = lax.broadcasted_iota(jnp.int32, (CHUNK, LANES), 1)
    return lane < (LANES // 2), lane >= (LANES // 2)


def _masked_stack(x, lo, hi):
    return jnp.concatenate([jnp.where(lo, x, 0.0), jnp.where(hi, x, 0.0)], axis=0)


def _pair_masks(fwd, inclusive):
    n = 2 * CHUNK
    i = lax.broadcasted_iota(jnp.int32, (n, n), 0)
    j = lax.broadcasted_iota(jnp.int32, (n, n), 1)
    same = (i < CHUNK) == (j < CHUNK)
    strict = (j < i) if fwd else (j > i)
    if inclusive:
        return same & (strict | (i == j))
    return same & strict


def _gla_direction(z, up, bias, tri, st_ref, o_ref, fwd):
    L = CHUNK
    q = z[:, 0:GLA_QK] * (GLA_DK ** -0.5)
    k = z[:, GLA_QK:2 * GLA_QK]
    v = z[:, 2 * GLA_QK:2 * GLA_QK + GLA_V]
    dn = z[:, GLA_COLS - LANES:GLA_COLS]
    la = _log_sigmoid(_mm_3x(dn, up) + bias) * (1.0 / GLA_TAU)
    b = _mm_wx(tri, la, 3)
    btot = b[L - 1:L, :] if fwd else b[0:1, :]
    qd = q * jnp.exp(b)
    kd = k * jnp.exp(-b)
    kt = k * jnp.exp(btot - b)
    dec = jnp.exp(btot)
    lo, hi = _lane_masks()
    mask = _pair_masks(fwd, inclusive=fwd)
    for p in range(GLA_HEADS // 2):
        sl = slice(p * LANES, (p + 1) * LANES)
        q_ms = _masked_stack(qd[:, sl], lo, hi)
        k_st = jnp.concatenate([kd[:, sl], kd[:, sl]], axis=0)
        kt_ms = _masked_stack(kt[:, sl], lo, hi)
        v_st = jnp.concatenate([v[:, (2 * p) * GLA_DV:(2 * p + 1) * GLA_DV],
                                v[:, (2 * p + 1) * GLA_DV:(2 * p + 2) * GLA_DV]], axis=0)
        st = st_ref[p]
        sc = jnp.where(mask, _mm_nt(q_ms, k_st), 0.0)
        o = _mm(sc, v_st) + _mm_nt(q_ms, st)
        o_ref[:, (2 * p) * GLA_DV:(2 * p + 1) * GLA_DV] = o[0:L]
        o_ref[:, (2 * p + 1) * GLA_DV:(2 * p + 2) * GLA_DV] = o[L:2 * L]
        st_ref[p] = st * dec[:, sl] + _mm_tn(v_st, kt_ms)


def _gla_body(zf_ref, zb_ref, upf_ref, bf_ref, upb_ref, bb_ref, trif_ref, trib_ref,
              of_ref, ob_ref, stf_ref, stb_ref):
    @pl.when(pl.program_id(1) == 0)
    def _():
        stf_ref[...] = jnp.zeros_like(stf_ref)
        stb_ref[...] = jnp.zeros_like(stb_ref)

    _gla_direction(zf_ref[...], upf_ref[...], bf_ref[...], trif_ref[...], stf_ref, of_ref, True)
    _gla_direction(zb_ref[...], upb_ref[...], bb_ref[...], trib_ref[...], stb_ref, ob_ref, False)


def _gla(z, upf, bf, upb, bb, trif, trib):
    bsz, s, _ = z.shape
    nc = s // CHUNK
    fmap = lambda b, c: (b, c, 0)
    bmap = lambda b, c: (b, nc - 1 - c, 0)
    zspec = lambda m: pl.BlockSpec((None, CHUNK, GLA_COLS), m)
    ospec = lambda m: pl.BlockSpec((None, CHUNK, GLA_V), m)
    state = pltpu.VMEM((GLA_HEADS // 2, GLA_DV, LANES), F32)
    return pl.pallas_call(
        _gla_body,
        grid=(bsz, nc),
        in_specs=[zspec(fmap), zspec(bmap), _const_spec(upf.shape), _const_spec(bf.shape),
                  _const_spec(upb.shape), _const_spec(bb.shape),
                  _const_spec(trif.shape), _const_spec(trib.shape)],
        out_specs=[ospec(fmap), ospec(bmap)],
        out_shape=[jax.ShapeDtypeStruct((bsz, s, GLA_V), F32)] * 2,
        scratch_shapes=[state, state],
        compiler_params=_params(("parallel", "arbitrary")),
        name="gla",
    )(z, z, upf, bf, upb, bb, trif, trib)


def _fnet_body(z_ref, cc_ref, sc_ref, cs_ref, o_ref, ab_ref, *, rows):
    s = z_ref.shape[0]
    z = z_ref[...]
    ab_ref[0:s, :] = jnp.dot(z, cc_ref[...], preferred_element_type=F32).astype(BF16)
    ab_ref[s:2 * s, :] = jnp.dot(z, sc_ref[...], preferred_element_type=F32).astype(BF16)
    for i in range(s // rows):
        o_ref[i * rows:(i + 1) * rows, :] = jnp.dot(
            cs_ref[i * rows:(i + 1) * rows, :], ab_ref[...], preferred_element_type=F32)


def _fnet(z, cc, sc, cs):
    bsz, s, w = z.shape
    rows = min(s, 512)
    blk = lambda: pl.BlockSpec((None, s, w), lambda b: (b, 0, 0))
    return pl.pallas_call(
        functools.partial(_fnet_body, rows=rows),
        grid=(bsz,),
        in_specs=[blk(), _const_spec(cc.shape), _const_spec(sc.shape), _const_spec(cs.shape)],
        out_specs=blk(),
        out_shape=jax.ShapeDtypeStruct((bsz, s, w), F32),
        scratch_shapes=[pltpu.VMEM((2 * s, w), BF16)],
        compiler_params=_params(("parallel",)),
        name="fnet",
    )(z, cc, sc, cs)


RWKV_BLOCK = 256
RWKV_NCH = RWKV_BLOCK // CHUNK
RWKV_PAIRS = RWKV_HEADS // 2


def _rwkv_prep(u_ref, up_ref, un_ref, blk, nblk, prm, d, prep_refs, dec_ref, extra_refs, fwd):
    L, R, W = CHUNK, RWKV_BLOCK, RWKV_W
    (mu, w0, a0, w2, a2, g2, k_k, k_a, r_k, tri, ones_bd, gsum) = prm
    u = u_ref[...]
    row = lax.broadcasted_iota(jnp.int32, (R, 1), 0)
    prev_row = jnp.where(blk > 0, up_ref[7:8, :], 0.0)
    next_row = jnp.where(blk < nblk - 1, un_ref[0:1, :], 0.0)
    prev = jnp.where(row == 0, prev_row, pltpu.roll(u, 1, 0))
    nxt = jnp.where(row == R - 1, next_row, pltpu.roll(u, R - 1, 0))
    u = u + mu * (0.5 * (prev + nxt) - u)

    r, k, v = u[:, 0:W], u[:, W:2 * W], u[:, 2 * W:3 * W]
    sm = u[:, 3 * W:3 * W + LANES]
    lw = -jnp.exp(_log_sigmoid(w0 + _mm_3x(jnp.tanh(sm), w2)) - 0.5)
    a = _sigmoid(a0 + _mm_3x(sm, a2))
    kk = k * k_k
    kap = kk * lax.rsqrt(_mm_xw(kk * kk, gsum, 2) + 1e-12)
    kd = k * (1.0 + (a - 1.0) * k_a)
    beta = kap * a
    inc = _mm_wx(tri, lw, 3)
    tot = _mm_wx(ones_bd, lw, 3)
    exc = inc - lw
    e_exc = jnp.exp(exc)
    e_ninc = jnp.exp(-inc)
    e_tail = jnp.exp(tot - inc)
    dec = jnp.exp(tot)
    outs = (kap * e_exc, r * (jnp.exp(inc) if fwd else e_exc), kd * e_ninc, beta * e_ninc,
            kd * e_tail, beta * e_tail, v)
    for ref, val in zip(prep_refs, outs):
        for j in range(RWKV_NCH):
            ref[d, j] = val[j * L:(j + 1) * L]
    for j in range(RWKV_NCH):
        dec_ref[d, j] = dec[j * L:j * L + 8]

    if extra_refs is not None:
        bonus_ref, g_ref = extra_refs
        gd = u[:, 3 * W + LANES:3 * W + 2 * LANES]
        g_ref[...] = _mm_3x(_sigmoid(gd), g2)
        bonus_ref[...] = _mm_xw(r * kd * r_k, gsum, 2) * v


def _rwkv_units():
    return [(d, p) for d in range(2) for p in range(RWKV_PAIRS)]


def _rwkv_solve(j, prep_refs, tinv_ref, pkb_ref, mkv_ref, masks):
    L = CHUNK
    n2 = 2 * L
    kaph_ref, rh_ref, kt_ref, bt_ref, _, _, v_ref = prep_refs
    lo, hi, eye, m_strict, m_read = masks
    units = _rwkv_units()
    nmat, t_inv, pw = {}, {}, {}
    for (d, p) in units:
        sl = pl.ds(p * LANES, LANES)
        xs = jnp.concatenate([_masked_stack(kaph_ref[d, j, :, sl], lo, hi),
                              _masked_stack(rh_ref[d, j, :, sl], lo, hi)], axis=0)
        kt = kt_ref[d, j, :, sl]
        bt = bt_ref[d, j, :, sl]
        s_all = _mm_nt(xs, jnp.concatenate([kt, kt, bt, bt], axis=0))
        m_k = jnp.where(m_strict[d], s_all[0:n2, 0:n2], 0.0)
        nmat[d, p] = jnp.where(m_strict[d], -s_all[0:n2, n2:2 * n2], 0.0)
        p_k = jnp.where(m_read[d], s_all[n2:2 * n2, 0:n2], 0.0)
        p_b = jnp.where(m_read[d], -s_all[n2:2 * n2, n2:2 * n2], 0.0)
        pkb_ref[d, j, p] = jnp.concatenate([p_k, p_b], axis=1).astype(BF16)
        mkv_ref[d, j, p] = _mm(m_k, _masked_stack(v_ref[d, j, :, sl], lo, hi))
    for u in units:
        t_inv[u] = eye + nmat[u]
        pw[u] = _mm_inv(nmat[u], nmat[u])
    for _ in range(4):
        for u in units:
            both = _mm_inv(jnp.concatenate([t_inv[u], pw[u]], axis=0), pw[u])
            t_inv[u] = t_inv[u] + both[0:n2]
            pw[u] = both[n2:2 * n2]
    for (d, p) in units:
        u = (d, p)
        tinv_ref[d, j, p] = (t_inv[u] + _mm_inv(t_inv[u], pw[u])).astype(BF16)


def _rwkv_scan(j, prep_refs, dec_ref, tinv_ref, pkb_ref, mkv_ref, st_ref, y_refs, masks):
    L = CHUNK
    n2 = 2 * L
    kaph_ref, rh_ref, _, _, ktail_ref, btail_ref, v_ref = prep_refs
    lo, hi = masks[0], masks[1]
    units = _rwkv_units()
    cj = {0: j, 1: RWKV_NCH - 1 - j}
    v_ms, xh, uu = {}, {}, {}
    for (d, p) in units:
        sl = pl.ds(p * LANES, LANES)
        xs = jnp.concatenate([_masked_stack(kaph_ref[d, cj[d], :, sl], lo, hi),
                              _masked_stack(rh_ref[d, cj[d], :, sl], lo, hi)], axis=0)
        xh[d, p] = _mm_nt(xs, st_ref[d, p])
    for (d, p) in units:
        sl = pl.ds(p * LANES, LANES)
        v_ms[d, p] = _masked_stack(v_ref[d, cj[d], :, sl], lo, hi)
        uu[d, p] = _mm(tinv_ref[d, cj[d], p], xh[d, p][0:n2] + mkv_ref[d, cj[d], p])
    for (d, p) in units:
        sl = pl.ds(p * LANES, LANES)
        y = xh[d, p][n2:2 * n2] + _mm(pkb_ref[d, cj[d], p], jnp.concatenate([v_ms[d, p], uu[d, p]], axis=0))
        row0 = pl.multiple_of(cj[d] * L, L)
        y_refs[d][pl.ds(row0, L), sl] = y[0:L] + y[L:n2]
        tails = jnp.concatenate([_masked_stack(ktail_ref[d, cj[d], :, sl], lo, hi),
                                 _masked_stack(btail_ref[d, cj[d], :, sl], lo, hi)], axis=0)
        dec = dec_ref[d, cj[d], 0:1, sl]
        st_ref[d, p] = st_ref[d, p] * dec + _mm_tn(jnp.concatenate([v_ms[d, p], -uu[d, p]], axis=0), tails)


def _rwkv_body(uf_ref, ufp_ref, ufn_ref, ub_ref, ubp_ref, ubn_ref,
               mu_ref, w0f_ref, w0b_ref, a0f_ref, a0b_ref, w2f_ref, w2b_ref, a2f_ref, a2b_ref, g2_ref,
               kk_ref, ka_ref, rk_ref, trif_ref, trib_ref, ones_ref, gsum_ref,
               yf_ref, yb_ref, bonus_ref, g_ref,
               p0, p1, p2, p3, p4, p5, p6, dec_ref, tinv_ref, pkb_ref, mkv_ref, st_ref):
    i = pl.program_id(1)
    nblk = pl.num_programs(1)
    prep_refs = (p0, p1, p2, p3, p4, p5, p6)

    @pl.when(i == 0)
    def _():
        st_ref[...] = jnp.zeros_like(st_ref)

    shared = (g2_ref[...], kk_ref[...], ka_ref[...], rk_ref[...])
    tables = (ones_ref[...], gsum_ref[...])
    prm_f = (mu_ref[...], w0f_ref[...], a0f_ref[...], w2f_ref[...], a2f_ref[...]) + shared + (trif_ref[...],) + tables
    prm_b = (mu_ref[...], w0b_ref[...], a0b_ref[...], w2b_ref[...], a2b_ref[...]) + shared + (trib_ref[...],) + tables
    _rwkv_prep(uf_ref, ufp_ref, ufn_ref, i, nblk, prm_f, 0, prep_refs, dec_ref, (bonus_ref, g_ref), True)
    _rwkv_prep(ub_ref, ubp_ref, ubn_ref, nblk - 1 - i, nblk, prm_b, 1, prep_refs, dec_ref, None, False)

    lo, hi = _lane_masks()
    n2 = 2 * CHUNK
    ii = lax.broadcasted_iota(jnp.int32, (n2, n2), 0)
    jj = lax.broadcasted_iota(jnp.int32, (n2, n2), 1)
    eye = (ii == jj).astype(F32)
    m_strict = (_pair_masks(True, inclusive=False), _pair_masks(False, inclusive=False))
    m_read = (_pair_masks(True, inclusive=True), _pair_masks(False, inclusive=False))
    masks = (lo, hi, eye, m_strict, m_read)

    def solve(j, carry):
        _rwkv_solve(j, prep_refs, tinv_ref, pkb_ref, mkv_ref, masks)
        return carry

    lax.fori_loop(0, RWKV_NCH, solve, 0)

    def scan(j, carry):
        _rwkv_scan(j, prep_refs, dec_ref, tinv_ref, pkb_ref, mkv_ref, st_ref, (yf_ref, yb_ref), masks)
        return carry

    lax.fori_loop(0, RWKV_NCH, scan, 0)


def _rwkv(u, consts):
    bsz, s, cols = u.shape
    R = RWKV_BLOCK
    nb = s // R
    sub = 8
    per = R // sub
    nsub = s // sub
    fc = lambda i: i
    bc = lambda i: nb - 1 - i
    main = lambda cm: pl.BlockSpec((None, R, cols), lambda b, i: (b, cm(i), 0))
    prev = lambda cm: pl.BlockSpec((None, sub, cols), lambda b, i: (b, jnp.maximum(cm(i) * per - 1, 0), 0))
    nxt = lambda cm: pl.BlockSpec((None, sub, cols), lambda b, i: (b, jnp.minimum((cm(i) + 1) * per, nsub - 1), 0))
    ospec = lambda cm: pl.BlockSpec((None, R, RWKV_W), lambda b, i: (b, cm(i), 0))
    out = jax.ShapeDtypeStruct((bsz, s, RWKV_W), F32)
    unit = (2, RWKV_NCH, RWKV_PAIRS)
    scratch = ([pltpu.VMEM((2, RWKV_NCH, CHUNK, RWKV_W), F32)] * 7
               + [pltpu.VMEM((2, RWKV_NCH, 8, RWKV_W), F32),
                  pltpu.VMEM(unit + (LANES, LANES), BF16),
                  pltpu.VMEM(unit + (LANES, 2 * LANES), BF16),
                  pltpu.VMEM(unit + (LANES, LANES), F32),
                  pltpu.VMEM((2, RWKV_PAIRS, LANES, LANES), F32)])
    return pl.pallas_call(
        _rwkv_body,
        grid=(bsz, nb),
        in_specs=[main(fc), prev(fc), nxt(fc), main(bc), prev(bc), nxt(bc)] + [_const_spec(a.shape) for a in consts],
        out_specs=[ospec(fc), ospec(bc), ospec(fc), ospec(fc)],
        out_shape=[out] * 4,
        scratch_shapes=scratch,
        compiler_params=_params(("parallel", "arbitrary")),
        name="rwkv",
    )(u, u, u, u, u, u, *consts)


def _merge_body(x_ref, gof_ref, gob_ref, gr_ref, fn_ref, ryf_ref, ryb_ref, rbon_ref, rg_ref, gate_ref,
                gn_ref, lng_ref, lnb_ref, gavg_ref, pg_ref, pf_ref, pr_ref, wo_ref, o_ref):
    o = gof_ref[...] + gob_ref[...]
    parts = []
    for h in range(GLA_HEADS):
        oh = o[:, h * GLA_DV:(h + 1) * GLA_DV]
        parts.append(oh * lax.rsqrt(jnp.mean(oh * oh, axis=-1, keepdims=True) + NORM_EPS))
    rg = gr_ref[...]
    y_a = jnp.concatenate(parts, axis=1) * gn_ref[...] * (rg * _sigmoid(rg))
    y = ryf_ref[...] + ryb_ref[...]
    mean = _mm_xw(y, gavg_ref[...], 3)
    yc = y - mean
    var = _mm_xw(yc * yc, gavg_ref[...], 2)
    y_c = (yc * lax.rsqrt(var + RWKV_LN_EPS) * lng_ref[...] + lnb_ref[...] + rbon_ref[...]) * rg_ref[...]
    d = x_ref.shape[1]
    gate = gate_ref[...]
    merged = (_sigmoid(gate[:, 0:d]) * _mm(y_a, pg_ref[...])
              + _sigmoid(gate[:, d:2 * d]) * _mm(fn_ref[...], pf_ref[...])
              + _sigmoid(gate[:, 2 * d:3 * d]) * _mm(y_c, pr_ref[...]))
    o_ref[...] = x_ref[...] + _mm(merged, wo_ref[...])


def _merge(x, gla_of, gla_ob, z_gla, y_fnet, r_yf, r_yb, r_bonus, r_g, z_gate,
           gn, lng, lnb, gavg, pg, pf, pr, wo, *, tm):
    t, d = x.shape
    row = lambda c: pl.BlockSpec((tm, c), lambda i: (i, 0))
    r_col = (2 * GLA_QK + GLA_V) // GLA_V
    gr_spec = pl.BlockSpec((tm, GLA_V), lambda i: (i, r_col))
    consts = (gn, lng, lnb, gavg, pg, pf, pr, wo)
    return pl.pallas_call(
        _merge_body,
        grid=(t // tm,),
        in_specs=[row(d), row(GLA_V), row(GLA_V), gr_spec, row(FNET_W), row(RWKV_W), row(RWKV_W),
                  row(RWKV_W), row(RWKV_W), row(GATE_COLS)] + [_const_spec(a.shape) for a in consts],
        out_specs=row(d),
        out_shape=jax.ShapeDtypeStruct((t, d), F32),
        compiler_params=_params(("parallel",)),
        name="merge",
    )(x, gla_of, gla_ob, z_gla, y_fnet, r_yf, r_yb, r_bonus, r_g, z_gate, *consts)


def _dft_tables(s):
    def cos_sin(n):
        idx = jnp.arange(n, dtype=jnp.int32)
        ang = (2.0 * jnp.pi / n) * ((idx[:, None] * idx[None, :]) % n).astype(F32)
        return jnp.cos(ang) * (n ** -0.5), jnp.sin(ang) * (n ** -0.5)

    cg, sg = cos_sin(FNET_GC)
    eye = jnp.eye(FNET_GROUPS, dtype=F32)
    cc = jnp.kron(eye, cg).astype(BF16)
    sc = jnp.kron(eye, sg).astype(BF16)
    cp, sp = cos_sin(s)
    cs = jnp.concatenate([cp, -sp], axis=1).astype(BF16)
    return cc, sc, cs


def _tri(fwd, n):
    i = jnp.arange(n)
    same = (i[None, :] // CHUNK) == (i[:, None] // CHUNK)
    m = (i[None, :] <= i[:, None]) if fwd else (i[None, :] >= i[:, None])
    return (m & same).astype(BF16)


def _chunk_ones(n):
    i = jnp.arange(n)
    return ((i[None, :] // CHUNK) == (i[:, None] // CHUNK)).astype(BF16)


def _pad_rows(w, offset, total):
    return jnp.zeros((total, w.shape[1]), F32).at[offset:offset + w.shape[0]].set(w.astype(F32))


def kernel(x, ffn1_norm, ffn1_gate, ffn1_up, ffn1_down, mix_norm, w_in, gla_up_f, gla_bias_f, gla_up_b, gla_bias_b, gla_norm, rwkv_mu, rwkv_w0_f, rwkv_w2_f, rwkv_w0_b, rwkv_w2_b, rwkv_a0_f, rwkv_a2_f, rwkv_a0_b, rwkv_a2_b, rwkv_g2, rwkv_k_k, rwkv_k_a, rwkv_r_k, rwkv_ln_g, rwkv_ln_b, proj_gla, proj_fnet, proj_rwkv, w_out, ffn2_norm, ffn2_gate, ffn2_up, ffn2_down, final_norm):
    bsz, s, d = x.shape
    depth = w_in.shape[0]
    t = bsz * s
    tm = 256
    assert s % RWKV_BLOCK == 0 and t % tm == 0 and d * 3 == GATE_COLS

    cc, sc, cs = _dft_tables(s)
    trif, trib = _tri(True, CHUNK), _tri(False, CHUNK)
    trif_r, trib_r, ones_r = _tri(True, RWKV_BLOCK), _tri(False, RWKV_BLOCK), _chunk_ones(RWKV_BLOCK)
    head_of = jnp.arange(RWKV_W) // RWKV_N
    same_head = head_of[:, None] == head_of[None, :]
    gsum = same_head.astype(BF16)
    gavg = (same_head.astype(F32) * (1.0 / RWKV_N)).astype(BF16)
    row = lambda a: a.astype(F32).reshape(1, -1)
    fnorm = row(final_norm)

    x2 = x.reshape(t, d)
    for l in range(depth):
        x2 = _ffn(x2, row(ffn1_norm[l]), ffn1_gate[l].astype(BF16), ffn1_up[l].astype(BF16),
                  ffn1_down[l].astype(BF16), fnorm, final=False, tm=tm)

        w = w_in[l]
        o_f, o_r, o_g = GLA_COLS_RAW, GLA_COLS_RAW + FNET_W, GLA_COLS_RAW + FNET_W + RWKV_COLS
        w_pad = jnp.concatenate([
            w[:, :o_f], jnp.zeros((d, GLA_COLS - GLA_COLS_RAW), w.dtype),
            w[:, o_f:o_r],
            w[:, o_r:o_g], jnp.zeros((d, RWKV_COLS_PAD - RWKV_COLS), w.dtype),
            w[:, o_g:]], axis=1).astype(BF16)
        z_gla, z_fnet, z_rwkv, z_gate = _inproj(x2, row(mix_norm[l]), w_pad, tm=tm)

        gla_of, gla_ob = _gla(
            z_gla.reshape(bsz, s, GLA_COLS),
            _pad_rows(gla_up_f[l], 0, LANES), row(gla_bias_f[l]),
            _pad_rows(gla_up_b[l], GLA_RANK, LANES), row(gla_bias_b[l]), trif, trib)

        y_fnet = _fnet(z_fnet.reshape(bsz, s, FNET_W), cc, sc, cs)

        mu = jnp.concatenate([rwkv_mu[l].astype(F32), jnp.zeros((RWKV_COLS_PAD - RWKV_COLS,), F32)]).reshape(1, -1)
        consts = (mu, row(rwkv_w0_f[l]), row(rwkv_w0_b[l]), row(rwkv_a0_f[l]), row(rwkv_a0_b[l]),
                  _pad_rows(rwkv_w2_f[l], 0, LANES), _pad_rows(rwkv_w2_b[l], 32, LANES),
                  _pad_rows(rwkv_a2_f[l], 64, LANES), _pad_rows(rwkv_a2_b[l], 96, LANES),
                  _pad_rows(rwkv_g2[l], 0, LANES),
                  row(rwkv_k_k[l]), row(rwkv_k_a[l]), row(rwkv_r_k[l]), trif_r, trib_r, ones_r, gsum)
        r_yf, r_yb, r_bonus, r_g = _rwkv(z_rwkv.reshape(bsz, s, RWKV_COLS_PAD), consts)

        flat = lambda a: a.reshape(t, a.shape[-1])
        x2 = _merge(x2, flat(gla_of), flat(gla_ob), z_gla, flat(y_fnet), flat(r_yf), flat(r_yb),
                    flat(r_bonus), flat(r_g), z_gate,
                    row(gla_norm[l]), row(rwkv_ln_g[l]), row(rwkv_ln_b[l]), gavg,
                    proj_gla[l].astype(BF16), proj_fnet[l].astype(BF16), proj_rwkv[l].astype(BF16),
                    w_out[l].astype(BF16), tm=tm)

        x2 = _ffn(x2, row(ffn2_norm[l]), ffn2_gate[l].astype(BF16), ffn2_up[l].astype(BF16),
                  ffn2_down[l].astype(BF16), fnorm, final=(l == depth - 1), tm=tm)
    return x2.reshape(bsz, s, d)
```

```python
import functools

import jax
import jax.numpy as jnp
from jax import lax
from jax.experimental import pallas as pl
from jax.experimental.pallas import tpu as pltpu

F32 = jnp.float32
BF16 = jnp.bfloat16

NORM_EPS = 1e-6
RWKV_LN_EPS = 64e-5
GLA_TAU = 16.0

GLA_HEADS, GLA_DK, GLA_DV, GLA_RANK = 4, 64, 128, 16
GLA_QK, GLA_V = GLA_HEADS * GLA_DK, GLA_HEADS * GLA_DV
FNET_GROUPS, FNET_GC = 4, 128
FNET_W = FNET_GROUPS * FNET_GC
RWKV_HEADS, RWKV_N = 8, 64
RWKV_W = RWKV_HEADS * RWKV_N
RWKV_LOWRANK = 128
RWKV_GATE_RANK = 96
RWKV_COLS = 3 * RWKV_W + RWKV_LOWRANK + RWKV_GATE_RANK

LANES = 128
CHUNK = 64

GLA_COLS_RAW = 2 * GLA_QK + 2 * GLA_V + 2 * GLA_RANK
GLA_COLS = 13 * LANES
RWKV_COLS_PAD = 14 * LANES
GATE_COLS = 3 * 1024

VMEM_LIMIT = 56 * 1024 * 1024


def _mm(a, b):
    return jnp.dot(a.astype(BF16), b.astype(BF16), preferred_element_type=F32)


def _mm_nt(a, b):
    return lax.dot_general(a.astype(BF16), b.astype(BF16), (((1,), (1,)), ((), ())),
                           preferred_element_type=F32)


def _mm_tn(a, b):
    return lax.dot_general(a.astype(BF16), b.astype(BF16), (((0,), (0,)), ((), ())),
                           preferred_element_type=F32)


def _split(x, n):
    pieces = []
    for _ in range(n - 1):
        p = x.astype(BF16)
        pieces.append(p)
        x = x - p.astype(F32)
    pieces.append(x.astype(BF16))
    return pieces


def _mm_xw(a, w, n):
    out = None
    for p in _split(a, n):
        d = jnp.dot(p, w, preferred_element_type=F32)
        out = d if out is None else out + d
    return out


def _mm_wx(w, a, n):
    out = None
    for p in _split(a, n):
        d = jnp.dot(w, p, preferred_element_type=F32)
        out = d if out is None else out + d
    return out


def _mm_3x(a, b):
    a_hi, a_lo = _split(a, 2)
    b_hi, b_lo = _split(b, 2)
    return (jnp.dot(a_hi, b_hi, preferred_element_type=F32) + jnp.dot(a_lo, b_hi, preferred_element_type=F32)
            + jnp.dot(a_hi, b_lo, preferred_element_type=F32))


_mm_inv = _mm


def _sigmoid(x):
    return 1.0 / (1.0 + jnp.exp(-x))


def _log_sigmoid(x):
    return jnp.minimum(x, 0.0) - jnp.log(1.0 + jnp.exp(-jnp.abs(x)))


def _rms(x, g):
    return x * lax.rsqrt(jnp.mean(x * x, axis=-1, keepdims=True) + NORM_EPS) * g


def _const_spec(shape):
    nd = len(shape)
    return pl.BlockSpec(shape, lambda *_: (0,) * nd, pipeline_mode=pl.Buffered(1))


def _params(sem):
    return pltpu.CompilerParams(dimension_semantics=sem, vmem_limit_bytes=VMEM_LIMIT)


def _ffn_body(x_ref, g_ref, wg_ref, wu_ref, wd_ref, fg_ref, o_ref, *, final):
    x = x_ref[...]
    h = _rms(x, g_ref[...]).astype(BF16)
    gate = jnp.dot(h, wg_ref[...], preferred_element_type=F32)
    up = jnp.dot(h, wu_ref[...], preferred_element_type=F32)
    act = (gate * _sigmoid(gate) * up).astype(BF16)
    y = x + 0.5 * jnp.dot(act, wd_ref[...], preferred_element_type=F32)
    if final:
        y = _rms(y, fg_ref[...])
    o_ref[...] = y


def _ffn(x, g, wg, wu, wd, fg, *, final, tm):
    t, d = x.shape
    f = wg.shape[1]
    row = pl.BlockSpec((tm, d), lambda i: (i, 0))
    return pl.pallas_call(
        functools.partial(_ffn_body, final=final),
        grid=(t // tm,),
        in_specs=[row, _const_spec((1, d)), _const_spec((d, f)), _const_spec((d, f)),
                  _const_spec((f, d)), _const_spec((1, d))],
        out_specs=row,
        out_shape=jax.ShapeDtypeStruct((t, d), F32),
        compiler_params=_params(("parallel",)),
        name="ffn",
    )(x, g, wg, wu, wd, fg)


def _inproj_body(x_ref, g_ref, w_ref, gla_ref, fnet_ref, rwkv_ref, gate_ref):
    h = _rms(x_ref[...], g_ref[...]).astype(BF16)
    c0, c1, c2 = GLA_COLS, GLA_COLS + FNET_W, GLA_COLS + FNET_W + RWKV_COLS_PAD
    gla_ref[...] = jnp.dot(h, w_ref[:, 0:c0], preferred_element_type=F32)
    fnet_ref[...] = jnp.dot(h, w_ref[:, c0:c1], preferred_element_type=F32).astype(BF16)
    rwkv_ref[...] = jnp.dot(h, w_ref[:, c1:c2], preferred_element_type=F32)
    gate_ref[...] = jnp.dot(h, w_ref[:, c2:c2 + GATE_COLS], preferred_element_type=F32).astype(BF16)


def _inproj(x, g, w, *, tm):
    t, d = x.shape
    n = w.shape[1]
    row = lambda c: pl.BlockSpec((tm, c), lambda i: (i, 0))
    return pl.pallas_call(
        _inproj_body,
        grid=(t // tm,),
        in_specs=[row(d), _const_spec((1, d)), _const_spec((d, n))],
        out_specs=[row(GLA_COLS), row(FNET_W), row(RWKV_COLS_PAD), row(GATE_COLS)],
        out_shape=[jax.ShapeDtypeStruct((t, GLA_COLS), F32),
                   jax.ShapeDtypeStruct((t, FNET_W), BF16),
                   jax.ShapeDtypeStruct((t, RWKV_COLS_PAD), F32),
                   jax.ShapeDtypeStruct((t, GATE_COLS), BF16)],
        compiler_params=_params(("parallel",)),
        name="inproj",
    )(x, g, w)


def _lane_masks():
    lane = lax.broadcasted_iota(jnp.int32, (CHUNK, LANES), 1)
    return lane < (LANES // 2), lane >= (LANES // 2)


def _masked_stack(x, lo, hi):
    return jnp.concatenate([jnp.where(lo, x, 0.0), jnp.where(hi, x, 0.0)], axis=0)


def _pair_masks(fwd, inclusive):
    n = 2 * CHUNK
    i = lax.broadcasted_iota(jnp.int32, (n, n), 0)
    j = lax.broadcasted_iota(jnp.int32, (n, n), 1)
    same = (i < CHUNK) == (j < CHUNK)
    strict = (j < i) if fwd else (j > i)
    if inclusive:
        return same & (strict | (i == j))
    return same & strict


SEQ_BLOCK = 256
BLOCK_NCH = SEQ_BLOCK // CHUNK
GLA_PAIRS = GLA_HEADS // 2


def _gla_prep(z_ref, up, bias, tri, ones_bd, d, qd_ref, kd_ref, kt_ref, v_ref, dec_ref):
    L = CHUNK
    z = z_ref[...]
    q = z[:, 0:GLA_QK] * (GLA_DK ** -0.5)
    k = z[:, GLA_QK:2 * GLA_QK]
    v = z[:, 2 * GLA_QK:2 * GLA_QK + GLA_V]
    dn = z[:, GLA_COLS - LANES:GLA_COLS]
    la = _log_sigmoid(_mm_3x(dn, up) + bias) * (1.0 / GLA_TAU)
    b = _mm_wx(tri, la, 3)
    btot = _mm_wx(ones_bd, la, 3)
    outs = ((qd_ref, q * jnp.exp(b)), (kd_ref, k * jnp.exp(-b)), (kt_ref, k * jnp.exp(btot - b)), (v_ref, v))
    dec = jnp.exp(btot)
    for j in range(BLOCK_NCH):
        for ref, val in outs:
            ref[d, j] = val[j * L:(j + 1) * L]
        dec_ref[d, j] = dec[j * L:j * L + 8]


def _gla_step(j, qd_ref, kd_ref, kt_ref, v_ref, dec_ref, st_ref, o_refs, masks):
    L = CHUNK
    lo, hi, m_read = masks
    units = [(d, p) for d in range(2) for p in range(GLA_PAIRS)]
    cj = {0: j, 1: BLOCK_NCH - 1 - j}
    q_ms, v_st, sc = {}, {}, {}
    for (d, p) in units:
        sl = pl.ds(p * LANES, LANES)
        q_ms[d, p] = _masked_stack(qd_ref[d, cj[d], :, sl], lo, hi)
        kd = kd_ref[d, cj[d], :, sl]
        sc[d, p] = jnp.where(m_read[d], _mm_nt(q_ms[d, p], jnp.concatenate([kd, kd], axis=0)), 0.0)
        v_st[d, p] = jnp.concatenate([v_ref[d, cj[d], :, pl.ds((2 * p) * GLA_DV, GLA_DV)],
                                      v_ref[d, cj[d], :, pl.ds((2 * p + 1) * GLA_DV, GLA_DV)]], axis=0)
    for (d, p) in units:
        sl = pl.ds(p * LANES, LANES)
        st = st_ref[d, p]
        o = _mm(sc[d, p], v_st[d, p]) + _mm_nt(q_ms[d, p], st)
        row0 = pl.multiple_of(cj[d] * L, L)
        o_refs[d][pl.ds(row0, L), pl.ds((2 * p) * GLA_DV, GLA_DV)] = o[0:L]
        o_refs[d][pl.ds(row0, L), pl.ds((2 * p + 1) * GLA_DV, GLA_DV)] = o[L:2 * L]
        kt_ms = _masked_stack(kt_ref[d, cj[d], :, sl], lo, hi)
        st_ref[d, p] = st * dec_ref[d, cj[d], 0:1, sl] + _mm_tn(v_st[d, p], kt_ms)


def _gla_body(zf_ref, zb_ref, upf_ref, bf_ref, upb_ref, bb_ref, trif_ref, trib_ref, ones_ref,
              of_ref, ob_ref, qd_ref, kd_ref, kt_ref, v_ref, dec_ref, st_ref):
    @pl.when(pl.program_id(1) == 0)
    def _():
        st_ref[...] = jnp.zeros_like(st_ref)

    scr = (qd_ref, kd_ref, kt_ref, v_ref, dec_ref)
    _gla_prep(zf_ref, upf_ref[...], bf_ref[...], trif_ref[...], ones_ref[...], 0, *scr)
    _gla_prep(zb_ref, upb_ref[...], bb_ref[...], trib_ref[...], ones_ref[...], 1, *scr)
    lo, hi = _lane_masks()
    masks = (lo, hi, (_pair_masks(True, inclusive=True), _pair_masks(False, inclusive=False)))

    def step(j, carry):
        _gla_step(j, *scr, st_ref, (of_ref, ob_ref), masks)
        return carry

    lax.fori_loop(0, BLOCK_NCH, step, 0)


def _gla(z, upf, bf, upb, bb, trif, trib, ones_bd):
    bsz, s, _ = z.shape
    nb = s // SEQ_BLOCK
    fmap = lambda b, i: (b, i, 0)
    bmap = lambda b, i: (b, nb - 1 - i, 0)
    zspec = lambda m: pl.BlockSpec((None, SEQ_BLOCK, GLA_COLS), m)
    ospec = lambda m: pl.BlockSpec((None, SEQ_BLOCK, GLA_V), m)
    per_chunk = lambda rows, w: pltpu.VMEM((2, BLOCK_NCH, rows, w), F32)
    consts = (upf, bf, upb, bb, trif, trib, ones_bd)
    return pl.pallas_call(
        _gla_body,
        grid=(bsz, nb),
        in_specs=[zspec(fmap), zspec(bmap)] + [_const_spec(a.shape) for a in consts],
        out_specs=[ospec(fmap), ospec(bmap)],
        out_shape=[jax.ShapeDtypeStruct((bsz, s, GLA_V), F32)] * 2,
        scratch_shapes=[per_chunk(CHUNK, GLA_QK), per_chunk(CHUNK, GLA_QK), per_chunk(CHUNK, GLA_QK),
                        per_chunk(CHUNK, GLA_V), per_chunk(8, GLA_QK),
                        pltpu.VMEM((2, GLA_PAIRS, GLA_DV, LANES), F32)],
        compiler_params=_params(("parallel", "arbitrary")),
        name="gla",
    )(z, z, *consts)


def _fnet_body(z_ref, cc_ref, sc_ref, cs_ref, o_ref, ab_ref, *, rows):
    s = z_ref.shape[0]
    z = z_ref[...]
    ab_ref[0:s, :] = jnp.dot(z, cc_ref[...], preferred_element_type=F32).astype(BF16)
    ab_ref[s:2 * s, :] = jnp.dot(z, sc_ref[...], preferred_element_type=F32).astype(BF16)
    for i in range(s // rows):
        o_ref[i * rows:(i + 1) * rows, :] = jnp.dot(
            cs_ref[i * rows:(i + 1) * rows, :], ab_ref[...], preferred_element_type=F32).astype(o_ref.dtype)


def _fnet(z, cc, sc, cs):
    bsz, s, w = z.shape
    rows = min(s, 512)
    blk = lambda: pl.BlockSpec((None, s, w), lambda b: (b, 0, 0))
    return pl.pallas_call(
        functools.partial(_fnet_body, rows=rows),
        grid=(bsz,),
        in_specs=[blk(), _const_spec(cc.shape), _const_spec(sc.shape), _const_spec(cs.shape)],
        out_specs=blk(),
        out_shape=jax.ShapeDtypeStruct((bsz, s, w), BF16),
        scratch_shapes=[pltpu.VMEM((2 * s, w), BF16)],
        compiler_params=_params(("parallel",)),
        name="fnet",
    )(z, cc, sc, cs)


RWKV_BLOCK = SEQ_BLOCK
RWKV_NCH = BLOCK_NCH
RWKV_PAIRS = RWKV_HEADS // 2


def _rwkv_prep(u_ref, up_ref, un_ref, blk, nblk, prm, d, prep_refs, dec_ref, extra_refs, fwd):
    L, R, W = CHUNK, RWKV_BLOCK, RWKV_W
    (mu, w0, a0, w2, a2, g2, k_k, k_a, r_k, tri, ones_bd, gsum) = prm
    u = u_ref[...]
    row = lax.broadcasted_iota(jnp.int32, (R, 1), 0)
    prev_row = jnp.where(blk > 0, up_ref[7:8, :], 0.0)
    next_row = jnp.where(blk < nblk - 1, un_ref[0:1, :], 0.0)
    prev = jnp.where(row == 0, prev_row, pltpu.roll(u, 1, 0))
    nxt = jnp.where(row == R - 1, next_row, pltpu.roll(u, R - 1, 0))
    u = u + mu * (0.5 * (prev + nxt) - u)

    r, k, v = u[:, 0:W], u[:, W:2 * W], u[:, 2 * W:3 * W]
    sm = u[:, 3 * W:3 * W + LANES]
    lw = -jnp.exp(_log_sigmoid(w0 + _mm_3x(jnp.tanh(sm), w2)) - 0.5)
    a = _sigmoid(a0 + _mm_3x(sm, a2))
    kk = k * k_k
    kap = kk * lax.rsqrt(_mm_xw(kk * kk, gsum, 2) + 1e-12)
    kd = k * (1.0 + (a - 1.0) * k_a)
    beta = kap * a
    inc = _mm_wx(tri, lw, 3)
    tot = _mm_wx(ones_bd, lw, 3)
    exc = inc - lw
    e_exc = jnp.exp(exc)
    e_ninc = jnp.exp(-inc)
    e_tail = jnp.exp(tot - inc)
    dec = jnp.exp(tot)
    outs = (kap * e_exc, r * (jnp.exp(inc) if fwd else e_exc), kd * e_ninc, beta * e_ninc,
            kd * e_tail, beta * e_tail, v)
    for ref, val in zip(prep_refs, outs):
        for j in range(RWKV_NCH):
            ref[d, j] = val[j * L:(j + 1) * L]
    for j in range(RWKV_NCH):
        dec_ref[d, j] = dec[j * L:j * L + 8]

    if extra_refs is not None:
        bonus_ref, g_ref = extra_refs
        gd = u[:, 3 * W + LANES:3 * W + 2 * LANES]
        g_ref[...] = _mm_3x(_sigmoid(gd), g2)
        bonus_ref[...] = _mm_xw(r * kd * r_k, gsum, 2) * v


def _rwkv_units():
    return [(d, p) for d in range(2) for p in range(RWKV_PAIRS)]


def _rwkv_solve(j, prep_refs, tinv_ref, pkb_ref, mkv_ref, masks):
    L = CHUNK
    n2 = 2 * L
    kaph_ref, rh_ref, kt_ref, bt_ref, _, _, v_ref = prep_refs
    lo, hi, eye, m_strict, m_read = masks
    units = _rwkv_units()
    nmat, t_inv, pw = {}, {}, {}
    for (d, p) in units:
        sl = pl.ds(p * LANES, LANES)
        xs = jnp.concatenate([_masked_stack(kaph_ref[d, j, :, sl], lo, hi),
                              _masked_stack(rh_ref[d, j, :, sl], lo, hi)], axis=0)
        kt = kt_ref[d, j, :, sl]
        bt = bt_ref[d, j, :, sl]
        s_all = _mm_nt(xs, jnp.concatenate([kt, kt, bt, bt], axis=0))
        m_k = jnp.where(m_strict[d], s_all[0:n2, 0:n2], 0.0)
        nmat[d, p] = jnp.where(m_strict[d], -s_all[0:n2, n2:2 * n2], 0.0)
        p_k = jnp.where(m_read[d], s_all[n2:2 * n2, 0:n2], 0.0)
        p_b = jnp.where(m_read[d], -s_all[n2:2 * n2, n2:2 * n2], 0.0)
        pkb_ref[d, j, p] = jnp.concatenate([p_k, p_b], axis=1).astype(BF16)
        mkv_ref[d, j, p] = _mm(m_k, _masked_stack(v_ref[d, j, :, sl], lo, hi))
    for u in units:
        t_inv[u] = eye + nmat[u]
        pw[u] = _mm_inv(nmat[u], nmat[u])
    for _ in range(4):
        for u in units:
            both = _mm_inv(jnp.concatenate([t_inv[u], pw[u]], axis=0), pw[u])
            t_inv[u] = t_inv[u] + both[0:n2]
            pw[u] = both[n2:2 * n2]
    for (d, p) in units:
        u = (d, p)
        tinv_ref[d, j, p] = (t_inv[u] + _mm_inv(t_inv[u], pw[u])).astype(BF16)


def _rwkv_scan(j, prep_refs, dec_ref, tinv_ref, pkb_ref, mkv_ref, st_ref, y_refs, masks):
    L = CHUNK
    n2 = 2 * L
    kaph_ref, rh_ref, _, _, ktail_ref, btail_ref, v_ref = prep_refs
    lo, hi = masks[0], masks[1]
    units = _rwkv_units()
    cj = {0: j, 1: RWKV_NCH - 1 - j}
    v_ms, xh, uu = {}, {}, {}
    for (d, p) in units:
        sl = pl.ds(p * LANES, LANES)
        xs = jnp.concatenate([_masked_stack(kaph_ref[d, cj[d], :, sl], lo, hi),
                              _masked_stack(rh_ref[d, cj[d], :, sl], lo, hi)], axis=0)
        xh[d, p] = _mm_nt(xs, st_ref[d, p])
    for (d, p) in units:
        sl = pl.ds(p * LANES, LANES)
        v_ms[d, p] = _masked_stack(v_ref[d, cj[d], :, sl], lo, hi)
        uu[d, p] = _mm(tinv_ref[d, cj[d], p], xh[d, p][0:n2] + mkv_ref[d, cj[d], p])
    for (d, p) in units:
        sl = pl.ds(p * LANES, LANES)
        y = xh[d, p][n2:2 * n2] + _mm(pkb_ref[d, cj[d], p], jnp.concatenate([v_ms[d, p], uu[d, p]], axis=0))
        row0 = pl.multiple_of(cj[d] * L, L)
        y_refs[d][pl.ds(row0, L), sl] = y[0:L] + y[L:n2]
        tails = jnp.concatenate([_masked_stack(ktail_ref[d, cj[d], :, sl], lo, hi),
                                 _masked_stack(btail_ref[d, cj[d], :, sl], lo, hi)], axis=0)
        dec = dec_ref[d, cj[d], 0:1, sl]
        st_ref[d, p] = st_ref[d, p] * dec + _mm_tn(jnp.concatenate([v_ms[d, p], -uu[d, p]], axis=0), tails)


def _rwkv_body(uf_ref, ufp_ref, ufn_ref, ub_ref, ubp_ref, ubn_ref,
               mu_ref, w0f_ref, w0b_ref, a0f_ref, a0b_ref, w2f_ref, w2b_ref, a2f_ref, a2b_ref, g2_ref,
               kk_ref, ka_ref, rk_ref, trif_ref, trib_ref, ones_ref, gsum_ref,
               yf_ref, yb_ref, bonus_ref, g_ref,
               p0, p1, p2, p3, p4, p5, p6, dec_ref, tinv_ref, pkb_ref, mkv_ref, st_ref):
    i = pl.program_id(1)
    nblk = pl.num_programs(1)
    prep_refs = (p0, p1, p2, p3, p4, p5, p6)

    @pl.when(i == 0)
    def _():
        st_ref[...] = jnp.zeros_like(st_ref)

    shared = (g2_ref[...], kk_ref[...], ka_ref[...], rk_ref[...])
    tables = (ones_ref[...], gsum_ref[...])
    prm_f = (mu_ref[...], w0f_ref[...], a0f_ref[...], w2f_ref[...], a2f_ref[...]) + shared + (trif_ref[...],) + tables
    prm_b = (mu_ref[...], w0b_ref[...], a0b_ref[...], w2b_ref[...], a2b_ref[...]) + shared + (trib_ref[...],) + tables
    _rwkv_prep(uf_ref, ufp_ref, ufn_ref, i, nblk, prm_f, 0, prep_refs, dec_ref, (bonus_ref, g_ref), True)
    _rwkv_prep(ub_ref, ubp_ref, ubn_ref, nblk - 1 - i, nblk, prm_b, 1, prep_refs, dec_ref, None, False)

    lo, hi = _lane_masks()
    n2 = 2 * CHUNK
    ii = lax.broadcasted_iota(jnp.int32, (n2, n2), 0)
    jj = lax.broadcasted_iota(jnp.int32, (n2, n2), 1)
    eye = (ii == jj).astype(F32)
    m_strict = (_pair_masks(True, inclusive=False), _pair_masks(False, inclusive=False))
    m_read = (_pair_masks(True, inclusive=True), _pair_masks(False, inclusive=False))
    masks = (lo, hi, eye, m_strict, m_read)

    def solve(j, carry):
        _rwkv_solve(j, prep_refs, tinv_ref, pkb_ref, mkv_ref, masks)
        return carry

    lax.fori_loop(0, RWKV_NCH, solve, 0)

    def scan(j, carry):
        _rwkv_scan(j, prep_refs, dec_ref, tinv_ref, pkb_ref, mkv_ref, st_ref, (yf_ref, yb_ref), masks)
        return carry

    lax.fori_loop(0, RWKV_NCH, scan, 0)


def _rwkv(u, consts):
    bsz, s, cols = u.shape
    R = RWKV_BLOCK
    nb = s // R
    sub = 8
    per = R // sub
    nsub = s // sub
    fc = lambda i: i
    bc = lambda i: nb - 1 - i
    main = lambda cm: pl.BlockSpec((None, R, cols), lambda b, i: (b, cm(i), 0))
    prev = lambda cm: pl.BlockSpec((None, sub, cols), lambda b, i: (b, jnp.maximum(cm(i) * per - 1, 0), 0))
    nxt = lambda cm: pl.BlockSpec((None, sub, cols), lambda b, i: (b, jnp.minimum((cm(i) + 1) * per, nsub - 1), 0))
    ospec = lambda cm: pl.BlockSpec((None, R, RWKV_W), lambda b, i: (b, cm(i), 0))
    out = jax.ShapeDtypeStruct((bsz, s, RWKV_W), F32)
    unit = (2, RWKV_NCH, RWKV_PAIRS)
    scratch = ([pltpu.VMEM((2, RWKV_NCH, CHUNK, RWKV_W), F32)] * 7
               + [pltpu.VMEM((2, RWKV_NCH, 8, RWKV_W), F32),
                  pltpu.VMEM(unit + (LANES, LANES), BF16),
                  pltpu.VMEM(unit + (LANES, 2 * LANES), BF16),
                  pltpu.VMEM(unit + (LANES, LANES), F32),
                  pltpu.VMEM((2, RWKV_PAIRS, LANES, LANES), F32)])
    return pl.pallas_call(
        _rwkv_body,
        grid=(bsz, nb),
        in_specs=[main(fc), prev(fc), nxt(fc), main(bc), prev(bc), nxt(bc)] + [_const_spec(a.shape) for a in consts],
        out_specs=[ospec(fc), ospec(bc), ospec(fc), ospec(fc)],
        out_shape=[out] * 4,
        scratch_shapes=scratch,
        compiler_params=_params(("parallel", "arbitrary")),
        name="rwkv",
    )(u, u, u, u, u, u, *consts)


def _merge_body(x_ref, gof_ref, gob_ref, gr_ref, fn_ref, ryf_ref, ryb_ref, rbon_ref, rg_ref, gate_ref,
                gn_ref, lng_ref, lnb_ref, gavg_ref, pg_ref, pf_ref, pr_ref, wo_ref, o_ref):
    o = gof_ref[...] + gob_ref[...]
    parts = []
    for h in range(GLA_HEADS):
        oh = o[:, h * GLA_DV:(h + 1) * GLA_DV]
        parts.append(oh * lax.rsqrt(jnp.mean(oh * oh, axis=-1, keepdims=True) + NORM_EPS))
    rg = gr_ref[...]
    y_a = jnp.concatenate(parts, axis=1) * gn_ref[...] * (rg * _sigmoid(rg))
    y = ryf_ref[...] + ryb_ref[...]
    mean = _mm_xw(y, gavg_ref[...], 3)
    yc = y - mean
    var = _mm_xw(yc * yc, gavg_ref[...], 2)
    y_c = (yc * lax.rsqrt(var + RWKV_LN_EPS) * lng_ref[...] + lnb_ref[...] + rbon_ref[...]) * rg_ref[...]
    d = x_ref.shape[1]
    gate = gate_ref[...].astype(F32)
    merged = (_sigmoid(gate[:, 0:d]) * _mm(y_a, pg_ref[...])
              + _sigmoid(gate[:, d:2 * d]) * _mm(fn_ref[...], pf_ref[...])
              + _sigmoid(gate[:, 2 * d:3 * d]) * _mm(y_c, pr_ref[...]))
    o_ref[...] = x_ref[...] + _mm(merged, wo_ref[...])


def _merge(x, gla_of, gla_ob, z_gla, y_fnet, r_yf, r_yb, r_bonus, r_g, z_gate,
           gn, lng, lnb, gavg, pg, pf, pr, wo, *, tm):
    t, d = x.shape
    row = lambda c: pl.BlockSpec((tm, c), lambda i: (i, 0))
    r_col = (2 * GLA_QK + GLA_V) // GLA_V
    gr_spec = pl.BlockSpec((tm, GLA_V), lambda i: (i, r_col))
    consts = (gn, lng, lnb, gavg, pg, pf, pr, wo)
    return pl.pallas_call(
        _merge_body,
        grid=(t // tm,),
        in_specs=[row(d), row(GLA_V), row(GLA_V), gr_spec, row(FNET_W), row(RWKV_W), row(RWKV_W),
                  row(RWKV_W), row(RWKV_W), row(GATE_COLS)] + [_const_spec(a.shape) for a in consts],
        out_specs=row(d),
        out_shape=jax.ShapeDtypeStruct((t, d), F32),
        compiler_params=_params(("parallel",)),
        name="merge",
    )(x, gla_of, gla_ob, z_gla, y_fnet, r_yf, r_yb, r_bonus, r_g, z_gate, *consts)


def _dft_tables(s):
    def cos_sin(n):
        idx = jnp.arange(n, dtype=jnp.int32)
        ang = (2.0 * jnp.pi / n) * ((idx[:, None] * idx[None, :]) % n).astype(F32)
        return jnp.cos(ang) * (n ** -0.5), jnp.sin(ang) * (n ** -0.5)

    cg, sg = cos_sin(FNET_GC)
    eye = jnp.eye(FNET_GROUPS, dtype=F32)
    cc = jnp.kron(eye, cg).astype(BF16)
    sc = jnp.kron(eye, sg).astype(BF16)
    cp, sp = cos_sin(s)
    cs = jnp.concatenate([cp, -sp], axis=1).astype(BF16)
    return cc, sc, cs


def _tri(fwd, n):
    i = jnp.arange(n)
    same = (i[None, :] // CHUNK) == (i[:, None] // CHUNK)
    m = (i[None, :] <= i[:, None]) if fwd else (i[None, :] >= i[:, None])
    return (m & same).astype(BF16)


def _chunk_ones(n):
    i = jnp.arange(n)
    return ((i[None, :] // CHUNK) == (i[:, None] // CHUNK)).astype(BF16)


def _pad_rows(w, offset, total):
    return jnp.zeros((total, w.shape[1]), F32).at[offset:offset + w.shape[0]].set(w.astype(F32))


def kernel(x, ffn1_norm, ffn1_gate, ffn1_up, ffn1_down, mix_norm, w_in, gla_up_f, gla_bias_f, gla_up_b, gla_bias_b, gla_norm, rwkv_mu, rwkv_w0_f, rwkv_w2_f, rwkv_w0_b, rwkv_w2_b, rwkv_a0_f, rwkv_a2_f, rwkv_a0_b, rwkv_a2_b, rwkv_g2, rwkv_k_k, rwkv_k_a, rwkv_r_k, rwkv_ln_g, rwkv_ln_b, proj_gla, proj_fnet, proj_rwkv, w_out, ffn2_norm, ffn2_gate, ffn2_up, ffn2_down, final_norm):
    bsz, s, d = x.shape
    depth = w_in.shape[0]
    t = bsz * s
    tm = 256
    assert s % SEQ_BLOCK == 0 and t % tm == 0 and d * 3 == GATE_COLS

    cc, sc, cs = _dft_tables(s)
    trif_r, trib_r, ones_r = _tri(True, SEQ_BLOCK), _tri(False, SEQ_BLOCK), _chunk_ones(SEQ_BLOCK)
    head_of = jnp.arange(RWKV_W) // RWKV_N
    same_head = head_of[:, None] == head_of[None, :]
    gsum = same_head.astype(BF16)
    gavg = (same_head.astype(F32) * (1.0 / RWKV_N)).astype(BF16)
    row = lambda a: a.astype(F32).reshape(1, -1)
    fnorm = row(final_norm)

    x2 = x.reshape(t, d)
    for l in range(depth):
        x2 = _ffn(x2, row(ffn1_norm[l]), ffn1_gate[l].astype(BF16), ffn1_up[l].astype(BF16),
                  ffn1_down[l].astype(BF16), fnorm, final=False, tm=tm)

        w = w_in[l]
        o_f, o_r, o_g = GLA_COLS_RAW, GLA_COLS_RAW + FNET_W, GLA_COLS_RAW + FNET_W + RWKV_COLS
        w_pad = jnp.concatenate([
            w[:, :o_f], jnp.zeros((d, GLA_COLS - GLA_COLS_RAW), w.dtype),
            w[:, o_f:o_r],
            w[:, o_r:o_g], jnp.zeros((d, RWKV_COLS_PAD - RWKV_COLS), w.dtype),
            w[:, o_g:]], axis=1).astype(BF16)
        z_gla, z_fnet, z_rwkv, z_gate = _inproj(x2, row(mix_norm[l]), w_pad, tm=tm)

        gla_of, gla_ob = _gla(
            z_gla.reshape(bsz, s, GLA_COLS),
            _pad_rows(gla_up_f[l], 0, LANES), row(gla_bias_f[l]),
            _pad_rows(gla_up_b[l], GLA_RANK, LANES), row(gla_bias_b[l]), trif_r, trib_r, ones_r)

        y_fnet = _fnet(z_fnet.reshape(bsz, s, FNET_W), cc, sc, cs)

        mu = jnp.concatenate([rwkv_mu[l].astype(F32), jnp.zeros((RWKV_COLS_PAD - RWKV_COLS,), F32)]).reshape(1, -1)
        consts = (mu, row(rwkv_w0_f[l]), row(rwkv_w0_b[l]), row(rwkv_a0_f[l]), row(rwkv_a0_b[l]),
                  _pad_rows(rwkv_w2_f[l], 0, LANES), _pad_rows(rwkv_w2_b[l], 32, LANES),
                  _pad_rows(rwkv_a2_f[l], 64, LANES), _pad_rows(rwkv_a2_b[l], 96, LANES),
                  _pad_rows(rwkv_g2[l], 0, LANES),
                  row(rwkv_k_k[l]), row(rwkv_k_a[l]), row(rwkv_r_k[l]), trif_r, trib_r, ones_r, gsum)
        r_yf, r_yb, r_bonus, r_g = _rwkv(z_rwkv.reshape(bsz, s, RWKV_COLS_PAD), consts)

        flat = lambda a: a.reshape(t, a.shape[-1])
        x2 = _merge(x2, flat(gla_of), flat(gla_ob), z_gla, flat(y_fnet), flat(r_yf), flat(r_yb),
                    flat(r_bonus), flat(r_g), z_gate,
                    row(gla_norm[l]), row(rwkv_ln_g[l]), row(rwkv_ln_b[l]), gavg,
                    proj_gla[l].astype(BF16), proj_fnet[l].astype(BF16), proj_rwkv[l].astype(BF16),
                    w_out[l].astype(BF16), tm=tm)

        x2 = _ffn(x2, row(ffn2_norm[l]), ffn2_gate[l].astype(BF16), ffn2_up[l].astype(BF16),
                  ffn2_down[l].astype(BF16), fnorm, final=(l == depth - 1), tm=tm)
    return x2.reshape(bsz, s, d)
```

```python
import functools

import jax
import jax.numpy as jnp
from jax import lax
from jax.experimental import pallas as pl
from jax.experimental.pallas import tpu as pltpu

F32 = jnp.float32
BF16 = jnp.bfloat16

NORM_EPS = 1e-6
RWKV_LN_EPS = 64e-5
GLA_TAU = 16.0

GLA_HEADS, GLA_DK, GLA_DV, GLA_RANK = 4, 64, 128, 16
GLA_QK, GLA_V = GLA_HEADS * GLA_DK, GLA_HEADS * GLA_DV
FNET_GROUPS, FNET_GC = 4, 128
FNET_W = FNET_GROUPS * FNET_GC
RWKV_HEADS, RWKV_N = 8, 64
RWKV_W = RWKV_HEADS * RWKV_N
RWKV_LOWRANK = 128
RWKV_GATE_RANK = 96
RWKV_COLS = 3 * RWKV_W + RWKV_LOWRANK + RWKV_GATE_RANK

LANES = 128
CHUNK = 64

GLA_COLS_RAW = 2 * GLA_QK + 2 * GLA_V + 2 * GLA_RANK
GLA_COLS = 13 * LANES
RWKV_COLS_PAD = 14 * LANES
GATE_COLS = 3 * 1024

VMEM_LIMIT = 56 * 1024 * 1024


def _mm(a, b):
    return jnp.dot(a.astype(BF16), b.astype(BF16), preferred_element_type=F32)


def _mm_nt(a, b):
    return lax.dot_general(a.astype(BF16), b.astype(BF16), (((1,), (1,)), ((), ())),
                           preferred_element_type=F32)


def _mm_tn(a, b):
    return lax.dot_general(a.astype(BF16), b.astype(BF16), (((0,), (0,)), ((), ())),
                           preferred_element_type=F32)


def _split(x, n):
    pieces = []
    for _ in range(n - 1):
        p = x.astype(BF16)
        pieces.append(p)
        x = x - p.astype(F32)
    pieces.append(x.astype(BF16))
    return pieces


def _mm_xw(a, w, n):
    out = None
    for p in _split(a, n):
        d = jnp.dot(p, w, preferred_element_type=F32)
        out = d if out is None else out + d
    return out


def _mm_parts(w, parts):
    out = None
    for p in parts:
        d = jnp.dot(w, p, preferred_element_type=F32)
        out = d if out is None else out + d
    return out


def _mm_3x(a, b):
    a_hi, a_lo = _split(a, 2)
    b_hi, b_lo = _split(b, 2)
    return (jnp.dot(a_hi, b_hi, preferred_element_type=F32) + jnp.dot(a_lo, b_hi, preferred_element_type=F32)
            + jnp.dot(a_hi, b_lo, preferred_element_type=F32))


def _sigmoid(x):
    return 0.5 * jnp.tanh(0.5 * x) + 0.5


def _log_sigmoid(x):
    return jnp.minimum(x, 0.0) - jnp.log(1.0 + jnp.exp(-jnp.abs(x)))


def _rms(x, g):
    return x * lax.rsqrt(jnp.mean(x * x, axis=-1, keepdims=True) + NORM_EPS) * g


def _const_spec(shape):
    nd = len(shape)
    return pl.BlockSpec(shape, lambda *_: (0,) * nd, pipeline_mode=pl.Buffered(1))


def _params(sem):
    return pltpu.CompilerParams(dimension_semantics=sem, vmem_limit_bytes=VMEM_LIMIT)


def _ffn_body(x_ref, g_ref, wg_ref, wu_ref, wd_ref, fg_ref, o_ref, *, final):
    x = x_ref[...]
    h = _rms(x, g_ref[...]).astype(BF16)
    gate = jnp.dot(h, wg_ref[...], preferred_element_type=F32)
    up = jnp.dot(h, wu_ref[...], preferred_element_type=F32)
    act = (gate * _sigmoid(gate) * up).astype(BF16)
    y = x + 0.5 * jnp.dot(act, wd_ref[...], preferred_element_type=F32)
    if final:
        y = _rms(y, fg_ref[...])
    o_ref[...] = y


def _ffn(x, g, wg, wu, wd, fg, *, final, tm):
    t, d = x.shape
    f = wg.shape[1]
    row = pl.BlockSpec((tm, d), lambda i: (i, 0))
    return pl.pallas_call(
        functools.partial(_ffn_body, final=final),
        grid=(t // tm,),
        in_specs=[row, _const_spec((1, d)), _const_spec((d, f)), _const_spec((d, f)),
                  _const_spec((f, d)), _const_spec((1, d))],
        out_specs=row,
        out_shape=jax.ShapeDtypeStruct((t, d), F32),
        compiler_params=_params(("parallel",)),
        name="ffn",
    )(x, g, wg, wu, wd, fg)


def _inproj_body(x_ref, g_ref, w_ref, gla_ref, fnet_ref, rwkv_ref, gate_ref):
    h = _rms(x_ref[...], g_ref[...]).astype(BF16)
    c0, c1, c2 = GLA_COLS, GLA_COLS + FNET_W, GLA_COLS + FNET_W + RWKV_COLS_PAD
    gla_ref[...] = jnp.dot(h, w_ref[:, 0:c0], preferred_element_type=F32)
    fnet_ref[...] = jnp.dot(h, w_ref[:, c0:c1], preferred_element_type=F32).astype(BF16)
    rwkv_ref[...] = jnp.dot(h, w_ref[:, c1:c2], preferred_element_type=F32)
    gate_ref[...] = jnp.dot(h, w_ref[:, c2:c2 + GATE_COLS], preferred_element_type=F32).astype(BF16)


def _inproj(x, g, w, *, tm):
    t, d = x.shape
    n = w.shape[1]
    row = lambda c: pl.BlockSpec((tm, c), lambda i: (i, 0))
    return pl.pallas_call(
        _inproj_body,
        grid=(t // tm,),
        in_specs=[row(d), _const_spec((1, d)), _const_spec((d, n))],
        out_specs=[row(GLA_COLS), row(FNET_W), row(RWKV_COLS_PAD), row(GATE_COLS)],
        out_shape=[jax.ShapeDtypeStruct((t, GLA_COLS), F32),
                   jax.ShapeDtypeStruct((t, FNET_W), BF16),
                   jax.ShapeDtypeStruct((t, RWKV_COLS_PAD), F32),
                   jax.ShapeDtypeStruct((t, GATE_COLS), BF16)],
        compiler_params=_params(("parallel",)),
        name="inproj",
    )(x, g, w)


def _lane_masks():
    lane = lax.broadcasted_iota(jnp.int32, (CHUNK, LANES), 1)
    return lane < (LANES // 2), lane >= (LANES // 2)


def _masked_stack(x, lo, hi):
    return jnp.concatenate([jnp.where(lo, x, 0.0), jnp.where(hi, x, 0.0)], axis=0)


def _pair_masks(fwd, inclusive):
    n = 2 * CHUNK
    i = lax.broadcasted_iota(jnp.int32, (n, n), 0)
    j = lax.broadcasted_iota(jnp.int32, (n, n), 1)
    same = (i < CHUNK) == (j < CHUNK)
    strict = (j < i) if fwd else (j > i)
    if inclusive:
        return same & (strict | (i == j))
    return same & strict


SEQ_BLOCK = 256
BLOCK_NCH = SEQ_BLOCK // CHUNK
GLA_PAIRS = GLA_HEADS // 2


def _gla_prep(z_ref, up, bias, tri, sel, d, qd_ref, kd_ref, v_ref, dec_ref):
    L = CHUNK
    z = z_ref[...]
    q = z[:, 0:GLA_QK] * (GLA_DK ** -0.5)
    k = z[:, GLA_QK:2 * GLA_QK]
    v = z[:, 2 * GLA_QK:2 * GLA_QK + GLA_V]
    dn = z[:, GLA_COLS - LANES:GLA_COLS]
    la = _log_sigmoid(_mm_3x(dn, up) + bias) * (1.0 / GLA_TAU)
    la_parts = _split(la, 2)
    b = _mm_parts(tri, la_parts)
    btot = _mm_parts(sel, la_parts)
    outs = ((qd_ref, q * jnp.exp(b)), (kd_ref, k * jnp.exp(-b)), (v_ref, v))
    for j in range(BLOCK_NCH):
        for ref, val in outs:
            ref[d, j] = val[j * L:(j + 1) * L]
    dec_ref[d] = jnp.exp(btot)


def _gla_step(j, qd_ref, kd_ref, v_ref, dec_ref, st_ref, o_refs, masks):
    L = CHUNK
    lo, hi, m_read = masks
    units = [(d, p) for d in range(2) for p in range(GLA_PAIRS)]
    cj = {0: j, 1: BLOCK_NCH - 1 - j}
    q_ms, kd_ms, v_st, sc = {}, {}, {}, {}
    for (d, p) in units:
        sl = pl.ds(p * LANES, LANES)
        q_ms[d, p] = _masked_stack(qd_ref[d, cj[d], :, sl], lo, hi)
        kd_ms[d, p] = _masked_stack(kd_ref[d, cj[d], :, sl], lo, hi)
        sc[d, p] = jnp.where(m_read[d], _mm_nt(q_ms[d, p], kd_ms[d, p]), 0.0)
        v_st[d, p] = jnp.concatenate([v_ref[d, cj[d], :, pl.ds((2 * p) * GLA_DV, GLA_DV)],
                                      v_ref[d, cj[d], :, pl.ds((2 * p + 1) * GLA_DV, GLA_DV)]], axis=0)
    for (d, p) in units:
        sl = pl.ds(p * LANES, LANES)
        st = st_ref[d, p]
        o = _mm(sc[d, p], v_st[d, p]) + _mm_nt(q_ms[d, p], st)
        row0 = pl.multiple_of(cj[d] * L, L)
        o_refs[d][pl.ds(row0, L), pl.ds((2 * p) * GLA_DV, GLA_DV)] = o[0:L]
        o_refs[d][pl.ds(row0, L), pl.ds((2 * p + 1) * GLA_DV, GLA_DV)] = o[L:2 * L]
        dec = dec_ref[d, pl.ds(pl.multiple_of(cj[d] * 8, 8), 1), sl]
        st_ref[d, p] = (st + _mm_tn(v_st[d, p], kd_ms[d, p])) * dec


def _gla_body(zf_ref, zb_ref, upf_ref, bf_ref, upb_ref, bb_ref, trif_ref, trib_ref, sel_ref,
              of_ref, ob_ref, qd_ref, kd_ref, v_ref, dec_ref, st_ref):
    @pl.when(pl.program_id(1) == 0)
    def _():
        st_ref[...] = jnp.zeros_like(st_ref)

    scr = (qd_ref, kd_ref, v_ref, dec_ref)
    _gla_prep(zf_ref, upf_ref[...], bf_ref[...], trif_ref[...], sel_ref[...], 0, *scr)
    _gla_prep(zb_ref, upb_ref[...], bb_ref[...], trib_ref[...], sel_ref[...], 1, *scr)
    lo, hi = _lane_masks()
    masks = (lo, hi, (_pair_masks(True, inclusive=True), _pair_masks(False, inclusive=False)))

    def step(j, carry):
        _gla_step(j, *scr, st_ref, (of_ref, ob_ref), masks)
        return carry

    lax.fori_loop(0, BLOCK_NCH, step, 0)


def _gla(z, upf, bf, upb, bb, trif, trib, sel):
    bsz, s, _ = z.shape
    nb = s // SEQ_BLOCK
    fmap = lambda b, i: (b, i, 0)
    bmap = lambda b, i: (b, nb - 1 - i, 0)
    zspec = lambda m: pl.BlockSpec((None, SEQ_BLOCK, GLA_COLS), m)
    ospec = lambda m: pl.BlockSpec((None, SEQ_BLOCK, GLA_V), m)
    per_chunk = lambda rows, w: pltpu.VMEM((2, BLOCK_NCH, rows, w), F32)
    consts = (upf, bf, upb, bb, trif, trib, sel)
    return pl.pallas_call(
        _gla_body,
        grid=(bsz, nb),
        in_specs=[zspec(fmap), zspec(bmap)] + [_const_spec(a.shape) for a in consts],
        out_specs=[ospec(fmap), ospec(bmap)],
        out_shape=[jax.ShapeDtypeStruct((bsz, s, GLA_V), F32)] * 2,
        scratch_shapes=[per_chunk(CHUNK, GLA_QK), per_chunk(CHUNK, GLA_QK), per_chunk(CHUNK, GLA_V),
                        pltpu.VMEM((2, BLOCK_NCH * 8, GLA_QK), F32),
                        pltpu.VMEM((2, GLA_PAIRS, GLA_DV, LANES), F32)],
        compiler_params=_params(("parallel", "arbitrary")),
        name="gla",
    )(z, z, *consts)


def _fnet_body(z_ref, cc_ref, sc_ref, cs_ref, o_ref, ab_ref, *, rows):
    s = z_ref.shape[0]
    z = z_ref[...]
    ab_ref[0:s, :] = jnp.dot(z, cc_ref[...], preferred_element_type=F32).astype(BF16)
    ab_ref[s:2 * s, :] = jnp.dot(z, sc_ref[...], preferred_element_type=F32).astype(BF16)
    for i in range(s // rows):
        o_ref[i * rows:(i + 1) * rows, :] = jnp.dot(
            cs_ref[i * rows:(i + 1) * rows, :], ab_ref[...], preferred_element_type=F32).astype(o_ref.dtype)


def _fnet(z, cc, sc, cs):
    bsz, s, w = z.shape
    rows = min(s, 512)
    blk = lambda: pl.BlockSpec((None, s, w), lambda b: (b, 0, 0))
    return pl.pallas_call(
        functools.partial(_fnet_body, rows=rows),
        grid=(bsz,),
        in_specs=[blk(), _const_spec(cc.shape), _const_spec(sc.shape), _const_spec(cs.shape)],
        out_specs=blk(),
        out_shape=jax.ShapeDtypeStruct((bsz, s, w), BF16),
        scratch_shapes=[pltpu.VMEM((2 * s, w), BF16)],
        compiler_params=_params(("parallel",)),
        name="fnet",
    )(z, cc, sc, cs)


RWKV_BLOCK = SEQ_BLOCK
RWKV_NCH = BLOCK_NCH
RWKV_PAIRS = RWKV_HEADS // 2
SOLVE_GROUP = 2


def _rwkv_prep(u_ref, up_ref, un_ref, blk, nblk, prm, d, prep_refs, dec_ref, extra_refs, fwd):
    L, R, W = CHUNK, RWKV_BLOCK, RWKV_W
    (mu, w0, a0, w2, a2, g2, k_k, k_a, r_k, tri, sel, gsum) = prm
    u = u_ref[...]
    row = lax.broadcasted_iota(jnp.int32, (8, 1), 0)
    prev_row = jnp.where(blk > 0, up_ref[7:8, :], 0.0)
    next_row = jnp.where(blk < nblk - 1, un_ref[0:1, :], 0.0)
    prev = pltpu.roll(u, 1, 0)
    nxt = pltpu.roll(u, R - 1, 0)
    prev = jnp.concatenate([jnp.where(row == 0, prev_row, prev[0:8]), prev[8:]], axis=0)
    nxt = jnp.concatenate([nxt[:R - 8], jnp.where(row == 7, next_row, nxt[R - 8:])], axis=0)
    u = u * (1.0 - mu) + (prev + nxt) * (0.5 * mu)

    r, k, v = u[:, 0:W], u[:, W:2 * W], u[:, 2 * W:3 * W]
    sm = u[:, 3 * W:3 * W + LANES]
    lw = (-(2.718281828459045 ** -0.5)) * _sigmoid(w0 + _mm_3x(jnp.tanh(sm), w2))
    a = _sigmoid(a0 + _mm_3x(sm, a2))
    kk = k * k_k
    kap = kk * lax.rsqrt(_mm_xw(kk * kk, gsum, 1) + 1e-12)
    kd = k * (1.0 + (a - 1.0) * k_a)
    beta = kap * a
    lw_parts = _split(lw, 2)
    inc = _mm_parts(tri, lw_parts)
    tot = _mm_parts(sel, lw_parts)
    exc = inc - lw
    e_exc = jnp.exp(exc)
    e_ninc = jnp.exp(-inc)
    outs = (kap * e_exc, r * (jnp.exp(inc) if fwd else e_exc), kd * e_ninc, beta * e_ninc, v)
    for ref, val in zip(prep_refs, outs):
        for j in range(RWKV_NCH):
            ref[d, j] = val[j * L:(j + 1) * L]
    dec_ref[d] = jnp.exp(tot)

    if extra_refs is not None:
        bonus_ref, g_ref = extra_refs
        gd = u[:, 3 * W + LANES:3 * W + 2 * LANES]
        g_ref[...] = _mm_3x(_sigmoid(gd), g2)
        bonus_ref[...] = _mm_xw(r * kd * r_k, gsum, 1) * v


def _rwkv_units():
    return [(d, p) for d in range(2) for p in range(RWKV_PAIRS)]


def _rwkv_solve(chunks, prep_refs, tinv_ref, pkb_ref, mkv_ref, masks):
    L = CHUNK
    n2 = 2 * L
    kaph_ref, rh_ref, kt_ref, bt_ref, v_ref = prep_refs
    lo, hi, eye, m_strict, m_read = masks
    units = [(d, g, p) for g in range(len(chunks)) for d in range(2) for p in range(RWKV_PAIRS)]
    t_inv, pw = {}, {}
    for u in units:
        d, j, p = u[0], chunks[u[1]], u[2]
        sl = pl.ds(p * LANES, LANES)
        xs = jnp.concatenate([_masked_stack(kaph_ref[d, j, :, sl], lo, hi),
                              _masked_stack(rh_ref[d, j, :, sl], lo, hi)], axis=0)
        kt = kt_ref[d, j, :, sl]
        bt = bt_ref[d, j, :, sl]
        s_all = _mm_nt(xs, jnp.concatenate([kt, kt, bt, bt], axis=0))
        m_k = jnp.where(m_strict[d], s_all[0:n2, 0:n2], 0.0)
        nmat = jnp.where(m_strict[d], -s_all[0:n2, n2:2 * n2], 0.0)
        p_k = jnp.where(m_read[d], s_all[n2:2 * n2, 0:n2], 0.0)
        p_b = jnp.where(m_read[d], -s_all[n2:2 * n2, n2:2 * n2], 0.0)
        pkb_ref[d, j, p] = jnp.concatenate([p_k, p_b], axis=1).astype(BF16)
        mkv_ref[d, j, p] = _mm(m_k, _masked_stack(v_ref[d, j, :, sl], lo, hi))
        t_inv[u] = eye + nmat
        pw[u] = nmat.astype(BF16)
    for u in units:
        pw[u] = jnp.dot(pw[u], pw[u], preferred_element_type=F32).astype(BF16)
    for _ in range(4):
        for u in units:
            both = jnp.dot(jnp.concatenate([t_inv[u].astype(BF16), pw[u]], axis=0), pw[u],
                           preferred_element_type=F32)
            t_inv[u] = t_inv[u] + both[0:n2]
            pw[u] = both[n2:2 * n2].astype(BF16)
    for u in units:
        d, j, p = u[0], chunks[u[1]], u[2]
        tinv_ref[d, j, p] = (t_inv[u] + jnp.dot(t_inv[u].astype(BF16), pw[u],
                                                preferred_element_type=F32)).astype(BF16)


def _rwkv_scan(j, prep_refs, dec_ref, tinv_ref, pkb_ref, mkv_ref, st_ref, y_refs, masks):
    L = CHUNK
    n2 = 2 * L
    kaph_ref, rh_ref, kt_ref, bt_ref, v_ref = prep_refs
    lo, hi = masks[0], masks[1]
    units = _rwkv_units()
    cj = {0: j, 1: RWKV_NCH - 1 - j}
    v_ms, xh, uu = {}, {}, {}
    for (d, p) in units:
        sl = pl.ds(p * LANES, LANES)
        xs = jnp.concatenate([_masked_stack(kaph_ref[d, cj[d], :, sl], lo, hi),
                              _masked_stack(rh_ref[d, cj[d], :, sl], lo, hi)], axis=0)
        xh[d, p] = _mm_nt(xs, st_ref[d, p])
    for (d, p) in units:
        sl = pl.ds(p * LANES, LANES)
        v_ms[d, p] = _masked_stack(v_ref[d, cj[d], :, sl], lo, hi)
        uu[d, p] = _mm(tinv_ref[d, cj[d], p], xh[d, p][0:n2] + mkv_ref[d, cj[d], p])
    for (d, p) in units:
        sl = pl.ds(p * LANES, LANES)
        y = xh[d, p][n2:2 * n2] + _mm(pkb_ref[d, cj[d], p], jnp.concatenate([v_ms[d, p], uu[d, p]], axis=0))
        row0 = pl.multiple_of(cj[d] * L, L)
        y_refs[d][pl.ds(row0, L), sl] = y[0:L] + y[L:n2]
        kb = jnp.concatenate([_masked_stack(kt_ref[d, cj[d], :, sl], lo, hi),
                              _masked_stack(bt_ref[d, cj[d], :, sl], lo, hi)], axis=0)
        dec = dec_ref[d, pl.ds(pl.multiple_of(cj[d] * 8, 8), 1), sl]
        st_ref[d, p] = (st_ref[d, p] + _mm_tn(jnp.concatenate([v_ms[d, p], -uu[d, p]], axis=0), kb)) * dec


def _rwkv_body(uf_ref, ufp_ref, ufn_ref, ub_ref, ubp_ref, ubn_ref,
               mu_ref, w0f_ref, w0b_ref, a0f_ref, a0b_ref, w2f_ref, w2b_ref, a2f_ref, a2b_ref, g2_ref,
               kk_ref, ka_ref, rk_ref, trif_ref, trib_ref, sel_ref, gsum_ref,
               yf_ref, yb_ref, bonus_ref, g_ref,
               p0, p1, p2, p3, p4, dec_ref, tinv_ref, pkb_ref, mkv_ref, st_ref):
    i = pl.program_id(1)
    nblk = pl.num_programs(1)
    prep_refs = (p0, p1, p2, p3, p4)

    @pl.when(i == 0)
    def _():
        st_ref[...] = jnp.zeros_like(st_ref)

    shared = (g2_ref[...], kk_ref[...], ka_ref[...], rk_ref[...])
    tables = (sel_ref[...], gsum_ref[...])
    prm_f = (mu_ref[...], w0f_ref[...], a0f_ref[...], w2f_ref[...], a2f_ref[...]) + shared + (trif_ref[...],) + tables
    prm_b = (mu_ref[...], w0b_ref[...], a0b_ref[...], w2b_ref[...], a2b_ref[...]) + shared + (trib_ref[...],) + tables
    _rwkv_prep(uf_ref, ufp_ref, ufn_ref, i, nblk, prm_f, 0, prep_refs, dec_ref, (bonus_ref, g_ref), True)
    _rwkv_prep(ub_ref, ubp_ref, ubn_ref, nblk - 1 - i, nblk, prm_b, 1, prep_refs, dec_ref, None, False)

    lo, hi = _lane_masks()
    n2 = 2 * CHUNK
    ii = lax.broadcasted_iota(jnp.int32, (n2, n2), 0)
    jj = lax.broadcasted_iota(jnp.int32, (n2, n2), 1)
    eye = (ii == jj).astype(F32)
    m_strict = (_pair_masks(True, inclusive=False), _pair_masks(False, inclusive=False))
    m_read = (_pair_masks(True, inclusive=True), _pair_masks(False, inclusive=False))
    masks = (lo, hi, eye, m_strict, m_read)

    def solve(jj, carry):
        chunks = [jj * SOLVE_GROUP + g for g in range(SOLVE_GROUP)]
        _rwkv_solve(chunks, prep_refs, tinv_ref, pkb_ref, mkv_ref, masks)
        return carry

    lax.fori_loop(0, RWKV_NCH // SOLVE_GROUP, solve, 0)

    def scan(j, carry):
        _rwkv_scan(j, prep_refs, dec_ref, tinv_ref, pkb_ref, mkv_ref, st_ref, (yf_ref, yb_ref), masks)
        return carry

    lax.fori_loop(0, RWKV_NCH, scan, 0)


def _rwkv(u, consts):
    bsz, s, cols = u.shape
    R = RWKV_BLOCK
    nb = s // R
    sub = 8
    per = R // sub
    nsub = s // sub
    fc = lambda i: i
    bc = lambda i: nb - 1 - i
    main = lambda cm: pl.BlockSpec((None, R, cols), lambda b, i: (b, cm(i), 0))
    prev = lambda cm: pl.BlockSpec((None, sub, cols), lambda b, i: (b, jnp.maximum(cm(i) * per - 1, 0), 0))
    nxt = lambda cm: pl.BlockSpec((None, sub, cols), lambda b, i: (b, jnp.minimum((cm(i) + 1) * per, nsub - 1), 0))
    ospec = lambda cm: pl.BlockSpec((None, R, RWKV_W), lambda b, i: (b, cm(i), 0))
    out = jax.ShapeDtypeStruct((bsz, s, RWKV_W), F32)
    unit = (2, RWKV_NCH, RWKV_PAIRS)
    scratch = ([pltpu.VMEM((2, RWKV_NCH, CHUNK, RWKV_W), F32)] * 5
               + [pltpu.VMEM((2, RWKV_NCH * 8, RWKV_W), F32),
                  pltpu.VMEM(unit + (LANES, LANES), BF16),
                  pltpu.VMEM(unit + (LANES, 2 * LANES), BF16),
                  pltpu.VMEM(unit + (LANES, LANES), F32),
                  pltpu.VMEM((2, RWKV_PAIRS, LANES, LANES), F32)])
    return pl.pallas_call(
        _rwkv_body,
        grid=(bsz, nb),
        in_specs=[main(fc), prev(fc), nxt(fc), main(bc), prev(bc), nxt(bc)] + [_const_spec(a.shape) for a in consts],
        out_specs=[ospec(fc), ospec(bc), ospec(fc), ospec(fc)],
        out_shape=[out] * 4,
        scratch_shapes=scratch,
        compiler_params=_params(("parallel", "arbitrary")),
        name="rwkv",
    )(u, u, u, u, u, u, *consts)


def _merge_body(x_ref, gof_ref, gob_ref, gr_ref, fn_ref, ryf_ref, ryb_ref, rbon_ref, rg_ref, gate_ref,
                gn_ref, lng_ref, lnb_ref, gavg_ref, pg_ref, pf_ref, pr_ref, wo_ref, o_ref):
    o = gof_ref[...] + gob_ref[...]
    parts = []
    for h in range(GLA_HEADS):
        oh = o[:, h * GLA_DV:(h + 1) * GLA_DV]
        parts.append(oh * lax.rsqrt(jnp.mean(oh * oh, axis=-1, keepdims=True) + NORM_EPS))
    rg = gr_ref[...]
    y_a = jnp.concatenate(parts, axis=1) * gn_ref[...] * (rg * _sigmoid(rg))
    y = ryf_ref[...] + ryb_ref[...]
    mean = _mm_xw(y, gavg_ref[...], 2)
    yc = y - mean
    var = _mm_xw(yc * yc, gavg_ref[...], 1)
    y_c = (yc * lax.rsqrt(var + RWKV_LN_EPS) * lng_ref[...] + lnb_ref[...] + rbon_ref[...]) * rg_ref[...]
    d = x_ref.shape[1]
    gate = gate_ref[...].astype(F32)
    merged = (_sigmoid(gate[:, 0:d]) * _mm(y_a, pg_ref[...])
              + _sigmoid(gate[:, d:2 * d]) * _mm(fn_ref[...], pf_ref[...])
              + _sigmoid(gate[:, 2 * d:3 * d]) * _mm(y_c, pr_ref[...]))
    o_ref[...] = x_ref[...] + _mm(merged, wo_ref[...])


def _merge(x, gla_of, gla_ob, z_gla, y_fnet, r_yf, r_yb, r_bonus, r_g, z_gate,
           gn, lng, lnb, gavg, pg, pf, pr, wo, *, tm):
    t, d = x.shape
    row = lambda c: pl.BlockSpec((tm, c), lambda i: (i, 0))
    r_col = (2 * GLA_QK + GLA_V) // GLA_V
    gr_spec = pl.BlockSpec((tm, GLA_V), lambda i: (i, r_col))
    consts = (gn, lng, lnb, gavg, pg, pf, pr, wo)
    return pl.pallas_call(
        _merge_body,
        grid=(t // tm,),
        in_specs=[row(d), row(GLA_V), row(GLA_V), gr_spec, row(FNET_W), row(RWKV_W), row(RWKV_W),
                  row(RWKV_W), row(RWKV_W), row(GATE_COLS)] + [_const_spec(a.shape) for a in consts],
        out_specs=row(d),
        out_shape=jax.ShapeDtypeStruct((t, d), F32),
        compiler_params=_params(("parallel",)),
        name="merge",
    )(x, gla_of, gla_ob, z_gla, y_fnet, r_yf, r_yb, r_bonus, r_g, z_gate, *consts)


def _dft_tables(s):
    def cos_sin(n):
        idx = jnp.arange(n, dtype=jnp.int32)
        ang = (2.0 * jnp.pi / n) * ((idx[:, None] * idx[None, :]) % n).astype(F32)
        return jnp.cos(ang) * (n ** -0.5), jnp.sin(ang) * (n ** -0.5)

    cg, sg = cos_sin(FNET_GC)
    eye = jnp.eye(FNET_GROUPS, dtype=F32)
    cc = jnp.kron(eye, cg).astype(BF16)
    sc = jnp.kron(eye, sg).astype(BF16)
    cp, sp = cos_sin(s)
    cs = jnp.concatenate([cp, -sp], axis=1).astype(BF16)
    return cc, sc, cs


def _tri(fwd, n):
    i = jnp.arange(n)
    same = (i[None, :] // CHUNK) == (i[:, None] // CHUNK)
    m = (i[None, :] <= i[:, None]) if fwd else (i[None, :] >= i[:, None])
    return (m & same).astype(BF16)


def _chunk_sel(n):
    return (jnp.arange(n // CHUNK * 8)[:, None] // 8 == jnp.arange(n)[None, :] // CHUNK).astype(BF16)


def _pad_rows(w, offset, total):
    return jnp.zeros((total, w.shape[1]), F32).at[offset:offset + w.shape[0]].set(w.astype(F32))


def kernel(x, ffn1_norm, ffn1_gate, ffn1_up, ffn1_down, mix_norm, w_in, gla_up_f, gla_bias_f, gla_up_b, gla_bias_b, gla_norm, rwkv_mu, rwkv_w0_f, rwkv_w2_f, rwkv_w0_b, rwkv_w2_b, rwkv_a0_f, rwkv_a2_f, rwkv_a0_b, rwkv_a2_b, rwkv_g2, rwkv_k_k, rwkv_k_a, rwkv_r_k, rwkv_ln_g, rwkv_ln_b, proj_gla, proj_fnet, proj_rwkv, w_out, ffn2_norm, ffn2_gate, ffn2_up, ffn2_down, final_norm):
    bsz, s, d = x.shape
    depth = w_in.shape[0]
    t = bsz * s
    tm = 256
    assert s % SEQ_BLOCK == 0 and t % tm == 0 and d * 3 == GATE_COLS

    cc, sc, cs = _dft_tables(s)
    trif_r, trib_r, sel_r = _tri(True, SEQ_BLOCK), _tri(False, SEQ_BLOCK), _chunk_sel(SEQ_BLOCK)
    head_of = jnp.arange(RWKV_W) // RWKV_N
    same_head = head_of[:, None] == head_of[None, :]
    gsum = same_head.astype(BF16)
    gavg = (same_head.astype(F32) * (1.0 / RWKV_N)).astype(BF16)
    row = lambda a: a.astype(F32).reshape(1, -1)
    fnorm = row(final_norm)

    x2 = x.reshape(t, d)
    for l in range(depth):
        x2 = _ffn(x2, row(ffn1_norm[l]), ffn1_gate[l].astype(BF16), ffn1_up[l].astype(BF16),
                  ffn1_down[l].astype(BF16), fnorm, final=False, tm=tm)

        w = w_in[l]
        o_f, o_r, o_g = GLA_COLS_RAW, GLA_COLS_RAW + FNET_W, GLA_COLS_RAW + FNET_W + RWKV_COLS
        w_pad = jnp.concatenate([
            w[:, :o_f], jnp.zeros((d, GLA_COLS - GLA_COLS_RAW), w.dtype),
            w[:, o_f:o_r],
            w[:, o_r:o_g], jnp.zeros((d, RWKV_COLS_PAD - RWKV_COLS), w.dtype),
            w[:, o_g:]], axis=1).astype(BF16)
        z_gla, z_fnet, z_rwkv, z_gate = _inproj(x2, row(mix_norm[l]), w_pad, tm=tm)

        gla_of, gla_ob = _gla(
            z_gla.reshape(bsz, s, GLA_COLS),
            _pad_rows(gla_up_f[l], 0, LANES), row(gla_bias_f[l]),
            _pad_rows(gla_up_b[l], GLA_RANK, LANES), row(gla_bias_b[l]), trif_r, trib_r, sel_r)

        y_fnet = _fnet(z_fnet.reshape(bsz, s, FNET_W), cc, sc, cs)

        mu = jnp.concatenate([rwkv_mu[l].astype(F32), jnp.zeros((RWKV_COLS_PAD - RWKV_COLS,), F32)]).reshape(1, -1)
        consts = (mu, row(rwkv_w0_f[l]), row(rwkv_w0_b[l]), row(rwkv_a0_f[l]), row(rwkv_a0_b[l]),
                  _pad_rows(rwkv_w2_f[l], 0, LANES), _pad_rows(rwkv_w2_b[l], 32, LANES),
                  _pad_rows(rwkv_a2_f[l], 64, LANES), _pad_rows(rwkv_a2_b[l], 96, LANES),
                  _pad_rows(rwkv_g2[l], 0, LANES),
                  row(rwkv_k_k[l]), row(rwkv_k_a[l]), row(rwkv_r_k[l]), trif_r, trib_r, sel_r, gsum)
        r_yf, r_yb, r_bonus, r_g = _rwkv(z_rwkv.reshape(bsz, s, RWKV_COLS_PAD), consts)

        flat = lambda a: a.reshape(t, a.shape[-1])
        x2 = _merge(x2, flat(gla_of), flat(gla_ob), z_gla, flat(y_fnet), flat(r_yf), flat(r_yb),
                    flat(r_bonus), flat(r_g), z_gate,
                    row(gla_norm[l]), row(rwkv_ln_g[l]), row(rwkv_ln_b[l]), gavg,
                    proj_gla[l].astype(BF16), proj_fnet[l].astype(BF16), proj_rwkv[l].astype(BF16),
                    w_out[l].astype(BF16), tm=tm)

        x2 = _ffn(x2, row(ffn2_norm[l]), ffn2_gate[l].astype(BF16), ffn2_up[l].astype(BF16),
                  ffn2_down[l].astype(BF16), fnorm, final=(l == depth - 1), tm=tm)
    return x2.reshape(bsz, s, d)
```

```python
import functools

import jax
import jax.numpy as jnp
from jax import lax
from jax.experimental import pallas as pl
from jax.experimental.pallas import tpu as pltpu

F32 = jnp.float32
BF16 = jnp.bfloat16

NORM_EPS = 1e-6
RWKV_LN_EPS = 64e-5
GLA_TAU = 16.0

GLA_HEADS, GLA_DK, GLA_DV, GLA_RANK = 4, 64, 128, 16
GLA_QK, GLA_V = GLA_HEADS * GLA_DK, GLA_HEADS * GLA_DV
FNET_GROUPS, FNET_GC = 4, 128
FNET_W = FNET_GROUPS * FNET_GC
RWKV_HEADS, RWKV_N = 8, 64
RWKV_W = RWKV_HEADS * RWKV_N
RWKV_LOWRANK = 128
RWKV_GATE_RANK = 96
RWKV_COLS = 3 * RWKV_W + RWKV_LOWRANK + RWKV_GATE_RANK

LANES = 128
CHUNK = 64

GLA_COLS_RAW = 2 * GLA_QK + 2 * GLA_V + 2 * GLA_RANK
GLA_COLS = 13 * LANES
RWKV_COLS_PAD = 14 * LANES
GATE_COLS = 3 * 1024

VMEM_LIMIT = 56 * 1024 * 1024


def _mm(a, b):
    return jnp.dot(a.astype(BF16), b.astype(BF16), preferred_element_type=F32)


def _mm_nt(a, b):
    return lax.dot_general(a.astype(BF16), b.astype(BF16), (((1,), (1,)), ((), ())),
                           preferred_element_type=F32)


def _mm_tn(a, b):
    return lax.dot_general(a.astype(BF16), b.astype(BF16), (((0,), (0,)), ((), ())),
                           preferred_element_type=F32)


def _split(x, n):
    pieces = []
    for _ in range(n - 1):
        p = x.astype(BF16)
        pieces.append(p)
        x = x - p.astype(F32)
    pieces.append(x.astype(BF16))
    return pieces


def _mm_xw(a, w, n):
    out = None
    for p in _split(a, n):
        d = jnp.dot(p, w, preferred_element_type=F32)
        out = d if out is None else out + d
    return out


def _mm_parts(w, parts):
    out = None
    for p in parts:
        d = jnp.dot(w, p, preferred_element_type=F32)
        out = d if out is None else out + d
    return out


def _mm_3x(a, b):
    a_hi, a_lo = _split(a, 2)
    b_hi, b_lo = _split(b, 2)
    return (jnp.dot(a_hi, b_hi, preferred_element_type=F32) + jnp.dot(a_lo, b_hi, preferred_element_type=F32)
            + jnp.dot(a_hi, b_lo, preferred_element_type=F32))


def _sigmoid(x):
    return 0.5 * jnp.tanh(0.5 * x) + 0.5


def _log_sigmoid(x):
    return jnp.minimum(x, 0.0) - jnp.log(1.0 + jnp.exp(-jnp.abs(x)))


def _rms(x, g):
    return x * lax.rsqrt(jnp.mean(x * x, axis=-1, keepdims=True) + NORM_EPS) * g


def _const_spec(shape):
    nd = len(shape)
    return pl.BlockSpec(shape, lambda *_: (0,) * nd, pipeline_mode=pl.Buffered(1))


def _params(sem):
    return pltpu.CompilerParams(dimension_semantics=sem, vmem_limit_bytes=VMEM_LIMIT)


def _ffn_body(x_ref, g_ref, wg_ref, wu_ref, wd_ref, fg_ref, o_ref, *, final):
    x = x_ref[...]
    h = _rms(x, g_ref[...]).astype(BF16)
    gate = jnp.dot(h, wg_ref[...], preferred_element_type=F32)
    up = jnp.dot(h, wu_ref[...], preferred_element_type=F32)
    act = (gate * _sigmoid(gate) * up).astype(BF16)
    y = x + 0.5 * jnp.dot(act, wd_ref[...], preferred_element_type=F32)
    if final:
        y = _rms(y, fg_ref[...])
    o_ref[...] = y


def _ffn(x, g, wg, wu, wd, fg, *, final, tm):
    t, d = x.shape
    f = wg.shape[1]
    row = pl.BlockSpec((tm, d), lambda i: (i, 0))
    return pl.pallas_call(
        functools.partial(_ffn_body, final=final),
        grid=(t // tm,),
        in_specs=[row, _const_spec((1, d)), _const_spec((d, f)), _const_spec((d, f)),
                  _const_spec((f, d)), _const_spec((1, d))],
        out_specs=row,
        out_shape=jax.ShapeDtypeStruct((t, d), F32),
        compiler_params=_params(("parallel",)),
        name="ffn",
    )(x, g, wg, wu, wd, fg)


SEQ_BLOCK = 256
BLOCK_NCH = SEQ_BLOCK // CHUNK
E_NEG_HALF = 2.718281828459045 ** -0.5


def _rwkv_features(u, prev_row, next_row, prm):
    R, W = SEQ_BLOCK, RWKV_W
    (mu, w0f, w0b, a0f, a0b, w2, a2, g2, k_k, k_a, r_k, trif, trib, sel, gsum) = prm
    row = lax.broadcasted_iota(jnp.int32, (8, 1), 0)
    prev = pltpu.roll(u, 1, 0)
    nxt = pltpu.roll(u, R - 1, 0)
    prev = jnp.concatenate([jnp.where(row == 0, prev_row, prev[0:8]), prev[8:]], axis=0)
    nxt = jnp.concatenate([nxt[:R - 8], jnp.where(row == 7, next_row, nxt[R - 8:])], axis=0)
    u = u * (1.0 - mu) + (prev + nxt) * (0.5 * mu)

    r, k, v = u[:, 0:W], u[:, W:2 * W], u[:, 2 * W:3 * W]
    sm = u[:, 3 * W:3 * W + LANES]
    gd = u[:, 3 * W + LANES:3 * W + 2 * LANES]
    zw = _mm(jnp.tanh(sm), w2)
    za = _mm(sm, a2)
    kk = k * k_k
    kap = kk * lax.rsqrt(_mm_xw(kk * kk, gsum, 1) + 1e-12)
    dirs = []
    for fwd, w0, a0, tri, sl in ((True, w0f, a0f, trif, slice(0, W)), (False, w0b, a0b, trib, slice(W, 2 * W))):
        lw = (-E_NEG_HALF) * _sigmoid(w0 + zw[:, sl])
        a = _sigmoid(a0 + za[:, sl])
        kd = k * (1.0 + (a - 1.0) * k_a)
        beta = kap * a
        lw_parts = _split(lw, 2)
        inc = _mm_parts(tri, lw_parts)
        tot = _mm_parts(sel, lw_parts)
        e_exc = jnp.exp(inc - lw)
        e_ninc = jnp.exp(-inc)
        dirs.append((kap * e_exc, r * (jnp.exp(inc) if fwd else e_exc), kd * e_ninc, beta * e_ninc, jnp.exp(tot)))
        if fwd:
            bonus = _mm_xw(r * kd * r_k, gsum, 1) * v
    g = _mm(_sigmoid(gd), g2)
    return dirs, v, bonus, g


def _inproj_body(x_ref, xp_ref, xn_ref, g_ref, w_ref,
                 mu_ref, w0f_ref, w0b_ref, a0f_ref, a0b_ref, w2_ref, a2_ref, g2_ref, kk_ref, ka_ref, rk_ref,
                 trif_ref, trib_ref, sel_ref, gsum_ref,
                 gla_ref, fnet_ref, gate_ref,
                 kaf_ref, rhf_ref, ktf_ref, btf_ref, decf_ref, kab_ref, rhb_ref, ktb_ref, btb_ref, decb_ref,
                 v_ref, bonus_ref, rg_ref, *, blocks_per_seq):
    i = pl.program_id(0)
    c0, c1, c2 = GLA_COLS, GLA_COLS + FNET_W, GLA_COLS + FNET_W + RWKV_COLS_PAD
    g = g_ref[...]
    h = _rms(x_ref[...], g).astype(BF16)
    halo = _rms(jnp.concatenate([xp_ref[...], xn_ref[...]], axis=0), g).astype(BF16)
    z_halo = jnp.dot(halo, w_ref[:, c1:c2], preferred_element_type=F32)
    pos = i % blocks_per_seq
    prev_row = jnp.where(pos > 0, z_halo[7:8, :], 0.0)
    next_row = jnp.where(pos < blocks_per_seq - 1, z_halo[8:9, :], 0.0)
    u = jnp.dot(h, w_ref[:, c1:c2], preferred_element_type=F32)
    gla_ref[...] = jnp.dot(h, w_ref[:, 0:c0], preferred_element_type=F32)
    fnet_ref[...] = jnp.dot(h, w_ref[:, c0:c1], preferred_element_type=F32).astype(BF16)
    gate_ref[...] = jnp.dot(h, w_ref[:, c2:c2 + GATE_COLS], preferred_element_type=F32).astype(BF16)
    prm = tuple(ref[...] for ref in (mu_ref, w0f_ref, w0b_ref, a0f_ref, a0b_ref, w2_ref, a2_ref, g2_ref,
                                     kk_ref, ka_ref, rk_ref, trif_ref, trib_ref, sel_ref, gsum_ref))
    dirs, v, bonus, rg = _rwkv_features(u, prev_row, next_row, prm)
    for refs, vals in (((kaf_ref, rhf_ref, ktf_ref, btf_ref, decf_ref), dirs[0]),
                       ((kab_ref, rhb_ref, ktb_ref, btb_ref, decb_ref), dirs[1])):
        for ref, val in zip(refs, vals):
            ref[...] = val.astype(ref.dtype)
    v_ref[...] = v.astype(BF16)
    bonus_ref[...] = bonus
    rg_ref[...] = rg


def _inproj(x, g, w, rwkv_consts, *, seq):
    t, d = x.shape
    n = w.shape[1]
    tm = SEQ_BLOCK
    sub = 8
    per = tm // sub
    row = lambda c: pl.BlockSpec((tm, c), lambda i: (i, 0))
    prev = pl.BlockSpec((sub, d), lambda i: (jnp.maximum(i * per - 1, 0), 0))
    nxt = pl.BlockSpec((sub, d), lambda i: (jnp.minimum((i + 1) * per, t // sub - 1), 0))
    dec = pl.BlockSpec((BLOCK_NCH * 8, RWKV_W), lambda i: (i, 0))
    op = jax.ShapeDtypeStruct((t, RWKV_W), BF16)
    dec_shape = jax.ShapeDtypeStruct((t // CHUNK * 8, RWKV_W), F32)
    wide = jax.ShapeDtypeStruct((t, RWKV_W), F32)
    return pl.pallas_call(
        functools.partial(_inproj_body, blocks_per_seq=seq // tm),
        grid=(t // tm,),
        in_specs=[row(d), prev, nxt, _const_spec((1, d)), _const_spec((d, n))]
                 + [_const_spec(a.shape) for a in rwkv_consts],
        out_specs=[row(GLA_COLS), row(FNET_W), row(GATE_COLS)]
                  + [row(RWKV_W)] * 4 + [dec] + [row(RWKV_W)] * 4 + [dec] + [row(RWKV_W)] * 3,
        out_shape=[jax.ShapeDtypeStruct((t, GLA_COLS), F32),
                   jax.ShapeDtypeStruct((t, FNET_W), BF16),
                   jax.ShapeDtypeStruct((t, GATE_COLS), BF16)]
                  + [op] * 4 + [dec_shape] + [op] * 4 + [dec_shape] + [op, wide, wide],
        compiler_params=_params(("parallel",)),
        name="inproj",
    )(x, x, x, g, w, *rwkv_consts)


def _lane_masks():
    lane = lax.broadcasted_iota(jnp.int32, (CHUNK, LANES), 1)
    return lane < (LANES // 2), lane >= (LANES // 2)


def _masked_stack(x, lo, hi):
    return jnp.concatenate([jnp.where(lo, x, 0.0), jnp.where(hi, x, 0.0)], axis=0)


def _pair_masks(fwd, inclusive):
    n = 2 * CHUNK
    i = lax.broadcasted_iota(jnp.int32, (n, n), 0)
    j = lax.broadcasted_iota(jnp.int32, (n, n), 1)
    same = (i < CHUNK) == (j < CHUNK)
    strict = (j < i) if fwd else (j > i)
    if inclusive:
        return same & (strict | (i == j))
    return same & strict


GLA_PAIRS = GLA_HEADS // 2


def _gla_prep(z_ref, up, bias, tri, sel, d, qd_ref, kd_ref, v_ref, dec_ref):
    L = CHUNK
    z = z_ref[...]
    q = z[:, 0:GLA_QK] * (GLA_DK ** -0.5)
    k = z[:, GLA_QK:2 * GLA_QK]
    v = z[:, 2 * GLA_QK:2 * GLA_QK + GLA_V]
    dn = z[:, GLA_COLS - LANES:GLA_COLS]
    la = _log_sigmoid(_mm(dn, up) + bias) * (1.0 / GLA_TAU)
    la_parts = _split(la, 2)
    b = _mm_parts(tri, la_parts)
    btot = _mm_parts(sel, la_parts)
    outs = ((qd_ref, q * jnp.exp(b)), (kd_ref, k * jnp.exp(-b)), (v_ref, v))
    for j in range(BLOCK_NCH):
        for ref, val in outs:
            ref[d, j] = val[j * L:(j + 1) * L]
    dec_ref[d] = jnp.exp(btot)


def _gla_step(j, qd_ref, kd_ref, v_ref, dec_ref, st_ref, o_refs, masks):
    L = CHUNK
    lo, hi, m_read = masks
    units = [(d, p) for d in range(2) for p in range(GLA_PAIRS)]
    cj = {0: j, 1: BLOCK_NCH - 1 - j}
    q_ms, kd_ms, v_st, sc = {}, {}, {}, {}
    for (d, p) in units:
        sl = pl.ds(p * LANES, LANES)
        q_ms[d, p] = _masked_stack(qd_ref[d, cj[d], :, sl], lo, hi)
        kd_ms[d, p] = _masked_stack(kd_ref[d, cj[d], :, sl], lo, hi)
        sc[d, p] = jnp.where(m_read[d], _mm_nt(q_ms[d, p], kd_ms[d, p]), 0.0)
        v_st[d, p] = jnp.concatenate([v_ref[d, cj[d], :, pl.ds((2 * p) * GLA_DV, GLA_DV)],
                                      v_ref[d, cj[d], :, pl.ds((2 * p + 1) * GLA_DV, GLA_DV)]], axis=0)
    for (d, p) in units:
        sl = pl.ds(p * LANES, LANES)
        st = st_ref[d, p]
        o = _mm(sc[d, p], v_st[d, p]) + _mm_nt(q_ms[d, p], st)
        row0 = pl.multiple_of(cj[d] * L, L)
        o_refs[d][pl.ds(row0, L), pl.ds((2 * p) * GLA_DV, GLA_DV)] = o[0:L]
        o_refs[d][pl.ds(row0, L), pl.ds((2 * p + 1) * GLA_DV, GLA_DV)] = o[L:2 * L]
        dec = dec_ref[d, pl.ds(pl.multiple_of(cj[d] * 8, 8), 1), sl]
        st_ref[d, p] = (st + _mm_tn(v_st[d, p], kd_ms[d, p])) * dec


def _gla_body(zf_ref, zb_ref, upf_ref, bf_ref, upb_ref, bb_ref, trif_ref, trib_ref, sel_ref,
              of_ref, ob_ref, qd_ref, kd_ref, v_ref, dec_ref, st_ref):
    @pl.when(pl.program_id(1) == 0)
    def _():
        st_ref[...] = jnp.zeros_like(st_ref)

    scr = (qd_ref, kd_ref, v_ref, dec_ref)
    _gla_prep(zf_ref, upf_ref[...], bf_ref[...], trif_ref[...], sel_ref[...], 0, *scr)
    _gla_prep(zb_ref, upb_ref[...], bb_ref[...], trib_ref[...], sel_ref[...], 1, *scr)
    lo, hi = _lane_masks()
    masks = (lo, hi, (_pair_masks(True, inclusive=True), _pair_masks(False, inclusive=False)))

    def step(j, carry):
        _gla_step(j, *scr, st_ref, (of_ref, ob_ref), masks)
        return carry

    lax.fori_loop(0, BLOCK_NCH, step, 0)


def _gla(z, upf, bf, upb, bb, trif, trib, sel):
    bsz, s, _ = z.shape
    nb = s // SEQ_BLOCK
    fmap = lambda b, i: (b, i, 0)
    bmap = lambda b, i: (b, nb - 1 - i, 0)
    zspec = lambda m: pl.BlockSpec((None, SEQ_BLOCK, GLA_COLS), m)
    ospec = lambda m: pl.BlockSpec((None, SEQ_BLOCK, GLA_V), m)
    per_chunk = lambda rows, w: pltpu.VMEM((2, BLOCK_NCH, rows, w), F32)
    consts = (upf, bf, upb, bb, trif, trib, sel)
    return pl.pallas_call(
        _gla_body,
        grid=(bsz, nb),
        in_specs=[zspec(fmap), zspec(bmap)] + [_const_spec(a.shape) for a in consts],
        out_specs=[ospec(fmap), ospec(bmap)],
        out_shape=[jax.ShapeDtypeStruct((bsz, s, GLA_V), F32)] * 2,
        scratch_shapes=[per_chunk(CHUNK, GLA_QK), per_chunk(CHUNK, GLA_QK), per_chunk(CHUNK, GLA_V),
                        pltpu.VMEM((2, BLOCK_NCH * 8, GLA_QK), F32),
                        pltpu.VMEM((2, GLA_PAIRS, GLA_DV, LANES), F32)],
        compiler_params=_params(("parallel", "arbitrary")),
        name="gla",
    )(z, z, *consts)


def _fnet_body(z_ref, cc_ref, sc_ref, cs_ref, o_ref, ab_ref, *, rows):
    s = z_ref.shape[0]
    z = z_ref[...]
    ab_ref[0:s, :] = jnp.dot(z, cc_ref[...], preferred_element_type=F32).astype(BF16)
    ab_ref[s:2 * s, :] = jnp.dot(z, sc_ref[...], preferred_element_type=F32).astype(BF16)
    for i in range(s // rows):
        o_ref[i * rows:(i + 1) * rows, :] = jnp.dot(
            cs_ref[i * rows:(i + 1) * rows, :], ab_ref[...], preferred_element_type=F32).astype(o_ref.dtype)


def _fnet(z, cc, sc, cs):
    bsz, s, w = z.shape
    rows = min(s, 512)
    blk = lambda: pl.BlockSpec((None, s, w), lambda b: (b, 0, 0))
    return pl.pallas_call(
        functools.partial(_fnet_body, rows=rows),
        grid=(bsz,),
        in_specs=[blk(), _const_spec(cc.shape), _const_spec(sc.shape), _const_spec(cs.shape)],
        out_specs=blk(),
        out_shape=jax.ShapeDtypeStruct((bsz, s, w), BF16),
        scratch_shapes=[pltpu.VMEM((2 * s, w), BF16)],
        compiler_params=_params(("parallel",)),
        name="fnet",
    )(z, cc, sc, cs)


RWKV_BLOCK = SEQ_BLOCK
RWKV_NCH = BLOCK_NCH
RWKV_PAIRS = RWKV_HEADS // 2
SOLVE_GROUP = 2


def _chunk(ref, j, sl):
    return ref[pl.ds(pl.multiple_of(j * CHUNK, CHUNK), CHUNK), sl]


def _rwkv_units():
    return [(d, p) for d in range(2) for p in range(RWKV_PAIRS)]


def _rwkv_solve(chunks, ops, tinv_ref, pkb_ref, mkv_ref, masks):
    L = CHUNK
    n2 = 2 * L
    lo, hi, eye, m_strict, m_read = masks
    units = [(d, g, p) for g in range(len(chunks)) for d in range(2) for p in range(RWKV_PAIRS)]
    t_inv, pw = {}, {}
    for u in units:
        d, j, p = u[0], chunks[u[1]], u[2]
        sl = pl.ds(p * LANES, LANES)
        kaph_ref, rh_ref, kt_ref, bt_ref, v_ref, _ = ops[d]
        xs = jnp.concatenate([_masked_stack(_chunk(kaph_ref, j, sl), lo, hi),
                              _masked_stack(_chunk(rh_ref, j, sl), lo, hi)], axis=0)
        kt = _chunk(kt_ref, j, sl)
        bt = _chunk(bt_ref, j, sl)
        s_all = _mm_nt(xs, jnp.concatenate([kt, kt, bt, bt], axis=0))
        m_k = jnp.where(m_strict[d], s_all[0:n2, 0:n2], 0.0)
        nmat = jnp.where(m_strict[d], -s_all[0:n2, n2:2 * n2], 0.0)
        p_k = jnp.where(m_read[d], s_all[n2:2 * n2, 0:n2], 0.0)
        p_b = jnp.where(m_read[d], -s_all[n2:2 * n2, n2:2 * n2], 0.0)
        pkb_ref[d, j, p] = jnp.concatenate([p_k, p_b], axis=1).astype(BF16)
        mkv_ref[d, j, p] = _mm(m_k, _masked_stack(_chunk(v_ref, j, sl), lo, hi))
        t_inv[u] = eye + nmat
        pw[u] = nmat.astype(BF16)
    for u in units:
        pw[u] = jnp.dot(pw[u], pw[u], preferred_element_type=F32).astype(BF16)
    for _ in range(4):
        for u in units:
            both = jnp.dot(jnp.concatenate([t_inv[u].astype(BF16), pw[u]], axis=0), pw[u],
                           preferred_element_type=F32)
            t_inv[u] = t_inv[u] + both[0:n2]
            pw[u] = both[n2:2 * n2].astype(BF16)
    for u in units:
        d, j, p = u[0], chunks[u[1]], u[2]
        tinv_ref[d, j, p] = (t_inv[u] + jnp.dot(t_inv[u].astype(BF16), pw[u],
                                                preferred_element_type=F32)).astype(BF16)


def _rwkv_scan(j, ops, tinv_ref, pkb_ref, mkv_ref, st_ref, y_refs, masks):
    L = CHUNK
    n2 = 2 * L
    lo, hi = masks[0], masks[1]
    units = _rwkv_units()
    cj = {0: j, 1: RWKV_NCH - 1 - j}
    v_ms, xh, uu = {}, {}, {}
    for (d, p) in units:
        sl = pl.ds(p * LANES, LANES)
        xs = jnp.concatenate([_masked_stack(_chunk(ops[d][0], cj[d], sl), lo, hi),
                              _masked_stack(_chunk(ops[d][1], cj[d], sl), lo, hi)], axis=0)
        xh[d, p] = _mm_nt(xs, st_ref[d, p])
    for (d, p) in units:
        sl = pl.ds(p * LANES, LANES)
        v_ms[d, p] = _masked_stack(_chunk(ops[d][4], cj[d], sl), lo, hi)
        uu[d, p] = _mm(tinv_ref[d, cj[d], p], xh[d, p][0:n2] + mkv_ref[d, cj[d], p])
    for (d, p) in units:
        sl = pl.ds(p * LANES, LANES)
        y = xh[d, p][n2:2 * n2] + _mm(pkb_ref[d, cj[d], p], jnp.concatenate([v_ms[d, p], uu[d, p]], axis=0))
        row0 = pl.multiple_of(cj[d] * L, L)
        y_refs[d][pl.ds(row0, L), sl] = y[0:L] + y[L:n2]
        kb = jnp.concatenate([_masked_stack(_chunk(ops[d][2], cj[d], sl), lo, hi),
                              _masked_stack(_chunk(ops[d][3], cj[d], sl), lo, hi)], axis=0)
        dec = ops[d][5][pl.ds(pl.multiple_of(cj[d] * 8, 8), 1), sl]
        st_ref[d, p] = (st_ref[d, p] + _mm_tn(jnp.concatenate([v_ms[d, p], -uu[d, p]], axis=0), kb)) * dec


def _rwkv_body(kaf_ref, rhf_ref, ktf_ref, btf_ref, vf_ref, decf_ref,
               kab_ref, rhb_ref, ktb_ref, btb_ref, vb_ref, decb_ref,
               yf_ref, yb_ref, tinv_ref, pkb_ref, mkv_ref, st_ref):
    @pl.when(pl.program_id(1) == 0)
    def _():
        st_ref[...] = jnp.zeros_like(st_ref)

    ops = ((kaf_ref, rhf_ref, ktf_ref, btf_ref, vf_ref, decf_ref),
           (kab_ref, rhb_ref, ktb_ref, btb_ref, vb_ref, decb_ref))
    lo, hi = _lane_masks()
    n2 = 2 * CHUNK
    ii = lax.broadcasted_iota(jnp.int32, (n2, n2), 0)
    jj = lax.broadcasted_iota(jnp.int32, (n2, n2), 1)
    eye = (ii == jj).astype(F32)
    m_strict = (_pair_masks(True, inclusive=False), _pair_masks(False, inclusive=False))
    m_read = (_pair_masks(True, inclusive=True), _pair_masks(False, inclusive=False))
    masks = (lo, hi, eye, m_strict, m_read)

    def solve(jj, carry):
        chunks = [jj * SOLVE_GROUP + g for g in range(SOLVE_GROUP)]
        _rwkv_solve(chunks, ops, tinv_ref, pkb_ref, mkv_ref, masks)
        return carry

    lax.fori_loop(0, RWKV_NCH // SOLVE_GROUP, solve, 0)

    def scan(j, carry):
        _rwkv_scan(j, ops, tinv_ref, pkb_ref, mkv_ref, st_ref, (yf_ref, yb_ref), masks)
        return carry

    lax.fori_loop(0, RWKV_NCH, scan, 0)


def _rwkv(fwd_ops, bwd_ops, bsz, s):
    R = RWKV_BLOCK
    nb = s // R
    fc = lambda i: i
    bc = lambda i: nb - 1 - i
    blk = lambda cm: pl.BlockSpec((None, R, RWKV_W), lambda b, i: (b, cm(i), 0))
    dec = lambda cm: pl.BlockSpec((None, RWKV_NCH * 8, RWKV_W), lambda b, i: (b, cm(i), 0))
    seq = lambda a: a.reshape(bsz, -1, RWKV_W)
    out = jax.ShapeDtypeStruct((bsz, s, RWKV_W), F32)
    unit = (2, RWKV_NCH, RWKV_PAIRS)
    scratch = [pltpu.VMEM(unit + (LANES, LANES), BF16),
               pltpu.VMEM(unit + (LANES, 2 * LANES), BF16),
               pltpu.VMEM(unit + (LANES, LANES), F32),
               pltpu.VMEM((2, RWKV_PAIRS, LANES, LANES), F32)]
    return pl.pallas_call(
        _rwkv_body,
        grid=(bsz, nb),
        in_specs=[blk(fc)] * 5 + [dec(fc)] + [blk(bc)] * 5 + [dec(bc)],
        out_specs=[blk(fc), blk(bc)],
        out_shape=[out] * 2,
        scratch_shapes=scratch,
        compiler_params=_params(("parallel", "arbitrary")),
        name="rwkv",
    )(*[seq(a) for a in fwd_ops], *[seq(a) for a in bwd_ops])


def _merge_body(x_ref, gof_ref, gob_ref, gr_ref, fn_ref, ryf_ref, ryb_ref, rbon_ref, rg_ref, gate_ref,
                gn_ref, lng_ref, lnb_ref, gavg_ref, pg_ref, pf_ref, pr_ref, wo_ref, o_ref):
    o = gof_ref[...] + gob_ref[...]
    parts = []
    for h in range(GLA_HEADS):
        oh = o[:, h * GLA_DV:(h + 1) * GLA_DV]
        parts.append(oh * lax.rsqrt(jnp.mean(oh * oh, axis=-1, keepdims=True) + NORM_EPS))
    rg = gr_ref[...]
    y_a = jnp.concatenate(parts, axis=1) * gn_ref[...] * (rg * _sigmoid(rg))
    y = ryf_ref[...] + ryb_ref[...]
    mean = _mm_xw(y, gavg_ref[...], 2)
    yc = y - mean
    var = _mm_xw(yc * yc, gavg_ref[...], 1)
    y_c = (yc * lax.rsqrt(var + RWKV_LN_EPS) * lng_ref[...] + lnb_ref[...] + rbon_ref[...]) * rg_ref[...]
    d = x_ref.shape[1]
    gate = gate_ref[...].astype(F32)
    merged = (_sigmoid(gate[:, 0:d]) * _mm(y_a, pg_ref[...])
              + _sigmoid(gate[:, d:2 * d]) * _mm(fn_ref[...], pf_ref[...])
              + _sigmoid(gate[:, 2 * d:3 * d]) * _mm(y_c, pr_ref[...]))
    o_ref[...] = x_ref[...] + _mm(merged, wo_ref[...])


def _merge(x, gla_of, gla_ob, z_gla, y_fnet, r_yf, r_yb, r_bonus, r_g, z_gate,
           gn, lng, lnb, gavg, pg, pf, pr, wo, *, tm):
    t, d = x.shape
    row = lambda c: pl.BlockSpec((tm, c), lambda i: (i, 0))
    r_col = (2 * GLA_QK + GLA_V) // GLA_V
    gr_spec = pl.BlockSpec((tm, GLA_V), lambda i: (i, r_col))
    consts = (gn, lng, lnb, gavg, pg, pf, pr, wo)
    return pl.pallas_call(
        _merge_body,
        grid=(t // tm,),
        in_specs=[row(d), row(GLA_V), row(GLA_V), gr_spec, row(FNET_W), row(RWKV_W), row(RWKV_W),
                  row(RWKV_W), row(RWKV_W), row(GATE_COLS)] + [_const_spec(a.shape) for a in consts],
        out_specs=row(d),
        out_shape=jax.ShapeDtypeStruct((t, d), F32),
        compiler_params=_params(("parallel",)),
        name="merge",
    )(x, gla_of, gla_ob, z_gla, y_fnet, r_yf, r_yb, r_bonus, r_g, z_gate, *consts)


def _dft_tables(s):
    def cos_sin(n):
        idx = jnp.arange(n, dtype=jnp.int32)
        ang = (2.0 * jnp.pi / n) * ((idx[:, None] * idx[None, :]) % n).astype(F32)
        return jnp.cos(ang) * (n ** -0.5), jnp.sin(ang) * (n ** -0.5)

    cg, sg = cos_sin(FNET_GC)
    eye = jnp.eye(FNET_GROUPS, dtype=F32)
    cc = jnp.kron(eye, cg).astype(BF16)
    sc = jnp.kron(eye, sg).astype(BF16)
    cp, sp = cos_sin(s)
    cs = jnp.concatenate([cp, -sp], axis=1).astype(BF16)
    return cc, sc, cs


def _tri(fwd, n):
    i = jnp.arange(n)
    same = (i[None, :] // CHUNK) == (i[:, None] // CHUNK)
    m = (i[None, :] <= i[:, None]) if fwd else (i[None, :] >= i[:, None])
    return (m & same).astype(BF16)


def _chunk_sel(n):
    return (jnp.arange(n // CHUNK * 8)[:, None] // 8 == jnp.arange(n)[None, :] // CHUNK).astype(BF16)


def _pad_rows(w, offset, total):
    return jnp.zeros((total, w.shape[1]), F32).at[offset:offset + w.shape[0]].set(w.astype(F32))


def kernel(x, ffn1_norm, ffn1_gate, ffn1_up, ffn1_down, mix_norm, w_in, gla_up_f, gla_bias_f, gla_up_b, gla_bias_b, gla_norm, rwkv_mu, rwkv_w0_f, rwkv_w2_f, rwkv_w0_b, rwkv_w2_b, rwkv_a0_f, rwkv_a2_f, rwkv_a0_b, rwkv_a2_b, rwkv_g2, rwkv_k_k, rwkv_k_a, rwkv_r_k, rwkv_ln_g, rwkv_ln_b, proj_gla, proj_fnet, proj_rwkv, w_out, ffn2_norm, ffn2_gate, ffn2_up, ffn2_down, final_norm):
    bsz, s, d = x.shape
    depth = w_in.shape[0]
    t = bsz * s
    tm = 256
    assert s % SEQ_BLOCK == 0 and t % tm == 0 and d * 3 == GATE_COLS

    cc, sc, cs = _dft_tables(s)
    trif_r, trib_r, sel_r = _tri(True, SEQ_BLOCK), _tri(False, SEQ_BLOCK), _chunk_sel(SEQ_BLOCK)
    head_of = jnp.arange(RWKV_W) // RWKV_N
    same_head = head_of[:, None] == head_of[None, :]
    gsum = same_head.astype(BF16)
    gavg = (same_head.astype(F32) * (1.0 / RWKV_N)).astype(BF16)
    row = lambda a: a.astype(F32).reshape(1, -1)
    fnorm = row(final_norm)

    x2 = x.reshape(t, d)
    for l in range(depth):
        x2 = _ffn(x2, row(ffn1_norm[l]), ffn1_gate[l].astype(BF16), ffn1_up[l].astype(BF16),
                  ffn1_down[l].astype(BF16), fnorm, final=False, tm=tm)

        w = w_in[l]
        o_f, o_r, o_g = GLA_COLS_RAW, GLA_COLS_RAW + FNET_W, GLA_COLS_RAW + FNET_W + RWKV_COLS
        w_pad = jnp.concatenate([
            w[:, :o_f], jnp.zeros((d, GLA_COLS - GLA_COLS_RAW), w.dtype),
            w[:, o_f:o_r],
            w[:, o_r:o_g], jnp.zeros((d, RWKV_COLS_PAD - RWKV_COLS), w.dtype),
            w[:, o_g:]], axis=1).astype(BF16)
        mu = jnp.concatenate([rwkv_mu[l].astype(F32), jnp.zeros((RWKV_COLS_PAD - RWKV_COLS,), F32)]).reshape(1, -1)
        rwkv_consts = (mu, row(rwkv_w0_f[l]), row(rwkv_w0_b[l]), row(rwkv_a0_f[l]), row(rwkv_a0_b[l]),
                       jnp.concatenate([_pad_rows(rwkv_w2_f[l], 0, LANES), _pad_rows(rwkv_w2_b[l], 32, LANES)], axis=1),
                       jnp.concatenate([_pad_rows(rwkv_a2_f[l], 64, LANES), _pad_rows(rwkv_a2_b[l], 96, LANES)], axis=1),
                       _pad_rows(rwkv_g2[l], 0, LANES),
                       row(rwkv_k_k[l]), row(rwkv_k_a[l]), row(rwkv_r_k[l]), trif_r, trib_r, sel_r, gsum)
        (z_gla, z_fnet, z_gate, ka_f, rh_f, kt_f, bt_f, dec_f, ka_b, rh_b, kt_b, bt_b, dec_b,
         r_v, r_bonus, r_g) = _inproj(x2, row(mix_norm[l]), w_pad, rwkv_consts, seq=s)

        gla_of, gla_ob = _gla(
            z_gla.reshape(bsz, s, GLA_COLS),
            _pad_rows(gla_up_f[l], 0, LANES), row(gla_bias_f[l]),
            _pad_rows(gla_up_b[l], GLA_RANK, LANES), row(gla_bias_b[l]), trif_r, trib_r, sel_r)

        y_fnet = _fnet(z_fnet.reshape(bsz, s, FNET_W), cc, sc, cs)

        r_yf, r_yb = _rwkv((ka_f, rh_f, kt_f, bt_f, r_v, dec_f), (ka_b, rh_b, kt_b, bt_b, r_v, dec_b), bsz, s)

        flat = lambda a: a.reshape(t, a.shape[-1])
        x2 = _merge(x2, flat(gla_of), flat(gla_ob), z_gla, flat(y_fnet), flat(r_yf), flat(r_yb),
                    r_bonus, r_g, z_gate,
                    row(gla_norm[l]), row(rwkv_ln_g[l]), row(rwkv_ln_b[l]), gavg,
                    proj_gla[l].astype(BF16), proj_fnet[l].astype(BF16), proj_rwkv[l].astype(BF16),
                    w_out[l].astype(BF16), tm=tm)

        x2 = _ffn(x2, row(ffn2_norm[l]), ffn2_gate[l].astype(BF16), ffn2_up[l].astype(BF16),
                  ffn2_down[l].astype(BF16), fnorm, final=(l == depth - 1), tm=tm)
    return x2.reshape(bsz, s, d)
```

```python
import functools

import jax
import jax.numpy as jnp
from jax import lax
from jax.experimental import pallas as pl
from jax.experimental.pallas import tpu as pltpu

F32 = jnp.float32
BF16 = jnp.bfloat16

NORM_EPS = 1e-6
RWKV_LN_EPS = 64e-5
GLA_TAU = 16.0

GLA_HEADS, GLA_DK, GLA_DV, GLA_RANK = 4, 64, 128, 16
GLA_QK, GLA_V = GLA_HEADS * GLA_DK, GLA_HEADS * GLA_DV
FNET_GROUPS, FNET_GC = 4, 128
FNET_W = FNET_GROUPS * FNET_GC
RWKV_HEADS, RWKV_N = 8, 64
RWKV_W = RWKV_HEADS * RWKV_N
RWKV_LOWRANK = 128
RWKV_GATE_RANK = 96
RWKV_COLS = 3 * RWKV_W + RWKV_LOWRANK + RWKV_GATE_RANK

LANES = 128
CHUNK = 64

GLA_COLS_RAW = 2 * GLA_QK + 2 * GLA_V + 2 * GLA_RANK
GLA_COLS = 13 * LANES
RWKV_COLS_PAD = 14 * LANES
GATE_COLS = 3 * 1024

VMEM_LIMIT = 56 * 1024 * 1024


def _mm(a, b):
    return jnp.dot(a.astype(BF16), b.astype(BF16), preferred_element_type=F32)


def _mm_nt(a, b):
    return lax.dot_general(a.astype(BF16), b.astype(BF16), (((1,), (1,)), ((), ())),
                           preferred_element_type=F32)


def _mm_tn(a, b):
    return lax.dot_general(a.astype(BF16), b.astype(BF16), (((0,), (0,)), ((), ())),
                           preferred_element_type=F32)


def _split(x, n):
    pieces = []
    for _ in range(n - 1):
        p = x.astype(BF16)
        pieces.append(p)
        x = x - p.astype(F32)
    pieces.append(x.astype(BF16))
    return pieces


def _mm_xw(a, w, n):
    out = None
    for p in _split(a, n):
        d = jnp.dot(p, w, preferred_element_type=F32)
        out = d if out is None else out + d
    return out


def _mm_parts(w, parts):
    out = None
    for p in parts:
        d = jnp.dot(w, p, preferred_element_type=F32)
        out = d if out is None else out + d
    return out


def _mm_3x(a, b):
    a_hi, a_lo = _split(a, 2)
    b_hi, b_lo = _split(b, 2)
    return (jnp.dot(a_hi, b_hi, preferred_element_type=F32) + jnp.dot(a_lo, b_hi, preferred_element_type=F32)
            + jnp.dot(a_hi, b_lo, preferred_element_type=F32))


def _sigmoid(x):
    return 0.5 * jnp.tanh(0.5 * x) + 0.5


def _log_sigmoid(x):
    return jnp.minimum(x, 0.0) - jnp.log(1.0 + jnp.exp(-jnp.abs(x)))


def _rms(x, g):
    return x * lax.rsqrt(jnp.mean(x * x, axis=-1, keepdims=True) + NORM_EPS) * g


def _const_spec(shape):
    nd = len(shape)
    return pl.BlockSpec(shape, lambda *_: (0,) * nd, pipeline_mode=pl.Buffered(1))


def _params(sem):
    return pltpu.CompilerParams(dimension_semantics=sem, vmem_limit_bytes=VMEM_LIMIT)


def _ffn_body(x_ref, g_ref, wg_ref, wu_ref, wd_ref, fg_ref, o_ref, *, final):
    x = x_ref[...]
    h = _rms(x, g_ref[...]).astype(BF16)
    gate = jnp.dot(h, wg_ref[...], preferred_element_type=F32)
    up = jnp.dot(h, wu_ref[...], preferred_element_type=F32)
    act = (gate * _sigmoid(gate) * up).astype(BF16)
    y = x + 0.5 * jnp.dot(act, wd_ref[...], preferred_element_type=F32)
    if final:
        y = _rms(y, fg_ref[...])
    o_ref[...] = y


def _ffn(x, g, wg, wu, wd, fg, *, final, tm):
    t, d = x.shape
    f = wg.shape[1]
    row = pl.BlockSpec((tm, d), lambda i: (i, 0))
    return pl.pallas_call(
        functools.partial(_ffn_body, final=final),
        grid=(t // tm,),
        in_specs=[row, _const_spec((1, d)), _const_spec((d, f)), _const_spec((d, f)),
                  _const_spec((f, d)), _const_spec((1, d))],
        out_specs=row,
        out_shape=jax.ShapeDtypeStruct((t, d), F32),
        compiler_params=_params(("parallel",)),
        name="ffn",
    )(x, g, wg, wu, wd, fg)


SEQ_BLOCK = 256
BLOCK_NCH = SEQ_BLOCK // CHUNK
E_NEG_HALF = 2.718281828459045 ** -0.5


def _rwkv_features(u, prev_row, next_row, prm):
    R, W = SEQ_BLOCK, RWKV_W
    (mu, w0f, w0b, a0f, a0b, w2, a2, g2, k_k, k_a, r_k, trif, trib, sel, gsum) = prm
    row = lax.broadcasted_iota(jnp.int32, (8, 1), 0)
    prev = pltpu.roll(u, 1, 0)
    nxt = pltpu.roll(u, R - 1, 0)
    prev = jnp.concatenate([jnp.where(row == 0, prev_row, prev[0:8]), prev[8:]], axis=0)
    nxt = jnp.concatenate([nxt[:R - 8], jnp.where(row == 7, next_row, nxt[R - 8:])], axis=0)
    u = u * (1.0 - mu) + (prev + nxt) * (0.5 * mu)

    r, k, v = u[:, 0:W], u[:, W:2 * W], u[:, 2 * W:3 * W]
    sm = u[:, 3 * W:3 * W + LANES]
    gd = u[:, 3 * W + LANES:3 * W + 2 * LANES]
    zw = _mm(jnp.tanh(sm), w2)
    za = _mm(sm, a2)
    kk = k * k_k
    kap = kk * lax.rsqrt(_mm_xw(kk * kk, gsum, 1) + 1e-12)
    dirs = []
    for fwd, w0, a0, tri, sl in ((True, w0f, a0f, trif, slice(0, W)), (False, w0b, a0b, trib, slice(W, 2 * W))):
        lw = (-E_NEG_HALF) * _sigmoid(w0 + zw[:, sl])
        a = _sigmoid(a0 + za[:, sl])
        kd = k * (1.0 + (a - 1.0) * k_a)
        beta = kap * a
        lw_parts = _split(lw, 2)
        inc = _mm_parts(tri, lw_parts)
        tot = _mm_parts(sel, lw_parts)
        e_exc = jnp.exp(inc - lw)
        e_ninc = jnp.exp(-inc)
        k_t, b_t = kd * e_ninc, beta * e_ninc
        kb_cm = jnp.concatenate(
            [jnp.concatenate([k_t[j * CHUNK:(j + 1) * CHUNK], b_t[j * CHUNK:(j + 1) * CHUNK]], axis=0).T
             for j in range(BLOCK_NCH)], axis=0)
        dirs.append((kap * e_exc, r * (jnp.exp(inc) if fwd else e_exc), k_t, b_t, jnp.exp(tot), kb_cm))
        if fwd:
            bonus = _mm_xw(r * kd * r_k, gsum, 1) * v
    g = _mm(_sigmoid(gd), g2)
    return dirs, v, bonus, g


def _inproj_body(x_ref, xp_ref, xn_ref, g_ref, w_ref,
                 mu_ref, w0f_ref, w0b_ref, a0f_ref, a0b_ref, w2_ref, a2_ref, g2_ref, kk_ref, ka_ref, rk_ref,
                 trif_ref, trib_ref, sel_ref, gsum_ref,
                 gla_ref, fnet_ref, gate_ref,
                 kaf_ref, rhf_ref, ktf_ref, btf_ref, decf_ref, kbf_ref,
                 kab_ref, rhb_ref, ktb_ref, btb_ref, decb_ref, kbb_ref,
                 v_ref, bonus_ref, rg_ref, *, blocks_per_seq):
    i = pl.program_id(0)
    c0, c1, c2 = GLA_COLS, GLA_COLS + FNET_W, GLA_COLS + FNET_W + RWKV_COLS_PAD
    g = g_ref[...]
    h = _rms(x_ref[...], g).astype(BF16)
    halo = _rms(jnp.concatenate([xp_ref[...], xn_ref[...]], axis=0), g).astype(BF16)
    z_halo = jnp.dot(halo, w_ref[:, c1:c2], preferred_element_type=F32)
    pos = i % blocks_per_seq
    prev_row = jnp.where(pos > 0, z_halo[7:8, :], 0.0)
    next_row = jnp.where(pos < blocks_per_seq - 1, z_halo[8:9, :], 0.0)
    u = jnp.dot(h, w_ref[:, c1:c2], preferred_element_type=F32)
    gla_ref[...] = jnp.dot(h, w_ref[:, 0:c0], preferred_element_type=F32)
    fnet_ref[...] = jnp.dot(h, w_ref[:, c0:c1], preferred_element_type=F32).astype(BF16)
    gate_ref[...] = jnp.dot(h, w_ref[:, c2:c2 + GATE_COLS], preferred_element_type=F32).astype(BF16)
    prm = tuple(ref[...] for ref in (mu_ref, w0f_ref, w0b_ref, a0f_ref, a0b_ref, w2_ref, a2_ref, g2_ref,
                                     kk_ref, ka_ref, rk_ref, trif_ref, trib_ref, sel_ref, gsum_ref))
    dirs, v, bonus, rg = _rwkv_features(u, prev_row, next_row, prm)
    for refs, vals in (((kaf_ref, rhf_ref, ktf_ref, btf_ref, decf_ref, kbf_ref), dirs[0]),
                       ((kab_ref, rhb_ref, ktb_ref, btb_ref, decb_ref, kbb_ref), dirs[1])):
        for ref, val in zip(refs, vals):
            ref[...] = val.astype(ref.dtype)
    v_ref[...] = v.astype(BF16)
    bonus_ref[...] = bonus
    rg_ref[...] = rg


def _inproj(x, g, w, rwkv_consts, *, seq):
    t, d = x.shape
    n = w.shape[1]
    tm = SEQ_BLOCK
    sub = 8
    per = tm // sub
    row = lambda c: pl.BlockSpec((tm, c), lambda i: (i, 0))
    prev = pl.BlockSpec((sub, d), lambda i: (jnp.maximum(i * per - 1, 0), 0))
    nxt = pl.BlockSpec((sub, d), lambda i: (jnp.minimum((i + 1) * per, t // sub - 1), 0))
    dec = pl.BlockSpec((BLOCK_NCH * 8, RWKV_W), lambda i: (i, 0))
    op = jax.ShapeDtypeStruct((t, RWKV_W), BF16)
    dec_shape = jax.ShapeDtypeStruct((t // CHUNK * 8, RWKV_W), F32)
    cm = pl.BlockSpec((BLOCK_NCH * RWKV_W, LANES), lambda i: (i, 0))
    cm_shape = jax.ShapeDtypeStruct((t // CHUNK * RWKV_W, LANES), BF16)
    per_dir = lambda a, b, c: [a] * 4 + [b, c]
    wide = jax.ShapeDtypeStruct((t, RWKV_W), F32)
    return pl.pallas_call(
        functools.partial(_inproj_body, blocks_per_seq=seq // tm),
        grid=(t // tm,),
        in_specs=[row(d), prev, nxt, _const_spec((1, d)), _const_spec((d, n))]
                 + [_const_spec(a.shape) for a in rwkv_consts],
        out_specs=[row(GLA_COLS), row(FNET_W), row(GATE_COLS)]
                  + per_dir(row(RWKV_W), dec, cm) * 2 + [row(RWKV_W)] * 3,
        out_shape=[jax.ShapeDtypeStruct((t, GLA_COLS), F32),
                   jax.ShapeDtypeStruct((t, FNET_W), BF16),
                   jax.ShapeDtypeStruct((t, GATE_COLS), BF16)]
                  + per_dir(op, dec_shape, cm_shape) * 2 + [op, wide, wide],
        compiler_params=_params(("parallel",)),
        name="inproj",
    )(x, x, x, g, w, *rwkv_consts)


def _lane_masks():
    lane = lax.broadcasted_iota(jnp.int32, (CHUNK, LANES), 1)
    return lane < (LANES // 2), lane >= (LANES // 2)


def _masked_stack(x, lo, hi):
    return jnp.concatenate([jnp.where(lo, x, 0.0), jnp.where(hi, x, 0.0)], axis=0)


def _pair_masks(fwd, inclusive):
    n = 2 * CHUNK
    i = lax.broadcasted_iota(jnp.int32, (n, n), 0)
    j = lax.broadcasted_iota(jnp.int32, (n, n), 1)
    same = (i < CHUNK) == (j < CHUNK)
    strict = (j < i) if fwd else (j > i)
    if inclusive:
        return same & (strict | (i == j))
    return same & strict


GLA_PAIRS = GLA_HEADS // 2


def _gla_prep(z_ref, up, bias, tri, sel, d, qd_ref, kd_ref, v_ref, dec_ref):
    L = CHUNK
    z = z_ref[...]
    q = z[:, 0:GLA_QK] * (GLA_DK ** -0.5)
    k = z[:, GLA_QK:2 * GLA_QK]
    v = z[:, 2 * GLA_QK:2 * GLA_QK + GLA_V]
    dn = z[:, GLA_COLS - LANES:GLA_COLS]
    la = _log_sigmoid(_mm(dn, up) + bias) * (1.0 / GLA_TAU)
    la_parts = _split(la, 2)
    b = _mm_parts(tri, la_parts)
    btot = _mm_parts(sel, la_parts)
    outs = ((qd_ref, q * jnp.exp(b)), (kd_ref, k * jnp.exp(-b)), (v_ref, v))
    for j in range(BLOCK_NCH):
        for ref, val in outs:
            ref[d, j] = val[j * L:(j + 1) * L]
    dec_ref[d] = jnp.exp(btot)


def _gla_step(j, qd_ref, kd_ref, v_ref, dec_ref, st_ref, o_refs, masks):
    L = CHUNK
    lo, hi, m_read = masks
    units = [(d, p) for d in range(2) for p in range(GLA_PAIRS)]
    cj = {0: j, 1: BLOCK_NCH - 1 - j}
    q_ms, kd_ms, v_st, sc = {}, {}, {}, {}
    for (d, p) in units:
        sl = pl.ds(p * LANES, LANES)
        q_ms[d, p] = _masked_stack(qd_ref[d, cj[d], :, sl], lo, hi)
        kd_ms[d, p] = _masked_stack(kd_ref[d, cj[d], :, sl], lo, hi)
        sc[d, p] = jnp.where(m_read[d], _mm_nt(q_ms[d, p], kd_ms[d, p]), 0.0)
        v_st[d, p] = jnp.concatenate([v_ref[d, cj[d], :, pl.ds((2 * p) * GLA_DV, GLA_DV)],
                                      v_ref[d, cj[d], :, pl.ds((2 * p + 1) * GLA_DV, GLA_DV)]], axis=0)
    for (d, p) in units:
        sl = pl.ds(p * LANES, LANES)
        st = st_ref[d, p]
        o = _mm(sc[d, p], v_st[d, p]) + _mm_nt(q_ms[d, p], st)
        row0 = pl.multiple_of(cj[d] * L, L)
        o_refs[d][pl.ds(row0, L), pl.ds((2 * p) * GLA_DV, GLA_DV)] = o[0:L]
        o_refs[d][pl.ds(row0, L), pl.ds((2 * p + 1) * GLA_DV, GLA_DV)] = o[L:2 * L]
        dec = dec_ref[d, pl.ds(pl.multiple_of(cj[d] * 8, 8), 1), sl]
        st_ref[d, p] = (st + _mm_tn(v_st[d, p], kd_ms[d, p])) * dec


def _gla_body(zf_ref, zb_ref, upf_ref, bf_ref, upb_ref, bb_ref, trif_ref, trib_ref, sel_ref,
              of_ref, ob_ref, qd_ref, kd_ref, v_ref, dec_ref, st_ref):
    @pl.when(pl.program_id(1) == 0)
    def _():
        st_ref[...] = jnp.zeros_like(st_ref)

    scr = (qd_ref, kd_ref, v_ref, dec_ref)
    _gla_prep(zf_ref, upf_ref[...], bf_ref[...], trif_ref[...], sel_ref[...], 0, *scr)
    _gla_prep(zb_ref, upb_ref[...], bb_ref[...], trib_ref[...], sel_ref[...], 1, *scr)
    lo, hi = _lane_masks()
    masks = (lo, hi, (_pair_masks(True, inclusive=True), _pair_masks(False, inclusive=False)))

    def step(j, carry):
        _gla_step(j, *scr, st_ref, (of_ref, ob_ref), masks)
        return carry

    lax.fori_loop(0, BLOCK_NCH, step, 0)


def _gla(z, upf, bf, upb, bb, trif, trib, sel):
    bsz, s, _ = z.shape
    nb = s // SEQ_BLOCK
    fmap = lambda b, i: (b, i, 0)
    bmap = lambda b, i: (b, nb - 1 - i, 0)
    zspec = lambda m: pl.BlockSpec((None, SEQ_BLOCK, GLA_COLS), m)
    ospec = lambda m: pl.BlockSpec((None, SEQ_BLOCK, GLA_V), m)
    per_chunk = lambda rows, w: pltpu.VMEM((2, BLOCK_NCH, rows, w), F32)
    consts = (upf, bf, upb, bb, trif, trib, sel)
    return pl.pallas_call(
        _gla_body,
        grid=(bsz, nb),
        in_specs=[zspec(fmap), zspec(bmap)] + [_const_spec(a.shape) for a in consts],
        out_specs=[ospec(fmap), ospec(bmap)],
        out_shape=[jax.ShapeDtypeStruct((bsz, s, GLA_V), F32)] * 2,
        scratch_shapes=[per_chunk(CHUNK, GLA_QK), per_chunk(CHUNK, GLA_QK), per_chunk(CHUNK, GLA_V),
                        pltpu.VMEM((2, BLOCK_NCH * 8, GLA_QK), F32),
                        pltpu.VMEM((2, GLA_PAIRS, GLA_DV, LANES), F32)],
        compiler_params=_params(("parallel", "arbitrary")),
        name="gla",
    )(z, z, *consts)


def _fnet_body(z_ref, cc_ref, sc_ref, cs_ref, o_ref, ab_ref, *, rows):
    s = z_ref.shape[0]
    z = z_ref[...]
    ab_ref[0:s, :] = jnp.dot(z, cc_ref[...], preferred_element_type=F32).astype(BF16)
    ab_ref[s:2 * s, :] = jnp.dot(z, sc_ref[...], preferred_element_type=F32).astype(BF16)
    for i in range(s // rows):
        o_ref[i * rows:(i + 1) * rows, :] = jnp.dot(
            cs_ref[i * rows:(i + 1) * rows, :], ab_ref[...], preferred_element_type=F32).astype(o_ref.dtype)


def _fnet(z, cc, sc, cs):
    bsz, s, w = z.shape
    rows = min(s, 512)
    blk = lambda: pl.BlockSpec((None, s, w), lambda b: (b, 0, 0))
    return pl.pallas_call(
        functools.partial(_fnet_body, rows=rows),
        grid=(bsz,),
        in_specs=[blk(), _const_spec(cc.shape), _const_spec(sc.shape), _const_spec(cs.shape)],
        out_specs=blk(),
        out_shape=jax.ShapeDtypeStruct((bsz, s, w), BF16),
        scratch_shapes=[pltpu.VMEM((2 * s, w), BF16)],
        compiler_params=_params(("parallel",)),
        name="fnet",
    )(z, cc, sc, cs)


RWKV_BLOCK = SEQ_BLOCK
RWKV_NCH = BLOCK_NCH
RWKV_PAIRS = RWKV_HEADS // 2


def _chunk(ref, j, sl):
    return ref[pl.ds(j * CHUNK, CHUNK), sl]


def _rwkv_solve_stages(units, ops, tinv_ref, pkb_ref, mkv_ref, masks):
    L = CHUNK
    n2 = 2 * L
    lo, hi, eye, m_strict, m_read = masks
    top = (lax.broadcasted_iota(jnp.int32, (2 * n2, n2), 0) & (n2 - 1)) < L
    t_inv, pw, s_kb, v_ms, m_k = {}, {}, {}, {}, {}
    for n, u in enumerate(units):
        d, j, p = u
        sl = pl.ds(p * LANES, LANES)
        kaph_ref, rh_ref, _, _, v_ref, _, kbcm_ref = ops[d]
        xs = jnp.concatenate([_masked_stack(_chunk(kaph_ref, j, sl), lo, hi),
                              _masked_stack(_chunk(rh_ref, j, sl), lo, hi)], axis=0)
        s_kb[u] = _mm(xs, kbcm_ref[pl.ds(j * RWKV_W + p * LANES, LANES), :])
        v_ms[u] = _masked_stack(_chunk(v_ref, j, sl), lo, hi)
        if n % 2:
            yield
    for n, u in enumerate(units):
        d, j, p = u
        s_sw = pltpu.roll(s_kb[u], L, 1)
        vs_k = jnp.where(top, s_kb[u], s_sw)
        vs_b = jnp.where(top, s_sw, s_kb[u])
        m_k[u] = jnp.where(m_strict[d], vs_k[0:n2], 0.0).astype(BF16)
        nmat = jnp.where(m_strict[d], -vs_b[0:n2], 0.0)
        p_k = jnp.where(m_read[d], vs_k[n2:2 * n2], 0.0)
        p_b = jnp.where(m_read[d], -vs_b[n2:2 * n2], 0.0)
        pkb_ref[d, j, p] = jnp.concatenate([p_k, p_b], axis=1).astype(BF16)
        t_inv[u] = eye + nmat
        pw[u] = nmat.astype(BF16)
        if n % 2:
            yield
    for n, u in enumerate(units):
        d, j, p = u
        mkv_ref[d, j, p] = _mm(m_k[u], v_ms[u])
        if n % 2:
            yield
    for n, u in enumerate(units):
        pw[u] = jnp.dot(pw[u], pw[u], preferred_element_type=F32).astype(BF16)
        if n % 2:
            yield
    for _ in range(4):
        for n, u in enumerate(units):
            both = jnp.dot(jnp.concatenate([t_inv[u].astype(BF16), pw[u]], axis=0), pw[u],
                           preferred_element_type=F32)
            t_inv[u] = t_inv[u] + both[0:n2]
            pw[u] = both[n2:2 * n2].astype(BF16)
            if n % 2:
                yield
    for n, u in enumerate(units):
        d, j, p = u
        tinv_ref[d, j, p] = (t_inv[u] + jnp.dot(t_inv[u].astype(BF16), pw[u],
                                                preferred_element_type=F32)).astype(BF16)
        if n % 2:
            yield


def _rwkv_scan_stages(steps, ops, tinv_ref, pkb_ref, mkv_ref, st_ref, y_refs, masks):
    L = CHUNK
    n2 = 2 * L
    lo, hi = masks[0], masks[1]
    units = [(d, p) for d in range(2) for p in range(RWKV_PAIRS)]
    for j in steps:
        cj = {0: j, 1: RWKV_NCH - 1 - j}
        v_ms, xh, uu = {}, {}, {}
        for (d, p) in units:
            sl = pl.ds(p * LANES, LANES)
            xs = jnp.concatenate([_masked_stack(_chunk(ops[d][0], cj[d], sl), lo, hi),
                                  _masked_stack(_chunk(ops[d][1], cj[d], sl), lo, hi)], axis=0)
            xh[d, p] = _mm_nt(xs, st_ref[d, p])
            yield
        for (d, p) in units:
            sl = pl.ds(p * LANES, LANES)
            v_ms[d, p] = _masked_stack(_chunk(ops[d][4], cj[d], sl), lo, hi)
            uu[d, p] = _mm(tinv_ref[d, cj[d], p], xh[d, p][0:n2] + mkv_ref[d, cj[d], p])
            yield
        for (d, p) in units:
            sl = pl.ds(p * LANES, LANES)
            y = xh[d, p][n2:2 * n2] + _mm(pkb_ref[d, cj[d], p], jnp.concatenate([v_ms[d, p], uu[d, p]], axis=0))
            y_refs[d][pl.ds(cj[d] * L, L), sl] = y[0:L] + y[L:n2]
            kb = jnp.concatenate([_masked_stack(_chunk(ops[d][2], cj[d], sl), lo, hi),
                                  _masked_stack(_chunk(ops[d][3], cj[d], sl), lo, hi)], axis=0)
            dec = ops[d][5][pl.ds(cj[d] * 8, 1), sl]
            st_ref[d, p] = (st_ref[d, p] + _mm_tn(jnp.concatenate([v_ms[d, p], -uu[d, p]], axis=0), kb)) * dec
            yield


def _weave(*gens):
    live = list(gens)
    while live:
        for g in list(live):
            try:
                next(g)
            except StopIteration:
                live.remove(g)


def _rwkv_body(kaf_ref, rhf_ref, ktf_ref, btf_ref, vf_ref, decf_ref, kbf_ref,
               kab_ref, rhb_ref, ktb_ref, btb_ref, vb_ref, decb_ref, kbb_ref,
               yf_ref, yb_ref, tinv_ref, pkb_ref, mkv_ref, st_ref):
    @pl.when(pl.program_id(1) == 0)
    def _():
        st_ref[...] = jnp.zeros_like(st_ref)

    ops = ((kaf_ref, rhf_ref, ktf_ref, btf_ref, vf_ref, decf_ref, kbf_ref),
           (kab_ref, rhb_ref, ktb_ref, btb_ref, vb_ref, decb_ref, kbb_ref))
    lo, hi = _lane_masks()
    n2 = 2 * CHUNK
    ii = lax.broadcasted_iota(jnp.int32, (n2, n2), 0)
    jj = lax.broadcasted_iota(jnp.int32, (n2, n2), 1)
    eye = (ii == jj).astype(F32)
    m_strict = (_pair_masks(True, inclusive=False), _pair_masks(False, inclusive=False))
    m_read = (_pair_masks(True, inclusive=True), _pair_masks(False, inclusive=False))
    masks = (lo, hi, eye, m_strict, m_read)

    def solve(steps):
        units = [(d, j if d == 0 else RWKV_NCH - 1 - j, p)
                 for j in steps for d in range(2) for p in range(RWKV_PAIRS)]
        return _rwkv_solve_stages(units, ops, tinv_ref, pkb_ref, mkv_ref, masks)

    def scan(steps):
        return _rwkv_scan_stages(steps, ops, tinv_ref, pkb_ref, mkv_ref, st_ref, (yf_ref, yb_ref), masks)

    half = RWKV_NCH // 2
    first, second = list(range(half)), list(range(half, RWKV_NCH))
    _weave(solve(first))
    _weave(solve(second), scan(first))
    _weave(scan(second))


def _rwkv(fwd_ops, bwd_ops, bsz, s):
    R = RWKV_BLOCK
    nb = s // R
    fc = lambda i: i
    bc = lambda i: nb - 1 - i
    blk = lambda cm: pl.BlockSpec((None, R, RWKV_W), lambda b, i: (b, cm(i), 0))
    dec = lambda cm: pl.BlockSpec((None, RWKV_NCH * 8, RWKV_W), lambda b, i: (b, cm(i), 0))
    cm = lambda cm_: pl.BlockSpec((None, RWKV_NCH * RWKV_W, LANES), lambda b, i: (b, cm_(i), 0))
    seq = lambda a: a.reshape(bsz, -1, a.shape[-1])
    out = jax.ShapeDtypeStruct((bsz, s, RWKV_W), F32)
    unit = (2, RWKV_NCH, RWKV_PAIRS)
    scratch = [pltpu.VMEM(unit + (LANES, LANES), BF16),
               pltpu.VMEM(unit + (LANES, 2 * LANES), BF16),
               pltpu.VMEM(unit + (LANES, LANES), F32),
               pltpu.VMEM((2, RWKV_PAIRS, LANES, LANES), F32)]
    return pl.pallas_call(
        _rwkv_body,
        grid=(bsz, nb),
        in_specs=[blk(fc)] * 5 + [dec(fc), cm(fc)] + [blk(bc)] * 5 + [dec(bc), cm(bc)],
        out_specs=[blk(fc), blk(bc)],
        out_shape=[out] * 2,
        scratch_shapes=scratch,
        compiler_params=_params(("parallel", "arbitrary")),
        name="rwkv",
    )(*[seq(a) for a in fwd_ops], *[seq(a) for a in bwd_ops])


def _merge_body(x_ref, gof_ref, gob_ref, gr_ref, fn_ref, ryf_ref, ryb_ref, rbon_ref, rg_ref, gate_ref,
                gn_ref, lng_ref, lnb_ref, gavg_ref, pg_ref, pf_ref, pr_ref, wo_ref, o_ref):
    o = gof_ref[...] + gob_ref[...]
    parts = []
    for h in range(GLA_HEADS):
        oh = o[:, h * GLA_DV:(h + 1) * GLA_DV]
        parts.append(oh * lax.rsqrt(jnp.mean(oh * oh, axis=-1, keepdims=True) + NORM_EPS))
    rg = gr_ref[...]
    y_a = jnp.concatenate(parts, axis=1) * gn_ref[...] * (rg * _sigmoid(rg))
    y = ryf_ref[...] + ryb_ref[...]
    mean = _mm_xw(y, gavg_ref[...], 2)
    yc = y - mean
    var = _mm_xw(yc * yc, gavg_ref[...], 1)
    y_c = (yc * lax.rsqrt(var + RWKV_LN_EPS) * lng_ref[...] + lnb_ref[...] + rbon_ref[...]) * rg_ref[...]
    d = x_ref.shape[1]
    gate = gate_ref[...].astype(F32)
    merged = (_sigmoid(gate[:, 0:d]) * _mm(y_a, pg_ref[...])
              + _sigmoid(gate[:, d:2 * d]) * _mm(fn_ref[...], pf_ref[...])
              + _sigmoid(gate[:, 2 * d:3 * d]) * _mm(y_c, pr_ref[...]))
    o_ref[...] = x_ref[...] + _mm(merged, wo_ref[...])


def _merge(x, gla_of, gla_ob, z_gla, y_fnet, r_yf, r_yb, r_bonus, r_g, z_gate,
           gn, lng, lnb, gavg, pg, pf, pr, wo, *, tm):
    t, d = x.shape
    row = lambda c: pl.BlockSpec((tm, c), lambda i: (i, 0))
    r_col = (2 * GLA_QK + GLA_V) // GLA_V
    gr_spec = pl.BlockSpec((tm, GLA_V), lambda i: (i, r_col))
    consts = (gn, lng, lnb, gavg, pg, pf, pr, wo)
    return pl.pallas_call(
        _merge_body,
        grid=(t // tm,),
        in_specs=[row(d), row(GLA_V), row(GLA_V), gr_spec, row(FNET_W), row(RWKV_W), row(RWKV_W),
                  row(RWKV_W), row(RWKV_W), row(GATE_COLS)] + [_const_spec(a.shape) for a in consts],
        out_specs=row(d),
        out_shape=jax.ShapeDtypeStruct((t, d), F32),
        compiler_params=_params(("parallel",)),
        name="merge",
    )(x, gla_of, gla_ob, z_gla, y_fnet, r_yf, r_yb, r_bonus, r_g, z_gate, *consts)


def _dft_tables(s):
    def cos_sin(n):
        idx = jnp.arange(n, dtype=jnp.int32)
        ang = (2.0 * jnp.pi / n) * ((idx[:, None] * idx[None, :]) % n).astype(F32)
        return jnp.cos(ang) * (n ** -0.5), jnp.sin(ang) * (n ** -0.5)

    cg, sg = cos_sin(FNET_GC)
    eye = jnp.eye(FNET_GROUPS, dtype=F32)
    cc = jnp.kron(eye, cg).astype(BF16)
    sc = jnp.kron(eye, sg).astype(BF16)
    cp, sp = cos_sin(s)
    cs = jnp.concatenate([cp, -sp], axis=1).astype(BF16)
    return cc, sc, cs


def _tri(fwd, n):
    i = jnp.arange(n)
    same = (i[None, :] // CHUNK) == (i[:, None] // CHUNK)
    m = (i[None, :] <= i[:, None]) if fwd else (i[None, :] >= i[:, None])
    return (m & same).astype(BF16)


def _chunk_sel(n):
    return (jnp.arange(n // CHUNK * 8)[:, None] // 8 == jnp.arange(n)[None, :] // CHUNK).astype(BF16)


def _pad_rows(w, offset, total):
    return jnp.zeros((total, w.shape[1]), F32).at[offset:offset + w.shape[0]].set(w.astype(F32))


def kernel(x, ffn1_norm, ffn1_gate, ffn1_up, ffn1_down, mix_norm, w_in, gla_up_f, gla_bias_f, gla_up_b, gla_bias_b, gla_norm, rwkv_mu, rwkv_w0_f, rwkv_w2_f, rwkv_w0_b, rwkv_w2_b, rwkv_a0_f, rwkv_a2_f, rwkv_a0_b, rwkv_a2_b, rwkv_g2, rwkv_k_k, rwkv_k_a, rwkv_r_k, rwkv_ln_g, rwkv_ln_b, proj_gla, proj_fnet, proj_rwkv, w_out, ffn2_norm, ffn2_gate, ffn2_up, ffn2_down, final_norm):
    bsz, s, d = x.shape
    depth = w_in.shape[0]
    t = bsz * s
    tm = 256
    assert s % SEQ_BLOCK == 0 and t % tm == 0 and d * 3 == GATE_COLS

    cc, sc, cs = _dft_tables(s)
    trif_r, trib_r, sel_r = _tri(True, SEQ_BLOCK), _tri(False, SEQ_BLOCK), _chunk_sel(SEQ_BLOCK)
    head_of = jnp.arange(RWKV_W) // RWKV_N
    same_head = head_of[:, None] == head_of[None, :]
    gsum = same_head.astype(BF16)
    gavg = (same_head.astype(F32) * (1.0 / RWKV_N)).astype(BF16)
    row = lambda a: a.astype(F32).reshape(1, -1)
    fnorm = row(final_norm)

    x2 = x.reshape(t, d)
    for l in range(depth):
        x2 = _ffn(x2, row(ffn1_norm[l]), ffn1_gate[l].astype(BF16), ffn1_up[l].astype(BF16),
                  ffn1_down[l].astype(BF16), fnorm, final=False, tm=tm)

        w = w_in[l]
        o_f, o_r, o_g = GLA_COLS_RAW, GLA_COLS_RAW + FNET_W, GLA_COLS_RAW + FNET_W + RWKV_COLS
        w_pad = jnp.concatenate([
            w[:, :o_f], jnp.zeros((d, GLA_COLS - GLA_COLS_RAW), w.dtype),
            w[:, o_f:o_r],
            w[:, o_r:o_g], jnp.zeros((d, RWKV_COLS_PAD - RWKV_COLS), w.dtype),
            w[:, o_g:]], axis=1).astype(BF16)
        mu = jnp.concatenate([rwkv_mu[l].astype(F32), jnp.zeros((RWKV_COLS_PAD - RWKV_COLS,), F32)]).reshape(1, -1)
        rwkv_consts = (mu, row(rwkv_w0_f[l]), row(rwkv_w0_b[l]), row(rwkv_a0_f[l]), row(rwkv_a0_b[l]),
                       jnp.concatenate([_pad_rows(rwkv_w2_f[l], 0, LANES), _pad_rows(rwkv_w2_b[l], 32, LANES)], axis=1),
                       jnp.concatenate([_pad_rows(rwkv_a2_f[l], 64, LANES), _pad_rows(rwkv_a2_b[l], 96, LANES)], axis=1),
                       _pad_rows(rwkv_g2[l], 0, LANES),
                       row(rwkv_k_k[l]), row(rwkv_k_a[l]), row(rwkv_r_k[l]), trif_r, trib_r, sel_r, gsum)
        (z_gla, z_fnet, z_gate, ka_f, rh_f, kt_f, bt_f, dec_f, kb_f, ka_b, rh_b, kt_b, bt_b, dec_b, kb_b,
         r_v, r_bonus, r_g) = _inproj(x2, row(mix_norm[l]), w_pad, rwkv_consts, seq=s)

        gla_of, gla_ob = _gla(
            z_gla.reshape(bsz, s, GLA_COLS),
            _pad_rows(gla_up_f[l], 0, LANES), row(gla_bias_f[l]),
            _pad_rows(gla_up_b[l], GLA_RANK, LANES), row(gla_bias_b[l]), trif_r, trib_r, sel_r)

        y_fnet = _fnet(z_fnet.reshape(bsz, s, FNET_W), cc, sc, cs)

        r_yf, r_yb = _rwkv((ka_f, rh_f, kt_f, bt_f, r_v, dec_f, kb_f), (ka_b, rh_b, kt_b, bt_b, r_v, dec_b, kb_b),
                           bsz, s)

        flat = lambda a: a.reshape(t, a.shape[-1])
        x2 = _merge(x2, flat(gla_of), flat(gla_ob), z_gla, flat(y_fnet), flat(r_yf), flat(r_yb),
                    r_bonus, r_g, z_gate,
                    row(gla_norm[l]), row(rwkv_ln_g[l]), row(rwkv_ln_b[l]), gavg,
                    proj_gla[l].astype(BF16), proj_fnet[l].astype(BF16), proj_rwkv[l].astype(BF16),
                    w_out[l].astype(BF16), tm=tm)

        x2 = _ffn(x2, row(ffn2_norm[l]), ffn2_gate[l].astype(BF16), ffn2_up[l].astype(BF16),
                  ffn2_down[l].astype(BF16), fnorm, final=(l == depth - 1), tm=tm)
    return x2.reshape(bsz, s, d)
```

```python
import functools

import jax
import jax.numpy as jnp
from jax import lax
from jax.experimental import pallas as pl
from jax.experimental.pallas import tpu as pltpu

F32 = jnp.float32
BF16 = jnp.bfloat16

NORM_EPS = 1e-6
RWKV_LN_EPS = 64e-5
GLA_TAU = 16.0

GLA_HEADS, GLA_DK, GLA_DV, GLA_RANK = 4, 64, 128, 16
GLA_QK, GLA_V = GLA_HEADS * GLA_DK, GLA_HEADS * GLA_DV
FNET_GROUPS, FNET_GC = 4, 128
FNET_W = FNET_GROUPS * FNET_GC
RWKV_HEADS, RWKV_N = 8, 64
RWKV_W = RWKV_HEADS * RWKV_N
RWKV_LOWRANK = 128
RWKV_GATE_RANK = 96
RWKV_COLS = 3 * RWKV_W + RWKV_LOWRANK + RWKV_GATE_RANK

LANES = 128
CHUNK = 64

GLA_COLS_RAW = 2 * GLA_QK + 2 * GLA_V + 2 * GLA_RANK
GLA_COLS = 13 * LANES
RWKV_COLS_PAD = 14 * LANES
GATE_COLS = 3 * 1024

VMEM_LIMIT = 56 * 1024 * 1024


def _mm(a, b):
    return jnp.dot(a.astype(BF16), b.astype(BF16), preferred_element_type=F32)


def _mm_nt(a, b):
    return lax.dot_general(a.astype(BF16), b.astype(BF16), (((1,), (1,)), ((), ())),
                           preferred_element_type=F32)


def _mm_tn(a, b):
    return lax.dot_general(a.astype(BF16), b.astype(BF16), (((0,), (0,)), ((), ())),
                           preferred_element_type=F32)


def _split(x, n):
    pieces = []
    for _ in range(n - 1):
        p = x.astype(BF16)
        pieces.append(p)
        x = x - p.astype(F32)
    pieces.append(x.astype(BF16))
    return pieces


def _mm_xw(a, w, n):
    out = None
    for p in _split(a, n):
        d = jnp.dot(p, w, preferred_element_type=F32)
        out = d if out is None else out + d
    return out


def _mm_parts(w, parts):
    out = None
    for p in parts:
        d = jnp.dot(w, p, preferred_element_type=F32)
        out = d if out is None else out + d
    return out


def _mm_3x(a, b):
    a_hi, a_lo = _split(a, 2)
    b_hi, b_lo = _split(b, 2)
    return (jnp.dot(a_hi, b_hi, preferred_element_type=F32) + jnp.dot(a_lo, b_hi, preferred_element_type=F32)
            + jnp.dot(a_hi, b_lo, preferred_element_type=F32))


def _sigmoid(x):
    return 0.5 * jnp.tanh(0.5 * x) + 0.5


def _log_sigmoid(x):
    return jnp.minimum(x, 0.0) - jnp.log(1.0 + jnp.exp(-jnp.abs(x)))


def _rms(x, g):
    return x * lax.rsqrt(jnp.mean(x * x, axis=-1, keepdims=True) + NORM_EPS) * g


def _const_spec(shape):
    nd = len(shape)
    return pl.BlockSpec(shape, lambda *_: (0,) * nd, pipeline_mode=pl.Buffered(1))


def _params(sem):
    return pltpu.CompilerParams(dimension_semantics=sem, vmem_limit_bytes=VMEM_LIMIT)


def _ffn_body(x_ref, g_ref, wg_ref, wu_ref, wd_ref, fg_ref, o_ref, *, final):
    x = x_ref[...]
    h = _rms(x, g_ref[...]).astype(BF16)
    gate = jnp.dot(h, wg_ref[...], preferred_element_type=F32)
    up = jnp.dot(h, wu_ref[...], preferred_element_type=F32)
    act = (gate * _sigmoid(gate) * up).astype(BF16)
    y = x + 0.5 * jnp.dot(act, wd_ref[...], preferred_element_type=F32)
    if final:
        y = _rms(y, fg_ref[...])
    o_ref[...] = y


def _ffn(x, g, wg, wu, wd, fg, *, final, tm):
    t, d = x.shape
    f = wg.shape[1]
    row = pl.BlockSpec((tm, d), lambda i: (i, 0))
    return pl.pallas_call(
        functools.partial(_ffn_body, final=final),
        grid=(t // tm,),
        in_specs=[row, _const_spec((1, d)), _const_spec((d, f)), _const_spec((d, f)),
                  _const_spec((f, d)), _const_spec((1, d))],
        out_specs=row,
        out_shape=jax.ShapeDtypeStruct((t, d), F32),
        compiler_params=_params(("parallel",)),
        name="ffn",
    )(x, g, wg, wu, wd, fg)


SEQ_BLOCK = 256
BLOCK_NCH = SEQ_BLOCK // CHUNK
E_NEG_HALF = 2.718281828459045 ** -0.5


def _rwkv_feature_stages(u, prev_row, next_row, prm, dir_refs, v_ref, bonus_ref, rg_ref):
    R, W = SEQ_BLOCK, RWKV_W
    (mu, w0f, w0b, a0f, a0b, w2, a2, g2, k_k, k_a, r_k, trif, trib, sel, gsum) = prm
    row = lax.broadcasted_iota(jnp.int32, (8, 1), 0)
    prev = pltpu.roll(u, 1, 0)
    nxt = pltpu.roll(u, R - 1, 0)
    prev = jnp.concatenate([jnp.where(row == 0, prev_row, prev[0:8]), prev[8:]], axis=0)
    nxt = jnp.concatenate([nxt[:R - 8], jnp.where(row == 7, next_row, nxt[R - 8:])], axis=0)
    u = u * (1.0 - mu) + (prev + nxt) * (0.5 * mu)
    yield
    r, k, v = u[:, 0:W], u[:, W:2 * W], u[:, 2 * W:3 * W]
    sm = u[:, 3 * W:3 * W + LANES]
    gd = u[:, 3 * W + LANES:3 * W + 2 * LANES]
    v_ref[...] = v.astype(BF16)
    zw = _mm(jnp.tanh(sm), w2)
    za = _mm(sm, a2)
    rg_ref[...] = _mm(_sigmoid(gd), g2).astype(rg_ref.dtype)
    kk = k * k_k
    kap = kk * lax.rsqrt(_mm_xw(kk * kk, gsum, 1) + 1e-12)
    yield
    dirs = ((True, w0f, a0f, trif, slice(0, W)), (False, w0b, a0b, trib, slice(W, 2 * W)))
    lw, kd, beta, inc, tot = [], [], [], [], []
    for fwd, w0, a0, tri, sl in dirs:
        lw.append((-E_NEG_HALF) * _sigmoid(w0 + zw[:, sl]))
        a = _sigmoid(a0 + za[:, sl])
        kd.append(k * (1.0 + (a - 1.0) * k_a))
        beta.append(kap * a)
        yield
    bonus_ref[...] = (_mm_xw(r * kd[0] * r_k, gsum, 1) * v).astype(bonus_ref.dtype)
    for d, (fwd, w0, a0, tri, sl) in enumerate(dirs):
        lw_parts = _split(lw[d], 2)
        inc.append(_mm_parts(tri, lw_parts))
        tot.append(_mm_parts(sel, lw_parts))
        yield
    for d, (fwd, w0, a0, tri, sl) in enumerate(dirs):
        ka_ref, rh_ref, kt_ref, bt_ref, dec_ref, kb_ref = dir_refs[d]
        e_exc = jnp.exp(inc[d] - lw[d])
        e_ninc = jnp.exp(-inc[d])
        ka_ref[...] = (kap * e_exc).astype(BF16)
        rh_ref[...] = (r * (jnp.exp(inc[d]) if fwd else e_exc)).astype(BF16)
        dec_ref[...] = jnp.exp(tot[d])
        yield
        k_t, b_t = kd[d] * e_ninc, beta[d] * e_ninc
        kt_ref[...] = k_t.astype(BF16)
        bt_ref[...] = b_t.astype(BF16)
        yield
        for j in range(BLOCK_NCH):
            rows = slice(j * CHUNK, (j + 1) * CHUNK)
            kb_ref[j * W:(j + 1) * W, :] = jnp.concatenate([k_t[rows], b_t[rows]], axis=0).T.astype(BF16)
        yield


def _column_dots(h, w_ref, pieces, width):
    for off, cols, ref in pieces:
        for a in range(0, cols, width):
            b = min(a + width, cols)
            ref[:, a:b] = jnp.dot(h, w_ref[:, off + a:off + b], preferred_element_type=F32).astype(ref.dtype)
            yield


def _inproj_body(x_ref, xp_ref, xn_ref, g_ref, w_ref,
                 mu_ref, w0f_ref, w0b_ref, a0f_ref, a0b_ref, w2_ref, a2_ref, g2_ref, kk_ref, ka_ref, rk_ref,
                 trif_ref, trib_ref, sel_ref, gsum_ref,
                 gla_ref, fnet_ref, gate_ref,
                 kaf_ref, rhf_ref, ktf_ref, btf_ref, decf_ref, kbf_ref,
                 kab_ref, rhb_ref, ktb_ref, btb_ref, decb_ref, kbb_ref,
                 v_ref, bonus_ref, rg_ref, *, blocks_per_seq):
    i = pl.program_id(0)
    c0, c1, c2 = GLA_COLS, GLA_COLS + FNET_W, GLA_COLS + FNET_W + RWKV_COLS_PAD
    g = g_ref[...]
    h = _rms(x_ref[...], g).astype(BF16)
    halo = _rms(jnp.concatenate([xp_ref[...], xn_ref[...]], axis=0), g).astype(BF16)
    z_halo = jnp.dot(halo, w_ref[:, c1:c2], preferred_element_type=F32)
    pos = i % blocks_per_seq
    prev_row = jnp.where(pos > 0, z_halo[7:8, :], 0.0)
    next_row = jnp.where(pos < blocks_per_seq - 1, z_halo[8:9, :], 0.0)
    u = jnp.dot(h, w_ref[:, c1:c2], preferred_element_type=F32)
    prm = tuple(ref[...] for ref in (mu_ref, w0f_ref, w0b_ref, a0f_ref, a0b_ref, w2_ref, a2_ref, g2_ref,
                                     kk_ref, ka_ref, rk_ref, trif_ref, trib_ref, sel_ref, gsum_ref))
    dir_refs = ((kaf_ref, rhf_ref, ktf_ref, btf_ref, decf_ref, kbf_ref),
                (kab_ref, rhb_ref, ktb_ref, btb_ref, decb_ref, kbb_ref))
    _weave(_rwkv_feature_stages(u, prev_row, next_row, prm, dir_refs, v_ref, bonus_ref, rg_ref),
           _column_dots(h, w_ref, ((c2, GATE_COLS, gate_ref), (0, c0, gla_ref), (c0, FNET_W, fnet_ref)), 512))


def _inproj(x, g, w, rwkv_consts, *, seq):
    t, d = x.shape
    n = w.shape[1]
    tm = SEQ_BLOCK
    sub = 8
    per = tm // sub
    row = lambda c: pl.BlockSpec((tm, c), lambda i: (i, 0))
    prev = pl.BlockSpec((sub, d), lambda i: (jnp.maximum(i * per - 1, 0), 0))
    nxt = pl.BlockSpec((sub, d), lambda i: (jnp.minimum((i + 1) * per, t // sub - 1), 0))
    dec = pl.BlockSpec((BLOCK_NCH * 8, RWKV_W), lambda i: (i, 0))
    op = jax.ShapeDtypeStruct((t, RWKV_W), BF16)
    dec_shape = jax.ShapeDtypeStruct((t // CHUNK * 8, RWKV_W), F32)
    cm = pl.BlockSpec((BLOCK_NCH * RWKV_W, LANES), lambda i: (i, 0))
    cm_shape = jax.ShapeDtypeStruct((t // CHUNK * RWKV_W, LANES), BF16)
    per_dir = lambda a, b, c: [a] * 4 + [b, c]
    return pl.pallas_call(
        functools.partial(_inproj_body, blocks_per_seq=seq // tm),
        grid=(t // tm,),
        in_specs=[row(d), prev, nxt, _const_spec((1, d)), _const_spec((d, n))]
                 + [_const_spec(a.shape) for a in rwkv_consts],
        out_specs=[row(GLA_COLS), row(FNET_W), row(GATE_COLS)]
                  + per_dir(row(RWKV_W), dec, cm) * 2 + [row(RWKV_W)] * 3,
        out_shape=[jax.ShapeDtypeStruct((t, GLA_COLS), F32),
                   jax.ShapeDtypeStruct((t, FNET_W), BF16),
                   jax.ShapeDtypeStruct((t, GATE_COLS), BF16)]
                  + per_dir(op, dec_shape, cm_shape) * 2 + [op] * 3,
        compiler_params=_params(("parallel",)),
        name="inproj",
    )(x, x, x, g, w, *rwkv_consts)


def _lane_masks():
    lane = lax.broadcasted_iota(jnp.int32, (CHUNK, LANES), 1)
    return lane < (LANES // 2), lane >= (LANES // 2)


def _masked_stack(x, lo, hi):
    return jnp.concatenate([jnp.where(lo, x, 0.0), jnp.where(hi, x, 0.0)], axis=0)


def _pair_masks(fwd, inclusive):
    n = 2 * CHUNK
    i = lax.broadcasted_iota(jnp.int32, (n, n), 0)
    j = lax.broadcasted_iota(jnp.int32, (n, n), 1)
    same = (i < CHUNK) == (j < CHUNK)
    strict = (j < i) if fwd else (j > i)
    if inclusive:
        return same & (strict | (i == j))
    return same & strict


GLA_PAIRS = GLA_HEADS // 2


def _gla_prep(z_ref, up, bias, tri, sel, d, qd_ref, kd_ref, v_ref, dec_ref):
    L = CHUNK
    z = z_ref[...]
    q = z[:, 0:GLA_QK] * (GLA_DK ** -0.5)
    k = z[:, GLA_QK:2 * GLA_QK]
    v = z[:, 2 * GLA_QK:2 * GLA_QK + GLA_V]
    dn = z[:, GLA_COLS - LANES:GLA_COLS]
    la = _log_sigmoid(_mm(dn, up) + bias) * (1.0 / GLA_TAU)
    la_parts = _split(la, 2)
    b = _mm_parts(tri, la_parts)
    btot = _mm_parts(sel, la_parts)
    outs = ((qd_ref, q * jnp.exp(b)), (kd_ref, k * jnp.exp(-b)), (v_ref, v))
    for j in range(BLOCK_NCH):
        for ref, val in outs:
            ref[d, j] = val[j * L:(j + 1) * L]
    dec_ref[d] = jnp.exp(btot)


def _gla_steps(qd_ref, kd_ref, v_ref, dec_ref, st_ref, o_refs, masks):
    L = CHUNK
    lo, hi, m_read = masks
    units = [(j, d, p) for j in range(BLOCK_NCH) for d in range(2) for p in range(GLA_PAIRS)]
    chunk = lambda j, d: j if d == 0 else BLOCK_NCH - 1 - j
    q_ms, kd_ms, v_st, sc, inc, o_intra = {}, {}, {}, {}, {}, {}
    for u in units:
        j, d, p = u
        c = chunk(j, d)
        sl = pl.ds(p * LANES, LANES)
        q_ms[u] = _masked_stack(qd_ref[d, c, :, sl], lo, hi)
        kd_ms[u] = _masked_stack(kd_ref[d, c, :, sl], lo, hi)
        v_st[u] = jnp.concatenate([v_ref[d, c, :, pl.ds((2 * p) * GLA_DV, GLA_DV)],
                                   v_ref[d, c, :, pl.ds((2 * p + 1) * GLA_DV, GLA_DV)]], axis=0)
        sc[u] = jnp.where(m_read[d], _mm_nt(q_ms[u], kd_ms[u]), 0.0)
        inc[u] = _mm_tn(v_st[u], kd_ms[u])
    for u in units:
        o_intra[u] = _mm(sc[u], v_st[u])
    for d in range(2):
        for p in range(GLA_PAIRS):
            sl = pl.ds(p * LANES, LANES)
            st = st_ref[d, p]
            for j in range(BLOCK_NCH):
                u = (j, d, p)
                c = chunk(j, d)
                o = o_intra[u] + _mm_nt(q_ms[u], st)
                o_refs[d][pl.ds(c * L, L), pl.ds((2 * p) * GLA_DV, GLA_DV)] = o[0:L].astype(BF16)
                o_refs[d][pl.ds(c * L, L), pl.ds((2 * p + 1) * GLA_DV, GLA_DV)] = o[L:2 * L].astype(BF16)
                st = (st + inc[u]) * dec_ref[d, pl.ds(c * 8, 1), sl]
            st_ref[d, p] = st


def _gla_body(zf_ref, zb_ref, upf_ref, bf_ref, upb_ref, bb_ref, trif_ref, trib_ref, sel_ref,
              of_ref, ob_ref, qd_ref, kd_ref, v_ref, dec_ref, st_ref):
    @pl.when(pl.program_id(1) == 0)
    def _():
        st_ref[...] = jnp.zeros_like(st_ref)

    scr = (qd_ref, kd_ref, v_ref, dec_ref)
    _gla_prep(zf_ref, upf_ref[...], bf_ref[...], trif_ref[...], sel_ref[...], 0, *scr)
    _gla_prep(zb_ref, upb_ref[...], bb_ref[...], trib_ref[...], sel_ref[...], 1, *scr)
    lo, hi = _lane_masks()
    masks = (lo, hi, (_pair_masks(True, inclusive=True), _pair_masks(False, inclusive=False)))

    _gla_steps(*scr, st_ref, (of_ref, ob_ref), masks)


def _gla(z, upf, bf, upb, bb, trif, trib, sel):
    bsz, s, _ = z.shape
    nb = s // SEQ_BLOCK
    fmap = lambda b, i: (b, i, 0)
    bmap = lambda b, i: (b, nb - 1 - i, 0)
    zspec = lambda m: pl.BlockSpec((None, SEQ_BLOCK, GLA_COLS), m)
    ospec = lambda m: pl.BlockSpec((None, SEQ_BLOCK, GLA_V), m)
    per_chunk = lambda rows, w: pltpu.VMEM((2, BLOCK_NCH, rows, w), F32)
    consts = (upf, bf, upb, bb, trif, trib, sel)
    return pl.pallas_call(
        _gla_body,
        grid=(bsz, nb),
        in_specs=[zspec(fmap), zspec(bmap)] + [_const_spec(a.shape) for a in consts],
        out_specs=[ospec(fmap), ospec(bmap)],
        out_shape=[jax.ShapeDtypeStruct((bsz, s, GLA_V), BF16)] * 2,
        scratch_shapes=[per_chunk(CHUNK, GLA_QK), per_chunk(CHUNK, GLA_QK), per_chunk(CHUNK, GLA_V),
                        pltpu.VMEM((2, BLOCK_NCH * 8, GLA_QK), F32),
                        pltpu.VMEM((2, GLA_PAIRS, GLA_DV, LANES), F32)],
        compiler_params=_params(("parallel", "arbitrary")),
        name="gla",
    )(z, z, *consts)


def _fnet_body(z_ref, cc_ref, sc_ref, cs_ref, o_ref, ab_ref, *, rows):
    s = z_ref.shape[0]
    z = z_ref[...]
    ab_ref[0:s, :] = jnp.dot(z, cc_ref[...], preferred_element_type=F32).astype(BF16)
    ab_ref[s:2 * s, :] = jnp.dot(z, sc_ref[...], preferred_element_type=F32).astype(BF16)
    for i in range(s // rows):
        o_ref[i * rows:(i + 1) * rows, :] = jnp.dot(
            cs_ref[i * rows:(i + 1) * rows, :], ab_ref[...], preferred_element_type=F32).astype(o_ref.dtype)


def _fnet(z, cc, sc, cs):
    bsz, s, w = z.shape
    rows = min(s, 512)
    blk = lambda: pl.BlockSpec((None, s, w), lambda b: (b, 0, 0))
    return pl.pallas_call(
        functools.partial(_fnet_body, rows=rows),
        grid=(bsz,),
        in_specs=[blk(), _const_spec(cc.shape), _const_spec(sc.shape), _const_spec(cs.shape)],
        out_specs=blk(),
        out_shape=jax.ShapeDtypeStruct((bsz, s, w), BF16),
        scratch_shapes=[pltpu.VMEM((2 * s, w), BF16)],
        compiler_params=_params(("parallel",)),
        name="fnet",
    )(z, cc, sc, cs)


RWKV_BLOCK = SEQ_BLOCK
RWKV_NCH = BLOCK_NCH
RWKV_PAIRS = RWKV_HEADS // 2


def _chunk(ref, j, sl):
    return ref[pl.ds(j * CHUNK, CHUNK), sl]


def _rwkv_solve_stages(units, ops, tinv_ref, pkb_ref, mkv_ref, masks):
    L = CHUNK
    n2 = 2 * L
    lo, hi, eye, m_strict, m_read = masks
    top = (lax.broadcasted_iota(jnp.int32, (2 * n2, n2), 0) & (n2 - 1)) < L
    t_inv, pw, s_kb, v_ms, m_k = {}, {}, {}, {}, {}
    for n, u in enumerate(units):
        d, j, p = u
        sl = pl.ds(p * LANES, LANES)
        kaph_ref, rh_ref, _, _, v_ref, _, kbcm_ref = ops[d]
        xs = jnp.concatenate([_masked_stack(_chunk(kaph_ref, j, sl), lo, hi),
                              _masked_stack(_chunk(rh_ref, j, sl), lo, hi)], axis=0)
        s_kb[u] = _mm(xs, kbcm_ref[pl.ds(j * RWKV_W + p * LANES, LANES), :])
        v_ms[u] = _masked_stack(_chunk(v_ref, j, sl), lo, hi)
        if n % 2:
            yield
    for n, u in enumerate(units):
        d, j, p = u
        s_sw = pltpu.roll(s_kb[u], L, 1)
        vs_k = jnp.where(top, s_kb[u], s_sw)
        vs_b = jnp.where(top, s_sw, s_kb[u])
        m_k[u] = jnp.where(m_strict[d], vs_k[0:n2], 0.0).astype(BF16)
        nmat = jnp.where(m_strict[d], -vs_b[0:n2], 0.0)
        p_k = jnp.where(m_read[d], vs_k[n2:2 * n2], 0.0)
        p_b = jnp.where(m_read[d], -vs_b[n2:2 * n2], 0.0)
        pkb_ref[d, j, p] = jnp.concatenate([p_k, p_b], axis=1).astype(BF16)
        t_inv[u] = eye + nmat
        pw[u] = nmat.astype(BF16)
        if n % 2:
            yield
    for n, u in enumerate(units):
        d, j, p = u
        mkv_ref[d, j, p] = _mm(m_k[u], v_ms[u])
        if n % 2:
            yield
    for n, u in enumerate(units):
        pw[u] = jnp.dot(pw[u], pw[u], preferred_element_type=F32).astype(BF16)
        if n % 2:
            yield
    for _ in range(4):
        for n, u in enumerate(units):
            both = jnp.dot(jnp.concatenate([t_inv[u].astype(BF16), pw[u]], axis=0), pw[u],
                           preferred_element_type=F32)
            t_inv[u] = t_inv[u] + both[0:n2]
            pw[u] = both[n2:2 * n2].astype(BF16)
            if n % 2:
                yield
    for n, u in enumerate(units):
        d, j, p = u
        tinv_ref[d, j, p] = (t_inv[u] + jnp.dot(t_inv[u].astype(BF16), pw[u],
                                                preferred_element_type=F32)).astype(BF16)
        if n % 2:
            yield


def _rwkv_scan_stages(steps, ops, tinv_ref, pkb_ref, mkv_ref, st_ref, y_refs, masks):
    L = CHUNK
    n2 = 2 * L
    lo, hi = masks[0], masks[1]
    units = [(d, p) for d in range(2) for p in range(RWKV_PAIRS)]
    for j in steps:
        cj = {0: j, 1: RWKV_NCH - 1 - j}
        v_ms, xh, uu = {}, {}, {}
        for (d, p) in units:
            sl = pl.ds(p * LANES, LANES)
            xs = jnp.concatenate([_masked_stack(_chunk(ops[d][0], cj[d], sl), lo, hi),
                                  _masked_stack(_chunk(ops[d][1], cj[d], sl), lo, hi)], axis=0)
            xh[d, p] = _mm_nt(xs, st_ref[d, p])
            yield
        for (d, p) in units:
            sl = pl.ds(p * LANES, LANES)
            v_ms[d, p] = _masked_stack(_chunk(ops[d][4], cj[d], sl), lo, hi)
            uu[d, p] = _mm(tinv_ref[d, cj[d], p], xh[d, p][0:n2] + mkv_ref[d, cj[d], p])
            yield
        for (d, p) in units:
            sl = pl.ds(p * LANES, LANES)
            y = xh[d, p][n2:2 * n2] + _mm(pkb_ref[d, cj[d], p], jnp.concatenate([v_ms[d, p], uu[d, p]], axis=0))
            y_refs[d][pl.ds(cj[d] * L, L), sl] = (y[0:L] + y[L:n2]).astype(BF16)
            kb = jnp.concatenate([_masked_stack(_chunk(ops[d][2], cj[d], sl), lo, hi),
                                  _masked_stack(_chunk(ops[d][3], cj[d], sl), lo, hi)], axis=0)
            dec = ops[d][5][pl.ds(cj[d] * 8, 1), sl]
            st_ref[d, p] = (st_ref[d, p] + _mm_tn(jnp.concatenate([v_ms[d, p], -uu[d, p]], axis=0), kb)) * dec
            yield


def _weave(*gens):
    live = list(gens)
    while live:
        for g in list(live):
            try:
                next(g)
            except StopIteration:
                live.remove(g)


def _rwkv_body(kaf_ref, rhf_ref, ktf_ref, btf_ref, vf_ref, decf_ref, kbf_ref,
               kab_ref, rhb_ref, ktb_ref, btb_ref, vb_ref, decb_ref, kbb_ref,
               yf_ref, yb_ref, tinv_ref, pkb_ref, mkv_ref, st_ref):
    @pl.when(pl.program_id(1) == 0)
    def _():
        st_ref[...] = jnp.zeros_like(st_ref)

    ops = ((kaf_ref, rhf_ref, ktf_ref, btf_ref, vf_ref, decf_ref, kbf_ref),
           (kab_ref, rhb_ref, ktb_ref, btb_ref, vb_ref, decb_ref, kbb_ref))
    lo, hi = _lane_masks()
    n2 = 2 * CHUNK
    ii = lax.broadcasted_iota(jnp.int32, (n2, n2), 0)
    jj = lax.broadcasted_iota(jnp.int32, (n2, n2), 1)
    eye = (ii == jj).astype(F32)
    m_strict = (_pair_masks(True, inclusive=False), _pair_masks(False, inclusive=False))
    m_read = (_pair_masks(True, inclusive=True), _pair_masks(False, inclusive=False))
    masks = (lo, hi, eye, m_strict, m_read)

    def solve(steps):
        units = [(d, j if d == 0 else RWKV_NCH - 1 - j, p)
                 for j in steps for d in range(2) for p in range(RWKV_PAIRS)]
        return _rwkv_solve_stages(units, ops, tinv_ref, pkb_ref, mkv_ref, masks)

    def scan(steps):
        return _rwkv_scan_stages(steps, ops, tinv_ref, pkb_ref, mkv_ref, st_ref, (yf_ref, yb_ref), masks)

    half = RWKV_NCH // 2
    first, second = list(range(half)), list(range(half, RWKV_NCH))
    _weave(solve(first))
    _weave(solve(second), scan(first))
    _weave(scan(second))


def _rwkv(fwd_ops, bwd_ops, bsz, s):
    R = RWKV_BLOCK
    nb = s // R
    fc = lambda i: i
    bc = lambda i: nb - 1 - i
    blk = lambda cm: pl.BlockSpec((None, R, RWKV_W), lambda b, i: (b, cm(i), 0))
    dec = lambda cm: pl.BlockSpec((None, RWKV_NCH * 8, RWKV_W), lambda b, i: (b, cm(i), 0))
    cm = lambda cm_: pl.BlockSpec((None, RWKV_NCH * RWKV_W, LANES), lambda b, i: (b, cm_(i), 0))
    seq = lambda a: a.reshape(bsz, -1, a.shape[-1])
    out = jax.ShapeDtypeStruct((bsz, s, RWKV_W), BF16)
    unit = (2, RWKV_NCH, RWKV_PAIRS)
    scratch = [pltpu.VMEM(unit + (LANES, LANES), BF16),
               pltpu.VMEM(unit + (LANES, 2 * LANES), BF16),
               pltpu.VMEM(unit + (LANES, LANES), F32),
               pltpu.VMEM((2, RWKV_PAIRS, LANES, LANES), F32)]
    return pl.pallas_call(
        _rwkv_body,
        grid=(bsz, nb),
        in_specs=[blk(fc)] * 5 + [dec(fc), cm(fc)] + [blk(bc)] * 5 + [dec(bc), cm(bc)],
        out_specs=[blk(fc), blk(bc)],
        out_shape=[out] * 2,
        scratch_shapes=scratch,
        compiler_params=_params(("parallel", "arbitrary")),
        name="rwkv",
    )(*[seq(a) for a in fwd_ops], *[seq(a) for a in bwd_ops])


def _merge_body(x_ref, gof_ref, gob_ref, gr_ref, fn_ref, ryf_ref, ryb_ref, rbon_ref, rg_ref, gate_ref,
                gn_ref, lng_ref, lnb_ref, gavg_ref, pg_ref, pf_ref, pr_ref, wo_ref, o_ref):
    o = gof_ref[...].astype(F32) + gob_ref[...].astype(F32)
    parts = []
    for h in range(GLA_HEADS):
        oh = o[:, h * GLA_DV:(h + 1) * GLA_DV]
        parts.append(oh * lax.rsqrt(jnp.mean(oh * oh, axis=-1, keepdims=True) + NORM_EPS))
    rg = gr_ref[...]
    y_a = jnp.concatenate(parts, axis=1) * gn_ref[...] * (rg * _sigmoid(rg))
    y = ryf_ref[...].astype(F32) + ryb_ref[...].astype(F32)
    mean = _mm_xw(y, gavg_ref[...], 2)
    yc = y - mean
    var = _mm_xw(yc * yc, gavg_ref[...], 1)
    y_c = (yc * lax.rsqrt(var + RWKV_LN_EPS) * lng_ref[...] + lnb_ref[...] + rbon_ref[...].astype(F32)) * rg_ref[...].astype(F32)
    d = x_ref.shape[1]
    gate = gate_ref[...].astype(F32)
    merged = (_sigmoid(gate[:, 0:d]) * _mm(y_a, pg_ref[...])
              + _sigmoid(gate[:, d:2 * d]) * _mm(fn_ref[...], pf_ref[...])
              + _sigmoid(gate[:, 2 * d:3 * d]) * _mm(y_c, pr_ref[...]))
    o_ref[...] = x_ref[...] + _mm(merged, wo_ref[...])


def _merge(x, gla_of, gla_ob, z_gla, y_fnet, r_yf, r_yb, r_bonus, r_g, z_gate,
           gn, lng, lnb, gavg, pg, pf, pr, wo, *, tm):
    t, d = x.shape
    row = lambda c: pl.BlockSpec((tm, c), lambda i: (i, 0))
    r_col = (2 * GLA_QK + GLA_V) // GLA_V
    gr_spec = pl.BlockSpec((tm, GLA_V), lambda i: (i, r_col))
    consts = (gn, lng, lnb, gavg, pg, pf, pr, wo)
    return pl.pallas_call(
        _merge_body,
        grid=(t // tm,),
        in_specs=[row(d), row(GLA_V), row(GLA_V), gr_spec, row(FNET_W), row(RWKV_W), row(RWKV_W),
                  row(RWKV_W), row(RWKV_W), row(GATE_COLS)] + [_const_spec(a.shape) for a in consts],
        out_specs=row(d),
        out_shape=jax.ShapeDtypeStruct((t, d), F32),
        compiler_params=_params(("parallel",)),
        name="merge",
    )(x, gla_of, gla_ob, z_gla, y_fnet, r_yf, r_yb, r_bonus, r_g, z_gate, *consts)


def _dft_tables(s):
    def cos_sin(n):
        idx = jnp.arange(n, dtype=jnp.int32)
        ang = (2.0 * jnp.pi / n) * ((idx[:, None] * idx[None, :]) % n).astype(F32)
        return jnp.cos(ang) * (n ** -0.5), jnp.sin(ang) * (n ** -0.5)

    cg, sg = cos_sin(FNET_GC)
    eye = jnp.eye(FNET_GROUPS, dtype=F32)
    cc = jnp.kron(eye, cg).astype(BF16)
    sc = jnp.kron(eye, sg).astype(BF16)
    cp, sp = cos_sin(s)
    cs = jnp.concatenate([cp, -sp], axis=1).astype(BF16)
    return cc, sc, cs


def _tri(fwd, n):
    i = jnp.arange(n)
    same = (i[None, :] // CHUNK) == (i[:, None] // CHUNK)
    m = (i[None, :] <= i[:, None]) if fwd else (i[None, :] >= i[:, None])
    return (m & same).astype(BF16)


def _chunk_sel(n):
    return (jnp.arange(n // CHUNK * 8)[:, None] // 8 == jnp.arange(n)[None, :] // CHUNK).astype(BF16)


def _pad_rows(w, offset, total):
    return jnp.zeros((total, w.shape[1]), F32).at[offset:offset + w.shape[0]].set(w.astype(F32))


def kernel(x, ffn1_norm, ffn1_gate, ffn1_up, ffn1_down, mix_norm, w_in, gla_up_f, gla_bias_f, gla_up_b, gla_bias_b, gla_norm, rwkv_mu, rwkv_w0_f, rwkv_w2_f, rwkv_w0_b, rwkv_w2_b, rwkv_a0_f, rwkv_a2_f, rwkv_a0_b, rwkv_a2_b, rwkv_g2, rwkv_k_k, rwkv_k_a, rwkv_r_k, rwkv_ln_g, rwkv_ln_b, proj_gla, proj_fnet, proj_rwkv, w_out, ffn2_norm, ffn2_gate, ffn2_up, ffn2_down, final_norm):
    bsz, s, d = x.shape
    depth = w_in.shape[0]
    t = bsz * s
    tm = 256
    assert s % SEQ_BLOCK == 0 and t % tm == 0 and d * 3 == GATE_COLS

    cc, sc, cs = _dft_tables(s)
    trif_r, trib_r, sel_r = _tri(True, SEQ_BLOCK), _tri(False, SEQ_BLOCK), _chunk_sel(SEQ_BLOCK)
    head_of = jnp.arange(RWKV_W) // RWKV_N
    same_head = head_of[:, None] == head_of[None, :]
    gsum = same_head.astype(BF16)
    gavg = (same_head.astype(F32) * (1.0 / RWKV_N)).astype(BF16)
    row = lambda a: a.astype(F32).reshape(1, -1)
    fnorm = row(final_norm)

    x2 = x.reshape(t, d)
    for l in range(depth):
        x2 = _ffn(x2, row(ffn1_norm[l]), ffn1_gate[l].astype(BF16), ffn1_up[l].astype(BF16),
                  ffn1_down[l].astype(BF16), fnorm, final=False, tm=tm)

        w = w_in[l]
        o_f, o_r, o_g = GLA_COLS_RAW, GLA_COLS_RAW + FNET_W, GLA_COLS_RAW + FNET_W + RWKV_COLS
        w_pad = jnp.concatenate([
            w[:, :o_f], jnp.zeros((d, GLA_COLS - GLA_COLS_RAW), w.dtype),
            w[:, o_f:o_r],
            w[:, o_r:o_g], jnp.zeros((d, RWKV_COLS_PAD - RWKV_COLS), w.dtype),
            w[:, o_g:]], axis=1).astype(BF16)
        mu = jnp.concatenate([rwkv_mu[l].astype(F32), jnp.zeros((RWKV_COLS_PAD - RWKV_COLS,), F32)]).reshape(1, -1)
        rwkv_consts = (mu, row(rwkv_w0_f[l]), row(rwkv_w0_b[l]), row(rwkv_a0_f[l]), row(rwkv_a0_b[l]),
                       jnp.concatenate([_pad_rows(rwkv_w2_f[l], 0, LANES), _pad_rows(rwkv_w2_b[l], 32, LANES)], axis=1),
                       jnp.concatenate([_pad_rows(rwkv_a2_f[l], 64, LANES), _pad_rows(rwkv_a2_b[l], 96, LANES)], axis=1),
                       _pad_rows(rwkv_g2[l], 0, LANES),
                       row(rwkv_k_k[l]), row(rwkv_k_a[l]), row(rwkv_r_k[l]), trif_r, trib_r, sel_r, gsum)
        (z_gla, z_fnet, z_gate, ka_f, rh_f, kt_f, bt_f, dec_f, kb_f, ka_b, rh_b, kt_b, bt_b, dec_b, kb_b,
         r_v, r_bonus, r_g) = _inproj(x2, row(mix_norm[l]), w_pad, rwkv_consts, seq=s)

        gla_of, gla_ob = _gla(
            z_gla.reshape(bsz, s, GLA_COLS),
            _pad_rows(gla_up_f[l], 0, LANES), row(gla_bias_f[l]),
            _pad_rows(gla_up_b[l], GLA_RANK, LANES), row(gla_bias_b[l]), trif_r, trib_r, sel_r)

        y_fnet = _fnet(z_fnet.reshape(bsz, s, FNET_W), cc, sc, cs)

        r_yf, r_yb = _rwkv((ka_f, rh_f, kt_f, bt_f, r_v, dec_f, kb_f), (ka_b, rh_b, kt_b, bt_b, r_v, dec_b, kb_b),
                           bsz, s)

        flat = lambda a: a.reshape(t, a.shape[-1])
        x2 = _merge(x2, flat(gla_of), flat(gla_ob), z_gla, flat(y_fnet), flat(r_yf), flat(r_yb),
                    r_bonus, r_g, z_gate,
                    row(gla_norm[l]), row(rwkv_ln_g[l]), row(rwkv_ln_b[l]), gavg,
                    proj_gla[l].astype(BF16), proj_fnet[l].astype(BF16), proj_rwkv[l].astype(BF16),
                    w_out[l].astype(BF16), tm=tm)

        x2 = _ffn(x2, row(ffn2_norm[l]), ffn2_gate[l].astype(BF16), ffn2_up[l].astype(BF16),
                  ffn2_down[l].astype(BF16), fnorm, final=(l == depth - 1), tm=tm)
    return x2.reshape(bsz, s, d)
```

```python
import functools

import jax
import jax.numpy as jnp
from jax import lax
from jax.experimental import pallas as pl
from jax.experimental.pallas import tpu as pltpu

F32 = jnp.float32
BF16 = jnp.bfloat16

NORM_EPS = 1e-6
RWKV_LN_EPS = 64e-5
GLA_TAU = 16.0

GLA_HEADS, GLA_DK, GLA_DV, GLA_RANK = 4, 64, 128, 16
GLA_QK, GLA_V = GLA_HEADS * GLA_DK, GLA_HEADS * GLA_DV
FNET_GROUPS, FNET_GC = 4, 128
FNET_W = FNET_GROUPS * FNET_GC
RWKV_HEADS, RWKV_N = 8, 64
RWKV_W = RWKV_HEADS * RWKV_N
RWKV_LOWRANK = 128
RWKV_GATE_RANK = 96
RWKV_COLS = 3 * RWKV_W + RWKV_LOWRANK + RWKV_GATE_RANK

LANES = 128
CHUNK = 64

GLA_COLS_RAW = 2 * GLA_QK + 2 * GLA_V + 2 * GLA_RANK
GLA_COLS = 13 * LANES
RWKV_COLS_PAD = 14 * LANES
GATE_COLS = 3 * 1024

VMEM_LIMIT = 56 * 1024 * 1024


def _mm(a, b):
    return jnp.dot(a.astype(BF16), b.astype(BF16), preferred_element_type=F32)


def _mm_nt(a, b):
    return lax.dot_general(a.astype(BF16), b.astype(BF16), (((1,), (1,)), ((), ())),
                           preferred_element_type=F32)


def _mm_tn(a, b):
    return lax.dot_general(a.astype(BF16), b.astype(BF16), (((0,), (0,)), ((), ())),
                           preferred_element_type=F32)


def _split(x, n):
    pieces = []
    for _ in range(n - 1):
        p = x.astype(BF16)
        pieces.append(p)
        x = x - p.astype(F32)
    pieces.append(x.astype(BF16))
    return pieces


def _group_sums(a, w, n):
    parts = _split(a, n)
    tiles = []
    for c in range(0, a.shape[1], LANES):
        out = None
        for p in parts:
            d = jnp.dot(p[:, c:c + LANES], w, preferred_element_type=F32)
            out = d if out is None else out + d
        tiles.append(out)
    return jnp.concatenate(tiles, axis=1)


def _mm_parts(w, parts):
    out = None
    for p in parts:
        d = jnp.dot(w, p, preferred_element_type=F32)
        out = d if out is None else out + d
    return out


def _mm_3x(a, b):
    a_hi, a_lo = _split(a, 2)
    b_hi, b_lo = _split(b, 2)
    return (jnp.dot(a_hi, b_hi, preferred_element_type=F32) + jnp.dot(a_lo, b_hi, preferred_element_type=F32)
            + jnp.dot(a_hi, b_lo, preferred_element_type=F32))


def _sigmoid(x):
    return 0.5 * jnp.tanh(0.5 * x) + 0.5


def _log_sigmoid(x):
    return jnp.minimum(x, 0.0) - jnp.log(1.0 + jnp.exp(-jnp.abs(x)))


def _rms(x, g):
    return x * lax.rsqrt(jnp.mean(x * x, axis=-1, keepdims=True) + NORM_EPS) * g


def _const_spec(shape):
    nd = len(shape)
    return pl.BlockSpec(shape, lambda *_: (0,) * nd, pipeline_mode=pl.Buffered(1))


def _params(sem):
    return pltpu.CompilerParams(dimension_semantics=sem, vmem_limit_bytes=VMEM_LIMIT)


def _ffn_body(x_ref, g_ref, wg_ref, wu_ref, wd_ref, fg_ref, o_ref, *, final):
    x = x_ref[...]
    h = _rms(x, g_ref[...]).astype(BF16)
    gate = jnp.dot(h, wg_ref[...], preferred_element_type=F32)
    up = jnp.dot(h, wu_ref[...], preferred_element_type=F32)
    act = (gate * _sigmoid(gate) * up).astype(BF16)
    y = x + 0.5 * jnp.dot(act, wd_ref[...], preferred_element_type=F32)
    if final:
        y = _rms(y, fg_ref[...])
    o_ref[...] = y


def _ffn(x, g, wg, wu, wd, fg, *, final, tm):
    t, d = x.shape
    f = wg.shape[1]
    row = pl.BlockSpec((tm, d), lambda i: (i, 0))
    return pl.pallas_call(
        functools.partial(_ffn_body, final=final),
        grid=(t // tm,),
        in_specs=[row, _const_spec((1, d)), _const_spec((d, f)), _const_spec((d, f)),
                  _const_spec((f, d)), _const_spec((1, d))],
        out_specs=row,
        out_shape=jax.ShapeDtypeStruct((t, d), F32),
        compiler_params=_params(("parallel",)),
        name="ffn",
    )(x, g, wg, wu, wd, fg)


SEQ_BLOCK = 256
BLOCK_NCH = SEQ_BLOCK // CHUNK
E_NEG_HALF = 2.718281828459045 ** -0.5


def _rwkv_feature_stages(u, prev_row, next_row, prm, dir_refs, v_ref, bonus_ref, rg_ref):
    R, W = SEQ_BLOCK, RWKV_W
    (mu, w0f, w0b, a0f, a0b, w2, a2, g2, k_k, k_a, r_k, trif, trib, sel, gsum) = prm
    row = lax.broadcasted_iota(jnp.int32, (8, 1), 0)
    prev = pltpu.roll(u, 1, 0)
    nxt = pltpu.roll(u, R - 1, 0)
    prev = jnp.concatenate([jnp.where(row == 0, prev_row, prev[0:8]), prev[8:]], axis=0)
    nxt = jnp.concatenate([nxt[:R - 8], jnp.where(row == 7, next_row, nxt[R - 8:])], axis=0)
    u = u * (1.0 - mu) + (prev + nxt) * (0.5 * mu)
    yield
    r, k, v = u[:, 0:W], u[:, W:2 * W], u[:, 2 * W:3 * W]
    sm = u[:, 3 * W:3 * W + LANES]
    gd = u[:, 3 * W + LANES:3 * W + 2 * LANES]
    v_ref[...] = v.astype(BF16)
    zw = _mm(jnp.tanh(sm), w2)
    za = _mm(sm, a2)
    rg_ref[...] = _mm(_sigmoid(gd), g2).astype(rg_ref.dtype)
    kk = k * k_k
    kap = kk * lax.rsqrt(_group_sums(kk * kk, gsum, 1) + 1e-12)
    yield
    dirs = ((True, w0f, a0f, trif, slice(0, W)), (False, w0b, a0b, trib, slice(W, 2 * W)))
    lw, kd, beta, inc, tot = [], [], [], [], []
    for fwd, w0, a0, tri, sl in dirs:
        lw.append((-E_NEG_HALF) * _sigmoid(w0 + zw[:, sl]))
        a = _sigmoid(a0 + za[:, sl])
        kd.append(k * (1.0 + (a - 1.0) * k_a))
        beta.append(kap * a)
        yield
    bonus_ref[...] = (_group_sums(r * kd[0] * r_k, gsum, 1) * v).astype(bonus_ref.dtype)
    for d, (fwd, w0, a0, tri, sl) in enumerate(dirs):
        lw_parts = _split(lw[d], 2)
        inc.append(_mm_parts(tri, lw_parts))
        tot.append(_mm_parts(sel, lw_parts))
        yield
    for d, (fwd, w0, a0, tri, sl) in enumerate(dirs):
        ka_ref, rh_ref, kt_ref, bt_ref, dec_ref, kb_ref = dir_refs[d]
        e_exc = jnp.exp(inc[d] - lw[d])
        e_ninc = jnp.exp(-inc[d])
        ka_ref[...] = (kap * e_exc).astype(BF16)
        rh_ref[...] = (r * (jnp.exp(inc[d]) if fwd else e_exc)).astype(BF16)
        dec_ref[...] = jnp.exp(tot[d])
        yield
        k_t, b_t = kd[d] * e_ninc, beta[d] * e_ninc
        kt_ref[...] = k_t.astype(BF16)
        bt_ref[...] = b_t.astype(BF16)
        yield
        for j in range(BLOCK_NCH):
            rows = slice(j * CHUNK, (j + 1) * CHUNK)
            kb_ref[j * W:(j + 1) * W, :] = jnp.concatenate([k_t[rows], b_t[rows]], axis=0).T.astype(BF16)
        yield


def _column_dots(h, w_ref, pieces, width):
    for off, cols, ref in pieces:
        for a in range(0, cols, width):
            b = min(a + width, cols)
            ref[:, a:b] = jnp.dot(h, w_ref[:, off + a:off + b], preferred_element_type=F32).astype(ref.dtype)
            yield


def _inproj_body(x_ref, xp_ref, xn_ref, g_ref, w_ref,
                 mu_ref, w0f_ref, w0b_ref, a0f_ref, a0b_ref, w2_ref, a2_ref, g2_ref, kk_ref, ka_ref, rk_ref,
                 trif_ref, trib_ref, sel_ref, gsum_ref,
                 gla_ref, fnet_ref, gate_ref,
                 kaf_ref, rhf_ref, ktf_ref, btf_ref, decf_ref, kbf_ref,
                 kab_ref, rhb_ref, ktb_ref, btb_ref, decb_ref, kbb_ref,
                 v_ref, bonus_ref, rg_ref, *, blocks_per_seq):
    i = pl.program_id(0)
    c0, c1, c2 = GLA_COLS, GLA_COLS + FNET_W, GLA_COLS + FNET_W + RWKV_COLS_PAD
    g = g_ref[...]
    h = _rms(x_ref[...], g).astype(BF16)
    halo = _rms(jnp.concatenate([xp_ref[...], xn_ref[...]], axis=0), g).astype(BF16)
    z_halo = jnp.dot(halo, w_ref[:, c1:c2], preferred_element_type=F32)
    pos = i % blocks_per_seq
    prev_row = jnp.where(pos > 0, z_halo[7:8, :], 0.0)
    next_row = jnp.where(pos < blocks_per_seq - 1, z_halo[8:9, :], 0.0)
    u = jnp.dot(h, w_ref[:, c1:c2], preferred_element_type=F32)
    prm = tuple(ref[...] for ref in (mu_ref, w0f_ref, w0b_ref, a0f_ref, a0b_ref, w2_ref, a2_ref, g2_ref,
                                     kk_ref, ka_ref, rk_ref, trif_ref, trib_ref, sel_ref, gsum_ref))
    dir_refs = ((kaf_ref, rhf_ref, ktf_ref, btf_ref, decf_ref, kbf_ref),
                (kab_ref, rhb_ref, ktb_ref, btb_ref, decb_ref, kbb_ref))
    _weave(_rwkv_feature_stages(u, prev_row, next_row, prm, dir_refs, v_ref, bonus_ref, rg_ref),
           _column_dots(h, w_ref, ((c2, GATE_COLS, gate_ref), (0, c0, gla_ref), (c0, FNET_W, fnet_ref)), 512))


def _inproj(x, g, w, rwkv_consts, *, seq):
    t, d = x.shape
    n = w.shape[1]
    tm = SEQ_BLOCK
    sub = 8
    per = tm // sub
    row = lambda c: pl.BlockSpec((tm, c), lambda i: (i, 0))
    prev = pl.BlockSpec((sub, d), lambda i: (jnp.maximum(i * per - 1, 0), 0))
    nxt = pl.BlockSpec((sub, d), lambda i: (jnp.minimum((i + 1) * per, t // sub - 1), 0))
    dec = pl.BlockSpec((BLOCK_NCH * 8, RWKV_W), lambda i: (i, 0))
    op = jax.ShapeDtypeStruct((t, RWKV_W), BF16)
    dec_shape = jax.ShapeDtypeStruct((t // CHUNK * 8, RWKV_W), F32)
    cm = pl.BlockSpec((BLOCK_NCH * RWKV_W, LANES), lambda i: (i, 0))
    cm_shape = jax.ShapeDtypeStruct((t // CHUNK * RWKV_W, LANES), BF16)
    per_dir = lambda a, b, c: [a] * 4 + [b, c]
    return pl.pallas_call(
        functools.partial(_inproj_body, blocks_per_seq=seq // tm),
        grid=(t // tm,),
        in_specs=[row(d), prev, nxt, _const_spec((1, d)), _const_spec((d, n))]
                 + [_const_spec(a.shape) for a in rwkv_consts],
        out_specs=[row(GLA_COLS), row(FNET_W), row(GATE_COLS)]
                  + per_dir(row(RWKV_W), dec, cm) * 2 + [row(RWKV_W)] * 3,
        out_shape=[jax.ShapeDtypeStruct((t, GLA_COLS), F32),
                   jax.ShapeDtypeStruct((t, FNET_W), BF16),
                   jax.ShapeDtypeStruct((t, GATE_COLS), BF16)]
                  + per_dir(op, dec_shape, cm_shape) * 2 + [op] * 3,
        compiler_params=_params(("parallel",)),
        name="inproj",
    )(x, x, x, g, w, *rwkv_consts)


def _lane_masks():
    lane = lax.broadcasted_iota(jnp.int32, (CHUNK, LANES), 1)
    return lane < (LANES // 2), lane >= (LANES // 2)


def _masked_stack(x, lo, hi):
    return jnp.concatenate([jnp.where(lo, x, 0.0), jnp.where(hi, x, 0.0)], axis=0)


def _pair_masks(fwd, inclusive):
    n = 2 * CHUNK
    i = lax.broadcasted_iota(jnp.int32, (n, n), 0)
    j = lax.broadcasted_iota(jnp.int32, (n, n), 1)
    same = (i < CHUNK) == (j < CHUNK)
    strict = (j < i) if fwd else (j > i)
    if inclusive:
        return same & (strict | (i == j))
    return same & strict


GLA_PAIRS = GLA_HEADS // 2


def _gla_prep(z_ref, up, bias, tri, sel, d, qd_ref, kd_ref, v_ref, dec_ref):
    L = CHUNK
    z = z_ref[...]
    q = z[:, 0:GLA_QK] * (GLA_DK ** -0.5)
    k = z[:, GLA_QK:2 * GLA_QK]
    v = z[:, 2 * GLA_QK:2 * GLA_QK + GLA_V]
    dn = z[:, GLA_COLS - LANES:GLA_COLS]
    la = _log_sigmoid(_mm(dn, up) + bias) * (1.0 / GLA_TAU)
    la_parts = _split(la, 2)
    b = _mm_parts(tri, la_parts)
    btot = _mm_parts(sel, la_parts)
    outs = ((qd_ref, q * jnp.exp(b)), (kd_ref, k * jnp.exp(-b)), (v_ref, v))
    for j in range(BLOCK_NCH):
        for ref, val in outs:
            ref[d, j] = val[j * L:(j + 1) * L]
    dec_ref[d] = jnp.exp(btot)


def _gla_steps(qd_ref, kd_ref, v_ref, dec_ref, st_ref, o_refs, masks):
    L = CHUNK
    lo, hi, m_read = masks
    units = [(j, d, p) for j in range(BLOCK_NCH) for d in range(2) for p in range(GLA_PAIRS)]
    chunk = lambda j, d: j if d == 0 else BLOCK_NCH - 1 - j
    q_ms, kd_ms, v_st, sc, inc, o_intra = {}, {}, {}, {}, {}, {}
    for u in units:
        j, d, p = u
        c = chunk(j, d)
        sl = pl.ds(p * LANES, LANES)
        q_ms[u] = _masked_stack(qd_ref[d, c, :, sl], lo, hi)
        kd_ms[u] = _masked_stack(kd_ref[d, c, :, sl], lo, hi)
        v_st[u] = jnp.concatenate([v_ref[d, c, :, pl.ds((2 * p) * GLA_DV, GLA_DV)],
                                   v_ref[d, c, :, pl.ds((2 * p + 1) * GLA_DV, GLA_DV)]], axis=0)
        sc[u] = jnp.where(m_read[d], _mm_nt(q_ms[u], kd_ms[u]), 0.0)
        inc[u] = _mm_tn(v_st[u], kd_ms[u])
    for u in units:
        o_intra[u] = _mm(sc[u], v_st[u])
    for d in range(2):
        for p in range(GLA_PAIRS):
            sl = pl.ds(p * LANES, LANES)
            st = st_ref[d, p]
            for j in range(BLOCK_NCH):
                u = (j, d, p)
                c = chunk(j, d)
                o = o_intra[u] + _mm_nt(q_ms[u], st)
                o_refs[d][pl.ds(c * L, L), pl.ds((2 * p) * GLA_DV, GLA_DV)] = o[0:L].astype(BF16)
                o_refs[d][pl.ds(c * L, L), pl.ds((2 * p + 1) * GLA_DV, GLA_DV)] = o[L:2 * L].astype(BF16)
                st = (st + inc[u]) * dec_ref[d, pl.ds(c * 8, 1), sl]
            st_ref[d, p] = st


def _gla_body(zf_ref, zb_ref, upf_ref, bf_ref, upb_ref, bb_ref, trif_ref, trib_ref, sel_ref,
              of_ref, ob_ref, qd_ref, kd_ref, v_ref, dec_ref, st_ref):
    @pl.when(pl.program_id(1) == 0)
    def _():
        st_ref[...] = jnp.zeros_like(st_ref)

    scr = (qd_ref, kd_ref, v_ref, dec_ref)
    _gla_prep(zf_ref, upf_ref[...], bf_ref[...], trif_ref[...], sel_ref[...], 0, *scr)
    _gla_prep(zb_ref, upb_ref[...], bb_ref[...], trib_ref[...], sel_ref[...], 1, *scr)
    lo, hi = _lane_masks()
    masks = (lo, hi, (_pair_masks(True, inclusive=True), _pair_masks(False, inclusive=False)))

    _gla_steps(*scr, st_ref, (of_ref, ob_ref), masks)


def _gla(z, upf, bf, upb, bb, trif, trib, sel):
    bsz, s, _ = z.shape
    nb = s // SEQ_BLOCK
    fmap = lambda b, i: (b, i, 0)
    bmap = lambda b, i: (b, nb - 1 - i, 0)
    zspec = lambda m: pl.BlockSpec((None, SEQ_BLOCK, GLA_COLS), m)
    ospec = lambda m: pl.BlockSpec((None, SEQ_BLOCK, GLA_V), m)
    per_chunk = lambda rows, w: pltpu.VMEM((2, BLOCK_NCH, rows, w), F32)
    consts = (upf, bf, upb, bb, trif, trib, sel)
    return pl.pallas_call(
        _gla_body,
        grid=(bsz, nb),
        in_specs=[zspec(fmap), zspec(bmap)] + [_const_spec(a.shape) for a in consts],
        out_specs=[ospec(fmap), ospec(bmap)],
        out_shape=[jax.ShapeDtypeStruct((bsz, s, GLA_V), BF16)] * 2,
        scratch_shapes=[per_chunk(CHUNK, GLA_QK), per_chunk(CHUNK, GLA_QK), per_chunk(CHUNK, GLA_V),
                        pltpu.VMEM((2, BLOCK_NCH * 8, GLA_QK), F32),
                        pltpu.VMEM((2, GLA_PAIRS, GLA_DV, LANES), F32)],
        compiler_params=_params(("parallel", "arbitrary")),
        name="gla",
    )(z, z, *consts)


def _fnet_body(z_ref, cc_ref, sc_ref, cs_ref, o_ref, ab_ref, *, rows):
    s = z_ref.shape[0]
    z = z_ref[...]
    ab_ref[0:s, :] = jnp.dot(z, cc_ref[...], preferred_element_type=F32).astype(BF16)
    ab_ref[s:2 * s, :] = jnp.dot(z, sc_ref[...], preferred_element_type=F32).astype(BF16)
    for i in range(s // rows):
        o_ref[i * rows:(i + 1) * rows, :] = jnp.dot(
            cs_ref[i * rows:(i + 1) * rows, :], ab_ref[...], preferred_element_type=F32).astype(o_ref.dtype)


def _fnet(z, cc, sc, cs):
    bsz, s, w = z.shape
    rows = min(s, 512)
    blk = lambda: pl.BlockSpec((None, s, w), lambda b: (b, 0, 0))
    return pl.pallas_call(
        functools.partial(_fnet_body, rows=rows),
        grid=(bsz,),
        in_specs=[blk(), _const_spec(cc.shape), _const_spec(sc.shape), _const_spec(cs.shape)],
        out_specs=blk(),
        out_shape=jax.ShapeDtypeStruct((bsz, s, w), BF16),
        scratch_shapes=[pltpu.VMEM((2 * s, w), BF16)],
        compiler_params=_params(("parallel",)),
        name="fnet",
    )(z, cc, sc, cs)


RWKV_BLOCK = SEQ_BLOCK
RWKV_NCH = BLOCK_NCH
RWKV_PAIRS = RWKV_HEADS // 2


def _chunk(ref, j, sl):
    return ref[pl.ds(j * CHUNK, CHUNK), sl]


def _rwkv_solve_stages(units, ops, tinv_ref, pkb_ref, mkv_ref, masks):
    L = CHUNK
    n2 = 2 * L
    lo, hi, eye, m_strict, m_read = masks
    top = (lax.broadcasted_iota(jnp.int32, (2 * n2, n2), 0) & (n2 - 1)) < L
    t_inv, pw, s_kb, v_ms, m_k = {}, {}, {}, {}, {}
    for n, u in enumerate(units):
        d, j, p = u
        sl = pl.ds(p * LANES, LANES)
        kaph_ref, rh_ref, _, _, v_ref, _, kbcm_ref = ops[d]
        xs = jnp.concatenate([_masked_stack(_chunk(kaph_ref, j, sl), lo, hi),
                              _masked_stack(_chunk(rh_ref, j, sl), lo, hi)], axis=0)
        s_kb[u] = _mm(xs, kbcm_ref[pl.ds(j * RWKV_W + p * LANES, LANES), :])
        v_ms[u] = _masked_stack(_chunk(v_ref, j, sl), lo, hi)
        if n % 2:
            yield
    for n, u in enumerate(units):
        d, j, p = u
        s_sw = pltpu.roll(s_kb[u], L, 1)
        vs_k = jnp.where(top, s_kb[u], s_sw)
        vs_b = jnp.where(top, s_sw, s_kb[u])
        m_k[u] = jnp.where(m_strict[d], vs_k[0:n2], 0.0).astype(BF16)
        nmat = jnp.where(m_strict[d], -vs_b[0:n2], 0.0)
        p_k = jnp.where(m_read[d], vs_k[n2:2 * n2], 0.0)
        p_b = jnp.where(m_read[d], -vs_b[n2:2 * n2], 0.0)
        pkb_ref[d, j, p] = jnp.concatenate([p_k, p_b], axis=1).astype(BF16)
        t_inv[u] = eye + nmat
        pw[u] = nmat.astype(BF16)
        if n % 2:
            yield
    for n, u in enumerate(units):
        d, j, p = u
        mkv_ref[d, j, p] = _mm(m_k[u], v_ms[u])
        if n % 2:
            yield
    for n, u in enumerate(units):
        pw[u] = jnp.dot(pw[u], pw[u], preferred_element_type=F32).astype(BF16)
        if n % 2:
            yield
    for _ in range(4):
        for n, u in enumerate(units):
            both = jnp.dot(jnp.concatenate([t_inv[u].astype(BF16), pw[u]], axis=0), pw[u],
                           preferred_element_type=F32)
            t_inv[u] = t_inv[u] + both[0:n2]
            pw[u] = both[n2:2 * n2].astype(BF16)
            if n % 2:
                yield
    for n, u in enumerate(units):
        d, j, p = u
        tinv_ref[d, j, p] = (t_inv[u] + jnp.dot(t_inv[u].astype(BF16), pw[u],
                                                preferred_element_type=F32)).astype(BF16)
        if n % 2:
            yield


def _rwkv_scan_stages(steps, ops, tinv_ref, pkb_ref, mkv_ref, st_ref, y_refs, masks):
    L = CHUNK
    n2 = 2 * L
    lo, hi = masks[0], masks[1]
    units = [(d, p) for d in range(2) for p in range(RWKV_PAIRS)]
    for j in steps:
        cj = {0: j, 1: RWKV_NCH - 1 - j}
        v_ms, xh, uu = {}, {}, {}
        for (d, p) in units:
            sl = pl.ds(p * LANES, LANES)
            xs = jnp.concatenate([_masked_stack(_chunk(ops[d][0], cj[d], sl), lo, hi),
                                  _masked_stack(_chunk(ops[d][1], cj[d], sl), lo, hi)], axis=0)
            xh[d, p] = _mm_nt(xs, st_ref[d, p])
            yield
        for (d, p) in units:
            sl = pl.ds(p * LANES, LANES)
            v_ms[d, p] = _masked_stack(_chunk(ops[d][4], cj[d], sl), lo, hi)
            uu[d, p] = _mm(tinv_ref[d, cj[d], p], xh[d, p][0:n2] + mkv_ref[d, cj[d], p])
            yield
        for (d, p) in units:
            sl = pl.ds(p * LANES, LANES)
            y = xh[d, p][n2:2 * n2] + _mm(pkb_ref[d, cj[d], p], jnp.concatenate([v_ms[d, p], uu[d, p]], axis=0))
            y_refs[d][pl.ds(cj[d] * L, L), sl] = (y[0:L] + y[L:n2]).astype(BF16)
            kb = jnp.concatenate([_masked_stack(_chunk(ops[d][2], cj[d], sl), lo, hi),
                                  _masked_stack(_chunk(ops[d][3], cj[d], sl), lo, hi)], axis=0)
            dec = ops[d][5][pl.ds(cj[d] * 8, 1), sl]
            st_ref[d, p] = (st_ref[d, p] + _mm_tn(jnp.concatenate([v_ms[d, p], -uu[d, p]], axis=0), kb)) * dec
            yield


def _weave(*gens):
    live = list(gens)
    while live:
        for g in list(live):
            try:
                next(g)
            except StopIteration:
                live.remove(g)


def _rwkv_body(kaf_ref, rhf_ref, ktf_ref, btf_ref, vf_ref, decf_ref, kbf_ref,
               kab_ref, rhb_ref, ktb_ref, btb_ref, vb_ref, decb_ref, kbb_ref,
               yf_ref, yb_ref, tinv_ref, pkb_ref, mkv_ref, st_ref):
    @pl.when(pl.program_id(1) == 0)
    def _():
        st_ref[...] = jnp.zeros_like(st_ref)

    ops = ((kaf_ref, rhf_ref, ktf_ref, btf_ref, vf_ref, decf_ref, kbf_ref),
           (kab_ref, rhb_ref, ktb_ref, btb_ref, vb_ref, decb_ref, kbb_ref))
    lo, hi = _lane_masks()
    n2 = 2 * CHUNK
    ii = lax.broadcasted_iota(jnp.int32, (n2, n2), 0)
    jj = lax.broadcasted_iota(jnp.int32, (n2, n2), 1)
    eye = (ii == jj).astype(F32)
    m_strict = (_pair_masks(True, inclusive=False), _pair_masks(False, inclusive=False))
    m_read = (_pair_masks(True, inclusive=True), _pair_masks(False, inclusive=False))
    masks = (lo, hi, eye, m_strict, m_read)

    def solve(steps):
        units = [(d, j if d == 0 else RWKV_NCH - 1 - j, p)
                 for j in steps for d in range(2) for p in range(RWKV_PAIRS)]
        return _rwkv_solve_stages(units, ops, tinv_ref, pkb_ref, mkv_ref, masks)

    def scan(steps):
        return _rwkv_scan_stages(steps, ops, tinv_ref, pkb_ref, mkv_ref, st_ref, (yf_ref, yb_ref), masks)

    half = RWKV_NCH // 2
    first, second = list(range(half)), list(range(half, RWKV_NCH))
    _weave(solve(first))
    _weave(solve(second), scan(first))
    _weave(scan(second))


def _rwkv(fwd_ops, bwd_ops, bsz, s):
    R = RWKV_BLOCK
    nb = s // R
    fc = lambda i: i
    bc = lambda i: nb - 1 - i
    blk = lambda cm: pl.BlockSpec((None, R, RWKV_W), lambda b, i: (b, cm(i), 0))
    dec = lambda cm: pl.BlockSpec((None, RWKV_NCH * 8, RWKV_W), lambda b, i: (b, cm(i), 0))
    cm = lambda cm_: pl.BlockSpec((None, RWKV_NCH * RWKV_W, LANES), lambda b, i: (b, cm_(i), 0))
    seq = lambda a: a.reshape(bsz, -1, a.shape[-1])
    out = jax.ShapeDtypeStruct((bsz, s, RWKV_W), BF16)
    unit = (2, RWKV_NCH, RWKV_PAIRS)
    scratch = [pltpu.VMEM(unit + (LANES, LANES), BF16),
               pltpu.VMEM(unit + (LANES, 2 * LANES), BF16),
               pltpu.VMEM(unit + (LANES, LANES), F32),
               pltpu.VMEM((2, RWKV_PAIRS, LANES, LANES), F32)]
    return pl.pallas_call(
        _rwkv_body,
        grid=(bsz, nb),
        in_specs=[blk(fc)] * 5 + [dec(fc), cm(fc)] + [blk(bc)] * 5 + [dec(bc), cm(bc)],
        out_specs=[blk(fc), blk(bc)],
        out_shape=[out] * 2,
        scratch_shapes=scratch,
        compiler_params=_params(("parallel", "arbitrary")),
        name="rwkv",
    )(*[seq(a) for a in fwd_ops], *[seq(a) for a in bwd_ops])


def _merge_body(x_ref, gof_ref, gob_ref, gr_ref, fn_ref, ryf_ref, ryb_ref, rbon_ref, rg_ref, gate_ref,
                gn_ref, lng_ref, lnb_ref, gavg_ref, pg_ref, pf_ref, pr_ref, wo_ref, o_ref):
    o = gof_ref[...].astype(F32) + gob_ref[...].astype(F32)
    parts = []
    for h in range(GLA_HEADS):
        oh = o[:, h * GLA_DV:(h + 1) * GLA_DV]
        parts.append(oh * lax.rsqrt(jnp.mean(oh * oh, axis=-1, keepdims=True) + NORM_EPS))
    rg = gr_ref[...]
    y_a = jnp.concatenate(parts, axis=1) * gn_ref[...] * (rg * _sigmoid(rg))
    y = ryf_ref[...].astype(F32) + ryb_ref[...].astype(F32)
    mean = _group_sums(y, gavg_ref[...], 2)
    yc = y - mean
    var = _group_sums(yc * yc, gavg_ref[...], 1)
    y_c = (yc * lax.rsqrt(var + RWKV_LN_EPS) * lng_ref[...] + lnb_ref[...] + rbon_ref[...].astype(F32)) * rg_ref[...].astype(F32)
    d = x_ref.shape[1]
    gate = gate_ref[...].astype(F32)
    merged = (_sigmoid(gate[:, 0:d]) * _mm(y_a, pg_ref[...])
              + _sigmoid(gate[:, d:2 * d]) * _mm(fn_ref[...], pf_ref[...])
              + _sigmoid(gate[:, 2 * d:3 * d]) * _mm(y_c, pr_ref[...]))
    o_ref[...] = x_ref[...] + _mm(merged, wo_ref[...])


def _merge(x, gla_of, gla_ob, z_gla, y_fnet, r_yf, r_yb, r_bonus, r_g, z_gate,
           gn, lng, lnb, gavg, pg, pf, pr, wo, *, tm):
    t, d = x.shape
    row = lambda c: pl.BlockSpec((tm, c), lambda i: (i, 0))
    r_col = (2 * GLA_QK + GLA_V) // GLA_V
    gr_spec = pl.BlockSpec((tm, GLA_V), lambda i: (i, r_col))
    consts = (gn, lng, lnb, gavg, pg, pf, pr, wo)
    return pl.pallas_call(
        _merge_body,
        grid=(t // tm,),
        in_specs=[row(d), row(GLA_V), row(GLA_V), gr_spec, row(FNET_W), row(RWKV_W), row(RWKV_W),
                  row(RWKV_W), row(RWKV_W), row(GATE_COLS)] + [_const_spec(a.shape) for a in consts],
        out_specs=row(d),
        out_shape=jax.ShapeDtypeStruct((t, d), F32),
        compiler_params=_params(("parallel",)),
        name="merge",
    )(x, gla_of, gla_ob, z_gla, y_fnet, r_yf, r_yb, r_bonus, r_g, z_gate, *consts)


def _dft_tables(s):
    def cos_sin(n, cols):
        ang = (2.0 * jnp.pi / n) * ((jnp.arange(n, dtype=jnp.int32)[:, None] * cols[None, :]) % n).astype(F32)
        return jnp.cos(ang), jnp.sin(ang)

    def dft(n):
        step = 64 if n % 64 == 0 and n > 64 else n
        c_lo, s_lo = cos_sin(n, jnp.arange(step, dtype=jnp.int32))
        c_hi, s_hi = cos_sin(n, jnp.arange(n // step, dtype=jnp.int32) * step)
        cos = c_hi[:, :, None] * c_lo[:, None, :] - s_hi[:, :, None] * s_lo[:, None, :]
        sin = s_hi[:, :, None] * c_lo[:, None, :] + c_hi[:, :, None] * s_lo[:, None, :]
        return cos.reshape(n, n) * (n ** -0.5), sin.reshape(n, n) * (n ** -0.5)

    cg, sg = dft(FNET_GC)
    eye = jnp.eye(FNET_GROUPS, dtype=F32)
    cc = jnp.kron(eye, cg).astype(BF16)
    sc = jnp.kron(eye, sg).astype(BF16)
    cp, sp = dft(s)
    cs = jnp.concatenate([cp, -sp], axis=1).astype(BF16)
    return cc, sc, cs


def _tri(fwd, n):
    i = jnp.arange(n)
    same = (i[None, :] // CHUNK) == (i[:, None] // CHUNK)
    m = (i[None, :] <= i[:, None]) if fwd else (i[None, :] >= i[:, None])
    return (m & same).astype(BF16)


def _chunk_sel(n):
    return (jnp.arange(n // CHUNK * 8)[:, None] // 8 == jnp.arange(n)[None, :] // CHUNK).astype(BF16)


def _pad_rows(w, offset, total):
    return jnp.zeros((total, w.shape[1]), F32).at[offset:offset + w.shape[0]].set(w.astype(F32))


def kernel(x, ffn1_norm, ffn1_gate, ffn1_up, ffn1_down, mix_norm, w_in, gla_up_f, gla_bias_f, gla_up_b, gla_bias_b, gla_norm, rwkv_mu, rwkv_w0_f, rwkv_w2_f, rwkv_w0_b, rwkv_w2_b, rwkv_a0_f, rwkv_a2_f, rwkv_a0_b, rwkv_a2_b, rwkv_g2, rwkv_k_k, rwkv_k_a, rwkv_r_k, rwkv_ln_g, rwkv_ln_b, proj_gla, proj_fnet, proj_rwkv, w_out, ffn2_norm, ffn2_gate, ffn2_up, ffn2_down, final_norm):
    bsz, s, d = x.shape
    depth = w_in.shape[0]
    t = bsz * s
    tm = 256
    assert s % SEQ_BLOCK == 0 and t % tm == 0 and d * 3 == GATE_COLS

    cc, sc, cs = _dft_tables(s)
    trif_r, trib_r, sel_r = _tri(True, SEQ_BLOCK), _tri(False, SEQ_BLOCK), _chunk_sel(SEQ_BLOCK)
    head_of = jnp.arange(LANES) // RWKV_N
    same_head = head_of[:, None] == head_of[None, :]
    gsum = same_head.astype(BF16)
    gavg = (same_head.astype(F32) * (1.0 / RWKV_N)).astype(BF16)
    row = lambda a: a.astype(F32).reshape(1, -1)
    fnorm = row(final_norm)

    x2 = x.reshape(t, d)
    for l in range(depth):
        x2 = _ffn(x2, row(ffn1_norm[l]), ffn1_gate[l].astype(BF16), ffn1_up[l].astype(BF16),
                  ffn1_down[l].astype(BF16), fnorm, final=False, tm=tm)

        w = w_in[l]
        o_f, o_r, o_g = GLA_COLS_RAW, GLA_COLS_RAW + FNET_W, GLA_COLS_RAW + FNET_W + RWKV_COLS
        w_pad = jnp.concatenate([
            w[:, :o_f], jnp.zeros((d, GLA_COLS - GLA_COLS_RAW), w.dtype),
            w[:, o_f:o_r],
            w[:, o_r:o_g], jnp.zeros((d, RWKV_COLS_PAD - RWKV_COLS), w.dtype),
            w[:, o_g:]], axis=1).astype(BF16)
        mu = jnp.concatenate([rwkv_mu[l].astype(F32), jnp.zeros((RWKV_COLS_PAD - RWKV_COLS,), F32)]).reshape(1, -1)
        rwkv_consts = (mu, row(rwkv_w0_f[l]), row(rwkv_w0_b[l]), row(rwkv_a0_f[l]), row(rwkv_a0_b[l]),
                       jnp.concatenate([_pad_rows(rwkv_w2_f[l], 0, LANES), _pad_rows(rwkv_w2_b[l], 32, LANES)], axis=1),
                       jnp.concatenate([_pad_rows(rwkv_a2_f[l], 64, LANES), _pad_rows(rwkv_a2_b[l], 96, LANES)], axis=1),
                       _pad_rows(rwkv_g2[l], 0, LANES),
                       row(rwkv_k_k[l]), row(rwkv_k_a[l]), row(rwkv_r_k[l]), trif_r, trib_r, sel_r, gsum)
        (z_gla, z_fnet, z_gate, ka_f, rh_f, kt_f, bt_f, dec_f, kb_f, ka_b, rh_b, kt_b, bt_b, dec_b, kb_b,
         r_v, r_bonus, r_g) = _inproj(x2, row(mix_norm[l]), w_pad, rwkv_consts, seq=s)

        gla_of, gla_ob = _gla(
            z_gla.reshape(bsz, s, GLA_COLS),
            _pad_rows(gla_up_f[l], 0, LANES), row(gla_bias_f[l]),
            _pad_rows(gla_up_b[l], GLA_RANK, LANES), row(gla_bias_b[l]), trif_r, trib_r, sel_r)

        y_fnet = _fnet(z_fnet.reshape(bsz, s, FNET_W), cc, sc, cs)

        r_yf, r_yb = _rwkv((ka_f, rh_f, kt_f, bt_f, r_v, dec_f, kb_f), (ka_b, rh_b, kt_b, bt_b, r_v, dec_b, kb_b),
                           bsz, s)

        flat = lambda a: a.reshape(t, a.shape[-1])
        x2 = _merge(x2, flat(gla_of), flat(gla_ob), z_gla, flat(y_fnet), flat(r_yf), flat(r_yb),
                    r_bonus, r_g, z_gate,
                    row(gla_norm[l]), row(rwkv_ln_g[l]), row(rwkv_ln_b[l]), gavg,
                    proj_gla[l].astype(BF16), proj_fnet[l].astype(BF16), proj_rwkv[l].astype(BF16),
                    w_out[l].astype(BF16), tm=tm)

        x2 = _ffn(x2, row(ffn2_norm[l]), ffn2_gate[l].astype(BF16), ffn2_up[l].astype(BF16),
                  ffn2_down[l].astype(BF16), fnorm, final=(l == depth - 1), tm=tm)
    return x2.reshape(bsz, s, d)
```

```python
import functools

import jax
import jax.numpy as jnp
from jax import lax
from jax.experimental import pallas as pl
from jax.experimental.pallas import tpu as pltpu

F32 = jnp.float32
BF16 = jnp.bfloat16

NORM_EPS = 1e-6
RWKV_LN_EPS = 64e-5
GLA_TAU = 16.0

GLA_HEADS, GLA_DK, GLA_DV, GLA_RANK = 4, 64, 128, 16
GLA_QK, GLA_V = GLA_HEADS * GLA_DK, GLA_HEADS * GLA_DV
FNET_GROUPS, FNET_GC = 4, 128
FNET_W = FNET_GROUPS * FNET_GC
RWKV_HEADS, RWKV_N = 8, 64
RWKV_W = RWKV_HEADS * RWKV_N
RWKV_LOWRANK = 128
RWKV_GATE_RANK = 96
RWKV_COLS = 3 * RWKV_W + RWKV_LOWRANK + RWKV_GATE_RANK

LANES = 128
CHUNK = 64

GLA_COLS_RAW = 2 * GLA_QK + 2 * GLA_V + 2 * GLA_RANK
GLA_COLS = 13 * LANES
RWKV_COLS_PAD = 14 * LANES
GATE_COLS = 3 * 1024

VMEM_LIMIT = 56 * 1024 * 1024


def _mm(a, b):
    return jnp.dot(a.astype(BF16), b.astype(BF16), preferred_element_type=F32)


def _mm_nt(a, b):
    return lax.dot_general(a.astype(BF16), b.astype(BF16), (((1,), (1,)), ((), ())),
                           preferred_element_type=F32)


def _mm_tn(a, b):
    return lax.dot_general(a.astype(BF16), b.astype(BF16), (((0,), (0,)), ((), ())),
                           preferred_element_type=F32)


def _split(x, n):
    pieces = []
    for _ in range(n - 1):
        p = x.astype(BF16)
        pieces.append(p)
        x = x - p.astype(F32)
    pieces.append(x.astype(BF16))
    return pieces


def _group_sums(a, w, n):
    parts = _split(a, n)
    tiles = []
    for c in range(0, a.shape[1], LANES):
        out = None
        for p in parts:
            d = jnp.dot(p[:, c:c + LANES], w, preferred_element_type=F32)
            out = d if out is None else out + d
        tiles.append(out)
    return jnp.concatenate(tiles, axis=1)


def _mm_parts(w, parts):
    out = None
    for p in parts:
        d = jnp.dot(w, p, preferred_element_type=F32)
        out = d if out is None else out + d
    return out


def _mm_3x(a, b):
    a_hi, a_lo = _split(a, 2)
    b_hi, b_lo = _split(b, 2)
    return (jnp.dot(a_hi, b_hi, preferred_element_type=F32) + jnp.dot(a_lo, b_hi, preferred_element_type=F32)
            + jnp.dot(a_hi, b_lo, preferred_element_type=F32))


def _sigmoid(x):
    return 0.5 * jnp.tanh(0.5 * x) + 0.5


def _log_sigmoid(x):
    return jnp.minimum(x, 0.0) - jnp.log(1.0 + jnp.exp(-jnp.abs(x)))


def _rms(x, g):
    return x * lax.rsqrt(jnp.mean(x * x, axis=-1, keepdims=True) + NORM_EPS) * g


def _const_spec(shape):
    nd = len(shape)
    return pl.BlockSpec(shape, lambda *_: (0,) * nd, pipeline_mode=pl.Buffered(1))


def _params(sem):
    return pltpu.CompilerParams(dimension_semantics=sem, vmem_limit_bytes=VMEM_LIMIT)


def _ffn_body(x_ref, g_ref, wg_ref, wu_ref, wd_ref, fg_ref, o_ref, *, final):
    x = x_ref[...]
    h = _rms(x, g_ref[...]).astype(BF16)
    gate = jnp.dot(h, wg_ref[...], preferred_element_type=F32)
    up = jnp.dot(h, wu_ref[...], preferred_element_type=F32)
    act = (gate * _sigmoid(gate) * up).astype(BF16)
    y = x + 0.5 * jnp.dot(act, wd_ref[...], preferred_element_type=F32)
    if final:
        y = _rms(y, fg_ref[...])
    o_ref[...] = y


def _ffn(x, g, wg, wu, wd, fg, *, final, tm):
    t, d = x.shape
    f = wg.shape[1]
    row = pl.BlockSpec((tm, d), lambda i: (i, 0))
    return pl.pallas_call(
        functools.partial(_ffn_body, final=final),
        grid=(t // tm,),
        in_specs=[row, _const_spec((1, d)), _const_spec((d, f)), _const_spec((d, f)),
                  _const_spec((f, d)), _const_spec((1, d))],
        out_specs=row,
        out_shape=jax.ShapeDtypeStruct((t, d), F32),
        compiler_params=_params(("parallel",)),
        name="ffn",
    )(x, g, wg, wu, wd, fg)


SEQ_BLOCK = 256
BLOCK_NCH = SEQ_BLOCK // CHUNK
E_NEG_HALF = 2.718281828459045 ** -0.5


def _rwkv_feature_stages(u, prev_row, next_row, prm, dir_refs, v_ref, bonus_ref, rg_ref):
    R, W = SEQ_BLOCK, RWKV_W
    (mu, w0f, w0b, a0f, a0b, w2, a2, g2, k_k, k_a, r_k, trif, trib, sel, gsum) = prm
    row = lax.broadcasted_iota(jnp.int32, (8, 1), 0)
    prev = pltpu.roll(u, 1, 0)
    nxt = pltpu.roll(u, R - 1, 0)
    prev = jnp.concatenate([jnp.where(row == 0, prev_row, prev[0:8]), prev[8:]], axis=0)
    nxt = jnp.concatenate([nxt[:R - 8], jnp.where(row == 7, next_row, nxt[R - 8:])], axis=0)
    u = u * (1.0 - mu) + (prev + nxt) * (0.5 * mu)
    yield
    r, k, v = u[:, 0:W], u[:, W:2 * W], u[:, 2 * W:3 * W]
    sm = u[:, 3 * W:3 * W + LANES]
    gd = u[:, 3 * W + LANES:3 * W + 2 * LANES]
    v_ref[...] = v.astype(BF16)
    zw = _mm(jnp.tanh(sm), w2)
    za = _mm(sm, a2)
    rg_ref[...] = _mm(_sigmoid(gd), g2).astype(rg_ref.dtype)
    kk = k * k_k
    kap = kk * lax.rsqrt(_group_sums(kk * kk, gsum, 1) + 1e-12)
    yield
    dirs = ((True, w0f, a0f, trif, slice(0, W)), (False, w0b, a0b, trib, slice(W, 2 * W)))
    lw, kd, beta, inc, tot = [], [], [], [], []
    for fwd, w0, a0, tri, sl in dirs:
        lw.append((-E_NEG_HALF) * _sigmoid(w0 + zw[:, sl]))
        a = _sigmoid(a0 + za[:, sl])
        kd.append(k * (1.0 + (a - 1.0) * k_a))
        beta.append(kap * a)
        yield
    bonus_ref[...] = (_group_sums(r * kd[0] * r_k, gsum, 1) * v).astype(bonus_ref.dtype)
    for d, (fwd, w0, a0, tri, sl) in enumerate(dirs):
        lw_parts = _split(lw[d], 2)
        inc.append(_mm_parts(tri, lw_parts))
        tot.append(_mm_parts(sel, lw_parts))
        yield
    for d, (fwd, w0, a0, tri, sl) in enumerate(dirs):
        ka_ref, rh_ref, kt_ref, bt_ref, dec_ref, kb_ref = dir_refs[d]
        e_exc = jnp.exp(inc[d] - lw[d])
        e_ninc = jnp.exp(-inc[d])
        ka_ref[...] = (kap * e_exc).astype(BF16)
        rh_ref[...] = (r * (jnp.exp(inc[d]) if fwd else e_exc)).astype(BF16)
        dec_ref[...] = jnp.exp(tot[d])
        yield
        k_t, b_t = kd[d] * e_ninc, beta[d] * e_ninc
        kt_ref[...] = k_t.astype(BF16)
        bt_ref[...] = b_t.astype(BF16)
        yield
        for j in range(BLOCK_NCH):
            rows = slice(j * CHUNK, (j + 1) * CHUNK)
            kb_ref[j * W:(j + 1) * W, :] = jnp.concatenate([k_t[rows], b_t[rows]], axis=0).T.astype(BF16)
        yield


def _column_dots(h, w_ref, pieces, width):
    for off, cols, ref in pieces:
        for a in range(0, cols, width):
            b = min(a + width, cols)
            ref[:, a:b] = jnp.dot(h, w_ref[:, off + a:off + b], preferred_element_type=F32).astype(ref.dtype)
            yield


def _inproj_body(x_ref, xp_ref, xn_ref, g_ref, w_ref,
                 mu_ref, w0f_ref, w0b_ref, a0f_ref, a0b_ref, w2_ref, a2_ref, g2_ref, kk_ref, ka_ref, rk_ref,
                 trif_ref, trib_ref, sel_ref, gsum_ref,
                 gla_ref, fnet_ref, gate_ref,
                 kaf_ref, rhf_ref, ktf_ref, btf_ref, decf_ref, kbf_ref,
                 kab_ref, rhb_ref, ktb_ref, btb_ref, decb_ref, kbb_ref,
                 v_ref, bonus_ref, rg_ref, *, blocks_per_seq):
    i = pl.program_id(0)
    c0, c1, c2 = GLA_COLS, GLA_COLS + FNET_W, GLA_COLS + FNET_W + RWKV_COLS_PAD
    g = g_ref[...]
    h = _rms(x_ref[...], g).astype(BF16)
    halo = _rms(jnp.concatenate([xp_ref[...], xn_ref[...]], axis=0), g).astype(BF16)
    z_halo = jnp.dot(halo, w_ref[:, c1:c2], preferred_element_type=F32)
    pos = i % blocks_per_seq
    prev_row = jnp.where(pos > 0, z_halo[7:8, :], 0.0)
    next_row = jnp.where(pos < blocks_per_seq - 1, z_halo[8:9, :], 0.0)
    u = jnp.dot(h, w_ref[:, c1:c2], preferred_element_type=F32)
    prm = tuple(ref[...] for ref in (mu_ref, w0f_ref, w0b_ref, a0f_ref, a0b_ref, w2_ref, a2_ref, g2_ref,
                                     kk_ref, ka_ref, rk_ref, trif_ref, trib_ref, sel_ref, gsum_ref))
    dir_refs = ((kaf_ref, rhf_ref, ktf_ref, btf_ref, decf_ref, kbf_ref),
                (kab_ref, rhb_ref, ktb_ref, btb_ref, decb_ref, kbb_ref))
    _weave(_rwkv_feature_stages(u, prev_row, next_row, prm, dir_refs, v_ref, bonus_ref, rg_ref),
           _column_dots(h, w_ref, ((c2, GATE_COLS, gate_ref), (0, c0, gla_ref), (c0, FNET_W, fnet_ref)), 512))


def _inproj(x, g, w, rwkv_consts, *, seq):
    t, d = x.shape
    n = w.shape[1]
    tm = SEQ_BLOCK
    sub = 8
    per = tm // sub
    row = lambda c: pl.BlockSpec((tm, c), lambda i: (i, 0))
    prev = pl.BlockSpec((sub, d), lambda i: (jnp.maximum(i * per - 1, 0), 0))
    nxt = pl.BlockSpec((sub, d), lambda i: (jnp.minimum((i + 1) * per, t // sub - 1), 0))
    dec = pl.BlockSpec((BLOCK_NCH * 8, RWKV_W), lambda i: (i, 0))
    op = jax.ShapeDtypeStruct((t, RWKV_W), BF16)
    dec_shape = jax.ShapeDtypeStruct((t // CHUNK * 8, RWKV_W), F32)
    cm = pl.BlockSpec((BLOCK_NCH * RWKV_W, LANES), lambda i: (i, 0))
    cm_shape = jax.ShapeDtypeStruct((t // CHUNK * RWKV_W, LANES), BF16)
    per_dir = lambda a, b, c: [a] * 4 + [b, c]
    return pl.pallas_call(
        functools.partial(_inproj_body, blocks_per_seq=seq // tm),
        grid=(t // tm,),
        in_specs=[row(d), prev, nxt, _const_spec((1, d)), _const_spec((d, n))]
                 + [_const_spec(a.shape) for a in rwkv_consts],
        out_specs=[row(GLA_COLS), row(FNET_W), row(GATE_COLS)]
                  + per_dir(row(RWKV_W), dec, cm) * 2 + [row(RWKV_W)] * 3,
        out_shape=[jax.ShapeDtypeStruct((t, GLA_COLS), F32),
                   jax.ShapeDtypeStruct((t, FNET_W), BF16),
                   jax.ShapeDtypeStruct((t, GATE_COLS), BF16)]
                  + per_dir(op, dec_shape, cm_shape) * 2 + [op] * 3,
        compiler_params=_params(("parallel",)),
        name="inproj",
    )(x, x, x, g, w, *rwkv_consts)


def _weave(*gens):
    live = list(gens)
    while live:
        for g in list(live):
            try:
                next(g)
            except StopIteration:
                live.remove(g)


def _lane_masks():
    lane = lax.broadcasted_iota(jnp.int32, (CHUNK, LANES), 1)
    return lane < (LANES // 2), lane >= (LANES // 2)


def _masked_stack(x, lo, hi):
    return jnp.concatenate([jnp.where(lo, x, 0.0), jnp.where(hi, x, 0.0)], axis=0)


def _pair_masks(fwd, inclusive):
    n = 2 * CHUNK
    i = lax.broadcasted_iota(jnp.int32, (n, n), 0)
    j = lax.broadcasted_iota(jnp.int32, (n, n), 1)
    same = (i < CHUNK) == (j < CHUNK)
    strict = (j < i) if fwd else (j > i)
    if inclusive:
        return same & (strict | (i == j))
    return same & strict


GLA_PAIRS = GLA_HEADS // 2


def _gla_prep_stages(z_ref, up, bias, tri, sel, d, qd_ref, kd_ref, v_ref, dec_ref):
    L = CHUNK
    z = z_ref[...]
    q = z[:, 0:GLA_QK] * (GLA_DK ** -0.5)
    k = z[:, GLA_QK:2 * GLA_QK]
    v = z[:, 2 * GLA_QK:2 * GLA_QK + GLA_V]
    dn = z[:, GLA_COLS - LANES:GLA_COLS]
    for j in range(BLOCK_NCH):
        v_ref[d, j] = v[j * L:(j + 1) * L]
    la = _log_sigmoid(_mm(dn, up) + bias) * (1.0 / GLA_TAU)
    yield
    la_parts = _split(la, 2)
    b = _mm_parts(tri, la_parts)
    dec_ref[d] = jnp.exp(_mm_parts(sel, la_parts))
    yield
    for ref, val in ((qd_ref, q * jnp.exp(b)), (kd_ref, k * jnp.exp(-b))):
        for j in range(BLOCK_NCH):
            ref[d, j] = val[j * L:(j + 1) * L]
        yield


def _gla_step_stages(d, qd_ref, kd_ref, v_ref, dec_ref, st_ref, o_ref, masks):
    L = CHUNK
    lo, hi, m_read = masks
    units = [(j, p) for j in range(BLOCK_NCH) for p in range(GLA_PAIRS)]
    chunk = lambda j: j if d == 0 else BLOCK_NCH - 1 - j
    q_ms, kd_ms, v_st, sc, inc, o_intra = {}, {}, {}, {}, {}, {}
    for n, u in enumerate(units):
        j, p = u
        c = chunk(j)
        sl = pl.ds(p * LANES, LANES)
        q_ms[u] = _masked_stack(qd_ref[d, c, :, sl], lo, hi)
        kd_ms[u] = _masked_stack(kd_ref[d, c, :, sl], lo, hi)
        v_st[u] = jnp.concatenate([v_ref[d, c, :, pl.ds((2 * p) * GLA_DV, GLA_DV)],
                                   v_ref[d, c, :, pl.ds((2 * p + 1) * GLA_DV, GLA_DV)]], axis=0)
        sc[u] = jnp.where(m_read[d], _mm_nt(q_ms[u], kd_ms[u]), 0.0)
        inc[u] = _mm_tn(v_st[u], kd_ms[u])
        if n % 2:
            yield
    for n, u in enumerate(units):
        o_intra[u] = _mm(sc[u], v_st[u])
        if n % 2:
            yield
    for p in range(GLA_PAIRS):
        sl = pl.ds(p * LANES, LANES)
        st = st_ref[d, p]
        for j in range(BLOCK_NCH):
            u = (j, p)
            c = chunk(j)
            o = o_intra[u] + _mm_nt(q_ms[u], st)
            o_ref[pl.ds(c * L, L), pl.ds((2 * p) * GLA_DV, GLA_DV)] = o[0:L].astype(BF16)
            o_ref[pl.ds(c * L, L), pl.ds((2 * p + 1) * GLA_DV, GLA_DV)] = o[L:2 * L].astype(BF16)
            st = (st + inc[u]) * dec_ref[d, pl.ds(c * 8, 1), sl]
            if j % 2:
                yield
        st_ref[d, p] = st


def _gla_body(zf_ref, zb_ref, upf_ref, bf_ref, upb_ref, bb_ref, trif_ref, trib_ref, sel_ref,
              of_ref, ob_ref, qd_ref, kd_ref, v_ref, dec_ref, st_ref):
    @pl.when(pl.program_id(1) == 0)
    def _():
        st_ref[...] = jnp.zeros_like(st_ref)

    scr = (qd_ref, kd_ref, v_ref, dec_ref)
    lo, hi = _lane_masks()
    masks = (lo, hi, (_pair_masks(True, inclusive=True), _pair_masks(False, inclusive=False)))
    _weave(_gla_prep_stages(zf_ref, upf_ref[...], bf_ref[...], trif_ref[...], sel_ref[...], 0, *scr))
    _weave(_gla_prep_stages(zb_ref, upb_ref[...], bb_ref[...], trib_ref[...], sel_ref[...], 1, *scr),
           _gla_step_stages(0, *scr, st_ref, of_ref, masks))
    _weave(_gla_step_stages(1, *scr, st_ref, ob_ref, masks))


def _gla(z, upf, bf, upb, bb, trif, trib, sel):
    bsz, s, _ = z.shape
    nb = s // SEQ_BLOCK
    fmap = lambda b, i: (b, i, 0)
    bmap = lambda b, i: (b, nb - 1 - i, 0)
    zspec = lambda m: pl.BlockSpec((None, SEQ_BLOCK, GLA_COLS), m)
    ospec = lambda m: pl.BlockSpec((None, SEQ_BLOCK, GLA_V), m)
    per_chunk = lambda rows, w: pltpu.VMEM((2, BLOCK_NCH, rows, w), F32)
    consts = (upf, bf, upb, bb, trif, trib, sel)
    return pl.pallas_call(
        _gla_body,
        grid=(bsz, nb),
        in_specs=[zspec(fmap), zspec(bmap)] + [_const_spec(a.shape) for a in consts],
        out_specs=[ospec(fmap), ospec(bmap)],
        out_shape=[jax.ShapeDtypeStruct((bsz, s, GLA_V), BF16)] * 2,
        scratch_shapes=[per_chunk(CHUNK, GLA_QK), per_chunk(CHUNK, GLA_QK), per_chunk(CHUNK, GLA_V),
                        pltpu.VMEM((2, BLOCK_NCH * 8, GLA_QK), F32),
                        pltpu.VMEM((2, GLA_PAIRS, GLA_DV, LANES), F32)],
        compiler_params=_params(("parallel", "arbitrary")),
        name="gla",
    )(z, z, *consts)


def _fnet_body(z_ref, cc_ref, sc_ref, cp_ref, sp_ref, alt_ref, rev_ref, o_ref, a_ref, b_ref, m_ref, *, rows):
    s = z_ref.shape[0]
    h = s // 2
    z = z_ref[...]
    a_ref[...] = jnp.dot(z, cc_ref[...], preferred_element_type=F32).astype(BF16)
    b_ref[...] = jnp.dot(z, sc_ref[...], preferred_element_type=F32).astype(BF16)
    for i in range(h // rows):
        r = slice(i * rows, (i + 1) * rows)
        p = jnp.dot(cp_ref[r, :], a_ref[...], preferred_element_type=F32)
        q = jnp.dot(sp_ref[r, :], b_ref[...], preferred_element_type=F32)
        o_ref[r, :] = (p - q).astype(o_ref.dtype)
        m_ref[r, :] = (p + q).astype(BF16)
    for i in range(h // rows):
        r = slice(i * rows, (i + 1) * rows)
        o_ref[h + i * rows:h + (i + 1) * rows, :] = jnp.dot(
            rev_ref[r, :], m_ref[...], preferred_element_type=F32).astype(o_ref.dtype)
    mid = jnp.dot(alt_ref[...], a_ref[...], preferred_element_type=F32)
    o_ref[h:h + 1, :] = mid[0:1].astype(o_ref.dtype)


def _fnet(z, tables):
    bsz, s, w = z.shape
    rows = min(s // 2, 512)
    blk = lambda: pl.BlockSpec((None, s, w), lambda b: (b, 0, 0))
    return pl.pallas_call(
        functools.partial(_fnet_body, rows=rows),
        grid=(bsz,),
        in_specs=[blk()] + [_const_spec(a.shape) for a in tables],
        out_specs=blk(),
        out_shape=jax.ShapeDtypeStruct((bsz, s, w), BF16),
        scratch_shapes=[pltpu.VMEM((s, w), BF16), pltpu.VMEM((s, w), BF16), pltpu.VMEM((s // 2, w), BF16)],
        compiler_params=_params(("parallel",)),
        name="fnet",
    )(z, *tables)


RWKV_BLOCK = SEQ_BLOCK
RWKV_NCH = BLOCK_NCH
RWKV_PAIRS = RWKV_HEADS // 2


def _chunk(ref, j, sl):
    return ref[pl.ds(j * CHUNK, CHUNK), sl]


def _rwkv_solve_stages(units, ops, tinv_ref, pkb_ref, mkv_ref, masks):
    L = CHUNK
    n2 = 2 * L
    lo, hi, eye, m_strict, m_read = masks
    top = (lax.broadcasted_iota(jnp.int32, (2 * n2, n2), 0) & (n2 - 1)) < L
    t_inv, pw, s_kb, v_ms, m_k = {}, {}, {}, {}, {}
    for n, u in enumerate(units):
        d, j, p = u
        sl = pl.ds(p * LANES, LANES)
        kaph_ref, rh_ref, _, _, v_ref, _, kbcm_ref = ops[d]
        xs = jnp.concatenate([_masked_stack(_chunk(kaph_ref, j, sl), lo, hi),
                              _masked_stack(_chunk(rh_ref, j, sl), lo, hi)], axis=0)
        s_kb[u] = _mm(xs, kbcm_ref[pl.ds(j * RWKV_W + p * LANES, LANES), :])
        v_ms[u] = _masked_stack(_chunk(v_ref, j, sl), lo, hi)
        if n % 2:
            yield
    for n, u in enumerate(units):
        d, j, p = u
        s_sw = pltpu.roll(s_kb[u], L, 1)
        vs_k = jnp.where(top, s_kb[u], s_sw)
        vs_b = jnp.where(top, s_sw, s_kb[u])
        m_k[u] = jnp.where(m_strict[d], vs_k[0:n2], 0.0).astype(BF16)
        nmat = jnp.where(m_strict[d], -vs_b[0:n2], 0.0)
        p_k = jnp.where(m_read[d], vs_k[n2:2 * n2], 0.0)
        p_b = jnp.where(m_read[d], -vs_b[n2:2 * n2], 0.0)
        pkb_ref[d, j, p] = jnp.concatenate([p_k, p_b], axis=1).astype(BF16)
        t_inv[u] = eye + nmat
        pw[u] = nmat.astype(BF16)
        if n % 2:
            yield
    for n, u in enumerate(units):
        d, j, p = u
        mkv_ref[d, j, p] = _mm(m_k[u], v_ms[u])
        if n % 2:
            yield
    for n, u in enumerate(units):
        pw[u] = jnp.dot(pw[u], pw[u], preferred_element_type=F32).astype(BF16)
        if n % 2:
            yield
    for _ in range(4):
        for n, u in enumerate(units):
            both = jnp.dot(jnp.concatenate([t_inv[u].astype(BF16), pw[u]], axis=0), pw[u],
                           preferred_element_type=F32)
            t_inv[u] = t_inv[u] + both[0:n2]
            pw[u] = both[n2:2 * n2].astype(BF16)
            if n % 2:
                yield
    for n, u in enumerate(units):
        d, j, p = u
        tinv_ref[d, j, p] = (t_inv[u] + jnp.dot(t_inv[u].astype(BF16), pw[u],
                                                preferred_element_type=F32)).astype(BF16)
        if n % 2:
            yield


def _rwkv_scan_stages(steps, ops, tinv_ref, pkb_ref, mkv_ref, st_ref, y_refs, masks):
    L = CHUNK
    n2 = 2 * L
    lo, hi = masks[0], masks[1]
    units = [(d, p) for d in range(2) for p in range(RWKV_PAIRS)]
    for j in steps:
        cj = {0: j, 1: RWKV_NCH - 1 - j}
        v_ms, xh, uu = {}, {}, {}
        for (d, p) in units:
            sl = pl.ds(p * LANES, LANES)
            xs = jnp.concatenate([_masked_stack(_chunk(ops[d][0], cj[d], sl), lo, hi),
                                  _masked_stack(_chunk(ops[d][1], cj[d], sl), lo, hi)], axis=0)
            xh[d, p] = _mm_nt(xs, st_ref[d, p])
            yield
        for (d, p) in units:
            sl = pl.ds(p * LANES, LANES)
            v_ms[d, p] = _masked_stack(_chunk(ops[d][4], cj[d], sl), lo, hi)
            uu[d, p] = _mm(tinv_ref[d, cj[d], p], xh[d, p][0:n2] + mkv_ref[d, cj[d], p])
            yield
        for (d, p) in units:
            sl = pl.ds(p * LANES, LANES)
            y = xh[d, p][n2:2 * n2] + _mm(pkb_ref[d, cj[d], p], jnp.concatenate([v_ms[d, p], uu[d, p]], axis=0))
            y_refs[d][pl.ds(cj[d] * L, L), sl] = (y[0:L] + y[L:n2]).astype(BF16)
            kb = jnp.concatenate([_masked_stack(_chunk(ops[d][2], cj[d], sl), lo, hi),
                                  _masked_stack(_chunk(ops[d][3], cj[d], sl), lo, hi)], axis=0)
            dec = ops[d][5][pl.ds(cj[d] * 8, 1), sl]
            st_ref[d, p] = (st_ref[d, p] + _mm_tn(jnp.concatenate([v_ms[d, p], -uu[d, p]], axis=0), kb)) * dec
            yield


def _rwkv_body(kaf_ref, rhf_ref, ktf_ref, btf_ref, vf_ref, decf_ref, kbf_ref,
               kab_ref, rhb_ref, ktb_ref, btb_ref, vb_ref, decb_ref, kbb_ref,
               yf_ref, yb_ref, tinv_ref, pkb_ref, mkv_ref, st_ref):
    @pl.when(pl.program_id(1) == 0)
    def _():
        st_ref[...] = jnp.zeros_like(st_ref)

    ops = ((kaf_ref, rhf_ref, ktf_ref, btf_ref, vf_ref, decf_ref, kbf_ref),
           (kab_ref, rhb_ref, ktb_ref, btb_ref, vb_ref, decb_ref, kbb_ref))
    lo, hi = _lane_masks()
    n2 = 2 * CHUNK
    ii = lax.broadcasted_iota(jnp.int32, (n2, n2), 0)
    jj = lax.broadcasted_iota(jnp.int32, (n2, n2), 1)
    eye = (ii == jj).astype(F32)
    m_strict = (_pair_masks(True, inclusive=False), _pair_masks(False, inclusive=False))
    m_read = (_pair_masks(True, inclusive=True), _pair_masks(False, inclusive=False))
    masks = (lo, hi, eye, m_strict, m_read)

    def solve(steps):
        units = [(d, j if d == 0 else RWKV_NCH - 1 - j, p)
                 for j in steps for d in range(2) for p in range(RWKV_PAIRS)]
        return _rwkv_solve_stages(units, ops, tinv_ref, pkb_ref, mkv_ref, masks)

    def scan(steps):
        return _rwkv_scan_stages(steps, ops, tinv_ref, pkb_ref, mkv_ref, st_ref, (yf_ref, yb_ref), masks)

    half = RWKV_NCH // 2
    first, second = list(range(half)), list(range(half, RWKV_NCH))
    _weave(solve(first))
    _weave(solve(second), scan(first))
    _weave(scan(second))


def _rwkv(fwd_ops, bwd_ops, bsz, s):
    R = RWKV_BLOCK
    nb = s // R
    fc = lambda i: i
    bc = lambda i: nb - 1 - i
    blk = lambda cm: pl.BlockSpec((None, R, RWKV_W), lambda b, i: (b, cm(i), 0))
    dec = lambda cm: pl.BlockSpec((None, RWKV_NCH * 8, RWKV_W), lambda b, i: (b, cm(i), 0))
    cm = lambda cm_: pl.BlockSpec((None, RWKV_NCH * RWKV_W, LANES), lambda b, i: (b, cm_(i), 0))
    seq = lambda a: a.reshape(bsz, -1, a.shape[-1])
    out = jax.ShapeDtypeStruct((bsz, s, RWKV_W), BF16)
    unit = (2, RWKV_NCH, RWKV_PAIRS)
    scratch = [pltpu.VMEM(unit + (LANES, LANES), BF16),
               pltpu.VMEM(unit + (LANES, 2 * LANES), BF16),
               pltpu.VMEM(unit + (LANES, LANES), F32),
               pltpu.VMEM((2, RWKV_PAIRS, LANES, LANES), F32)]
    return pl.pallas_call(
        _rwkv_body,
        grid=(bsz, nb),
        in_specs=[blk(fc)] * 5 + [dec(fc), cm(fc)] + [blk(bc)] * 5 + [dec(bc), cm(bc)],
        out_specs=[blk(fc), blk(bc)],
        out_shape=[out] * 2,
        scratch_shapes=scratch,
        compiler_params=_params(("parallel", "arbitrary")),
        name="rwkv",
    )(*[seq(a) for a in fwd_ops], *[seq(a) for a in bwd_ops])


def _merge_body(x_ref, gof_ref, gob_ref, gr_ref, fn_ref, ryf_ref, ryb_ref, rbon_ref, rg_ref, gate_ref,
                gn_ref, lng_ref, lnb_ref, gavg_ref, pg_ref, pf_ref, pr_ref, wo_ref, o_ref):
    o = gof_ref[...].astype(F32) + gob_ref[...].astype(F32)
    parts = []
    for h in range(GLA_HEADS):
        oh = o[:, h * GLA_DV:(h + 1) * GLA_DV]
        parts.append(oh * lax.rsqrt(jnp.mean(oh * oh, axis=-1, keepdims=True) + NORM_EPS))
    rg = gr_ref[...]
    y_a = jnp.concatenate(parts, axis=1) * gn_ref[...] * (rg * _sigmoid(rg))
    y = ryf_ref[...].astype(F32) + ryb_ref[...].astype(F32)
    mean = _group_sums(y, gavg_ref[...], 2)
    yc = y - mean
    var = _group_sums(yc * yc, gavg_ref[...], 1)
    y_c = (yc * lax.rsqrt(var + RWKV_LN_EPS) * lng_ref[...] + lnb_ref[...] + rbon_ref[...].astype(F32)) * rg_ref[...].astype(F32)
    d = x_ref.shape[1]
    gate = gate_ref[...].astype(F32)
    merged = (_sigmoid(gate[:, 0:d]) * _mm(y_a, pg_ref[...])
              + _sigmoid(gate[:, d:2 * d]) * _mm(fn_ref[...], pf_ref[...])
              + _sigmoid(gate[:, 2 * d:3 * d]) * _mm(y_c, pr_ref[...]))
    o_ref[...] = x_ref[...] + _mm(merged, wo_ref[...])


def _merge(x, gla_of, gla_ob, z_gla, y_fnet, r_yf, r_yb, r_bonus, r_g, z_gate,
           gn, lng, lnb, gavg, pg, pf, pr, wo, *, tm):
    t, d = x.shape
    row = lambda c: pl.BlockSpec((tm, c), lambda i: (i, 0))
    r_col = (2 * GLA_QK + GLA_V) // GLA_V
    gr_spec = pl.BlockSpec((tm, GLA_V), lambda i: (i, r_col))
    consts = (gn, lng, lnb, gavg, pg, pf, pr, wo)
    return pl.pallas_call(
        _merge_body,
        grid=(t // tm,),
        in_specs=[row(d), row(GLA_V), row(GLA_V), gr_spec, row(FNET_W), row(RWKV_W), row(RWKV_W),
                  row(RWKV_W), row(RWKV_W), row(GATE_COLS)] + [_const_spec(a.shape) for a in consts],
        out_specs=row(d),
        out_shape=jax.ShapeDtypeStruct((t, d), F32),
        compiler_params=_params(("parallel",)),
        name="merge",
    )(x, gla_of, gla_ob, z_gla, y_fnet, r_yf, r_yb, r_bonus, r_g, z_gate, *consts)


def _dft_tables(s):
    def cos_sin(n, rows, cols):
        ang = (2.0 * jnp.pi / n) * ((jnp.arange(rows, dtype=jnp.int32)[:, None] * cols[None, :]) % n).astype(F32)
        return jnp.cos(ang), jnp.sin(ang)

    def dft(n, rows):
        step = 64 if n % 64 == 0 and n > 64 else n
        c_lo, s_lo = cos_sin(n, rows, jnp.arange(step, dtype=jnp.int32))
        c_hi, s_hi = cos_sin(n, rows, jnp.arange(n // step, dtype=jnp.int32) * step)
        cos = c_hi[:, :, None] * c_lo[:, None, :] - s_hi[:, :, None] * s_lo[:, None, :]
        sin = s_hi[:, :, None] * c_lo[:, None, :] + c_hi[:, :, None] * s_lo[:, None, :]
        return cos.reshape(rows, n) * (n ** -0.5), sin.reshape(rows, n) * (n ** -0.5)

    cg, sg = dft(FNET_GC, FNET_GC)
    eye = jnp.eye(FNET_GROUPS, dtype=F32)
    cc = jnp.kron(eye, cg).astype(BF16)
    sc = jnp.kron(eye, sg).astype(BF16)
    h = s // 2
    cp, sp = dft(s, h)
    m = jnp.arange(s)
    alt = jnp.broadcast_to(jnp.where(m % 2 == 0, 1.0, -1.0) * (s ** -0.5), (8, s))
    r = jnp.arange(h)
    rev = (r[None, :] == h - r[:, None]).astype(BF16)
    return cc, sc, cp.astype(BF16), sp.astype(BF16), alt.astype(BF16), rev


def _tri(fwd, n):
    i = jnp.arange(n)
    same = (i[None, :] // CHUNK) == (i[:, None] // CHUNK)
    m = (i[None, :] <= i[:, None]) if fwd else (i[None, :] >= i[:, None])
    return (m & same).astype(BF16)


def _chunk_sel(n):
    return (jnp.arange(n // CHUNK * 8)[:, None] // 8 == jnp.arange(n)[None, :] // CHUNK).astype(BF16)


def _pad_rows(w, offset, total):
    return jnp.zeros((total, w.shape[1]), F32).at[offset:offset + w.shape[0]].set(w.astype(F32))


def kernel(x, ffn1_norm, ffn1_gate, ffn1_up, ffn1_down, mix_norm, w_in, gla_up_f, gla_bias_f, gla_up_b, gla_bias_b, gla_norm, rwkv_mu, rwkv_w0_f, rwkv_w2_f, rwkv_w0_b, rwkv_w2_b, rwkv_a0_f, rwkv_a2_f, rwkv_a0_b, rwkv_a2_b, rwkv_g2, rwkv_k_k, rwkv_k_a, rwkv_r_k, rwkv_ln_g, rwkv_ln_b, proj_gla, proj_fnet, proj_rwkv, w_out, ffn2_norm, ffn2_gate, ffn2_up, ffn2_down, final_norm):
    bsz, s, d = x.shape
    depth = w_in.shape[0]
    t = bsz * s
    tm = 512 if t % 512 == 0 else 256
    assert s % SEQ_BLOCK == 0 and t % tm == 0 and d * 3 == GATE_COLS

    fnet_tables = _dft_tables(s)
    trif_r, trib_r, sel_r = _tri(True, SEQ_BLOCK), _tri(False, SEQ_BLOCK), _chunk_sel(SEQ_BLOCK)
    head_of = jnp.arange(LANES) // RWKV_N
    same_head = head_of[:, None] == head_of[None, :]
    gsum = same_head.astype(BF16)
    gavg = (same_head.astype(F32) * (1.0 / RWKV_N)).astype(BF16)
    row = lambda a: a.astype(F32).reshape(1, -1)
    fnorm = row(final_norm)

    x2 = x.reshape(t, d)
    for l in range(depth):
        x2 = _ffn(x2, row(ffn1_norm[l]), ffn1_gate[l].astype(BF16), ffn1_up[l].astype(BF16),
                  ffn1_down[l].astype(BF16), fnorm, final=False, tm=tm)

        w = w_in[l]
        o_f, o_r, o_g = GLA_COLS_RAW, GLA_COLS_RAW + FNET_W, GLA_COLS_RAW + FNET_W + RWKV_COLS
        w_pad = jnp.concatenate([
            w[:, :o_f], jnp.zeros((d, GLA_COLS - GLA_COLS_RAW), w.dtype),
            w[:, o_f:o_r],
            w[:, o_r:o_g], jnp.zeros((d, RWKV_COLS_PAD - RWKV_COLS), w.dtype),
            w[:, o_g:]], axis=1).astype(BF16)
        mu = jnp.concatenate([rwkv_mu[l].astype(F32), jnp.zeros((RWKV_COLS_PAD - RWKV_COLS,), F32)]).reshape(1, -1)
        rwkv_consts = (mu, row(rwkv_w0_f[l]), row(rwkv_w0_b[l]), row(rwkv_a0_f[l]), row(rwkv_a0_b[l]),
                       jnp.concatenate([_pad_rows(rwkv_w2_f[l], 0, LANES), _pad_rows(rwkv_w2_b[l], 32, LANES)], axis=1),
                       jnp.concatenate([_pad_rows(rwkv_a2_f[l], 64, LANES), _pad_rows(rwkv_a2_b[l], 96, LANES)], axis=1),
                       _pad_rows(rwkv_g2[l], 0, LANES),
                       row(rwkv_k_k[l]), row(rwkv_k_a[l]), row(rwkv_r_k[l]), trif_r, trib_r, sel_r, gsum)
        (z_gla, z_fnet, z_gate, ka_f, rh_f, kt_f, bt_f, dec_f, kb_f, ka_b, rh_b, kt_b, bt_b, dec_b, kb_b,
         r_v, r_bonus, r_g) = _inproj(x2, row(mix_norm[l]), w_pad, rwkv_consts, seq=s)

        gla_of, gla_ob = _gla(
            z_gla.reshape(bsz, s, GLA_COLS),
            _pad_rows(gla_up_f[l], 0, LANES), row(gla_bias_f[l]),
            _pad_rows(gla_up_b[l], GLA_RANK, LANES), row(gla_bias_b[l]), trif_r, trib_r, sel_r)

        y_fnet = _fnet(z_fnet.reshape(bsz, s, FNET_W), fnet_tables)

        r_yf, r_yb = _rwkv((ka_f, rh_f, kt_f, bt_f, r_v, dec_f, kb_f), (ka_b, rh_b, kt_b, bt_b, r_v, dec_b, kb_b),
                           bsz, s)

        flat = lambda a: a.reshape(t, a.shape[-1])
        x2 = _merge(x2, flat(gla_of), flat(gla_ob), z_gla, flat(y_fnet), flat(r_yf), flat(r_yb),
                    r_bonus, r_g, z_gate,
                    row(gla_norm[l]), row(rwkv_ln_g[l]), row(rwkv_ln_b[l]), gavg,
                    proj_gla[l].astype(BF16), proj_fnet[l].astype(BF16), proj_rwkv[l].astype(BF16),
                    w_out[l].astype(BF16), tm=tm)

        x2 = _ffn(x2, row(ffn2_norm[l]), ffn2_gate[l].astype(BF16), ffn2_up[l].astype(BF16),
                  ffn2_down[l].astype(BF16), fnorm, final=(l == depth - 1), tm=tm)
    return x2.reshape(bsz, s, d)
```

```python
import functools

import jax
import jax.numpy as jnp
from jax import lax
from jax.experimental import pallas as pl
from jax.experimental.pallas import tpu as pltpu

F32 = jnp.float32
BF16 = jnp.bfloat16

NORM_EPS = 1e-6
RWKV_LN_EPS = 64e-5
GLA_TAU = 16.0

GLA_HEADS, GLA_DK, GLA_DV, GLA_RANK = 4, 64, 128, 16
GLA_QK, GLA_V = GLA_HEADS * GLA_DK, GLA_HEADS * GLA_DV
FNET_GROUPS, FNET_GC = 4, 128
FNET_W = FNET_GROUPS * FNET_GC
RWKV_HEADS, RWKV_N = 8, 64
RWKV_W = RWKV_HEADS * RWKV_N
RWKV_LOWRANK = 128
RWKV_GATE_RANK = 96
RWKV_COLS = 3 * RWKV_W + RWKV_LOWRANK + RWKV_GATE_RANK

LANES = 128
CHUNK = 64

GLA_COLS_RAW = 2 * GLA_QK + 2 * GLA_V + 2 * GLA_RANK
GLA_COLS = 13 * LANES
RWKV_COLS_PAD = 14 * LANES
GATE_COLS = 3 * 1024

VMEM_LIMIT = 56 * 1024 * 1024


def _mm(a, b):
    return jnp.dot(a.astype(BF16), b.astype(BF16), preferred_element_type=F32)


def _mm_nt(a, b):
    return lax.dot_general(a.astype(BF16), b.astype(BF16), (((1,), (1,)), ((), ())),
                           preferred_element_type=F32)


def _mm_tn(a, b):
    return lax.dot_general(a.astype(BF16), b.astype(BF16), (((0,), (0,)), ((), ())),
                           preferred_element_type=F32)


def _split(x, n):
    pieces = []
    for _ in range(n - 1):
        p = x.astype(BF16)
        pieces.append(p)
        x = x - p.astype(F32)
    pieces.append(x.astype(BF16))
    return pieces


def _group_sums(a, w, n):
    parts = _split(a, n)
    tiles = []
    for c in range(0, a.shape[1], LANES):
        out = None
        for p in parts:
            d = jnp.dot(p[:, c:c + LANES], w, preferred_element_type=F32)
            out = d if out is None else out + d
        tiles.append(out)
    return jnp.concatenate(tiles, axis=1)


def _mm_parts(w, parts):
    out = None
    for p in parts:
        d = jnp.dot(w, p, preferred_element_type=F32)
        out = d if out is None else out + d
    return out


def _mm_3x(a, b):
    a_hi, a_lo = _split(a, 2)
    b_hi, b_lo = _split(b, 2)
    return (jnp.dot(a_hi, b_hi, preferred_element_type=F32) + jnp.dot(a_lo, b_hi, preferred_element_type=F32)
            + jnp.dot(a_hi, b_lo, preferred_element_type=F32))


def _sigmoid(x):
    return 0.5 * jnp.tanh(0.5 * x) + 0.5


def _log_sigmoid(x):
    return jnp.minimum(x, 0.0) - jnp.log(1.0 + jnp.exp(-jnp.abs(x)))


def _rms(x, g):
    return x * lax.rsqrt(jnp.mean(x * x, axis=-1, keepdims=True) + NORM_EPS) * g


def _const_spec(shape):
    nd = len(shape)
    return pl.BlockSpec(shape, lambda *_: (0,) * nd, pipeline_mode=pl.Buffered(1))


def _params(sem):
    return pltpu.CompilerParams(dimension_semantics=sem, vmem_limit_bytes=VMEM_LIMIT)


def _ffn_stages(rows, x_ref, g_ref, wg_ref, wu_ref, wd_ref, fg_ref, o_ref, final):
    x = x_ref[rows, :]
    h = _rms(x, g_ref[...]).astype(BF16)
    yield
    gate = jnp.dot(h, wg_ref[...], preferred_element_type=F32)
    up = jnp.dot(h, wu_ref[...], preferred_element_type=F32)
    yield
    act = (gate * _sigmoid(gate) * up).astype(BF16)
    y = x + 0.5 * jnp.dot(act, wd_ref[...], preferred_element_type=F32)
    if final:
        y = _rms(y, fg_ref[...])
    o_ref[rows, :] = y
    yield


def _ffn_body(x_ref, g_ref, wg_ref, wu_ref, wd_ref, fg_ref, o_ref, *, final):
    half = x_ref.shape[0] // 2
    args = (x_ref, g_ref, wg_ref, wu_ref, wd_ref, fg_ref, o_ref, final)
    _weave(_ffn_stages(slice(0, half), *args), _ffn_stages(slice(half, 2 * half), *args))


def _ffn(x, g, wg, wu, wd, fg, *, final, tm):
    t, d = x.shape
    f = wg.shape[1]
    row = pl.BlockSpec((tm, d), lambda i: (i, 0))
    return pl.pallas_call(
        functools.partial(_ffn_body, final=final),
        grid=(t // tm,),
        in_specs=[row, _const_spec((1, d)), _const_spec((d, f)), _const_spec((d, f)),
                  _const_spec((f, d)), _const_spec((1, d))],
        out_specs=row,
        out_shape=jax.ShapeDtypeStruct((t, d), F32),
        compiler_params=_params(("parallel",)),
        name="ffn",
    )(x, g, wg, wu, wd, fg)


SEQ_BLOCK = 256
BLOCK_NCH = SEQ_BLOCK // CHUNK
E_NEG_HALF = 2.718281828459045 ** -0.5


def _rwkv_feature_stages(u, prev_row, next_row, prm, dir_refs, v_ref, bonus_ref, rg_ref):
    R, W = SEQ_BLOCK, RWKV_W
    (mu, w0f, w0b, a0f, a0b, w2, a2, g2, k_k, k_a, r_k, trif, trib, sel, gsum) = prm
    row = lax.broadcasted_iota(jnp.int32, (8, 1), 0)
    prev = pltpu.roll(u, 1, 0)
    nxt = pltpu.roll(u, R - 1, 0)
    prev = jnp.concatenate([jnp.where(row == 0, prev_row, prev[0:8]), prev[8:]], axis=0)
    nxt = jnp.concatenate([nxt[:R - 8], jnp.where(row == 7, next_row, nxt[R - 8:])], axis=0)
    u = u * (1.0 - mu) + (prev + nxt) * (0.5 * mu)
    yield
    r, k, v = u[:, 0:W], u[:, W:2 * W], u[:, 2 * W:3 * W]
    sm = u[:, 3 * W:3 * W + LANES]
    gd = u[:, 3 * W + LANES:3 * W + 2 * LANES]
    v_ref[...] = v.astype(BF16)
    zw = _mm(jnp.tanh(sm), w2)
    za = _mm(sm, a2)
    rg_ref[...] = _mm(_sigmoid(gd), g2).astype(rg_ref.dtype)
    kk = k * k_k
    kap = kk * lax.rsqrt(_group_sums(kk * kk, gsum, 1) + 1e-12)
    yield
    dirs = ((True, w0f, a0f, trif, slice(0, W)), (False, w0b, a0b, trib, slice(W, 2 * W)))
    lw, kd, beta, inc, tot = [], [], [], [], []
    for fwd, w0, a0, tri, sl in dirs:
        lw.append((-E_NEG_HALF) * _sigmoid(w0 + zw[:, sl]))
        a = _sigmoid(a0 + za[:, sl])
        kd.append(k * (1.0 + (a - 1.0) * k_a))
        beta.append(kap * a)
        yield
    bonus_ref[...] = (_group_sums(r * kd[0] * r_k, gsum, 1) * v).astype(bonus_ref.dtype)
    for d, (fwd, w0, a0, tri, sl) in enumerate(dirs):
        lw_parts = _split(lw[d], 2)
        inc.append(_mm_parts(tri, lw_parts))
        tot.append(_mm_parts(sel, lw_parts))
        yield
    for d, (fwd, w0, a0, tri, sl) in enumerate(dirs):
        ka_ref, rh_ref, kt_ref, bt_ref, dec_ref, kb_ref = dir_refs[d]
        e_exc = jnp.exp(inc[d] - lw[d])
        e_ninc = jnp.exp(-inc[d])
        ka_ref[...] = (kap * e_exc).astype(BF16)
        rh_ref[...] = (r * (jnp.exp(inc[d]) if fwd else e_exc)).astype(BF16)
        dec_ref[...] = jnp.exp(tot[d])
        yield
        k_t, b_t = kd[d] * e_ninc, beta[d] * e_ninc
        kt_ref[...] = k_t.astype(BF16)
        bt_ref[...] = b_t.astype(BF16)
        yield
        for j in range(BLOCK_NCH):
            rows = slice(j * CHUNK, (j + 1) * CHUNK)
            kb_ref[j * W:(j + 1) * W, :] = jnp.concatenate([k_t[rows], b_t[rows]], axis=0).T.astype(BF16)
        yield


def _column_dots(h, w_ref, pieces, width):
    for off, cols, ref in pieces:
        for a in range(0, cols, width):
            b = min(a + width, cols)
            ref[:, a:b] = jnp.dot(h, w_ref[:, off + a:off + b], preferred_element_type=F32).astype(ref.dtype)
            yield


INPROJ_BLOCKS = 2


def _inproj_body(x_ref, xp_ref, xn_ref, g_ref, w_ref,
                 mu_ref, w0f_ref, w0b_ref, a0f_ref, a0b_ref, w2_ref, a2_ref, g2_ref, kk_ref, ka_ref, rk_ref,
                 trif_ref, trib_ref, sel_ref, gsum_ref,
                 gla_ref, fnet_ref, gate_ref,
                 kaf_ref, rhf_ref, ktf_ref, btf_ref, decf_ref, kbf_ref,
                 kab_ref, rhb_ref, ktb_ref, btb_ref, decb_ref, kbb_ref,
                 v_ref, bonus_ref, rg_ref, *, blocks_per_seq):
    R = SEQ_BLOCK
    c0, c1, c2 = GLA_COLS, GLA_COLS + FNET_W, GLA_COLS + FNET_W + RWKV_COLS_PAD
    g = g_ref[...]
    norm = lambda rows: _rms(rows, g).astype(BF16)
    rwkv_cols = lambda hh: jnp.dot(hh, w_ref[:, c1:c2], preferred_element_type=F32)
    z_halo = rwkv_cols(norm(jnp.concatenate([xp_ref[...], xn_ref[...], x_ref[R - 8:R + 8, :]], axis=0)))
    pos = (pl.program_id(0) * INPROJ_BLOCKS) % blocks_per_seq
    edge_rows = ((jnp.where(pos > 0, z_halo[7:8, :], 0.0), z_halo[24:25, :]),
                 (z_halo[23:24, :], jnp.where(pos + 1 < blocks_per_seq - 1, z_halo[8:9, :], 0.0)))
    prm = tuple(ref[...] for ref in (mu_ref, w0f_ref, w0b_ref, a0f_ref, a0b_ref, w2_ref, a2_ref, g2_ref,
                                     kk_ref, ka_ref, rk_ref, trif_ref, trib_ref, sel_ref, gsum_ref))

    def rows_of(ref, blk):
        n = ref.shape[0] // INPROJ_BLOCKS
        return ref.at[pl.ds(blk * n, n)]

    h, u = {}, {}
    h[0] = norm(x_ref[0:R, :])
    u[0] = rwkv_cols(h[0])

    def elementwise():
        for blk in range(INPROJ_BLOCKS):
            dir_refs = tuple(tuple(rows_of(ref, blk) for ref in refs) for refs in
                             ((kaf_ref, rhf_ref, ktf_ref, btf_ref, decf_ref, kbf_ref),
                              (kab_ref, rhb_ref, ktb_ref, btb_ref, decb_ref, kbb_ref)))
            yield from _rwkv_feature_stages(u[blk], *edge_rows[blk], prm, dir_refs, rows_of(v_ref, blk),
                                            rows_of(bonus_ref, blk), rows_of(rg_ref, blk))

    def matmuls():
        h[1] = norm(x_ref[R:2 * R, :])
        u[1] = rwkv_cols(h[1])
        yield
        for blk in range(INPROJ_BLOCKS):
            yield from _column_dots(h[blk], w_ref, ((c2, GATE_COLS, rows_of(gate_ref, blk)),
                                                    (0, c0, rows_of(gla_ref, blk)),
                                                    (c0, FNET_W, rows_of(fnet_ref, blk))), 512)

    _weave(elementwise(), matmuls())


def _inproj(x, g, w, rwkv_consts, *, seq):
    t, d = x.shape
    n = w.shape[1]
    nblk = INPROJ_BLOCKS
    tm = nblk * SEQ_BLOCK
    assert (seq // SEQ_BLOCK) % nblk == 0
    sub = 8
    per = tm // sub
    row = lambda c: pl.BlockSpec((tm, c), lambda i: (i, 0))
    prev = pl.BlockSpec((sub, d), lambda i: (jnp.maximum(i * per - 1, 0), 0))
    nxt = pl.BlockSpec((sub, d), lambda i: (jnp.minimum((i + 1) * per, t // sub - 1), 0))
    dec = pl.BlockSpec((nblk * BLOCK_NCH * 8, RWKV_W), lambda i: (i, 0))
    op = jax.ShapeDtypeStruct((t, RWKV_W), BF16)
    dec_shape = jax.ShapeDtypeStruct((t // CHUNK * 8, RWKV_W), F32)
    cm = pl.BlockSpec((nblk * BLOCK_NCH * RWKV_W, LANES), lambda i: (i, 0))
    cm_shape = jax.ShapeDtypeStruct((t // CHUNK * RWKV_W, LANES), BF16)
    per_dir = lambda a, b, c: [a] * 4 + [b, c]
    return pl.pallas_call(
        functools.partial(_inproj_body, blocks_per_seq=seq // SEQ_BLOCK),
        grid=(t // tm,),
        in_specs=[row(d), prev, nxt, _const_spec((1, d)), _const_spec((d, n))]
                 + [_const_spec(a.shape) for a in rwkv_consts],
        out_specs=[row(GLA_COLS), row(FNET_W), row(GATE_COLS)]
                  + per_dir(row(RWKV_W), dec, cm) * 2 + [row(RWKV_W)] * 3,
        out_shape=[jax.ShapeDtypeStruct((t, GLA_COLS), BF16),
                   jax.ShapeDtypeStruct((t, FNET_W), BF16),
                   jax.ShapeDtypeStruct((t, GATE_COLS), BF16)]
                  + per_dir(op, dec_shape, cm_shape) * 2 + [op] * 3,
        compiler_params=_params(("parallel",)),
        name="inproj",
    )(x, x, x, g, w, *rwkv_consts)


def _weave(*gens):
    live = list(gens)
    while live:
        for g in list(live):
            try:
                next(g)
            except StopIteration:
                live.remove(g)


def _lane_masks():
    lane = lax.broadcasted_iota(jnp.int32, (CHUNK, LANES), 1)
    return lane < (LANES // 2), lane >= (LANES // 2)


def _masked_stack(x, lo, hi):
    return jnp.concatenate([jnp.where(lo, x, 0.0), jnp.where(hi, x, 0.0)], axis=0)


def _pair_masks(fwd, inclusive):
    n = 2 * CHUNK
    i = lax.broadcasted_iota(jnp.int32, (n, n), 0)
    j = lax.broadcasted_iota(jnp.int32, (n, n), 1)
    same = (i < CHUNK) == (j < CHUNK)
    strict = (j < i) if fwd else (j > i)
    if inclusive:
        return same & (strict | (i == j))
    return same & strict


GLA_PAIRS = GLA_HEADS // 2


def _gla_prep_stages(z_ref, up, bias, tri, sel, d, qd_ref, kd_ref, v_ref, dec_ref):
    L = CHUNK
    z = z_ref[...].astype(F32)
    q = z[:, 0:GLA_QK] * (GLA_DK ** -0.5)
    k = z[:, GLA_QK:2 * GLA_QK]
    v = z[:, 2 * GLA_QK:2 * GLA_QK + GLA_V]
    dn = z[:, GLA_COLS - LANES:GLA_COLS]
    for j in range(BLOCK_NCH):
        v_ref[d, j] = v[j * L:(j + 1) * L]
    la = _log_sigmoid(_mm(dn, up) + bias) * (1.0 / GLA_TAU)
    yield
    la_parts = _split(la, 2)
    b = _mm_parts(tri, la_parts)
    dec_ref[d] = jnp.exp(_mm_parts(sel, la_parts))
    yield
    for ref, val in ((qd_ref, q * jnp.exp(b)), (kd_ref, k * jnp.exp(-b))):
        for j in range(BLOCK_NCH):
            ref[d, j] = val[j * L:(j + 1) * L]
        yield


def _gla_step_stages(d, qd_ref, kd_ref, v_ref, dec_ref, st_ref, o_ref, masks):
    L = CHUNK
    lo, hi, m_read = masks
    units = [(j, p) for j in range(BLOCK_NCH) for p in range(GLA_PAIRS)]
    chunk = lambda j: j if d == 0 else BLOCK_NCH - 1 - j
    q_ms, kd_ms, v_st, sc, inc, o_intra = {}, {}, {}, {}, {}, {}
    for n, u in enumerate(units):
        j, p = u
        c = chunk(j)
        sl = pl.ds(p * LANES, LANES)
        q_ms[u] = _masked_stack(qd_ref[d, c, :, sl], lo, hi)
        kd_ms[u] = _masked_stack(kd_ref[d, c, :, sl], lo, hi)
        v_st[u] = jnp.concatenate([v_ref[d, c, :, pl.ds((2 * p) * GLA_DV, GLA_DV)],
                                   v_ref[d, c, :, pl.ds((2 * p + 1) * GLA_DV, GLA_DV)]], axis=0)
        sc[u] = jnp.where(m_read[d], _mm_nt(q_ms[u], kd_ms[u]), 0.0)
        inc[u] = _mm_tn(v_st[u], kd_ms[u])
        if n % 2:
            yield
    for n, u in enumerate(units):
        o_intra[u] = _mm(sc[u], v_st[u])
        if n % 2:
            yield
    for p in range(GLA_PAIRS):
        sl = pl.ds(p * LANES, LANES)
        st = st_ref[d, p]
        for j in range(BLOCK_NCH):
            u = (j, p)
            c = chunk(j)
            o = o_intra[u] + _mm_nt(q_ms[u], st)
            o_ref[pl.ds(c * L, L), pl.ds((2 * p) * GLA_DV, GLA_DV)] = o[0:L].astype(BF16)
            o_ref[pl.ds(c * L, L), pl.ds((2 * p + 1) * GLA_DV, GLA_DV)] = o[L:2 * L].astype(BF16)
            st = (st + inc[u]) * dec_ref[d, pl.ds(c * 8, 1), sl]
            if j % 2:
                yield
        st_ref[d, p] = st


def _gla_body(zf_ref, zb_ref, upf_ref, bf_ref, upb_ref, bb_ref, trif_ref, trib_ref, sel_ref,
              of_ref, ob_ref, qd_ref, kd_ref, v_ref, dec_ref, st_ref):
    @pl.when(pl.program_id(1) == 0)
    def _():
        st_ref[...] = jnp.zeros_like(st_ref)

    scr = (qd_ref, kd_ref, v_ref, dec_ref)
    lo, hi = _lane_masks()
    masks = (lo, hi, (_pair_masks(True, inclusive=True), _pair_masks(False, inclusive=False)))
    _weave(_gla_prep_stages(zf_ref, upf_ref[...], bf_ref[...], trif_ref[...], sel_ref[...], 0, *scr))
    _weave(_gla_prep_stages(zb_ref, upb_ref[...], bb_ref[...], trib_ref[...], sel_ref[...], 1, *scr),
           _gla_step_stages(0, *scr, st_ref, of_ref, masks))
    _weave(_gla_step_stages(1, *scr, st_ref, ob_ref, masks))


def _gla(z, upf, bf, upb, bb, trif, trib, sel):
    bsz, s, _ = z.shape
    nb = s // SEQ_BLOCK
    fmap = lambda b, i: (b, i, 0)
    bmap = lambda b, i: (b, nb - 1 - i, 0)
    zspec = lambda m: pl.BlockSpec((None, SEQ_BLOCK, GLA_COLS), m)
    ospec = lambda m: pl.BlockSpec((None, SEQ_BLOCK, GLA_V), m)
    per_chunk = lambda rows, w: pltpu.VMEM((2, BLOCK_NCH, rows, w), F32)
    consts = (upf, bf, upb, bb, trif, trib, sel)
    return pl.pallas_call(
        _gla_body,
        grid=(bsz, nb),
        in_specs=[zspec(fmap), zspec(bmap)] + [_const_spec(a.shape) for a in consts],
        out_specs=[ospec(fmap), ospec(bmap)],
        out_shape=[jax.ShapeDtypeStruct((bsz, s, GLA_V), BF16)] * 2,
        scratch_shapes=[per_chunk(CHUNK, GLA_QK), per_chunk(CHUNK, GLA_QK), per_chunk(CHUNK, GLA_V),
                        pltpu.VMEM((2, BLOCK_NCH * 8, GLA_QK), F32),
                        pltpu.VMEM((2, GLA_PAIRS, GLA_DV, LANES), F32)],
        compiler_params=_params(("parallel", "arbitrary")),
        name="gla",
    )(z, z, *consts)


def _fnet_body(z_ref, cc_ref, sc_ref, cp_ref, sp_ref, alt_ref, rev_ref, o_ref, a_ref, b_ref, m_ref, *, rows):
    s = z_ref.shape[0]
    h = s // 2
    z = z_ref[...]
    a_ref[...] = jnp.dot(z, cc_ref[...], preferred_element_type=F32).astype(BF16)
    b_ref[...] = jnp.dot(z, sc_ref[...], preferred_element_type=F32).astype(BF16)
    for i in range(h // rows):
        r = slice(i * rows, (i + 1) * rows)
        p = jnp.dot(cp_ref[r, :], a_ref[...], preferred_element_type=F32)
        q = jnp.dot(sp_ref[r, :], b_ref[...], preferred_element_type=F32)
        o_ref[r, :] = (p - q).astype(o_ref.dtype)
        m_ref[r, :] = (p + q).astype(BF16)
    for i in range(h // rows):
        r = slice(i * rows, (i + 1) * rows)
        o_ref[h + i * rows:h + (i + 1) * rows, :] = jnp.dot(
            rev_ref[r, :], m_ref[...], preferred_element_type=F32).astype(o_ref.dtype)
    mid = jnp.dot(alt_ref[...], a_ref[...], preferred_element_type=F32)
    o_ref[h:h + 1, :] = mid[0:1].astype(o_ref.dtype)


def _fnet(z, tables):
    bsz, s, w = z.shape
    rows = min(s // 2, 512)
    blk = lambda: pl.BlockSpec((None, s, w), lambda b: (b, 0, 0))
    return pl.pallas_call(
        functools.partial(_fnet_body, rows=rows),
        grid=(bsz,),
        in_specs=[blk()] + [_const_spec(a.shape) for a in tables],
        out_specs=blk(),
        out_shape=jax.ShapeDtypeStruct((bsz, s, w), BF16),
        scratch_shapes=[pltpu.VMEM((s, w), BF16), pltpu.VMEM((s, w), BF16), pltpu.VMEM((s // 2, w), BF16)],
        compiler_params=_params(("parallel",)),
        name="fnet",
    )(z, *tables)


RWKV_BLOCK = SEQ_BLOCK
RWKV_NCH = BLOCK_NCH
RWKV_PAIRS = RWKV_HEADS // 2


def _chunk(ref, j, sl):
    return ref[pl.ds(j * CHUNK, CHUNK), sl]


def _rwkv_solve_stages(units, ops, tinv_ref, pkb_ref, mkv_ref, masks):
    L = CHUNK
    n2 = 2 * L
    lo, hi, eye, m_strict, m_read = masks
    top = (lax.broadcasted_iota(jnp.int32, (2 * n2, n2), 0) & (n2 - 1)) < L
    t_inv, pw, s_kb, v_ms, m_k = {}, {}, {}, {}, {}
    for n, u in enumerate(units):
        d, j, p = u
        sl = pl.ds(p * LANES, LANES)
        kaph_ref, rh_ref, _, _, v_ref, _, kbcm_ref = ops[d]
        xs = jnp.concatenate([_masked_stack(_chunk(kaph_ref, j, sl), lo, hi),
                              _masked_stack(_chunk(rh_ref, j, sl), lo, hi)], axis=0)
        s_kb[u] = _mm(xs, kbcm_ref[pl.ds(j * RWKV_W + p * LANES, LANES), :])
        v_ms[u] = _masked_stack(_chunk(v_ref, j, sl), lo, hi)
        if n % 2:
            yield
    for n, u in enumerate(units):
        d, j, p = u
        s_sw = pltpu.roll(s_kb[u], L, 1)
        vs_k = jnp.where(top, s_kb[u], s_sw)
        vs_b = jnp.where(top, s_sw, s_kb[u])
        m_k[u] = jnp.where(m_strict[d], vs_k[0:n2], 0.0).astype(BF16)
        nmat = jnp.where(m_strict[d], -vs_b[0:n2], 0.0)
        p_k = jnp.where(m_read[d], vs_k[n2:2 * n2], 0.0)
        p_b = jnp.where(m_read[d], -vs_b[n2:2 * n2], 0.0)
        pkb_ref[d, j, p] = jnp.concatenate([p_k, p_b], axis=1).astype(BF16)
        t_inv[u] = eye + nmat
        pw[u] = nmat.astype(BF16)
        if n % 2:
            yield
    for n, u in enumerate(units):
        d, j, p = u
        mkv_ref[d, j, p] = _mm(m_k[u], v_ms[u])
        if n % 2:
            yield
    for n, u in enumerate(units):
        pw[u] = jnp.dot(pw[u], pw[u], preferred_element_type=F32).astype(BF16)
        if n % 2:
            yield
    for _ in range(4):
        for n, u in enumerate(units):
            both = jnp.dot(jnp.concatenate([t_inv[u].astype(BF16), pw[u]], axis=0), pw[u],
                           preferred_element_type=F32)
            t_inv[u] = t_inv[u] + both[0:n2]
            pw[u] = both[n2:2 * n2].astype(BF16)
            if n % 2:
                yield
    for n, u in enumerate(units):
        d, j, p = u
        tinv_ref[d, j, p] = (t_inv[u] + jnp.dot(t_inv[u].astype(BF16), pw[u],
                                                preferred_element_type=F32)).astype(BF16)
        if n % 2:
            yield


def _rwkv_scan_stages(steps, ops, tinv_ref, pkb_ref, mkv_ref, st_ref, y_refs, masks):
    L = CHUNK
    n2 = 2 * L
    lo, hi = masks[0], masks[1]
    units = [(d, p) for d in range(2) for p in range(RWKV_PAIRS)]
    for j in steps:
        cj = {0: j, 1: RWKV_NCH - 1 - j}
        v_ms, xh, uu = {}, {}, {}
        for (d, p) in units:
            sl = pl.ds(p * LANES, LANES)
            xs = jnp.concatenate([_masked_stack(_chunk(ops[d][0], cj[d], sl), lo, hi),
                                  _masked_stack(_chunk(ops[d][1], cj[d], sl), lo, hi)], axis=0)
            xh[d, p] = _mm_nt(xs, st_ref[d, p])
            yield
        for (d, p) in units:
            sl = pl.ds(p * LANES, LANES)
            v_ms[d, p] = _masked_stack(_chunk(ops[d][4], cj[d], sl), lo, hi)
            uu[d, p] = _mm(tinv_ref[d, cj[d], p], xh[d, p][0:n2] + mkv_ref[d, cj[d], p])
            yield
        for (d, p) in units:
            sl = pl.ds(p * LANES, LANES)
            y = xh[d, p][n2:2 * n2] + _mm(pkb_ref[d, cj[d], p], jnp.concatenate([v_ms[d, p], uu[d, p]], axis=0))
            y_refs[d][pl.ds(cj[d] * L, L), sl] = (y[0:L] + y[L:n2]).astype(BF16)
            kb = jnp.concatenate([_masked_stack(_chunk(ops[d][2], cj[d], sl), lo, hi),
                                  _masked_stack(_chunk(ops[d][3], cj[d], sl), lo, hi)], axis=0)
            dec = ops[d][5][pl.ds(cj[d] * 8, 1), sl]
            st_ref[d, p] = (st_ref[d, p] + _mm_tn(jnp.concatenate([v_ms[d, p], -uu[d, p]], axis=0), kb)) * dec
            yield


def _rwkv_body(kaf_ref, rhf_ref, ktf_ref, btf_ref, vf_ref, decf_ref, kbf_ref,
               kab_ref, rhb_ref, ktb_ref, btb_ref, vb_ref, decb_ref, kbb_ref,
               yf_ref, yb_ref, tinv_ref, pkb_ref, mkv_ref, st_ref):
    @pl.when(pl.program_id(1) == 0)
    def _():
        st_ref[...] = jnp.zeros_like(st_ref)

    ops = ((kaf_ref, rhf_ref, ktf_ref, btf_ref, vf_ref, decf_ref, kbf_ref),
           (kab_ref, rhb_ref, ktb_ref, btb_ref, vb_ref, decb_ref, kbb_ref))
    lo, hi = _lane_masks()
    n2 = 2 * CHUNK
    ii = lax.broadcasted_iota(jnp.int32, (n2, n2), 0)
    jj = lax.broadcasted_iota(jnp.int32, (n2, n2), 1)
    eye = (ii == jj).astype(F32)
    m_strict = (_pair_masks(True, inclusive=False), _pair_masks(False, inclusive=False))
    m_read = (_pair_masks(True, inclusive=True), _pair_masks(False, inclusive=False))
    masks = (lo, hi, eye, m_strict, m_read)

    def solve(steps):
        units = [(d, j if d == 0 else RWKV_NCH - 1 - j, p)
                 for j in steps for d in range(2) for p in range(RWKV_PAIRS)]
        return _rwkv_solve_stages(units, ops, tinv_ref, pkb_ref, mkv_ref, masks)

    def scan(steps):
        return _rwkv_scan_stages(steps, ops, tinv_ref, pkb_ref, mkv_ref, st_ref, (yf_ref, yb_ref), masks)

    half = RWKV_NCH // 2
    first, second = list(range(half)), list(range(half, RWKV_NCH))
    _weave(solve(first))
    _weave(solve(second), scan(first))
    _weave(scan(second))


def _rwkv(fwd_ops, bwd_ops, bsz, s):
    R = RWKV_BLOCK
    nb = s // R
    fc = lambda i: i
    bc = lambda i: nb - 1 - i
    blk = lambda cm: pl.BlockSpec((None, R, RWKV_W), lambda b, i: (b, cm(i), 0))
    dec = lambda cm: pl.BlockSpec((None, RWKV_NCH * 8, RWKV_W), lambda b, i: (b, cm(i), 0))
    cm = lambda cm_: pl.BlockSpec((None, RWKV_NCH * RWKV_W, LANES), lambda b, i: (b, cm_(i), 0))
    seq = lambda a: a.reshape(bsz, -1, a.shape[-1])
    out = jax.ShapeDtypeStruct((bsz, s, RWKV_W), BF16)
    unit = (2, RWKV_NCH, RWKV_PAIRS)
    scratch = [pltpu.VMEM(unit + (LANES, LANES), BF16),
               pltpu.VMEM(unit + (LANES, 2 * LANES), BF16),
               pltpu.VMEM(unit + (LANES, LANES), F32),
               pltpu.VMEM((2, RWKV_PAIRS, LANES, LANES), F32)]
    return pl.pallas_call(
        _rwkv_body,
        grid=(bsz, nb),
        in_specs=[blk(fc)] * 5 + [dec(fc), cm(fc)] + [blk(bc)] * 5 + [dec(bc), cm(bc)],
        out_specs=[blk(fc), blk(bc)],
        out_shape=[out] * 2,
        scratch_shapes=scratch,
        compiler_params=_params(("parallel", "arbitrary")),
        name="rwkv",
    )(*[seq(a) for a in fwd_ops], *[seq(a) for a in bwd_ops])


def _merge_body(x_ref, gof_ref, gob_ref, gr_ref, fn_ref, ryf_ref, ryb_ref, rbon_ref, rg_ref, gate_ref,
                gn_ref, lng_ref, lnb_ref, gavg_ref, pg_ref, pf_ref, pr_ref, wo_ref, o_ref):
    o = gof_ref[...].astype(F32) + gob_ref[...].astype(F32)
    parts = []
    for h in range(GLA_HEADS):
        oh = o[:, h * GLA_DV:(h + 1) * GLA_DV]
        parts.append(oh * lax.rsqrt(jnp.mean(oh * oh, axis=-1, keepdims=True) + NORM_EPS))
    rg = gr_ref[...].astype(F32)
    y_a = jnp.concatenate(parts, axis=1) * gn_ref[...] * (rg * _sigmoid(rg))
    y = ryf_ref[...].astype(F32) + ryb_ref[...].astype(F32)
    mean = _group_sums(y, gavg_ref[...], 2)
    yc = y - mean
    var = _group_sums(yc * yc, gavg_ref[...], 1)
    y_c = (yc * lax.rsqrt(var + RWKV_LN_EPS) * lng_ref[...] + lnb_ref[...] + rbon_ref[...].astype(F32)) * rg_ref[...].astype(F32)
    d = x_ref.shape[1]
    gate = gate_ref[...].astype(F32)
    merged = (_sigmoid(gate[:, 0:d]) * _mm(y_a, pg_ref[...])
              + _sigmoid(gate[:, d:2 * d]) * _mm(fn_ref[...], pf_ref[...])
              + _sigmoid(gate[:, 2 * d:3 * d]) * _mm(y_c, pr_ref[...]))
    o_ref[...] = x_ref[...] + _mm(merged, wo_ref[...])


def _merge(x, gla_of, gla_ob, z_gla, y_fnet, r_yf, r_yb, r_bonus, r_g, z_gate,
           gn, lng, lnb, gavg, pg, pf, pr, wo, *, tm):
    t, d = x.shape
    row = lambda c: pl.BlockSpec((tm, c), lambda i: (i, 0))
    r_col = (2 * GLA_QK + GLA_V) // GLA_V
    gr_spec = pl.BlockSpec((tm, GLA_V), lambda i: (i, r_col))
    consts = (gn, lng, lnb, gavg, pg, pf, pr, wo)
    return pl.pallas_call(
        _merge_body,
        grid=(t // tm,),
        in_specs=[row(d), row(GLA_V), row(GLA_V), gr_spec, row(FNET_W), row(RWKV_W), row(RWKV_W),
                  row(RWKV_W), row(RWKV_W), row(GATE_COLS)] + [_const_spec(a.shape) for a in consts],
        out_specs=row(d),
        out_shape=jax.ShapeDtypeStruct((t, d), F32),
        compiler_params=_params(("parallel",)),
        name="merge",
    )(x, gla_of, gla_ob, z_gla, y_fnet, r_yf, r_yb, r_bonus, r_g, z_gate, *consts)


def _dft_tables(s):
    def cos_sin(n, rows, cols):
        ang = (2.0 * jnp.pi / n) * ((jnp.arange(rows, dtype=jnp.int32)[:, None] * cols[None, :]) % n).astype(F32)
        return jnp.cos(ang), jnp.sin(ang)

    def dft(n, rows):
        step = 64 if n % 64 == 0 and n > 64 else n
        c_lo, s_lo = cos_sin(n, rows, jnp.arange(step, dtype=jnp.int32))
        c_hi, s_hi = cos_sin(n, rows, jnp.arange(n // step, dtype=jnp.int32) * step)
        cos = c_hi[:, :, None] * c_lo[:, None, :] - s_hi[:, :, None] * s_lo[:, None, :]
        sin = s_hi[:, :, None] * c_lo[:, None, :] + c_hi[:, :, None] * s_lo[:, None, :]
        return cos.reshape(rows, n) * (n ** -0.5), sin.reshape(rows, n) * (n ** -0.5)

    cg, sg = dft(FNET_GC, FNET_GC)
    eye = jnp.eye(FNET_GROUPS, dtype=F32)
    cc = jnp.kron(eye, cg).astype(BF16)
    sc = jnp.kron(eye, sg).astype(BF16)
    h = s // 2
    cp, sp = dft(s, h)
    m = jnp.arange(s)
    alt = jnp.broadcast_to(jnp.where(m % 2 == 0, 1.0, -1.0) * (s ** -0.5), (8, s))
    r = jnp.arange(h)
    rev = (r[None, :] == h - r[:, None]).astype(BF16)
    return cc, sc, cp.astype(BF16), sp.astype(BF16), alt.astype(BF16), rev


def _tri(fwd, n):
    i = jnp.arange(n)
    same = (i[None, :] // CHUNK) == (i[:, None] // CHUNK)
    m = (i[None, :] <= i[:, None]) if fwd else (i[None, :] >= i[:, None])
    return (m & same).astype(BF16)


def _chunk_sel(n):
    return (jnp.arange(n // CHUNK * 8)[:, None] // 8 == jnp.arange(n)[None, :] // CHUNK).astype(BF16)


def _pad_rows(w, offset, total):
    return jnp.zeros((total, w.shape[1]), F32).at[offset:offset + w.shape[0]].set(w.astype(F32))


def kernel(x, ffn1_norm, ffn1_gate, ffn1_up, ffn1_down, mix_norm, w_in, gla_up_f, gla_bias_f, gla_up_b, gla_bias_b, gla_norm, rwkv_mu, rwkv_w0_f, rwkv_w2_f, rwkv_w0_b, rwkv_w2_b, rwkv_a0_f, rwkv_a2_f, rwkv_a0_b, rwkv_a2_b, rwkv_g2, rwkv_k_k, rwkv_k_a, rwkv_r_k, rwkv_ln_g, rwkv_ln_b, proj_gla, proj_fnet, proj_rwkv, w_out, ffn2_norm, ffn2_gate, ffn2_up, ffn2_down, final_norm):
    bsz, s, d = x.shape
    depth = w_in.shape[0]
    t = bsz * s
    tm = 512 if t % 512 == 0 else 256
    assert s % SEQ_BLOCK == 0 and t % tm == 0 and d * 3 == GATE_COLS

    fnet_tables = _dft_tables(s)
    trif_r, trib_r, sel_r = _tri(True, SEQ_BLOCK), _tri(False, SEQ_BLOCK), _chunk_sel(SEQ_BLOCK)
    head_of = jnp.arange(LANES) // RWKV_N
    same_head = head_of[:, None] == head_of[None, :]
    gsum = same_head.astype(BF16)
    gavg = (same_head.astype(F32) * (1.0 / RWKV_N)).astype(BF16)
    row = lambda a: a.astype(F32).reshape(1, -1)
    fnorm = row(final_norm)

    x2 = x.reshape(t, d)
    for l in range(depth):
        x2 = _ffn(x2, row(ffn1_norm[l]), ffn1_gate[l].astype(BF16), ffn1_up[l].astype(BF16),
                  ffn1_down[l].astype(BF16), fnorm, final=False, tm=tm)

        w = w_in[l]
        o_f, o_r, o_g = GLA_COLS_RAW, GLA_COLS_RAW + FNET_W, GLA_COLS_RAW + FNET_W + RWKV_COLS
        w_pad = jnp.concatenate([
            w[:, :o_f], jnp.zeros((d, GLA_COLS - GLA_COLS_RAW), w.dtype),
            w[:, o_f:o_r],
            w[:, o_r:o_g], jnp.zeros((d, RWKV_COLS_PAD - RWKV_COLS), w.dtype),
            w[:, o_g:]], axis=1).astype(BF16)
        mu = jnp.concatenate([rwkv_mu[l].astype(F32), jnp.zeros((RWKV_COLS_PAD - RWKV_COLS,), F32)]).reshape(1, -1)
        rwkv_consts = (mu, row(rwkv_w0_f[l]), row(rwkv_w0_b[l]), row(rwkv_a0_f[l]), row(rwkv_a0_b[l]),
                       jnp.concatenate([_pad_rows(rwkv_w2_f[l], 0, LANES), _pad_rows(rwkv_w2_b[l], 32, LANES)], axis=1),
                       jnp.concatenate([_pad_rows(rwkv_a2_f[l], 64, LANES), _pad_rows(rwkv_a2_b[l], 96, LANES)], axis=1),
                       _pad_rows(rwkv_g2[l], 0, LANES),
                       row(rwkv_k_k[l]), row(rwkv_k_a[l]), row(rwkv_r_k[l]), trif_r, trib_r, sel_r, gsum)
        (z_gla, z_fnet, z_gate, ka_f, rh_f, kt_f, bt_f, dec_f, kb_f, ka_b, rh_b, kt_b, bt_b, dec_b, kb_b,
         r_v, r_bonus, r_g) = _inproj(x2, row(mix_norm[l]), w_pad, rwkv_consts, seq=s)

        gla_of, gla_ob = _gla(
            z_gla.reshape(bsz, s, GLA_COLS),
            _pad_rows(gla_up_f[l], 0, LANES), row(gla_bias_f[l]),
            _pad_rows(gla_up_b[l], GLA_RANK, LANES), row(gla_bias_b[l]), trif_r, trib_r, sel_r)

        y_fnet = _fnet(z_fnet.reshape(bsz, s, FNET_W), fnet_tables)

        r_yf, r_yb = _rwkv((ka_f, rh_f, kt_f, bt_f, r_v, dec_f, kb_f), (ka_b, rh_b, kt_b, bt_b, r_v, dec_b, kb_b),
                           bsz, s)

        flat = lambda a: a.reshape(t, a.shape[-1])
        x2 = _merge(x2, flat(gla_of), flat(gla_ob), z_gla, flat(y_fnet), flat(r_yf), flat(r_yb),
                    r_bonus, r_g, z_gate,
                    row(gla_norm[l]), row(rwkv_ln_g[l]), row(rwkv_ln_b[l]), gavg,
                    proj_gla[l].astype(BF16), proj_fnet[l].astype(BF16), proj_rwkv[l].astype(BF16),
                    w_out[l].astype(BF16), tm=tm)

        x2 = _ffn(x2, row(ffn2_norm[l]), ffn2_gate[l].astype(BF16), ffn2_up[l].astype(BF16),
                  ffn2_down[l].astype(BF16), fnorm, final=(l == depth - 1), tm=tm)
    return x2.reshape(bsz, s, d)
```

```python
import functools

import jax
import jax.numpy as jnp
from jax import lax
from jax.experimental import pallas as pl
from jax.experimental.pallas import tpu as pltpu

F32 = jnp.float32
BF16 = jnp.bfloat16

NORM_EPS = 1e-6
RWKV_LN_EPS = 64e-5
GLA_TAU = 16.0

GLA_HEADS, GLA_DK, GLA_DV, GLA_RANK = 4, 64, 128, 16
GLA_QK, GLA_V = GLA_HEADS * GLA_DK, GLA_HEADS * GLA_DV
FNET_GROUPS, FNET_GC = 4, 128
FNET_W = FNET_GROUPS * FNET_GC
RWKV_HEADS, RWKV_N = 8, 64
RWKV_W = RWKV_HEADS * RWKV_N
RWKV_LOWRANK = 128
RWKV_GATE_RANK = 96
RWKV_COLS = 3 * RWKV_W + RWKV_LOWRANK + RWKV_GATE_RANK

LANES = 128
CHUNK = 64

GLA_COLS_RAW = 2 * GLA_QK + 2 * GLA_V + 2 * GLA_RANK
GLA_COLS = 13 * LANES
RWKV_COLS_PAD = 14 * LANES
GATE_COLS = 3 * 1024

VMEM_LIMIT = 56 * 1024 * 1024


def _mm(a, b):
    return jnp.dot(a.astype(BF16), b.astype(BF16), preferred_element_type=F32)


def _mm_nt(a, b):
    return lax.dot_general(a.astype(BF16), b.astype(BF16), (((1,), (1,)), ((), ())),
                           preferred_element_type=F32)


def _mm_tn(a, b):
    return lax.dot_general(a.astype(BF16), b.astype(BF16), (((0,), (0,)), ((), ())),
                           preferred_element_type=F32)


def _split(x, n):
    pieces = []
    for _ in range(n - 1):
        p = x.astype(BF16)
        pieces.append(p)
        x = x - p.astype(F32)
    pieces.append(x.astype(BF16))
    return pieces


def _group_sums(a, w, n):
    parts = _split(a, n)
    tiles = []
    for c in range(0, a.shape[1], LANES):
        out = None
        for p in parts:
            d = jnp.dot(p[:, c:c + LANES], w, preferred_element_type=F32)
            out = d if out is None else out + d
        tiles.append(out)
    return jnp.concatenate(tiles, axis=1)


def _mm_parts(w, parts):
    out = None
    for p in parts:
        d = jnp.dot(w, p, preferred_element_type=F32)
        out = d if out is None else out + d
    return out


def _mm_3x(a, b):
    a_hi, a_lo = _split(a, 2)
    b_hi, b_lo = _split(b, 2)
    return (jnp.dot(a_hi, b_hi, preferred_element_type=F32) + jnp.dot(a_lo, b_hi, preferred_element_type=F32)
            + jnp.dot(a_hi, b_lo, preferred_element_type=F32))


def _sigmoid(x):
    return 0.5 * jnp.tanh(0.5 * x) + 0.5


def _log_sigmoid(x):
    return jnp.minimum(x, 0.0) - jnp.log(1.0 + jnp.exp(-jnp.abs(x)))


def _rms(x, g):
    return x * lax.rsqrt(jnp.mean(x * x, axis=-1, keepdims=True) + NORM_EPS) * g


def _const_spec(shape):
    nd = len(shape)
    return pl.BlockSpec(shape, lambda *_: (0,) * nd, pipeline_mode=pl.Buffered(1))


def _resident(operands):
    specs, args = [], []
    for a in operands:
        if isinstance(a, tuple):
            arr, layer = a
            nd = arr.ndim - 1
            specs.append(pl.BlockSpec((None,) + arr.shape[1:], lambda *_, layer=layer, nd=nd: (layer,) + (0,) * nd,
                                      pipeline_mode=pl.Buffered(1)))
            args.append(arr)
        else:
            specs.append(_const_spec(a.shape))
            args.append(a)
    return specs, args


def _params(sem):
    return pltpu.CompilerParams(dimension_semantics=sem, vmem_limit_bytes=VMEM_LIMIT)


def _ffn_stages(rows, x_ref, g_ref, wg_ref, wu_ref, wd_ref, fg_ref, o_ref, final):
    x = x_ref[rows, :]
    h = _rms(x, g_ref[...]).astype(BF16)
    yield
    gate = jnp.dot(h, wg_ref[...], preferred_element_type=F32)
    up = jnp.dot(h, wu_ref[...], preferred_element_type=F32)
    yield
    act = (gate * _sigmoid(gate) * up).astype(BF16)
    y = x + 0.5 * jnp.dot(act, wd_ref[...], preferred_element_type=F32)
    if final:
        y = _rms(y, fg_ref[...])
    o_ref[rows, :] = y
    yield


def _ffn_body(x_ref, g_ref, wg_ref, wu_ref, wd_ref, fg_ref, o_ref, *, final):
    half = x_ref.shape[0] // 2
    args = (x_ref, g_ref, wg_ref, wu_ref, wd_ref, fg_ref, o_ref, final)
    _weave(_ffn_stages(slice(0, half), *args), _ffn_stages(slice(half, 2 * half), *args))


def _ffn(x, consts, *, final, tm):
    t, d = x.shape
    row = pl.BlockSpec((tm, d), lambda i: (i, 0))
    specs, args = _resident(consts)
    return pl.pallas_call(
        functools.partial(_ffn_body, final=final),
        grid=(t // tm,),
        in_specs=[row] + specs,
        out_specs=row,
        out_shape=jax.ShapeDtypeStruct((t, d), F32),
        compiler_params=_params(("parallel",)),
        name="ffn",
    )(x, *args)


SEQ_BLOCK = 256
BLOCK_NCH = SEQ_BLOCK // CHUNK
E_NEG_HALF = 2.718281828459045 ** -0.5


def _rwkv_feature_stages(u, prev_row, next_row, prm, dir_refs, v_ref, bonus_ref, rg_ref):
    R, W = SEQ_BLOCK, RWKV_W
    (mu, w0f, w0b, a0f, a0b, w2, a2, g2, k_k, k_a, r_k, trif, trib, sel, gsum) = prm
    row = lax.broadcasted_iota(jnp.int32, (8, 1), 0)
    prev = pltpu.roll(u, 1, 0)
    nxt = pltpu.roll(u, R - 1, 0)
    prev = jnp.concatenate([jnp.where(row == 0, prev_row, prev[0:8]), prev[8:]], axis=0)
    nxt = jnp.concatenate([nxt[:R - 8], jnp.where(row == 7, next_row, nxt[R - 8:])], axis=0)
    u = u * (1.0 - mu) + (prev + nxt) * (0.5 * mu)
    yield
    r, k, v = u[:, 0:W], u[:, W:2 * W], u[:, 2 * W:3 * W]
    sm = u[:, 3 * W:3 * W + LANES]
    gd = u[:, 3 * W + LANES:3 * W + 2 * LANES]
    v_ref[...] = v.astype(BF16)
    zw = _mm(jnp.tanh(sm), w2)
    za = _mm(sm, a2)
    rg_ref[...] = _mm(_sigmoid(gd), g2).astype(rg_ref.dtype)
    kk = k * k_k
    kap = kk * lax.rsqrt(_group_sums(kk * kk, gsum, 1) + 1e-12)
    yield
    dirs = ((True, w0f, a0f, trif, slice(0, W)), (False, w0b, a0b, trib, slice(W, 2 * W)))
    lw, kd, beta, inc, tot = [], [], [], [], []
    for fwd, w0, a0, tri, sl in dirs:
        lw.append((-E_NEG_HALF) * _sigmoid(w0 + zw[:, sl]))
        a = _sigmoid(a0 + za[:, sl])
        kd.append(k * (1.0 + (a - 1.0) * k_a))
        beta.append(kap * a)
        yield
    bonus_ref[...] = (_group_sums(r * kd[0] * r_k, gsum, 1) * v).astype(bonus_ref.dtype)
    for d, (fwd, w0, a0, tri, sl) in enumerate(dirs):
        lw_parts = _split(lw[d], 2)
        inc.append(_mm_parts(tri, lw_parts))
        tot.append(_mm_parts(sel, lw_parts))
        yield
    for d, (fwd, w0, a0, tri, sl) in enumerate(dirs):
        ka_ref, rh_ref, kt_ref, bt_ref, dec_ref, kb_ref = dir_refs[d]
        e_exc = jnp.exp(inc[d] - lw[d])
        e_ninc = jnp.exp(-inc[d])
        ka_ref[...] = (kap * e_exc).astype(BF16)
        rh_ref[...] = (r * (jnp.exp(inc[d]) if fwd else e_exc)).astype(BF16)
        dec_ref[...] = jnp.exp(tot[d])
        yield
        k_t, b_t = kd[d] * e_ninc, beta[d] * e_ninc
        kt_ref[...] = k_t.astype(BF16)
        bt_ref[...] = b_t.astype(BF16)
        yield
        for j in range(BLOCK_NCH):
            rows = slice(j * CHUNK, (j + 1) * CHUNK)
            kb_ref[j * W:(j + 1) * W, :] = jnp.concatenate([k_t[rows], b_t[rows]], axis=0).T.astype(BF16)
        yield


def _column_dots(h, w_ref, pieces, width):
    for off, cols, ref in pieces:
        for a in range(0, cols, width):
            b = min(a + width, cols)
            ref[:, a:b] = jnp.dot(h, w_ref[:, off + a:off + b], preferred_element_type=F32).astype(ref.dtype)
            yield


INPROJ_BLOCKS = 2


def _inproj_body(x_ref, xp_ref, xn_ref, g_ref, w_ref,
                 mu_ref, w0f_ref, w0b_ref, a0f_ref, a0b_ref, w2_ref, a2_ref, g2_ref, kk_ref, ka_ref, rk_ref,
                 trif_ref, trib_ref, sel_ref, gsum_ref,
                 gla_ref, fnet_ref, gate_ref,
                 kaf_ref, rhf_ref, ktf_ref, btf_ref, decf_ref, kbf_ref,
                 kab_ref, rhb_ref, ktb_ref, btb_ref, decb_ref, kbb_ref,
                 v_ref, bonus_ref, rg_ref, *, blocks_per_seq):
    R = SEQ_BLOCK
    c0, c1, c2 = GLA_COLS, GLA_COLS + FNET_W, GLA_COLS + FNET_W + RWKV_COLS_PAD
    g = g_ref[...]
    norm = lambda rows: _rms(rows, g).astype(BF16)
    rwkv_cols = lambda hh: jnp.dot(hh, w_ref[:, c1:c2], preferred_element_type=F32)
    z_halo = rwkv_cols(norm(jnp.concatenate([xp_ref[...], xn_ref[...], x_ref[R - 8:R + 8, :]], axis=0)))
    pos = (pl.program_id(0) * INPROJ_BLOCKS) % blocks_per_seq
    edge_rows = ((jnp.where(pos > 0, z_halo[7:8, :], 0.0), z_halo[24:25, :]),
                 (z_halo[23:24, :], jnp.where(pos + 1 < blocks_per_seq - 1, z_halo[8:9, :], 0.0)))
    prm = tuple(ref[...] for ref in (mu_ref, w0f_ref, w0b_ref, a0f_ref, a0b_ref, w2_ref, a2_ref, g2_ref,
                                     kk_ref, ka_ref, rk_ref, trif_ref, trib_ref, sel_ref, gsum_ref))

    def rows_of(ref, blk):
        n = ref.shape[0] // INPROJ_BLOCKS
        return ref.at[pl.ds(blk * n, n)]

    h, u = {}, {}
    h[0] = norm(x_ref[0:R, :])
    u[0] = rwkv_cols(h[0])

    def elementwise():
        for blk in range(INPROJ_BLOCKS):
            dir_refs = tuple(tuple(rows_of(ref, blk) for ref in refs) for refs in
                             ((kaf_ref, rhf_ref, ktf_ref, btf_ref, decf_ref, kbf_ref),
                              (kab_ref, rhb_ref, ktb_ref, btb_ref, decb_ref, kbb_ref)))
            yield from _rwkv_feature_stages(u[blk], *edge_rows[blk], prm, dir_refs, rows_of(v_ref, blk),
                                            rows_of(bonus_ref, blk), rows_of(rg_ref, blk))

    def matmuls():
        h[1] = norm(x_ref[R:2 * R, :])
        u[1] = rwkv_cols(h[1])
        yield
        for blk in range(INPROJ_BLOCKS):
            yield from _column_dots(h[blk], w_ref, ((c2, GATE_COLS, rows_of(gate_ref, blk)),
                                                    (0, c0, rows_of(gla_ref, blk)),
                                                    (c0, FNET_W, rows_of(fnet_ref, blk))), 512)

    _weave(elementwise(), matmuls())


def _inproj(x, consts, *, seq):
    t, d = x.shape
    specs, args = _resident(consts)
    nblk = INPROJ_BLOCKS
    tm = nblk * SEQ_BLOCK
    assert (seq // SEQ_BLOCK) % nblk == 0
    sub = 8
    per = tm // sub
    row = lambda c: pl.BlockSpec((tm, c), lambda i: (i, 0))
    prev = pl.BlockSpec((sub, d), lambda i: (jnp.maximum(i * per - 1, 0), 0))
    nxt = pl.BlockSpec((sub, d), lambda i: (jnp.minimum((i + 1) * per, t // sub - 1), 0))
    dec = pl.BlockSpec((nblk * BLOCK_NCH * 8, RWKV_W), lambda i: (i, 0))
    op = jax.ShapeDtypeStruct((t, RWKV_W), BF16)
    dec_shape = jax.ShapeDtypeStruct((t // CHUNK * 8, RWKV_W), F32)
    cm = pl.BlockSpec((nblk * BLOCK_NCH * RWKV_W, LANES), lambda i: (i, 0))
    cm_shape = jax.ShapeDtypeStruct((t // CHUNK * RWKV_W, LANES), BF16)
    per_dir = lambda a, b, c: [a] * 4 + [b, c]
    return pl.pallas_call(
        functools.partial(_inproj_body, blocks_per_seq=seq // SEQ_BLOCK),
        grid=(t // tm,),
        in_specs=[row(d), prev, nxt] + specs,
        out_specs=[row(GLA_COLS), row(FNET_W), row(GATE_COLS)]
                  + per_dir(row(RWKV_W), dec, cm) * 2 + [row(RWKV_W)] * 3,
        out_shape=[jax.ShapeDtypeStruct((t, GLA_COLS), BF16),
                   jax.ShapeDtypeStruct((t, FNET_W), BF16),
                   jax.ShapeDtypeStruct((t, GATE_COLS), BF16)]
                  + per_dir(op, dec_shape, cm_shape) * 2 + [op] * 3,
        compiler_params=_params(("parallel",)),
        name="inproj",
    )(x, x, x, *args)


def _weave(*gens):
    live = list(gens)
    while live:
        for g in list(live):
            try:
                next(g)
            except StopIteration:
                live.remove(g)


def _lane_masks():
    lane = lax.broadcasted_iota(jnp.int32, (CHUNK, LANES), 1)
    return lane < (LANES // 2), lane >= (LANES // 2)


def _masked_stack(x, lo, hi):
    return jnp.concatenate([jnp.where(lo, x, 0.0), jnp.where(hi, x, 0.0)], axis=0)


def _pair_masks(fwd, inclusive):
    n = 2 * CHUNK
    i = lax.broadcasted_iota(jnp.int32, (n, n), 0)
    j = lax.broadcasted_iota(jnp.int32, (n, n), 1)
    same = (i < CHUNK) == (j < CHUNK)
    strict = (j < i) if fwd else (j > i)
    if inclusive:
        return same & (strict | (i == j))
    return same & strict


GLA_PAIRS = GLA_HEADS // 2


def _gla_prep_stages(z_ref, up, bias, tri, sel, d, qd_ref, kd_ref, v_ref, dec_ref):
    L = CHUNK
    z = z_ref[...].astype(F32)
    q = z[:, 0:GLA_QK] * (GLA_DK ** -0.5)
    k = z[:, GLA_QK:2 * GLA_QK]
    v = z[:, 2 * GLA_QK:2 * GLA_QK + GLA_V]
    dn = z[:, GLA_COLS - LANES:GLA_COLS]
    for j in range(BLOCK_NCH):
        v_ref[d, j] = v[j * L:(j + 1) * L]
    la = _log_sigmoid(_mm(dn, up) + bias) * (1.0 / GLA_TAU)
    yield
    la_parts = _split(la, 2)
    b = _mm_parts(tri, la_parts)
    dec_ref[d] = jnp.exp(_mm_parts(sel, la_parts))
    yield
    for ref, val in ((qd_ref, q * jnp.exp(b)), (kd_ref, k * jnp.exp(-b))):
        for j in range(BLOCK_NCH):
            ref[d, j] = val[j * L:(j + 1) * L]
        yield


def _gla_step_stages(d, qd_ref, kd_ref, v_ref, dec_ref, st_ref, o_ref, masks):
    L = CHUNK
    lo, hi, m_read = masks
    units = [(j, p) for j in range(BLOCK_NCH) for p in range(GLA_PAIRS)]
    chunk = lambda j: j if d == 0 else BLOCK_NCH - 1 - j
    q_ms, kd_ms, v_st, sc, inc, o_intra = {}, {}, {}, {}, {}, {}
    for n, u in enumerate(units):
        j, p = u
        c = chunk(j)
        sl = pl.ds(p * LANES, LANES)
        q_ms[u] = _masked_stack(qd_ref[d, c, :, sl], lo, hi)
        kd_ms[u] = _masked_stack(kd_ref[d, c, :, sl], lo, hi)
        v_st[u] = jnp.concatenate([v_ref[d, c, :, pl.ds((2 * p) * GLA_DV, GLA_DV)],
                                   v_ref[d, c, :, pl.ds((2 * p + 1) * GLA_DV, GLA_DV)]], axis=0)
        sc[u] = jnp.where(m_read[d], _mm_nt(q_ms[u], kd_ms[u]), 0.0)
        inc[u] = _mm_tn(v_st[u], kd_ms[u])
        if n % 2:
            yield
    for n, u in enumerate(units):
        o_intra[u] = _mm(sc[u], v_st[u])
        if n % 2:
            yield
    for p in range(GLA_PAIRS):
        sl = pl.ds(p * LANES, LANES)
        st = st_ref[d, p]
        for j in range(BLOCK_NCH):
            u = (j, p)
            c = chunk(j)
            o = o_intra[u] + _mm_nt(q_ms[u], st)
            o_ref[pl.ds(c * L, L), pl.ds((2 * p) * GLA_DV, GLA_DV)] = o[0:L].astype(BF16)
            o_ref[pl.ds(c * L, L), pl.ds((2 * p + 1) * GLA_DV, GLA_DV)] = o[L:2 * L].astype(BF16)
            st = (st + inc[u]) * dec_ref[d, pl.ds(c * 8, 1), sl]
            if j % 2:
                yield
        st_ref[d, p] = st


def _gla_body(zf_ref, zb_ref, upf_ref, bf_ref, upb_ref, bb_ref, trif_ref, trib_ref, sel_ref,
              of_ref, ob_ref, qd_ref, kd_ref, v_ref, dec_ref, st_ref):
    @pl.when(pl.program_id(1) == 0)
    def _():
        st_ref[...] = jnp.zeros_like(st_ref)

    scr = (qd_ref, kd_ref, v_ref, dec_ref)
    lo, hi = _lane_masks()
    masks = (lo, hi, (_pair_masks(True, inclusive=True), _pair_masks(False, inclusive=False)))
    _weave(_gla_prep_stages(zf_ref, upf_ref[...], bf_ref[...], trif_ref[...], sel_ref[...], 0, *scr))
    _weave(_gla_prep_stages(zb_ref, upb_ref[...], bb_ref[...], trib_ref[...], sel_ref[...], 1, *scr),
           _gla_step_stages(0, *scr, st_ref, of_ref, masks))
    _weave(_gla_step_stages(1, *scr, st_ref, ob_ref, masks))


def _gla(z, upf, bf, upb, bb, trif, trib, sel):
    bsz, s, _ = z.shape
    nb = s // SEQ_BLOCK
    fmap = lambda b, i: (b, i, 0)
    bmap = lambda b, i: (b, nb - 1 - i, 0)
    zspec = lambda m: pl.BlockSpec((None, SEQ_BLOCK, GLA_COLS), m)
    ospec = lambda m: pl.BlockSpec((None, SEQ_BLOCK, GLA_V), m)
    per_chunk = lambda rows, w: pltpu.VMEM((2, BLOCK_NCH, rows, w), F32)
    specs, args = _resident((upf, bf, upb, bb, trif, trib, sel))
    return pl.pallas_call(
        _gla_body,
        grid=(bsz, nb),
        in_specs=[zspec(fmap), zspec(bmap)] + specs,
        out_specs=[ospec(fmap), ospec(bmap)],
        out_shape=[jax.ShapeDtypeStruct((bsz, s, GLA_V), BF16)] * 2,
        scratch_shapes=[per_chunk(CHUNK, GLA_QK), per_chunk(CHUNK, GLA_QK), per_chunk(CHUNK, GLA_V),
                        pltpu.VMEM((2, BLOCK_NCH * 8, GLA_QK), F32),
                        pltpu.VMEM((2, GLA_PAIRS, GLA_DV, LANES), F32)],
        compiler_params=_params(("parallel", "arbitrary")),
        name="gla",
    )(z, z, *args)


def _fnet_body(z_ref, cc_ref, sc_ref, cp_ref, sp_ref, alt_ref, rev_ref, o_ref, a_ref, b_ref, m_ref, *, rows):
    s = z_ref.shape[0]
    h = s // 2
    z = z_ref[...]
    a_ref[...] = jnp.dot(z, cc_ref[...], preferred_element_type=F32).astype(BF16)
    b_ref[...] = jnp.dot(z, sc_ref[...], preferred_element_type=F32).astype(BF16)
    for i in range(h // rows):
        r = slice(i * rows, (i + 1) * rows)
        p = jnp.dot(cp_ref[r, :], a_ref[...], preferred_element_type=F32)
        q = jnp.dot(sp_ref[r, :], b_ref[...], preferred_element_type=F32)
        o_ref[r, :] = (p - q).astype(o_ref.dtype)
        m_ref[r, :] = (p + q).astype(BF16)
    for i in range(h // rows):
        r = slice(i * rows, (i + 1) * rows)
        o_ref[h + i * rows:h + (i + 1) * rows, :] = jnp.dot(
            rev_ref[r, :], m_ref[...], preferred_element_type=F32).astype(o_ref.dtype)
    mid = jnp.dot(alt_ref[...], a_ref[...], preferred_element_type=F32)
    o_ref[h:h + 1, :] = mid[0:1].astype(o_ref.dtype)


def _fnet(z, tables):
    bsz, s, w = z.shape
    rows = min(s // 2, 512)
    blk = lambda: pl.BlockSpec((None, s, w), lambda b: (b, 0, 0))
    return pl.pallas_call(
        functools.partial(_fnet_body, rows=rows),
        grid=(bsz,),
        in_specs=[blk()] + [_const_spec(a.shape) for a in tables],
        out_specs=blk(),
        out_shape=jax.ShapeDtypeStruct((bsz, s, w), BF16),
        scratch_shapes=[pltpu.VMEM((s, w), BF16), pltpu.VMEM((s, w), BF16), pltpu.VMEM((s // 2, w), BF16)],
        compiler_params=_params(("parallel",)),
        name="fnet",
    )(z, *tables)


RWKV_BLOCK = SEQ_BLOCK
RWKV_NCH = BLOCK_NCH
RWKV_PAIRS = RWKV_HEADS // 2


def _chunk(ref, j, sl):
    return ref[pl.ds(j * CHUNK, CHUNK), sl]


def _rwkv_solve_stages(units, ops, tinv_ref, pkb_ref, mkv_ref, masks):
    L = CHUNK
    n2 = 2 * L
    lo, hi, eye, m_strict, m_read = masks
    top = (lax.broadcasted_iota(jnp.int32, (2 * n2, n2), 0) & (n2 - 1)) < L
    t_inv, pw, s_kb, v_ms, m_k = {}, {}, {}, {}, {}
    for n, u in enumerate(units):
        d, j, p = u
        sl = pl.ds(p * LANES, LANES)
        kaph_ref, rh_ref, _, _, v_ref, _, kbcm_ref = ops[d]
        xs = jnp.concatenate([_masked_stack(_chunk(kaph_ref, j, sl), lo, hi),
                              _masked_stack(_chunk(rh_ref, j, sl), lo, hi)], axis=0)
        s_kb[u] = _mm(xs, kbcm_ref[pl.ds(j * RWKV_W + p * LANES, LANES), :])
        v_ms[u] = _masked_stack(_chunk(v_ref, j, sl), lo, hi)
        if n % 2:
            yield
    for n, u in enumerate(units):
        d, j, p = u
        s_sw = pltpu.roll(s_kb[u], L, 1)
        vs_k = jnp.where(top, s_kb[u], s_sw)
        vs_b = jnp.where(top, s_sw, s_kb[u])
        m_k[u] = jnp.where(m_strict[d], vs_k[0:n2], 0.0).astype(BF16)
        nmat = jnp.where(m_strict[d], -vs_b[0:n2], 0.0)
        p_k = jnp.where(m_read[d], vs_k[n2:2 * n2], 0.0)
        p_b = jnp.where(m_read[d], -vs_b[n2:2 * n2], 0.0)
        pkb_ref[d, j, p] = jnp.concatenate([p_k, p_b], axis=1).astype(BF16)
        t_inv[u] = eye + nmat
        pw[u] = nmat.astype(BF16)
        if n % 2:
            yield
    for n, u in enumerate(units):
        d, j, p = u
        mkv_ref[d, j, p] = _mm(m_k[u], v_ms[u])
        if n % 2:
            yield
    for n, u in enumerate(units):
        pw[u] = jnp.dot(pw[u], pw[u], preferred_element_type=F32).astype(BF16)
        if n % 2:
            yield
    for _ in range(4):
        for n, u in enumerate(units):
            both = jnp.dot(jnp.concatenate([t_inv[u].astype(BF16), pw[u]], axis=0), pw[u],
                           preferred_element_type=F32)
            t_inv[u] = t_inv[u] + both[0:n2]
            pw[u] = both[n2:2 * n2].astype(BF16)
            if n % 2:
                yield
    for n, u in enumerate(units):
        d, j, p = u
        tinv_ref[d, j, p] = (t_inv[u] + jnp.dot(t_inv[u].astype(BF16), pw[u],
                                                preferred_element_type=F32)).astype(BF16)
        if n % 2:
            yield


def _rwkv_scan_stages(steps, ops, tinv_ref, pkb_ref, mkv_ref, st_ref, y_refs, masks):
    L = CHUNK
    n2 = 2 * L
    lo, hi = masks[0], masks[1]
    units = [(d, p) for d in range(2) for p in range(RWKV_PAIRS)]
    for j in steps:
        cj = {0: j, 1: RWKV_NCH - 1 - j}
        v_ms, xh, uu = {}, {}, {}
        for (d, p) in units:
            sl = pl.ds(p * LANES, LANES)
            xs = jnp.concatenate([_masked_stack(_chunk(ops[d][0], cj[d], sl), lo, hi),
                                  _masked_stack(_chunk(ops[d][1], cj[d], sl), lo, hi)], axis=0)
            xh[d, p] = _mm_nt(xs, st_ref[d, p])
            yield
        for (d, p) in units:
            sl = pl.ds(p * LANES, LANES)
            v_ms[d, p] = _masked_stack(_chunk(ops[d][4], cj[d], sl), lo, hi)
            uu[d, p] = _mm(tinv_ref[d, cj[d], p], xh[d, p][0:n2] + mkv_ref[d, cj[d], p])
            yield
        for (d, p) in units:
            sl = pl.ds(p * LANES, LANES)
            y = xh[d, p][n2:2 * n2] + _mm(pkb_ref[d, cj[d], p], jnp.concatenate([v_ms[d, p], uu[d, p]], axis=0))
            y_refs[d][pl.ds(cj[d] * L, L), sl] = (y[0:L] + y[L:n2]).astype(BF16)
            kb = jnp.concatenate([_masked_stack(_chunk(ops[d][2], cj[d], sl), lo, hi),
                                  _masked_stack(_chunk(ops[d][3], cj[d], sl), lo, hi)], axis=0)
            dec = ops[d][5][pl.ds(cj[d] * 8, 1), sl]
            st_ref[d, p] = (st_ref[d, p] + _mm_tn(jnp.concatenate([v_ms[d, p], -uu[d, p]], axis=0), kb)) * dec
            yield


def _rwkv_body(kaf_ref, rhf_ref, ktf_ref, btf_ref, vf_ref, decf_ref, kbf_ref,
               kab_ref, rhb_ref, ktb_ref, btb_ref, vb_ref, decb_ref, kbb_ref,
               yf_ref, yb_ref, tinv_ref, pkb_ref, mkv_ref, st_ref):
    @pl.when(pl.program_id(1) == 0)
    def _():
        st_ref[...] = jnp.zeros_like(st_ref)

    ops = ((kaf_ref, rhf_ref, ktf_ref, btf_ref, vf_ref, decf_ref, kbf_ref),
           (kab_ref, rhb_ref, ktb_ref, btb_ref, vb_ref, decb_ref, kbb_ref))
    lo, hi = _lane_masks()
    n2 = 2 * CHUNK
    ii = lax.broadcasted_iota(jnp.int32, (n2, n2), 0)
    jj = lax.broadcasted_iota(jnp.int32, (n2, n2), 1)
    eye = (ii == jj).astype(F32)
    m_strict = (_pair_masks(True, inclusive=False), _pair_masks(False, inclusive=False))
    m_read = (_pair_masks(True, inclusive=True), _pair_masks(False, inclusive=False))
    masks = (lo, hi, eye, m_strict, m_read)

    def solve(steps):
        units = [(d, j if d == 0 else RWKV_NCH - 1 - j, p)
                 for j in steps for d in range(2) for p in range(RWKV_PAIRS)]
        return _rwkv_solve_stages(units, ops, tinv_ref, pkb_ref, mkv_ref, masks)

    def scan(steps):
        return _rwkv_scan_stages(steps, ops, tinv_ref, pkb_ref, mkv_ref, st_ref, (yf_ref, yb_ref), masks)

    half = RWKV_NCH // 2
    first, second = list(range(half)), list(range(half, RWKV_NCH))
    _weave(solve(first))
    _weave(solve(second), scan(first))
    _weave(scan(second))


def _rwkv(fwd_ops, bwd_ops, bsz, s):
    R = RWKV_BLOCK
    nb = s // R
    fc = lambda i: i
    bc = lambda i: nb - 1 - i
    blk = lambda cm: pl.BlockSpec((None, R, RWKV_W), lambda b, i: (b, cm(i), 0))
    dec = lambda cm: pl.BlockSpec((None, RWKV_NCH * 8, RWKV_W), lambda b, i: (b, cm(i), 0))
    cm = lambda cm_: pl.BlockSpec((None, RWKV_NCH * RWKV_W, LANES), lambda b, i: (b, cm_(i), 0))
    seq = lambda a: a.reshape(bsz, -1, a.shape[-1])
    out = jax.ShapeDtypeStruct((bsz, s, RWKV_W), BF16)
    unit = (2, RWKV_NCH, RWKV_PAIRS)
    scratch = [pltpu.VMEM(unit + (LANES, LANES), BF16),
               pltpu.VMEM(unit + (LANES, 2 * LANES), BF16),
               pltpu.VMEM(unit + (LANES, LANES), F32),
               pltpu.VMEM((2, RWKV_PAIRS, LANES, LANES), F32)]
    return pl.pallas_call(
        _rwkv_body,
        grid=(bsz, nb),
        in_specs=[blk(fc)] * 5 + [dec(fc), cm(fc)] + [blk(bc)] * 5 + [dec(bc), cm(bc)],
        out_specs=[blk(fc), blk(bc)],
        out_shape=[out] * 2,
        scratch_shapes=scratch,
        compiler_params=_params(("parallel", "arbitrary")),
        name="rwkv",
    )(*[seq(a) for a in fwd_ops], *[seq(a) for a in bwd_ops])


def _merge_body(x_ref, gof_ref, gob_ref, gr_ref, fn_ref, ryf_ref, ryb_ref, rbon_ref, rg_ref, gate_ref,
                gn_ref, lng_ref, lnb_ref, gavg_ref, pg_ref, pf_ref, pr_ref, wo_ref, o_ref):
    o = gof_ref[...].astype(F32) + gob_ref[...].astype(F32)
    parts = []
    for h in range(GLA_HEADS):
        oh = o[:, h * GLA_DV:(h + 1) * GLA_DV]
        parts.append(oh * lax.rsqrt(jnp.mean(oh * oh, axis=-1, keepdims=True) + NORM_EPS))
    rg = gr_ref[...].astype(F32)
    y_a = jnp.concatenate(parts, axis=1) * gn_ref[...] * (rg * _sigmoid(rg))
    y = ryf_ref[...].astype(F32) + ryb_ref[...].astype(F32)
    mean = _group_sums(y, gavg_ref[...], 2)
    yc = y - mean
    var = _group_sums(yc * yc, gavg_ref[...], 1)
    y_c = (yc * lax.rsqrt(var + RWKV_LN_EPS) * lng_ref[...] + lnb_ref[...] + rbon_ref[...].astype(F32)) * rg_ref[...].astype(F32)
    d = x_ref.shape[1]
    gate = gate_ref[...].astype(F32)
    merged = (_sigmoid(gate[:, 0:d]) * _mm(y_a, pg_ref[...])
              + _sigmoid(gate[:, d:2 * d]) * _mm(fn_ref[...], pf_ref[...])
              + _sigmoid(gate[:, 2 * d:3 * d]) * _mm(y_c, pr_ref[...]))
    o_ref[...] = x_ref[...] + _mm(merged, wo_ref[...])


def _merge(x, gla_of, gla_ob, z_gla, y_fnet, r_yf, r_yb, r_bonus, r_g, z_gate,
           gn, lng, lnb, gavg, pg, pf, pr, wo, *, tm):
    t, d = x.shape
    row = lambda c: pl.BlockSpec((tm, c), lambda i: (i, 0))
    r_col = (2 * GLA_QK + GLA_V) // GLA_V
    gr_spec = pl.BlockSpec((tm, GLA_V), lambda i: (i, r_col))
    specs, args = _resident((gn, lng, lnb, gavg, pg, pf, pr, wo))
    return pl.pallas_call(
        _merge_body,
        grid=(t // tm,),
        in_specs=[row(d), row(GLA_V), row(GLA_V), gr_spec, row(FNET_W), row(RWKV_W), row(RWKV_W),
                  row(RWKV_W), row(RWKV_W), row(GATE_COLS)] + specs,
        out_specs=row(d),
        out_shape=jax.ShapeDtypeStruct((t, d), F32),
        compiler_params=_params(("parallel",)),
        name="merge",
    )(x, gla_of, gla_ob, z_gla, y_fnet, r_yf, r_yb, r_bonus, r_g, z_gate, *args)


def _dft_tables(s):
    def cos_sin(n, rows, cols):
        ang = (2.0 * jnp.pi / n) * ((jnp.arange(rows, dtype=jnp.int32)[:, None] * cols[None, :]) % n).astype(F32)
        return jnp.cos(ang), jnp.sin(ang)

    def dft(n, rows):
        step = 64 if n % 64 == 0 and n > 64 else n
        c_lo, s_lo = cos_sin(n, rows, jnp.arange(step, dtype=jnp.int32))
        c_hi, s_hi = cos_sin(n, rows, jnp.arange(n // step, dtype=jnp.int32) * step)
        cos = c_hi[:, :, None] * c_lo[:, None, :] - s_hi[:, :, None] * s_lo[:, None, :]
        sin = s_hi[:, :, None] * c_lo[:, None, :] + c_hi[:, :, None] * s_lo[:, None, :]
        return cos.reshape(rows, n) * (n ** -0.5), sin.reshape(rows, n) * (n ** -0.5)

    cg, sg = dft(FNET_GC, FNET_GC)
    eye = jnp.eye(FNET_GROUPS, dtype=F32)
    cc = jnp.kron(eye, cg).astype(BF16)
    sc = jnp.kron(eye, sg).astype(BF16)
    h = s // 2
    cp, sp = dft(s, h)
    m = jnp.arange(s)
    alt = jnp.broadcast_to(jnp.where(m % 2 == 0, 1.0, -1.0) * (s ** -0.5), (8, s))
    r = jnp.arange(h)
    rev = (r[None, :] == h - r[:, None]).astype(BF16)
    return cc, sc, cp.astype(BF16), sp.astype(BF16), alt.astype(BF16), rev


def _tri(fwd, n):
    i = jnp.arange(n)
    same = (i[None, :] // CHUNK) == (i[:, None] // CHUNK)
    m = (i[None, :] <= i[:, None]) if fwd else (i[None, :] >= i[:, None])
    return (m & same).astype(BF16)


def _chunk_sel(n):
    return (jnp.arange(n // CHUNK * 8)[:, None] // 8 == jnp.arange(n)[None, :] // CHUNK).astype(BF16)


def _pad_rows(w, offset, total):
    return jnp.zeros((w.shape[0], total, w.shape[2]), F32).at[:, offset:offset + w.shape[1]].set(w.astype(F32))


def kernel(x, ffn1_norm, ffn1_gate, ffn1_up, ffn1_down, mix_norm, w_in, gla_up_f, gla_bias_f, gla_up_b, gla_bias_b, gla_norm, rwkv_mu, rwkv_w0_f, rwkv_w2_f, rwkv_w0_b, rwkv_w2_b, rwkv_a0_f, rwkv_a2_f, rwkv_a0_b, rwkv_a2_b, rwkv_g2, rwkv_k_k, rwkv_k_a, rwkv_r_k, rwkv_ln_g, rwkv_ln_b, proj_gla, proj_fnet, proj_rwkv, w_out, ffn2_norm, ffn2_gate, ffn2_up, ffn2_down, final_norm):
    bsz, s, d = x.shape
    depth = w_in.shape[0]
    t = bsz * s
    tm = 512 if t % 512 == 0 else 256
    assert s % SEQ_BLOCK == 0 and t % tm == 0 and d * 3 == GATE_COLS

    fnet_tables = _dft_tables(s)
    trif_r, trib_r, sel_r = _tri(True, SEQ_BLOCK), _tri(False, SEQ_BLOCK), _chunk_sel(SEQ_BLOCK)
    head_of = jnp.arange(LANES) // RWKV_N
    same_head = head_of[:, None] == head_of[None, :]
    gsum = same_head.astype(BF16)
    gavg = (same_head.astype(F32) * (1.0 / RWKV_N)).astype(BF16)
    rows = lambda a: a.astype(F32).reshape(depth, 1, -1)
    bf16 = lambda a: a.astype(BF16)
    fnorm = final_norm.astype(F32).reshape(1, -1)
    ffn1 = (rows(ffn1_norm), bf16(ffn1_gate), bf16(ffn1_up), bf16(ffn1_down))
    ffn2 = (rows(ffn2_norm), bf16(ffn2_gate), bf16(ffn2_up), bf16(ffn2_down))
    o_f, o_r, o_g = GLA_COLS_RAW, GLA_COLS_RAW + FNET_W, GLA_COLS_RAW + FNET_W + RWKV_COLS
    w_pad = bf16(jnp.concatenate([
        w_in[:, :, :o_f], jnp.zeros((depth, d, GLA_COLS - GLA_COLS_RAW), w_in.dtype),
        w_in[:, :, o_f:o_r],
        w_in[:, :, o_r:o_g], jnp.zeros((depth, d, RWKV_COLS_PAD - RWKV_COLS), w_in.dtype),
        w_in[:, :, o_g:]], axis=2))
    mu = jnp.concatenate([rwkv_mu.astype(F32), jnp.zeros((depth, RWKV_COLS_PAD - RWKV_COLS), F32)],
                         axis=1).reshape(depth, 1, -1)
    inproj_layered = (
        rows(mix_norm), w_pad, mu, rows(rwkv_w0_f), rows(rwkv_w0_b), rows(rwkv_a0_f), rows(rwkv_a0_b),
        jnp.concatenate([_pad_rows(rwkv_w2_f, 0, LANES), _pad_rows(rwkv_w2_b, 32, LANES)], axis=2),
        jnp.concatenate([_pad_rows(rwkv_a2_f, 64, LANES), _pad_rows(rwkv_a2_b, 96, LANES)], axis=2),
        _pad_rows(rwkv_g2, 0, LANES), rows(rwkv_k_k), rows(rwkv_k_a), rows(rwkv_r_k))
    gla_layered = (_pad_rows(gla_up_f, 0, LANES), rows(gla_bias_f), _pad_rows(gla_up_b, GLA_RANK, LANES),
                   rows(gla_bias_b))
    merge_layered = (rows(gla_norm), rows(rwkv_ln_g), rows(rwkv_ln_b))
    proj_layered = (bf16(proj_gla), bf16(proj_fnet), bf16(proj_rwkv), bf16(w_out))

    x2 = x.reshape(t, d)
    for l in range(depth):
        at = lambda arrays: tuple((a, l) for a in arrays)
        x2 = _ffn(x2, at(ffn1) + (fnorm,), final=False, tm=tm)

        (z_gla, z_fnet, z_gate, ka_f, rh_f, kt_f, bt_f, dec_f, kb_f, ka_b, rh_b, kt_b, bt_b, dec_b, kb_b,
         r_v, r_bonus, r_g) = _inproj(x2, at(inproj_layered) + (trif_r, trib_r, sel_r, gsum), seq=s)

        gla_of, gla_ob = _gla(z_gla.reshape(bsz, s, GLA_COLS), *at(gla_layered), trif_r, trib_r, sel_r)

        y_fnet = _fnet(z_fnet.reshape(bsz, s, FNET_W), fnet_tables)

        r_yf, r_yb = _rwkv((ka_f, rh_f, kt_f, bt_f, r_v, dec_f, kb_f), (ka_b, rh_b, kt_b, bt_b, r_v, dec_b, kb_b),
                           bsz, s)

        flat = lambda a: a.reshape(t, a.shape[-1])
        x2 = _merge(x2, flat(gla_of), flat(gla_ob), z_gla, flat(y_fnet), flat(r_yf), flat(r_yb),
                    r_bonus, r_g, z_gate, *at(merge_layered), gavg, *at(proj_layered), tm=tm)

        x2 = _ffn(x2, at(ffn2) + (fnorm,), final=(l == depth - 1), tm=tm)
    return x2.reshape(bsz, s, d)
```

```python
import functools

import jax
import jax.numpy as jnp
from jax import lax
from jax.experimental import pallas as pl
from jax.experimental.pallas import tpu as pltpu

F32 = jnp.float32
BF16 = jnp.bfloat16

NORM_EPS = 1e-6
RWKV_LN_EPS = 64e-5
GLA_TAU = 16.0

GLA_HEADS, GLA_DK, GLA_DV, GLA_RANK = 4, 64, 128, 16
GLA_QK, GLA_V = GLA_HEADS * GLA_DK, GLA_HEADS * GLA_DV
FNET_GROUPS, FNET_GC = 4, 128
FNET_W = FNET_GROUPS * FNET_GC
RWKV_HEADS, RWKV_N = 8, 64
RWKV_W = RWKV_HEADS * RWKV_N
RWKV_LOWRANK = 128
RWKV_GATE_RANK = 96
RWKV_COLS = 3 * RWKV_W + RWKV_LOWRANK + RWKV_GATE_RANK

LANES = 128
CHUNK = 64

GLA_COLS_RAW = 2 * GLA_QK + 2 * GLA_V + 2 * GLA_RANK
GLA_COLS = 13 * LANES
RWKV_COLS_PAD = 14 * LANES
GATE_COLS = 3 * 1024

VMEM_LIMIT = 56 * 1024 * 1024


def _mm(a, b):
    return jnp.dot(a.astype(BF16), b.astype(BF16), preferred_element_type=F32)


def _mm_nt(a, b):
    return lax.dot_general(a.astype(BF16), b.astype(BF16), (((1,), (1,)), ((), ())),
                           preferred_element_type=F32)


def _mm_tn(a, b):
    return lax.dot_general(a.astype(BF16), b.astype(BF16), (((0,), (0,)), ((), ())),
                           preferred_element_type=F32)


def _split(x, n):
    pieces = []
    for _ in range(n - 1):
        p = x.astype(BF16)
        pieces.append(p)
        x = x - p.astype(F32)
    pieces.append(x.astype(BF16))
    return pieces


def _group_sums(a, w, n):
    parts = _split(a, n)
    tiles = []
    for c in range(0, a.shape[1], LANES):
        out = None
        for p in parts:
            d = jnp.dot(p[:, c:c + LANES], w, preferred_element_type=F32)
            out = d if out is None else out + d
        tiles.append(out)
    return jnp.concatenate(tiles, axis=1)


def _mm_parts(w, parts):
    out = None
    for p in parts:
        d = jnp.dot(w, p, preferred_element_type=F32)
        out = d if out is None else out + d
    return out


def _mm_3x(a, b):
    a_hi, a_lo = _split(a, 2)
    b_hi, b_lo = _split(b, 2)
    return (jnp.dot(a_hi, b_hi, preferred_element_type=F32) + jnp.dot(a_lo, b_hi, preferred_element_type=F32)
            + jnp.dot(a_hi, b_lo, preferred_element_type=F32))


def _sigmoid(x):
    return 0.5 * jnp.tanh(0.5 * x) + 0.5


def _log_sigmoid(x):
    return jnp.minimum(x, 0.0) - jnp.log(1.0 + jnp.exp(-jnp.abs(x)))


def _rms(x, g):
    return x * lax.rsqrt(jnp.mean(x * x, axis=-1, keepdims=True) + NORM_EPS) * g


def _const_spec(shape):
    nd = len(shape)
    return pl.BlockSpec(shape, lambda *_: (0,) * nd, pipeline_mode=pl.Buffered(1))


def _resident(operands):
    specs, args = [], []
    for a in operands:
        if isinstance(a, tuple):
            arr, layer = a
            nd = arr.ndim - 1
            specs.append(pl.BlockSpec((None,) + arr.shape[1:], lambda *_, layer=layer, nd=nd: (layer,) + (0,) * nd,
                                      pipeline_mode=pl.Buffered(1)))
            args.append(arr)
        else:
            specs.append(_const_spec(a.shape))
            args.append(a)
    return specs, args


def _params(sem):
    return pltpu.CompilerParams(dimension_semantics=sem, vmem_limit_bytes=VMEM_LIMIT)


def _ffn_stages(rows, x_ref, g_ref, wg_ref, wu_ref, wd_ref, fg_ref, o_ref, final):
    x = x_ref[rows, :]
    h = _rms(x, g_ref[...]).astype(BF16)
    yield
    gate = jnp.dot(h, wg_ref[...], preferred_element_type=F32)
    up = jnp.dot(h, wu_ref[...], preferred_element_type=F32)
    yield
    act = (gate * _sigmoid(gate) * up).astype(BF16)
    y = x + 0.5 * jnp.dot(act, wd_ref[...], preferred_element_type=F32)
    if final:
        y = _rms(y, fg_ref[...])
    o_ref[rows, :] = y
    yield


def _ffn_body(x_ref, g_ref, wg_ref, wu_ref, wd_ref, fg_ref, o_ref, *, final):
    half = x_ref.shape[0] // 2
    args = (x_ref, g_ref, wg_ref, wu_ref, wd_ref, fg_ref, o_ref, final)
    _weave(_ffn_stages(slice(0, half), *args), _ffn_stages(slice(half, 2 * half), *args))


def _ffn(x, consts, *, final, tm):
    t, d = x.shape
    row = pl.BlockSpec((tm, d), lambda i: (i, 0))
    specs, args = _resident(consts)
    return pl.pallas_call(
        functools.partial(_ffn_body, final=final),
        grid=(t // tm,),
        in_specs=[row] + specs,
        out_specs=row,
        out_shape=jax.ShapeDtypeStruct((t, d), F32),
        compiler_params=_params(("parallel",)),
        name="ffn",
    )(x, *args)


SEQ_BLOCK = 256
BLOCK_NCH = SEQ_BLOCK // CHUNK
E_NEG_HALF = 2.718281828459045 ** -0.5


def _rwkv_feature_stages(u, prev_row, next_row, prm, dir_refs, v_ref, bonus_ref, rg_ref):
    R, W = SEQ_BLOCK, RWKV_W
    (mu, w0f, w0b, a0f, a0b, w2, a2, g2, k_k, k_a, r_k, trif, trib, sel, gsum) = prm
    row = lax.broadcasted_iota(jnp.int32, (8, 1), 0)
    prev = pltpu.roll(u, 1, 0)
    nxt = pltpu.roll(u, R - 1, 0)
    prev = jnp.concatenate([jnp.where(row == 0, prev_row, prev[0:8]), prev[8:]], axis=0)
    nxt = jnp.concatenate([nxt[:R - 8], jnp.where(row == 7, next_row, nxt[R - 8:])], axis=0)
    u = u * (1.0 - mu) + (prev + nxt) * (0.5 * mu)
    yield
    r, k, v = u[:, 0:W], u[:, W:2 * W], u[:, 2 * W:3 * W]
    sm = u[:, 3 * W:3 * W + LANES]
    gd = u[:, 3 * W + LANES:3 * W + 2 * LANES]
    v_ref[...] = v.astype(BF16)
    zw = _mm(jnp.tanh(sm), w2)
    za = _mm(sm, a2)
    rg_ref[...] = _mm(_sigmoid(gd), g2).astype(rg_ref.dtype)
    kk = k * k_k
    kap = kk * lax.rsqrt(_group_sums(kk * kk, gsum, 1) + 1e-12)
    yield
    dirs = ((True, w0f, a0f, trif, slice(0, W)), (False, w0b, a0b, trib, slice(W, 2 * W)))
    lw, kd, beta, inc, tot = [], [], [], [], []
    for fwd, w0, a0, tri, sl in dirs:
        lw.append((-E_NEG_HALF) * _sigmoid(w0 + zw[:, sl]))
        a = _sigmoid(a0 + za[:, sl])
        kd.append(k * (1.0 + (a - 1.0) * k_a))
        beta.append(kap * a)
        yield
    bonus_ref[...] = (_group_sums(r * kd[0] * r_k, gsum, 1) * v).astype(bonus_ref.dtype)
    for d, (fwd, w0, a0, tri, sl) in enumerate(dirs):
        lw_parts = _split(lw[d], 2)
        inc.append(_mm_parts(tri, lw_parts))
        tot.append(_mm_parts(sel, lw_parts))
        yield
    for d, (fwd, w0, a0, tri, sl) in enumerate(dirs):
        ka_ref, rh_ref, kt_ref, bt_ref, dec_ref, kb_ref = dir_refs[d]
        e_exc = jnp.exp(inc[d] - lw[d])
        e_ninc = jnp.exp(-inc[d])
        ka_ref[...] = (kap * e_exc).astype(BF16)
        rh_ref[...] = (r * (jnp.exp(inc[d]) if fwd else e_exc)).astype(BF16)
        dec_ref[...] = jnp.exp(tot[d])
        yield
        k_t, b_t = kd[d] * e_ninc, beta[d] * e_ninc
        kt_ref[...] = k_t.astype(BF16)
        bt_ref[...] = b_t.astype(BF16)
        yield
        for j in range(BLOCK_NCH):
            rows = slice(j * CHUNK, (j + 1) * CHUNK)
            kb_ref[j * W:(j + 1) * W, :] = jnp.concatenate([k_t[rows], b_t[rows]], axis=0).T.astype(BF16)
        yield


def _column_dots(h, w_ref, pieces, width):
    for off, cols, ref in pieces:
        for a in range(0, cols, width):
            b = min(a + width, cols)
            ref[:, a:b] = jnp.dot(h, w_ref[:, off + a:off + b], preferred_element_type=F32).astype(ref.dtype)
            yield


INPROJ_BLOCKS = 2


def _inproj_body(x_ref, xp_ref, xn_ref, g_ref, w_ref,
                 mu_ref, w0f_ref, w0b_ref, a0f_ref, a0b_ref, w2_ref, a2_ref, g2_ref, kk_ref, ka_ref, rk_ref,
                 trif_ref, trib_ref, sel_ref, gsum_ref,
                 gla_ref, fnet_ref, gate_ref,
                 kaf_ref, rhf_ref, ktf_ref, btf_ref, decf_ref, kbf_ref,
                 kab_ref, rhb_ref, ktb_ref, btb_ref, decb_ref, kbb_ref,
                 v_ref, bonus_ref, rg_ref, *, blocks_per_seq):
    R = SEQ_BLOCK
    c0, c1, c2 = GLA_COLS, GLA_COLS + FNET_W, GLA_COLS + FNET_W + RWKV_COLS_PAD
    g = g_ref[...]
    norm = lambda rows: _rms(rows, g).astype(BF16)
    rwkv_cols = lambda hh: jnp.dot(hh, w_ref[:, c1:c2], preferred_element_type=F32)
    z_halo = rwkv_cols(norm(jnp.concatenate([xp_ref[...], xn_ref[...], x_ref[R - 8:R + 8, :]], axis=0)))
    pos = (pl.program_id(0) * INPROJ_BLOCKS) % blocks_per_seq
    edge_rows = ((jnp.where(pos > 0, z_halo[7:8, :], 0.0), z_halo[24:25, :]),
                 (z_halo[23:24, :], jnp.where(pos + 1 < blocks_per_seq - 1, z_halo[8:9, :], 0.0)))
    prm = tuple(ref[...] for ref in (mu_ref, w0f_ref, w0b_ref, a0f_ref, a0b_ref, w2_ref, a2_ref, g2_ref,
                                     kk_ref, ka_ref, rk_ref, trif_ref, trib_ref, sel_ref, gsum_ref))

    def rows_of(ref, blk):
        n = ref.shape[0] // INPROJ_BLOCKS
        return ref.at[pl.ds(blk * n, n)]

    h, u = {}, {}
    h[0] = norm(x_ref[0:R, :])
    u[0] = rwkv_cols(h[0])

    def elementwise():
        for blk in range(INPROJ_BLOCKS):
            dir_refs = tuple(tuple(rows_of(ref, blk) for ref in refs) for refs in
                             ((kaf_ref, rhf_ref, ktf_ref, btf_ref, decf_ref, kbf_ref),
                              (kab_ref, rhb_ref, ktb_ref, btb_ref, decb_ref, kbb_ref)))
            yield from _rwkv_feature_stages(u[blk], *edge_rows[blk], prm, dir_refs, rows_of(v_ref, blk),
                                            rows_of(bonus_ref, blk), rows_of(rg_ref, blk))

    def matmuls():
        h[1] = norm(x_ref[R:2 * R, :])
        u[1] = rwkv_cols(h[1])
        yield
        for blk in range(INPROJ_BLOCKS):
            yield from _column_dots(h[blk], w_ref, ((c2, GATE_COLS, rows_of(gate_ref, blk)),
                                                    (0, c0, rows_of(gla_ref, blk)),
                                                    (c0, FNET_W, rows_of(fnet_ref, blk))), 512)

    _weave(elementwise(), matmuls())


def _inproj(x, consts, *, seq):
    t, d = x.shape
    specs, args = _resident(consts)
    nblk = INPROJ_BLOCKS
    tm = nblk * SEQ_BLOCK
    assert (seq // SEQ_BLOCK) % nblk == 0
    sub = 8
    per = tm // sub
    row = lambda c: pl.BlockSpec((tm, c), lambda i: (i, 0))
    prev = pl.BlockSpec((sub, d), lambda i: (jnp.maximum(i * per - 1, 0), 0))
    nxt = pl.BlockSpec((sub, d), lambda i: (jnp.minimum((i + 1) * per, t // sub - 1), 0))
    dec = pl.BlockSpec((nblk * BLOCK_NCH * 8, RWKV_W), lambda i: (i, 0))
    op = jax.ShapeDtypeStruct((t, RWKV_W), BF16)
    dec_shape = jax.ShapeDtypeStruct((t // CHUNK * 8, RWKV_W), F32)
    cm = pl.BlockSpec((nblk * BLOCK_NCH * RWKV_W, LANES), lambda i: (i, 0))
    cm_shape = jax.ShapeDtypeStruct((t // CHUNK * RWKV_W, LANES), BF16)
    per_dir = lambda a, b, c: [a] * 4 + [b, c]
    return pl.pallas_call(
        functools.partial(_inproj_body, blocks_per_seq=seq // SEQ_BLOCK),
        grid=(t // tm,),
        in_specs=[row(d), prev, nxt] + specs,
        out_specs=[row(GLA_COLS), row(FNET_W), row(GATE_COLS)]
                  + per_dir(row(RWKV_W), dec, cm) * 2 + [row(RWKV_W)] * 3,
        out_shape=[jax.ShapeDtypeStruct((t, GLA_COLS), BF16),
                   jax.ShapeDtypeStruct((t, FNET_W), BF16),
                   jax.ShapeDtypeStruct((t, GATE_COLS), BF16)]
                  + per_dir(op, dec_shape, cm_shape) * 2 + [op] * 3,
        compiler_params=_params(("parallel",)),
        name="inproj",
    )(x, x, x, *args)


def _woven(*gens, shares=None):
    live = list(zip(gens, shares or (1,) * len(gens)))
    while live:
        for entry in list(live):
            g, n = entry
            for _ in range(n):
                try:
                    next(g)
                except StopIteration:
                    live.remove(entry)
                    break
                yield


def _weave(*gens, shares=None):
    for _ in _woven(*gens, shares=shares):
        pass


def _lane_masks():
    lane = lax.broadcasted_iota(jnp.int32, (CHUNK, LANES), 1)
    return lane < (LANES // 2), lane >= (LANES // 2)


def _masked_stack(x, lo, hi):
    return jnp.concatenate([jnp.where(lo, x, 0.0), jnp.where(hi, x, 0.0)], axis=0)


def _pair_masks(fwd, inclusive):
    n = 2 * CHUNK
    i = lax.broadcasted_iota(jnp.int32, (n, n), 0)
    j = lax.broadcasted_iota(jnp.int32, (n, n), 1)
    same = (i < CHUNK) == (j < CHUNK)
    strict = (j < i) if fwd else (j > i)
    if inclusive:
        return same & (strict | (i == j))
    return same & strict


GLA_PAIRS = GLA_HEADS // 2


def _gla_prep_stages(z_ref, up, bias, tri, sel, d, qd_ref, kd_ref, v_ref, dec_ref):
    L = CHUNK
    z = z_ref[...].astype(F32)
    q = z[:, 0:GLA_QK] * (GLA_DK ** -0.5)
    k = z[:, GLA_QK:2 * GLA_QK]
    v = z[:, 2 * GLA_QK:2 * GLA_QK + GLA_V]
    dn = z[:, GLA_COLS - LANES:GLA_COLS]
    for j in range(BLOCK_NCH):
        v_ref[d, j] = v[j * L:(j + 1) * L]
    la = _log_sigmoid(_mm(dn, up) + bias) * (1.0 / GLA_TAU)
    yield
    la_parts = _split(la, 2)
    b = _mm_parts(tri, la_parts)
    dec_ref[d] = jnp.exp(_mm_parts(sel, la_parts))
    yield
    for ref, val in ((qd_ref, q * jnp.exp(b)), (kd_ref, k * jnp.exp(-b))):
        for j in range(BLOCK_NCH):
            ref[d, j] = val[j * L:(j + 1) * L]
        yield


def _gla_step_stages(d, qd_ref, kd_ref, v_ref, dec_ref, st_ref, o_ref, masks):
    L = CHUNK
    lo, hi, m_read = masks
    units = [(j, p) for j in range(BLOCK_NCH) for p in range(GLA_PAIRS)]
    chunk = lambda j: j if d == 0 else BLOCK_NCH - 1 - j
    q_ms, kd_ms, v_st, sc, inc, o_intra = {}, {}, {}, {}, {}, {}
    for n, u in enumerate(units):
        j, p = u
        c = chunk(j)
        sl = pl.ds(p * LANES, LANES)
        q_ms[u] = _masked_stack(qd_ref[d, c, :, sl], lo, hi)
        kd_ms[u] = _masked_stack(kd_ref[d, c, :, sl], lo, hi)
        v_st[u] = jnp.concatenate([v_ref[d, c, :, pl.ds((2 * p) * GLA_DV, GLA_DV)],
                                   v_ref[d, c, :, pl.ds((2 * p + 1) * GLA_DV, GLA_DV)]], axis=0)
        sc[u] = jnp.where(m_read[d], _mm_nt(q_ms[u], kd_ms[u]), 0.0)
        inc[u] = _mm_tn(v_st[u], kd_ms[u])
        if n % 2:
            yield
    for n, u in enumerate(units):
        o_intra[u] = _mm(sc[u], v_st[u])
        if n % 2:
            yield
    for p in range(GLA_PAIRS):
        sl = pl.ds(p * LANES, LANES)
        st = st_ref[d, p]
        for j in range(BLOCK_NCH):
            u = (j, p)
            c = chunk(j)
            o = o_intra[u] + _mm_nt(q_ms[u], st)
            o_ref[pl.ds(c * L, L), pl.ds((2 * p) * GLA_DV, GLA_DV)] = o[0:L].astype(BF16)
            o_ref[pl.ds(c * L, L), pl.ds((2 * p + 1) * GLA_DV, GLA_DV)] = o[L:2 * L].astype(BF16)
            st = (st + inc[u]) * dec_ref[d, pl.ds(c * 8, 1), sl]
            if j % 2:
                yield
        st_ref[d, p] = st


def _gla_program(zf_ref, zb_ref, upf_ref, bf_ref, upb_ref, bb_ref, trif_ref, trib_ref, sel_ref,
                 of_ref, ob_ref, qd_ref, kd_ref, v_ref, dec_ref, st_ref):
    scr = (qd_ref, kd_ref, v_ref, dec_ref)
    lo, hi = _lane_masks()
    masks = (lo, hi, (_pair_masks(True, inclusive=True), _pair_masks(False, inclusive=False)))
    yield from _gla_prep_stages(zf_ref, upf_ref[...], bf_ref[...], trif_ref[...], sel_ref[...], 0, *scr)
    yield from _woven(_gla_prep_stages(zb_ref, upb_ref[...], bb_ref[...], trib_ref[...], sel_ref[...], 1, *scr),
                      _gla_step_stages(0, *scr, st_ref, of_ref, masks))
    yield from _gla_step_stages(1, *scr, st_ref, ob_ref, masks)


def _fnet_body(z_ref, cc_ref, sc_ref, cp_ref, sp_ref, alt_ref, rev_ref, o_ref, a_ref, b_ref, m_ref, *, rows):
    s = z_ref.shape[0]
    h = s // 2
    z = z_ref[...]
    a_ref[...] = jnp.dot(z, cc_ref[...], preferred_element_type=F32).astype(BF16)
    b_ref[...] = jnp.dot(z, sc_ref[...], preferred_element_type=F32).astype(BF16)
    for i in range(h // rows):
        r = slice(i * rows, (i + 1) * rows)
        p = jnp.dot(cp_ref[r, :], a_ref[...], preferred_element_type=F32)
        q = jnp.dot(sp_ref[r, :], b_ref[...], preferred_element_type=F32)
        o_ref[r, :] = (p - q).astype(o_ref.dtype)
        m_ref[r, :] = (p + q).astype(BF16)
    for i in range(h // rows):
        r = slice(i * rows, (i + 1) * rows)
        o_ref[h + i * rows:h + (i + 1) * rows, :] = jnp.dot(
            rev_ref[r, :], m_ref[...], preferred_element_type=F32).astype(o_ref.dtype)
    mid = jnp.dot(alt_ref[...], a_ref[...], preferred_element_type=F32)
    o_ref[h:h + 1, :] = mid[0:1].astype(o_ref.dtype)


def _fnet(z, tables):
    bsz, s, w = z.shape
    rows = min(s // 2, 512)
    blk = lambda: pl.BlockSpec((None, s, w), lambda b: (b, 0, 0))
    return pl.pallas_call(
        functools.partial(_fnet_body, rows=rows),
        grid=(bsz,),
        in_specs=[blk()] + [_const_spec(a.shape) for a in tables],
        out_specs=blk(),
        out_shape=jax.ShapeDtypeStruct((bsz, s, w), BF16),
        scratch_shapes=[pltpu.VMEM((s, w), BF16), pltpu.VMEM((s, w), BF16), pltpu.VMEM((s // 2, w), BF16)],
        compiler_params=_params(("parallel",)),
        name="fnet",
    )(z, *tables)


RWKV_BLOCK = SEQ_BLOCK
RWKV_NCH = BLOCK_NCH
RWKV_PAIRS = RWKV_HEADS // 2


def _chunk(ref, j, sl):
    return ref[pl.ds(j * CHUNK, CHUNK), sl]


def _rwkv_solve_stages(units, ops, tinv_ref, pkb_ref, mkv_ref, masks):
    L = CHUNK
    n2 = 2 * L
    lo, hi, eye, m_strict, m_read = masks
    top = (lax.broadcasted_iota(jnp.int32, (2 * n2, n2), 0) & (n2 - 1)) < L
    t_inv, pw, s_kb, v_ms, m_k = {}, {}, {}, {}, {}
    for n, u in enumerate(units):
        d, j, p = u
        sl = pl.ds(p * LANES, LANES)
        kaph_ref, rh_ref, _, _, v_ref, _, kbcm_ref = ops[d]
        xs = jnp.concatenate([_masked_stack(_chunk(kaph_ref, j, sl), lo, hi),
                              _masked_stack(_chunk(rh_ref, j, sl), lo, hi)], axis=0)
        s_kb[u] = _mm(xs, kbcm_ref[pl.ds(j * RWKV_W + p * LANES, LANES), :])
        v_ms[u] = _masked_stack(_chunk(v_ref, j, sl), lo, hi)
        if n % 2:
            yield
    for n, u in enumerate(units):
        d, j, p = u
        s_sw = pltpu.roll(s_kb[u], L, 1)
        vs_k = jnp.where(top, s_kb[u], s_sw)
        vs_b = jnp.where(top, s_sw, s_kb[u])
        m_k[u] = jnp.where(m_strict[d], vs_k[0:n2], 0.0).astype(BF16)
        nmat = jnp.where(m_strict[d], -vs_b[0:n2], 0.0)
        p_k = jnp.where(m_read[d], vs_k[n2:2 * n2], 0.0)
        p_b = jnp.where(m_read[d], -vs_b[n2:2 * n2], 0.0)
        pkb_ref[d, j, p] = jnp.concatenate([p_k, p_b], axis=1).astype(BF16)
        t_inv[u] = eye + nmat
        pw[u] = nmat.astype(BF16)
        if n % 2:
            yield
    for n, u in enumerate(units):
        d, j, p = u
        mkv_ref[d, j, p] = _mm(m_k[u], v_ms[u])
        if n % 2:
            yield
    for n, u in enumerate(units):
        pw[u] = jnp.dot(pw[u], pw[u], preferred_element_type=F32).astype(BF16)
        if n % 2:
            yield
    for _ in range(4):
        for n, u in enumerate(units):
            both = jnp.dot(jnp.concatenate([t_inv[u].astype(BF16), pw[u]], axis=0), pw[u],
                           preferred_element_type=F32)
            t_inv[u] = t_inv[u] + both[0:n2]
            pw[u] = both[n2:2 * n2].astype(BF16)
            if n % 2:
                yield
    for n, u in enumerate(units):
        d, j, p = u
        tinv_ref[d, j, p] = (t_inv[u] + jnp.dot(t_inv[u].astype(BF16), pw[u],
                                                preferred_element_type=F32)).astype(BF16)
        if n % 2:
            yield


def _rwkv_scan_stages(steps, ops, tinv_ref, pkb_ref, mkv_ref, st_ref, y_refs, masks):
    L = CHUNK
    n2 = 2 * L
    lo, hi = masks[0], masks[1]
    units = [(d, p) for d in range(2) for p in range(RWKV_PAIRS)]
    for j in steps:
        cj = {0: j, 1: RWKV_NCH - 1 - j}
        v_ms, xh, uu = {}, {}, {}
        for (d, p) in units:
            sl = pl.ds(p * LANES, LANES)
            xs = jnp.concatenate([_masked_stack(_chunk(ops[d][0], cj[d], sl), lo, hi),
                                  _masked_stack(_chunk(ops[d][1], cj[d], sl), lo, hi)], axis=0)
            xh[d, p] = _mm_nt(xs, st_ref[d, p])
            yield
        for (d, p) in units:
            sl = pl.ds(p * LANES, LANES)
            v_ms[d, p] = _masked_stack(_chunk(ops[d][4], cj[d], sl), lo, hi)
            uu[d, p] = _mm(tinv_ref[d, cj[d], p], xh[d, p][0:n2] + mkv_ref[d, cj[d], p])
            yield
        for (d, p) in units:
            sl = pl.ds(p * LANES, LANES)
            y = xh[d, p][n2:2 * n2] + _mm(pkb_ref[d, cj[d], p], jnp.concatenate([v_ms[d, p], uu[d, p]], axis=0))
            y_refs[d][pl.ds(cj[d] * L, L), sl] = (y[0:L] + y[L:n2]).astype(BF16)
            kb = jnp.concatenate([_masked_stack(_chunk(ops[d][2], cj[d], sl), lo, hi),
                                  _masked_stack(_chunk(ops[d][3], cj[d], sl), lo, hi)], axis=0)
            dec = ops[d][5][pl.ds(cj[d] * 8, 1), sl]
            st_ref[d, p] = (st_ref[d, p] + _mm_tn(jnp.concatenate([v_ms[d, p], -uu[d, p]], axis=0), kb)) * dec
            yield


def _rwkv_program(kaf_ref, rhf_ref, ktf_ref, btf_ref, vf_ref, decf_ref, kbf_ref,
                  kab_ref, rhb_ref, ktb_ref, btb_ref, vb_ref, decb_ref, kbb_ref,
                  yf_ref, yb_ref, tinv_ref, pkb_ref, mkv_ref, st_ref):
    ops =((kaf_ref, rhf_ref, ktf_ref, btf_ref, vf_ref, decf_ref, kbf_ref),
           (kab_ref, rhb_ref, ktb_ref, btb_ref, vb_ref, decb_ref, kbb_ref))
    lo, hi = _lane_masks()
    n2 = 2 * CHUNK
    ii = lax.broadcasted_iota(jnp.int32, (n2, n2), 0)
    jj = lax.broadcasted_iota(jnp.int32, (n2, n2), 1)
    eye = (ii == jj).astype(F32)
    m_strict = (_pair_masks(True, inclusive=False), _pair_masks(False, inclusive=False))
    m_read = (_pair_masks(True, inclusive=True), _pair_masks(False, inclusive=False))
    masks = (lo, hi, eye, m_strict, m_read)

    def solve(steps):
        units = [(d, j if d == 0 else RWKV_NCH - 1 - j, p)
                 for j in steps for d in range(2) for p in range(RWKV_PAIRS)]
        return _rwkv_solve_stages(units, ops, tinv_ref, pkb_ref, mkv_ref, masks)

    def scan(steps):
        return _rwkv_scan_stages(steps, ops, tinv_ref, pkb_ref, mkv_ref, st_ref, (yf_ref, yb_ref), masks)

    half = RWKV_NCH // 2
    first, second = list(range(half)), list(range(half, RWKV_NCH))
    yield from solve(first)
    yield from _woven(solve(second), scan(first))
    yield from scan(second)


N_GLA_IN, N_RWKV_IN = 9, 14


def _recurrences_body(*refs):
    a, b = N_GLA_IN, N_GLA_IN + N_RWKV_IN
    gla_refs = refs[:a] + refs[b:b + 2] + refs[b + 4:b + 9]
    rwkv_refs = refs[a:b] + refs[b + 2:b + 4] + refs[b + 9:]

    @pl.when(pl.program_id(1) == 0)
    def _():
        for st_ref in (gla_refs[-1], rwkv_refs[-1]):
            st_ref[...] = jnp.zeros_like(st_ref)

    _weave(_rwkv_program(*rwkv_refs), _gla_program(*gla_refs), shares=(7, 1))


def _recurrences(z_gla, gla_consts, fwd_ops, bwd_ops, bsz, s):
    R = RWKV_BLOCK
    nb = s // R
    fc = lambda i: i
    bc = lambda i: nb - 1 - i
    blk = lambda cm: pl.BlockSpec((None, R, RWKV_W), lambda b, i: (b, cm(i), 0))
    dec = lambda cm: pl.BlockSpec((None, RWKV_NCH * 8, RWKV_W), lambda b, i: (b, cm(i), 0))
    cm = lambda cm_: pl.BlockSpec((None, RWKV_NCH * RWKV_W, LANES), lambda b, i: (b, cm_(i), 0))
    seq = lambda a: a.reshape(bsz, -1, a.shape[-1])
    zspec = lambda cm_: pl.BlockSpec((None, R, GLA_COLS), lambda b, i: (b, cm_(i), 0))
    gla_consts, gla_args = _resident(gla_consts)
    assert 2 + len(gla_args) == N_GLA_IN and len(fwd_ops) + len(bwd_ops) == N_RWKV_IN
    out = lambda w: jax.ShapeDtypeStruct((bsz, s, w), BF16)
    per_chunk = lambda rows, w: pltpu.VMEM((2, BLOCK_NCH, rows, w), F32)
    gla_scratch = [per_chunk(CHUNK, GLA_QK), per_chunk(CHUNK, GLA_QK), per_chunk(CHUNK, GLA_V),
                   pltpu.VMEM((2, BLOCK_NCH * 8, GLA_QK), F32),
                   pltpu.VMEM((2, GLA_PAIRS, GLA_DV, LANES), F32)]
    unit = (2, RWKV_NCH, RWKV_PAIRS)
    rwkv_scratch = [pltpu.VMEM(unit + (LANES, LANES), BF16),
                    pltpu.VMEM(unit + (LANES, 2 * LANES), BF16),
                    pltpu.VMEM(unit + (LANES, LANES), F32),
                    pltpu.VMEM((2, RWKV_PAIRS, LANES, LANES), F32)]
    return pl.pallas_call(
        _recurrences_body,
        grid=(bsz, nb),
        in_specs=[zspec(fc), zspec(bc)] + gla_consts
                 + [blk(fc)] * 5 + [dec(fc), cm(fc)] + [blk(bc)] * 5 + [dec(bc), cm(bc)],
        out_specs=[pl.BlockSpec((None, R, GLA_V), lambda b, i: (b, fc(i), 0)),
                   pl.BlockSpec((None, R, GLA_V), lambda b, i: (b, bc(i), 0)), blk(fc), blk(bc)],
        out_shape=[out(GLA_V)] * 2 + [out(RWKV_W)] * 2,
        scratch_shapes=gla_scratch + rwkv_scratch,
        compiler_params=_params(("parallel", "arbitrary")),
        name="recurrences",
    )(z_gla, z_gla, *gla_args, *[seq(a) for a in fwd_ops], *[seq(a) for a in bwd_ops])


def _merge_body(x_ref, gof_ref, gob_ref, gr_ref, fn_ref, ryf_ref, ryb_ref, rbon_ref, rg_ref, gate_ref,
                gn_ref, lng_ref, lnb_ref, gavg_ref, pg_ref, pf_ref, pr_ref, wo_ref, o_ref):
    o = gof_ref[...].astype(F32) + gob_ref[...].astype(F32)
    parts = []
    for h in range(GLA_HEADS):
        oh = o[:, h * GLA_DV:(h + 1) * GLA_DV]
        parts.append(oh * lax.rsqrt(jnp.mean(oh * oh, axis=-1, keepdims=True) + NORM_EPS))
    rg = gr_ref[...].astype(F32)
    y_a = jnp.concatenate(parts, axis=1) * gn_ref[...] * (rg * _sigmoid(rg))
    y = ryf_ref[...].astype(F32) + ryb_ref[...].astype(F32)
    mean = _group_sums(y, gavg_ref[...], 2)
    yc = y - mean
    var = _group_sums(yc * yc, gavg_ref[...], 1)
    y_c = (yc * lax.rsqrt(var + RWKV_LN_EPS) * lng_ref[...] + lnb_ref[...] + rbon_ref[...].astype(F32)) * rg_ref[...].astype(F32)
    d = x_ref.shape[1]
    gate = gate_ref[...].astype(F32)
    merged = (_sigmoid(gate[:, 0:d]) * _mm(y_a, pg_ref[...])
              + _sigmoid(gate[:, d:2 * d]) * _mm(fn_ref[...], pf_ref[...])
              + _sigmoid(gate[:, 2 * d:3 * d]) * _mm(y_c, pr_ref[...]))
    o_ref[...] = x_ref[...] + _mm(merged, wo_ref[...])


def _merge(x, gla_of, gla_ob, z_gla, y_fnet, r_yf, r_yb, r_bonus, r_g, z_gate,
           gn, lng, lnb, gavg, pg, pf, pr, wo, *, tm):
    t, d = x.shape
    row = lambda c: pl.BlockSpec((tm, c), lambda i: (i, 0))
    r_col = (2 * GLA_QK + GLA_V) // GLA_V
    gr_spec = pl.BlockSpec((tm, GLA_V), lambda i: (i, r_col))
    specs, args = _resident((gn, lng, lnb, gavg, pg, pf, pr, wo))
    return pl.pallas_call(
        _merge_body,
        grid=(t // tm,),
        in_specs=[row(d), row(GLA_V), row(GLA_V), gr_spec, row(FNET_W), row(RWKV_W), row(RWKV_W),
                  row(RWKV_W), row(RWKV_W), row(GATE_COLS)] + specs,
        out_specs=row(d),
        out_shape=jax.ShapeDtypeStruct((t, d), F32),
        compiler_params=_params(("parallel",)),
        name="merge",
    )(x, gla_of, gla_ob, z_gla, y_fnet, r_yf, r_yb, r_bonus, r_g, z_gate, *args)


def _dft_tables(s):
    def cos_sin(n, rows, cols):
        ang = (2.0 * jnp.pi / n) * ((jnp.arange(rows, dtype=jnp.int32)[:, None] * cols[None, :]) % n).astype(F32)
        return jnp.cos(ang), jnp.sin(ang)

    def dft(n, rows):
        step = 64 if n % 64 == 0 and n > 64 else n
        c_lo, s_lo = cos_sin(n, rows, jnp.arange(step, dtype=jnp.int32))
        c_hi, s_hi = cos_sin(n, rows, jnp.arange(n // step, dtype=jnp.int32) * step)
        cos = c_hi[:, :, None] * c_lo[:, None, :] - s_hi[:, :, None] * s_lo[:, None, :]
        sin = s_hi[:, :, None] * c_lo[:, None, :] + c_hi[:, :, None] * s_lo[:, None, :]
        return cos.reshape(rows, n) * (n ** -0.5), sin.reshape(rows, n) * (n ** -0.5)

    cg, sg = dft(FNET_GC, FNET_GC)
    eye = jnp.eye(FNET_GROUPS, dtype=F32)
    cc = jnp.kron(eye, cg).astype(BF16)
    sc = jnp.kron(eye, sg).astype(BF16)
    h = s // 2
    cp, sp = dft(s, h)
    m = jnp.arange(s)
    alt = jnp.broadcast_to(jnp.where(m % 2 == 0, 1.0, -1.0) * (s ** -0.5), (8, s))
    r = jnp.arange(h)
    rev = (r[None, :] == h - r[:, None]).astype(BF16)
    return cc, sc, cp.astype(BF16), sp.astype(BF16), alt.astype(BF16), rev


def _tri(fwd, n):
    i = jnp.arange(n)
    same = (i[None, :] // CHUNK) == (i[:, None] // CHUNK)
    m = (i[None, :] <= i[:, None]) if fwd else (i[None, :] >= i[:, None])
    return (m & same).astype(BF16)


def _chunk_sel(n):
    return (jnp.arange(n // CHUNK * 8)[:, None] // 8 == jnp.arange(n)[None, :] // CHUNK).astype(BF16)


def _pad_rows(w, offset, total):
    return jnp.zeros((w.shape[0], total, w.shape[2]), F32).at[:, offset:offset + w.shape[1]].set(w.astype(F32))


def kernel(x, ffn1_norm, ffn1_gate, ffn1_up, ffn1_down, mix_norm, w_in, gla_up_f, gla_bias_f, gla_up_b, gla_bias_b, gla_norm, rwkv_mu, rwkv_w0_f, rwkv_w2_f, rwkv_w0_b, rwkv_w2_b, rwkv_a0_f, rwkv_a2_f, rwkv_a0_b, rwkv_a2_b, rwkv_g2, rwkv_k_k, rwkv_k_a, rwkv_r_k, rwkv_ln_g, rwkv_ln_b, proj_gla, proj_fnet, proj_rwkv, w_out, ffn2_norm, ffn2_gate, ffn2_up, ffn2_down, final_norm):
    bsz, s, d = x.shape
    depth = w_in.shape[0]
    t = bsz * s
    tm = 512 if t % 512 == 0 else 256
    assert s % SEQ_BLOCK == 0 and t % tm == 0 and d * 3 == GATE_COLS

    fnet_tables = _dft_tables(s)
    trif_r, trib_r, sel_r = _tri(True, SEQ_BLOCK), _tri(False, SEQ_BLOCK), _chunk_sel(SEQ_BLOCK)
    head_of = jnp.arange(LANES) // RWKV_N
    same_head = head_of[:, None] == head_of[None, :]
    gsum = same_head.astype(BF16)
    gavg = (same_head.astype(F32) * (1.0 / RWKV_N)).astype(BF16)
    rows = lambda a: a.astype(F32).reshape(depth, 1, -1)
    bf16 = lambda a: a.astype(BF16)
    fnorm = final_norm.astype(F32).reshape(1, -1)
    ffn1 = (rows(ffn1_norm), bf16(ffn1_gate), bf16(ffn1_up), bf16(ffn1_down))
    ffn2 = (rows(ffn2_norm), bf16(ffn2_gate), bf16(ffn2_up), bf16(ffn2_down))
    o_f, o_r, o_g = GLA_COLS_RAW, GLA_COLS_RAW + FNET_W, GLA_COLS_RAW + FNET_W + RWKV_COLS
    w_pad = bf16(jnp.concatenate([
        w_in[:, :, :o_f], jnp.zeros((depth, d, GLA_COLS - GLA_COLS_RAW), w_in.dtype),
        w_in[:, :, o_f:o_r],
        w_in[:, :, o_r:o_g], jnp.zeros((depth, d, RWKV_COLS_PAD - RWKV_COLS), w_in.dtype),
        w_in[:, :, o_g:]], axis=2))
    mu = jnp.concatenate([rwkv_mu.astype(F32), jnp.zeros((depth, RWKV_COLS_PAD - RWKV_COLS), F32)],
                         axis=1).reshape(depth, 1, -1)
    inproj_layered = (
        rows(mix_norm), w_pad, mu, rows(rwkv_w0_f), rows(rwkv_w0_b), rows(rwkv_a0_f), rows(rwkv_a0_b),
        jnp.concatenate([_pad_rows(rwkv_w2_f, 0, LANES), _pad_rows(rwkv_w2_b, 32, LANES)], axis=2),
        jnp.concatenate([_pad_rows(rwkv_a2_f, 64, LANES), _pad_rows(rwkv_a2_b, 96, LANES)], axis=2),
        _pad_rows(rwkv_g2, 0, LANES), rows(rwkv_k_k), rows(rwkv_k_a), rows(rwkv_r_k))
    gla_layered = (_pad_rows(gla_up_f, 0, LANES), rows(gla_bias_f), _pad_rows(gla_up_b, GLA_RANK, LANES),
                   rows(gla_bias_b))
    merge_layered = (rows(gla_norm), rows(rwkv_ln_g), rows(rwkv_ln_b))
    proj_layered = (bf16(proj_gla), bf16(proj_fnet), bf16(proj_rwkv), bf16(w_out))

    x2 = x.reshape(t, d)
    for l in range(depth):
        at = lambda arrays: tuple((a, l) for a in arrays)
        x2 = _ffn(x2, at(ffn1) + (fnorm,), final=False, tm=tm)

        (z_gla, z_fnet, z_gate, ka_f, rh_f, kt_f, bt_f, dec_f, kb_f, ka_b, rh_b, kt_b, bt_b, dec_b, kb_b,
         r_v, r_bonus, r_g) = _inproj(x2, at(inproj_layered) + (trif_r, trib_r, sel_r, gsum), seq=s)

        y_fnet = _fnet(z_fnet.reshape(bsz, s, FNET_W), fnet_tables)

        gla_of, gla_ob, r_yf, r_yb = _recurrences(
            z_gla.reshape(bsz, s, GLA_COLS), at(gla_layered) + (trif_r, trib_r, sel_r),
            (ka_f, rh_f, kt_f, bt_f, r_v, dec_f, kb_f), (ka_b, rh_b, kt_b, bt_b, r_v, dec_b, kb_b), bsz, s)

        flat = lambda a: a.reshape(t, a.shape[-1])
        x2 = _merge(x2, flat(gla_of), flat(gla_ob), z_gla, flat(y_fnet), flat(r_yf), flat(r_yb),
                    r_bonus, r_g, z_gate, *at(merge_layered), gavg, *at(proj_layered), tm=tm)

        x2 = _ffn(x2, at(ffn2) + (fnorm,), final=(l == depth - 1), tm=tm)
    return x2.reshape(bsz, s, d)
```

```python
import functools

import jax
import jax.numpy as jnp
from jax import lax
from jax.experimental import pallas as pl
from jax.experimental.pallas import tpu as pltpu

F32 = jnp.float32
BF16 = jnp.bfloat16

NORM_EPS = 1e-6
RWKV_LN_EPS = 64e-5
GLA_TAU = 16.0

GLA_HEADS, GLA_DK, GLA_DV, GLA_RANK = 4, 64, 128, 16
GLA_QK, GLA_V = GLA_HEADS * GLA_DK, GLA_HEADS * GLA_DV
FNET_GROUPS, FNET_GC = 4, 128
FNET_W = FNET_GROUPS * FNET_GC
RWKV_HEADS, RWKV_N = 8, 64
RWKV_W = RWKV_HEADS * RWKV_N
RWKV_LOWRANK = 128
RWKV_GATE_RANK = 96
RWKV_COLS = 3 * RWKV_W + RWKV_LOWRANK + RWKV_GATE_RANK

LANES = 128
SUBLANES = 8
CHUNK = 64

GLA_COLS_RAW = 2 * GLA_QK + 2 * GLA_V + 2 * GLA_RANK
GLA_COLS = -(-GLA_COLS_RAW // LANES) * LANES
RWKV_COLS_PAD = -(-RWKV_COLS // LANES) * LANES
GATE_COLS = 3 * 1024

VMEM_V7X = 64 * 1024 * 1024
VMEM_LIMIT = VMEM_V7X - 8 * 1024 * 1024


def _mm(a, b):
    return jnp.dot(a.astype(BF16), b.astype(BF16), preferred_element_type=F32)


def _mm_nt(a, b):
    return lax.dot_general(a.astype(BF16), b.astype(BF16), (((1,), (1,)), ((), ())),
                           preferred_element_type=F32)


def _mm_tn(a, b):
    return lax.dot_general(a.astype(BF16), b.astype(BF16), (((0,), (0,)), ((), ())),
                           preferred_element_type=F32)


def _split(x, n):
    pieces = []
    for _ in range(n - 1):
        p = x.astype(BF16)
        pieces.append(p)
        x = x - p.astype(F32)
    pieces.append(x.astype(BF16))
    return pieces


def _group_sums(a, w, n):
    parts = _split(a, n)
    tiles = []
    for c in range(0, a.shape[1], LANES):
        out = None
        for p in parts:
            d = jnp.dot(p[:, c:c + LANES], w, preferred_element_type=F32)
            out = d if out is None else out + d
        tiles.append(out)
    return jnp.concatenate(tiles, axis=1)


def _mm_parts(w, parts):
    out = None
    for p in parts:
        d = jnp.dot(w, p, preferred_element_type=F32)
        out = d if out is None else out + d
    return out


def _sigmoid(x):
    return 0.5 * jnp.tanh(0.5 * x) + 0.5


def _log_sigmoid(x):
    return jnp.minimum(x, 0.0) - jnp.log(1.0 + jnp.exp(-jnp.abs(x)))


def _rms(x, g):
    return x * lax.rsqrt(jnp.mean(x * x, axis=-1, keepdims=True) + NORM_EPS) * g


def _const_spec(shape):
    nd = len(shape)
    return pl.BlockSpec(shape, lambda *_: (0,) * nd, pipeline_mode=pl.Buffered(1))


def _resident(operands):
    specs, args = [], []
    for a in operands:
        if isinstance(a, tuple):
            arr, layer = a
            nd = arr.ndim - 1
            specs.append(pl.BlockSpec((None,) + arr.shape[1:], lambda *_, layer=layer, nd=nd: (layer,) + (0,) * nd,
                                      pipeline_mode=pl.Buffered(1)))
            args.append(arr)
        else:
            specs.append(_const_spec(a.shape))
            args.append(a)
    return specs, args


def _params(sem):
    return pltpu.CompilerParams(dimension_semantics=sem, vmem_limit_bytes=VMEM_LIMIT)


def _ffn_stages(rows, x_ref, g_ref, wg_ref, wu_ref, wd_ref, fg_ref, o_ref, final):
    x = x_ref[rows, :]
    h = _rms(x, g_ref[...]).astype(BF16)
    yield
    gate = jnp.dot(h, wg_ref[...], preferred_element_type=F32)
    up = jnp.dot(h, wu_ref[...], preferred_element_type=F32)
    yield
    act = (gate * _sigmoid(gate) * up).astype(BF16)
    y = x + 0.5 * jnp.dot(act, wd_ref[...], preferred_element_type=F32)
    if final:
        y = _rms(y, fg_ref[...])
    o_ref[rows, :] = y
    yield


def _ffn_body(x_ref, g_ref, wg_ref, wu_ref, wd_ref, fg_ref, o_ref, *, final):
    half = x_ref.shape[0] // 2
    args = (x_ref, g_ref, wg_ref, wu_ref, wd_ref, fg_ref, o_ref, final)
    _weave(_ffn_stages(slice(0, half), *args), _ffn_stages(slice(half, 2 * half), *args))


def _ffn(x, consts, *, final, tm):
    t, d = x.shape
    row = pl.BlockSpec((tm, d), lambda i: (i, 0))
    specs, args = _resident(consts)
    return pl.pallas_call(
        functools.partial(_ffn_body, final=final),
        grid=(t // tm,),
        in_specs=[row] + specs,
        out_specs=row,
        out_shape=jax.ShapeDtypeStruct((t, d), F32),
        compiler_params=_params(("parallel",)),
        name="ffn",
    )(x, *args)


SEQ_BLOCK = 256
BLOCK_NCH = SEQ_BLOCK // CHUNK
E_NEG_HALF = 2.718281828459045 ** -0.5


def _rwkv_feature_stages(u, prev_row, next_row, prm, dir_refs, v_ref, bonus_ref, rg_ref):
    R, W = SEQ_BLOCK, RWKV_W
    (mu, w0f, w0b, a0f, a0b, w2, a2, g2, k_k, k_a, r_k, trif, trib, sel, gsum) = prm
    row = lax.broadcasted_iota(jnp.int32, (8, 1), 0)
    prev = pltpu.roll(u, 1, 0)
    nxt = pltpu.roll(u, R - 1, 0)
    prev = jnp.concatenate([jnp.where(row == 0, prev_row, prev[0:8]), prev[8:]], axis=0)
    nxt = jnp.concatenate([nxt[:R - 8], jnp.where(row == 7, next_row, nxt[R - 8:])], axis=0)
    u = u * (1.0 - mu) + (prev + nxt) * (0.5 * mu)
    yield
    r, k, v = u[:, 0:W], u[:, W:2 * W], u[:, 2 * W:3 * W]
    sm = u[:, 3 * W:3 * W + LANES]
    gd = u[:, 3 * W + LANES:3 * W + 2 * LANES]
    v_ref[...] = v.astype(BF16)
    zw = _mm(jnp.tanh(sm), w2)
    za = _mm(sm, a2)
    rg_ref[...] = _mm(_sigmoid(gd), g2).astype(rg_ref.dtype)
    kk = k * k_k
    kap = kk * lax.rsqrt(_group_sums(kk * kk, gsum, 1) + 1e-12)
    yield
    dirs = ((True, w0f, a0f, trif, slice(0, W)), (False, w0b, a0b, trib, slice(W, 2 * W)))
    lw, kd, beta, inc, tot = [], [], [], [], []
    for fwd, w0, a0, tri, sl in dirs:
        lw.append((-E_NEG_HALF) * _sigmoid(w0 + zw[:, sl]))
        a = _sigmoid(a0 + za[:, sl])
        kd.append(k * (1.0 + (a - 1.0) * k_a))
        beta.append(kap * a)
        yield
    bonus_ref[...] = (_group_sums(r * kd[0] * r_k, gsum, 1) * v).astype(bonus_ref.dtype)
    for d, (fwd, w0, a0, tri, sl) in enumerate(dirs):
        lw_parts = _split(lw[d], 2)
        inc.append(_mm_parts(tri, lw_parts))
        tot.append(_mm_parts(sel, lw_parts))
        yield
    for d, (fwd, w0, a0, tri, sl) in enumerate(dirs):
        ka_ref, rh_ref, kt_ref, bt_ref, dec_ref, kb_ref = dir_refs[d]
        e_exc = jnp.exp(inc[d] - lw[d])
        e_ninc = jnp.exp(-inc[d])
        ka_ref[...] = (kap * e_exc).astype(BF16)
        rh_ref[...] = (r * (jnp.exp(inc[d]) if fwd else e_exc)).astype(BF16)
        dec_ref[...] = jnp.exp(tot[d])
        yield
        k_t, b_t = kd[d] * e_ninc, beta[d] * e_ninc
        kt_ref[...] = k_t.astype(BF16)
        bt_ref[...] = b_t.astype(BF16)
        yield
        for j in range(BLOCK_NCH):
            rows = slice(j * CHUNK, (j + 1) * CHUNK)
            kb_ref[j * W:(j + 1) * W, :] = jnp.concatenate([k_t[rows], b_t[rows]], axis=0).T.astype(BF16)
        yield


def _column_dots(h, w_ref, pieces, width):
    for off, cols, ref in pieces:
        for a in range(0, cols, width):
            b = min(a + width, cols)
            ref[:, a:b] = jnp.dot(h, w_ref[:, off + a:off + b], preferred_element_type=F32).astype(ref.dtype)
            yield


INPROJ_BLOCKS = 2
INPROJ_SLAB = 512


def _inproj_body(x_ref, xp_ref, xn_ref, g_ref, w_ref,
                 mu_ref, w0f_ref, w0b_ref, a0f_ref, a0b_ref, w2_ref, a2_ref, g2_ref, kk_ref, ka_ref, rk_ref,
                 trif_ref, trib_ref, sel_ref, gsum_ref,
                 gla_ref, fnet_ref, gate_ref,
                 opf_ref, decf_ref, kbf_ref, opb_ref, decb_ref, kbb_ref, tok_ref, *, blocks_per_seq):
    R, W = SEQ_BLOCK, RWKV_W
    c0, c1, c2 = GLA_COLS, GLA_COLS + FNET_W, GLA_COLS + FNET_W + RWKV_COLS_PAD
    g = g_ref[...]
    norm = lambda rows: _rms(rows, g).astype(BF16)
    rwkv_cols = lambda hh: jnp.dot(hh, w_ref[:, c1:c2], preferred_element_type=F32)
    z_halo = rwkv_cols(norm(jnp.concatenate([xp_ref[...], xn_ref[...], x_ref[R - 8:R + 8, :]], axis=0)))
    pos = (pl.program_id(0) * INPROJ_BLOCKS) % blocks_per_seq
    edge_rows = ((jnp.where(pos > 0, z_halo[7:8, :], 0.0), z_halo[24:25, :]),
                 (z_halo[23:24, :], jnp.where(pos + 1 < blocks_per_seq - 1, z_halo[8:9, :], 0.0)))
    prm = tuple(ref[...] for ref in (mu_ref, w0f_ref, w0b_ref, a0f_ref, a0b_ref, w2_ref, a2_ref, g2_ref,
                                     kk_ref, ka_ref, rk_ref, trif_ref, trib_ref, sel_ref, gsum_ref))

    def rows_of(ref, blk):
        n = ref.shape[0] // INPROJ_BLOCKS
        return ref.at[pl.ds(blk * n, n)]

    def cols_of(ref, k):
        return ref.at[:, pl.ds(k * W, W)]

    h, u = {}, {}
    h[0] = norm(x_ref[0:R, :])
    u[0] = rwkv_cols(h[0])

    def elementwise():
        for blk in range(INPROJ_BLOCKS):
            dir_refs = tuple(tuple(cols_of(rows_of(op_ref, blk), k) for k in range(4))
                             + (rows_of(dec_ref, blk), rows_of(kb_ref, blk))
                             for op_ref, dec_ref, kb_ref in ((opf_ref, decf_ref, kbf_ref), (opb_ref, decb_ref, kbb_ref)))
            tok = rows_of(tok_ref, blk)
            yield from _rwkv_feature_stages(u[blk], *edge_rows[blk], prm, dir_refs,
                                            cols_of(tok, 0), cols_of(tok, 1), cols_of(tok, 2))

    def matmuls():
        h[1] = norm(x_ref[R:2 * R, :])
        u[1] = rwkv_cols(h[1])
        yield
        for blk in range(INPROJ_BLOCKS):
            yield from _column_dots(h[blk], w_ref, ((c2, GATE_COLS, rows_of(gate_ref, blk)),
                                                    (0, c0, rows_of(gla_ref, blk)),
                                                    (c0, FNET_W, rows_of(fnet_ref, blk))), INPROJ_SLAB)

    _weave(elementwise(), matmuls())


def _inproj(x, consts, *, seq):
    t, d = x.shape
    specs, args = _resident(consts)
    nblk = INPROJ_BLOCKS
    tm = nblk * SEQ_BLOCK
    assert (seq // SEQ_BLOCK) % nblk == 0
    sub = SUBLANES
    per = tm // sub
    row = lambda c: pl.BlockSpec((tm, c), lambda i: (i, 0))
    prev = pl.BlockSpec((sub, d), lambda i: (jnp.maximum(i * per - 1, 0), 0))
    nxt = pl.BlockSpec((sub, d), lambda i: (jnp.minimum((i + 1) * per, t // sub - 1), 0))
    dec = pl.BlockSpec((nblk * BLOCK_NCH * 8, RWKV_W), lambda i: (i, 0))
    op = lambda k: jax.ShapeDtypeStruct((t, k * RWKV_W), BF16)
    dec_shape = jax.ShapeDtypeStruct((t // CHUNK * 8, RWKV_W), F32)
    cm = pl.BlockSpec((nblk * BLOCK_NCH * RWKV_W, LANES), lambda i: (i, 0))
    cm_shape = jax.ShapeDtypeStruct((t // CHUNK * RWKV_W, LANES), BF16)
    return pl.pallas_call(
        functools.partial(_inproj_body, blocks_per_seq=seq // SEQ_BLOCK),
        grid=(t // tm,),
        in_specs=[row(d), prev, nxt] + specs,
        out_specs=[row(GLA_COLS), row(FNET_W), row(GATE_COLS)]
                  + [row(4 * RWKV_W), dec, cm] * 2 + [row(3 * RWKV_W)],
        out_shape=[jax.ShapeDtypeStruct((t, GLA_COLS), BF16),
                   jax.ShapeDtypeStruct((t, FNET_W), BF16),
                   jax.ShapeDtypeStruct((t, GATE_COLS), BF16)]
                  + [op(4), dec_shape, cm_shape] * 2 + [op(3)],
        compiler_params=_params(("parallel",)),
        name="inproj",
    )(x, x, x, *args)


def _woven(*gens, shares=None):
    live = list(zip(gens, shares or (1,) * len(gens)))
    while live:
        for entry in list(live):
            g, n = entry
            for _ in range(n):
                try:
                    next(g)
                except StopIteration:
                    live.remove(entry)
                    break
                yield


def _weave(*gens, shares=None):
    for _ in _woven(*gens, shares=shares):
        pass


def _lane_masks():
    lane = lax.broadcasted_iota(jnp.int32, (CHUNK, LANES), 1)
    return lane < (LANES // 2), lane >= (LANES // 2)


def _masked_stack(x, lo, hi):
    return jnp.concatenate([jnp.where(lo, x, 0.0), jnp.where(hi, x, 0.0)], axis=0)


def _pair_masks(fwd, inclusive):
    n = 2 * CHUNK
    i = lax.broadcasted_iota(jnp.int32, (n, n), 0)
    j = lax.broadcasted_iota(jnp.int32, (n, n), 1)
    same = (i < CHUNK) == (j < CHUNK)
    strict = (j < i) if fwd else (j > i)
    if inclusive:
        return same & (strict | (i == j))
    return same & strict


GLA_PAIRS = GLA_HEADS // 2


def _gla_prep_stages(z_ref, up, bias, tri, sel, d, qd_ref, kd_ref, v_ref, dec_ref):
    L = CHUNK
    z = z_ref[...].astype(F32)
    q = z[:, 0:GLA_QK] * (GLA_DK ** -0.5)
    k = z[:, GLA_QK:2 * GLA_QK]
    v = z[:, 2 * GLA_QK:2 * GLA_QK + GLA_V]
    dn = z[:, GLA_COLS - LANES:GLA_COLS]
    for j in range(BLOCK_NCH):
        v_ref[d, j] = v[j * L:(j + 1) * L]
    la = _log_sigmoid(_mm(dn, up) + bias) * (1.0 / GLA_TAU)
    yield
    la_parts = _split(la, 2)
    b = _mm_parts(tri, la_parts)
    dec_ref[d] = jnp.exp(_mm_parts(sel, la_parts))
    yield
    for ref, val in ((qd_ref, q * jnp.exp(b)), (kd_ref, k * jnp.exp(-b))):
        for j in range(BLOCK_NCH):
            ref[d, j] = val[j * L:(j + 1) * L]
        yield


def _gla_step_stages(d, qd_ref, kd_ref, v_ref, dec_ref, st_ref, o_ref, masks):
    L = CHUNK
    lo, hi, m_read = masks
    units = [(j, p) for j in range(BLOCK_NCH) for p in range(GLA_PAIRS)]
    chunk = lambda j: j if d == 0 else BLOCK_NCH - 1 - j
    q_ms, kd_ms, v_st, sc, inc, o_intra = {}, {}, {}, {}, {}, {}
    for n, u in enumerate(units):
        j, p = u
        c = chunk(j)
        sl = pl.ds(p * LANES, LANES)
        q_ms[u] = _masked_stack(qd_ref[d, c, :, sl], lo, hi)
        kd_ms[u] = _masked_stack(kd_ref[d, c, :, sl], lo, hi)
        v_st[u] = jnp.concatenate([v_ref[d, c, :, pl.ds((2 * p) * GLA_DV, GLA_DV)],
                                   v_ref[d, c, :, pl.ds((2 * p + 1) * GLA_DV, GLA_DV)]], axis=0)
        sc[u] = jnp.where(m_read[d], _mm_nt(q_ms[u], kd_ms[u]), 0.0)
        inc[u] = _mm_tn(v_st[u], kd_ms[u])
        if n % 2:
            yield
    for n, u in enumerate(units):
        o_intra[u] = _mm(sc[u], v_st[u])
        if n % 2:
            yield
    for p in range(GLA_PAIRS):
        sl = pl.ds(p * LANES, LANES)
        st = st_ref[d, p]
        for j in range(BLOCK_NCH):
            u = (j, p)
            c = chunk(j)
            o = o_intra[u] + _mm_nt(q_ms[u], st)
            o_ref[pl.ds(c * L, L), pl.ds((2 * p) * GLA_DV, GLA_DV)] = o[0:L].astype(BF16)
            o_ref[pl.ds(c * L, L), pl.ds((2 * p + 1) * GLA_DV, GLA_DV)] = o[L:2 * L].astype(BF16)
            st = (st + inc[u]) * dec_ref[d, pl.ds(c * 8, 1), sl]
            if j % 2:
                yield
        st_ref[d, p] = st


def _gla_program(zf_ref, zb_ref, upf_ref, bf_ref, upb_ref, bb_ref, trif_ref, trib_ref, sel_ref,
                 of_ref, ob_ref, qd_ref, kd_ref, v_ref, dec_ref, st_ref):
    scr = (qd_ref, kd_ref, v_ref, dec_ref)
    lo, hi = _lane_masks()
    masks = (lo, hi, (_pair_masks(True, inclusive=True), _pair_masks(False, inclusive=False)))
    yield from _gla_prep_stages(zf_ref, upf_ref[...], bf_ref[...], trif_ref[...], sel_ref[...], 0, *scr)
    yield from _woven(_gla_prep_stages(zb_ref, upb_ref[...], bb_ref[...], trib_ref[...], sel_ref[...], 1, *scr),
                      _gla_step_stages(0, *scr, st_ref, of_ref, masks))
    yield from _gla_step_stages(1, *scr, st_ref, ob_ref, masks)


def _fnet_body(z_ref, cc_ref, sc_ref, cp_ref, sp_ref, alt_ref, rev_ref, o_ref, a_ref, b_ref, m_ref, *, rows):
    s = z_ref.shape[0]
    h = s // 2
    z = z_ref[...]
    a_ref[...] = jnp.dot(z, cc_ref[...], preferred_element_type=F32).astype(BF16)
    b_ref[...] = jnp.dot(z, sc_ref[...], preferred_element_type=F32).astype(BF16)
    for i in range(h // rows):
        r = slice(i * rows, (i + 1) * rows)
        p = jnp.dot(cp_ref[r, :], a_ref[...], preferred_element_type=F32)
        q = jnp.dot(sp_ref[r, :], b_ref[...], preferred_element_type=F32)
        o_ref[r, :] = (p - q).astype(o_ref.dtype)
        m_ref[r, :] = (p + q).astype(BF16)
    for i in range(h // rows):
        r = slice(i * rows, (i + 1) * rows)
        o_ref[h + i * rows:h + (i + 1) * rows, :] = jnp.dot(
            rev_ref[r, :], m_ref[...], preferred_element_type=F32).astype(o_ref.dtype)
    mid = jnp.dot(alt_ref[...], a_ref[...], preferred_element_type=F32)
    o_ref[h:h + 1, :] = mid[0:1].astype(o_ref.dtype)


def _fnet(z, tables):
    bsz, s, w = z.shape
    rows = min(s // 2, 512)
    blk = lambda: pl.BlockSpec((None, s, w), lambda b: (b, 0, 0))
    return pl.pallas_call(
        functools.partial(_fnet_body, rows=rows),
        grid=(bsz,),
        in_specs=[blk()] + [_const_spec(a.shape) for a in tables],
        out_specs=blk(),
        out_shape=jax.ShapeDtypeStruct((bsz, s, w), BF16),
        scratch_shapes=[pltpu.VMEM((s, w), BF16), pltpu.VMEM((s, w), BF16), pltpu.VMEM((s // 2, w), BF16)],
        compiler_params=_params(("parallel",)),
        name="fnet",
    )(z, *tables)


RWKV_BLOCK = SEQ_BLOCK
RWKV_NCH = BLOCK_NCH
RWKV_PAIRS = RWKV_HEADS // 2


def _chunk(ref, j, sl):
    return ref[pl.ds(j * CHUNK, CHUNK), sl]


def _rwkv_solve_stages(units, ops, tinv_ref, pkb_ref, mkv_ref, masks):
    L = CHUNK
    n2 = 2 * L
    lo, hi, eye, m_strict, m_read = masks
    top = (lax.broadcasted_iota(jnp.int32, (2 * n2, n2), 0) & (n2 - 1)) < L
    t_inv, pw, s_kb, v_ms, m_k = {}, {}, {}, {}, {}
    for n, u in enumerate(units):
        d, j, p = u
        sl = pl.ds(p * LANES, LANES)
        kaph_ref, rh_ref, _, _, v_ref, _, kbcm_ref = ops[d]
        xs = jnp.concatenate([_masked_stack(_chunk(kaph_ref, j, sl), lo, hi),
                              _masked_stack(_chunk(rh_ref, j, sl), lo, hi)], axis=0)
        s_kb[u] = _mm(xs, kbcm_ref[pl.ds(j * RWKV_W + p * LANES, LANES), :])
        v_ms[u] = _masked_stack(_chunk(v_ref, j, sl), lo, hi)
        if n % 2:
            yield
    for n, u in enumerate(units):
        d, j, p = u
        s_sw = pltpu.roll(s_kb[u], L, 1)
        vs_k = jnp.where(top, s_kb[u], s_sw)
        vs_b = jnp.where(top, s_sw, s_kb[u])
        m_k[u] = jnp.where(m_strict[d], vs_k[0:n2], 0.0).astype(BF16)
        nmat = jnp.where(m_strict[d], -vs_b[0:n2], 0.0)
        p_k = jnp.where(m_read[d], vs_k[n2:2 * n2], 0.0)
        p_b = jnp.where(m_read[d], -vs_b[n2:2 * n2], 0.0)
        pkb_ref[d, j, p] = jnp.concatenate([p_k, p_b], axis=1).astype(BF16)
        t_inv[u] = eye + nmat
        pw[u] = nmat.astype(BF16)
        if n % 2:
            yield
    for n, u in enumerate(units):
        d, j, p = u
        mkv_ref[d, j, p] = _mm(m_k[u], v_ms[u])
        if n % 2:
            yield
    for n, u in enumerate(units):
        pw[u] = jnp.dot(pw[u], pw[u], preferred_element_type=F32).astype(BF16)
        if n % 2:
            yield
    for _ in range(4):
        for n, u in enumerate(units):
            both = jnp.dot(jnp.concatenate([t_inv[u].astype(BF16), pw[u]], axis=0), pw[u],
                           preferred_element_type=F32)
            t_inv[u] = t_inv[u] + both[0:n2]
            pw[u] = both[n2:2 * n2].astype(BF16)
            if n % 2:
                yield
    for n, u in enumerate(units):
        d, j, p = u
        tinv_ref[d, j, p] = (t_inv[u] + jnp.dot(t_inv[u].astype(BF16), pw[u],
                                                preferred_element_type=F32)).astype(BF16)
        if n % 2:
            yield


def _rwkv_scan_stages(steps, ops, tinv_ref, pkb_ref, mkv_ref, st_ref, y_refs, masks):
    L = CHUNK
    n2 = 2 * L
    lo, hi = masks[0], masks[1]
    units = [(d, p) for d in range(2) for p in range(RWKV_PAIRS)]
    for j in steps:
        cj = {0: j, 1: RWKV_NCH - 1 - j}
        v_ms, xh, uu = {}, {}, {}
        for (d, p) in units:
            sl = pl.ds(p * LANES, LANES)
            xs = jnp.concatenate([_masked_stack(_chunk(ops[d][0], cj[d], sl), lo, hi),
                                  _masked_stack(_chunk(ops[d][1], cj[d], sl), lo, hi)], axis=0)
            xh[d, p] = _mm_nt(xs, st_ref[d, p])
            yield
        for (d, p) in units:
            sl = pl.ds(p * LANES, LANES)
            v_ms[d, p] = _masked_stack(_chunk(ops[d][4], cj[d], sl), lo, hi)
            uu[d, p] = _mm(tinv_ref[d, cj[d], p], xh[d, p][0:n2] + mkv_ref[d, cj[d], p])
            yield
        for (d, p) in units:
            sl = pl.ds(p * LANES, LANES)
            y = xh[d, p][n2:2 * n2] + _mm(pkb_ref[d, cj[d], p], jnp.concatenate([v_ms[d, p], uu[d, p]], axis=0))
            y_refs[d][pl.ds(cj[d] * L, L), sl] = (y[0:L] + y[L:n2]).astype(BF16)
            kb = jnp.concatenate([_masked_stack(_chunk(ops[d][2], cj[d], sl), lo, hi),
                                  _masked_stack(_chunk(ops[d][3], cj[d], sl), lo, hi)], axis=0)
            dec = ops[d][5][pl.ds(cj[d] * 8, 1), sl]
            st_ref[d, p] = (st_ref[d, p] + _mm_tn(jnp.concatenate([v_ms[d, p], -uu[d, p]], axis=0), kb)) * dec
            yield


def _rwkv_program(kaf_ref, rhf_ref, ktf_ref, btf_ref, vf_ref, decf_ref, kbf_ref,
                  kab_ref, rhb_ref, ktb_ref, btb_ref, vb_ref, decb_ref, kbb_ref,
                  yf_ref, yb_ref, tinv_ref, pkb_ref, mkv_ref, st_ref):
    ops =((kaf_ref, rhf_ref, ktf_ref, btf_ref, vf_ref, decf_ref, kbf_ref),
           (kab_ref, rhb_ref, ktb_ref, btb_ref, vb_ref, decb_ref, kbb_ref))
    lo, hi = _lane_masks()
    n2 = 2 * CHUNK
    ii = lax.broadcasted_iota(jnp.int32, (n2, n2), 0)
    jj = lax.broadcasted_iota(jnp.int32, (n2, n2), 1)
    eye = (ii == jj).astype(F32)
    m_strict = (_pair_masks(True, inclusive=False), _pair_masks(False, inclusive=False))
    m_read = (_pair_masks(True, inclusive=True), _pair_masks(False, inclusive=False))
    masks = (lo, hi, eye, m_strict, m_read)

    def solve(steps):
        units = [(d, j if d == 0 else RWKV_NCH - 1 - j, p)
                 for j in steps for d in range(2) for p in range(RWKV_PAIRS)]
        return _rwkv_solve_stages(units, ops, tinv_ref, pkb_ref, mkv_ref, masks)

    def scan(steps):
        return _rwkv_scan_stages(steps, ops, tinv_ref, pkb_ref, mkv_ref, st_ref, (yf_ref, yb_ref), masks)

    half = RWKV_NCH // 2
    first, second = list(range(half)), list(range(half, RWKV_NCH))
    yield from solve(first)
    yield from _woven(solve(second), scan(first))
    yield from scan(second)


N_GLA_IN, N_RWKV_IN = 9, 14


def _recurrences_body(*refs):
    a, b = N_GLA_IN, N_GLA_IN + N_RWKV_IN
    gla_refs = refs[:a] + refs[b:b + 2] + refs[b + 4:b + 9]
    rwkv_refs = refs[a:b] + refs[b + 2:b + 4] + refs[b + 9:]

    @pl.when(pl.program_id(1) == 0)
    def _():
        for st_ref in (gla_refs[-1], rwkv_refs[-1]):
            st_ref[...] = jnp.zeros_like(st_ref)

    _weave(_rwkv_program(*rwkv_refs), _gla_program(*gla_refs), shares=(7, 1))


def _recurrences(z_gla, gla_consts, fwd_ops, bwd_ops, bsz, s):
    R = RWKV_BLOCK
    nb = s // R
    fc = lambda i: i
    bc = lambda i: nb - 1 - i
    blk = lambda cm, k=0: pl.BlockSpec((None, R, RWKV_W), lambda b, i: (b, cm(i), k))
    dec = lambda cm: pl.BlockSpec((None, RWKV_NCH * 8, RWKV_W), lambda b, i: (b, cm(i), 0))
    cm = lambda cm_: pl.BlockSpec((None, RWKV_NCH * RWKV_W, LANES), lambda b, i: (b, cm_(i), 0))
    seq = lambda a: a.reshape(bsz, -1, a.shape[-1])
    zspec = lambda cm_: pl.BlockSpec((None, R, GLA_COLS), lambda b, i: (b, cm_(i), 0))
    gla_consts, gla_args = _resident(gla_consts)
    assert 2 + len(gla_args) == N_GLA_IN and len(fwd_ops) + len(bwd_ops) == N_RWKV_IN
    out = lambda w: jax.ShapeDtypeStruct((bsz, s, w), BF16)
    per_chunk = lambda rows, w: pltpu.VMEM((2, BLOCK_NCH, rows, w), F32)
    gla_scratch = [per_chunk(CHUNK, GLA_QK), per_chunk(CHUNK, GLA_QK), per_chunk(CHUNK, GLA_V),
                   pltpu.VMEM((2, BLOCK_NCH * 8, GLA_QK), F32),
                   pltpu.VMEM((2, GLA_PAIRS, GLA_DV, LANES), F32)]
    unit = (2, RWKV_NCH, RWKV_PAIRS)
    rwkv_scratch = [pltpu.VMEM(unit + (LANES, LANES), BF16),
                    pltpu.VMEM(unit + (LANES, 2 * LANES), BF16),
                    pltpu.VMEM(unit + (LANES, LANES), F32),
                    pltpu.VMEM((2, RWKV_PAIRS, LANES, LANES), F32)]
    return pl.pallas_call(
        _recurrences_body,
        grid=(bsz, nb),
        in_specs=[zspec(fc), zspec(bc)] + gla_consts
                 + [blk(fc, k) for k in (0, 1, 2, 3, 0)] + [dec(fc), cm(fc)]
                 + [blk(bc, k) for k in (0, 1, 2, 3, 0)] + [dec(bc), cm(bc)],
        out_specs=[pl.BlockSpec((None, R, GLA_V), lambda b, i: (b, fc(i), 0)),
                   pl.BlockSpec((None, R, GLA_V), lambda b, i: (b, bc(i), 0)), blk(fc), blk(bc)],
        out_shape=[out(GLA_V)] * 2 + [out(RWKV_W)] * 2,
        scratch_shapes=gla_scratch + rwkv_scratch,
        compiler_params=_params(("parallel", "arbitrary")),
        name="recurrences",
    )(z_gla, z_gla, *gla_args, *[seq(a) for a in fwd_ops], *[seq(a) for a in bwd_ops])


def _merge_body(x_ref, gof_ref, gob_ref, gr_ref, fn_ref, ryf_ref, ryb_ref, rbon_ref, rg_ref, gate_ref,
                gn_ref, lng_ref, lnb_ref, gavg_ref, pg_ref, pf_ref, pr_ref, wo_ref, o_ref):
    o = gof_ref[...].astype(F32) + gob_ref[...].astype(F32)
    parts = []
    for h in range(GLA_HEADS):
        oh = o[:, h * GLA_DV:(h + 1) * GLA_DV]
        parts.append(oh * lax.rsqrt(jnp.mean(oh * oh, axis=-1, keepdims=True) + NORM_EPS))
    rg = gr_ref[...].astype(F32)
    y_a = jnp.concatenate(parts, axis=1) * gn_ref[...] * (rg * _sigmoid(rg))
    y = ryf_ref[...].astype(F32) + ryb_ref[...].astype(F32)
    mean = _group_sums(y, gavg_ref[...], 2)
    yc = y - mean
    var = _group_sums(yc * yc, gavg_ref[...], 1)
    y_c = (yc * lax.rsqrt(var + RWKV_LN_EPS) * lng_ref[...] + lnb_ref[...] + rbon_ref[...].astype(F32)) * rg_ref[...].astype(F32)
    d = x_ref.shape[1]
    gate = gate_ref[...].astype(F32)
    merged = (_sigmoid(gate[:, 0:d]) * _mm(y_a, pg_ref[...])
              + _sigmoid(gate[:, d:2 * d]) * _mm(fn_ref[...], pf_ref[...])
              + _sigmoid(gate[:, 2 * d:3 * d]) * _mm(y_c, pr_ref[...]))
    o_ref[...] = x_ref[...] + _mm(merged, wo_ref[...])


def _merge(x, gla_of, gla_ob, z_gla, y_fnet, r_yf, r_yb, r_bonus, r_g, z_gate,
           gn, lng, lnb, gavg, pg, pf, pr, wo, *, tm):
    t, d = x.shape
    row = lambda c: pl.BlockSpec((tm, c), lambda i: (i, 0))
    r_col = (2 * GLA_QK + GLA_V) // GLA_V
    gr_spec = pl.BlockSpec((tm, GLA_V), lambda i: (i, r_col))
    tok_col = lambda k: pl.BlockSpec((tm, RWKV_W), lambda i: (i, k))
    specs, args = _resident((gn, lng, lnb, gavg, pg, pf, pr, wo))
    return pl.pallas_call(
        _merge_body,
        grid=(t // tm,),
        in_specs=[row(d), row(GLA_V), row(GLA_V), gr_spec, row(FNET_W), row(RWKV_W), row(RWKV_W),
                  tok_col(1), tok_col(2), row(GATE_COLS)] + specs,
        out_specs=row(d),
        out_shape=jax.ShapeDtypeStruct((t, d), F32),
        compiler_params=_params(("parallel",)),
        name="merge",
    )(x, gla_of, gla_ob, z_gla, y_fnet, r_yf, r_yb, r_bonus, r_g, z_gate, *args)


def _dft_tables(s):
    def cos_sin(n, rows, cols):
        ang = (2.0 * jnp.pi / n) * ((jnp.arange(rows, dtype=jnp.int32)[:, None] * cols[None, :]) % n).astype(F32)
        return jnp.cos(ang), jnp.sin(ang)

    def dft(n, rows):
        step = 64 if n % 64 == 0 and n > 64 else n
        c_lo, s_lo = cos_sin(n, rows, jnp.arange(step, dtype=jnp.int32))
        c_hi, s_hi = cos_sin(n, rows, jnp.arange(n // step, dtype=jnp.int32) * step)
        cos = c_hi[:, :, None] * c_lo[:, None, :] - s_hi[:, :, None] * s_lo[:, None, :]
        sin = s_hi[:, :, None] * c_lo[:, None, :] + c_hi[:, :, None] * s_lo[:, None, :]
        return cos.reshape(rows, n) * (n ** -0.5), sin.reshape(rows, n) * (n ** -0.5)

    cg, sg = dft(FNET_GC, FNET_GC)
    eye = jnp.eye(FNET_GROUPS, dtype=F32)
    cc = jnp.kron(eye, cg).astype(BF16)
    sc = jnp.kron(eye, sg).astype(BF16)
    h = s // 2
    cp, sp = dft(s, h)
    m = jnp.arange(s)
    alt = jnp.broadcast_to(jnp.where(m % 2 == 0, 1.0, -1.0) * (s ** -0.5), (8, s))
    r = jnp.arange(h)
    rev = (r[None, :] == h - r[:, None]).astype(BF16)
    return cc, sc, cp.astype(BF16), sp.astype(BF16), alt.astype(BF16), rev


def _tri(fwd, n):
    i = jnp.arange(n)
    same = (i[None, :] // CHUNK) == (i[:, None] // CHUNK)
    m = (i[None, :] <= i[:, None]) if fwd else (i[None, :] >= i[:, None])
    return (m & same).astype(BF16)


def _chunk_sel(n):
    return (jnp.arange(n // CHUNK * 8)[:, None] // 8 == jnp.arange(n)[None, :] // CHUNK).astype(BF16)


def _pad_rows(w, offset, total):
    return jnp.zeros((w.shape[0], total, w.shape[2]), F32).at[:, offset:offset + w.shape[1]].set(w.astype(F32))


def kernel(x, ffn1_norm, ffn1_gate, ffn1_up, ffn1_down, mix_norm, w_in, gla_up_f, gla_bias_f, gla_up_b, gla_bias_b, gla_norm, rwkv_mu, rwkv_w0_f, rwkv_w2_f, rwkv_w0_b, rwkv_w2_b, rwkv_a0_f, rwkv_a2_f, rwkv_a0_b, rwkv_a2_b, rwkv_g2, rwkv_k_k, rwkv_k_a, rwkv_r_k, rwkv_ln_g, rwkv_ln_b, proj_gla, proj_fnet, proj_rwkv, w_out, ffn2_norm, ffn2_gate, ffn2_up, ffn2_down, final_norm):
    bsz, s, d = x.shape
    depth = w_in.shape[0]
    t = bsz * s
    tm = 512 if t % 512 == 0 else 256
    assert s % SEQ_BLOCK == 0 and t % tm == 0 and d * 3 == GATE_COLS

    fnet_tables = _dft_tables(s)
    trif_r, trib_r, sel_r = _tri(True, SEQ_BLOCK), _tri(False, SEQ_BLOCK), _chunk_sel(SEQ_BLOCK)
    head_of = jnp.arange(LANES) // RWKV_N
    same_head = head_of[:, None] == head_of[None, :]
    gsum = same_head.astype(BF16)
    gavg = (same_head.astype(F32) * (1.0 / RWKV_N)).astype(BF16)
    rows = lambda a: a.astype(F32).reshape(depth, 1, -1)
    bf16 = lambda a: a.astype(BF16)
    fnorm = final_norm.astype(F32).reshape(1, -1)
    ffn1 = (rows(ffn1_norm), bf16(ffn1_gate), bf16(ffn1_up), bf16(ffn1_down))
    ffn2 = (rows(ffn2_norm), bf16(ffn2_gate), bf16(ffn2_up), bf16(ffn2_down))
    o_f, o_r, o_g = GLA_COLS_RAW, GLA_COLS_RAW + FNET_W, GLA_COLS_RAW + FNET_W + RWKV_COLS
    w_pad = bf16(jnp.concatenate([
        w_in[:, :, :o_f], jnp.zeros((depth, d, GLA_COLS - GLA_COLS_RAW), w_in.dtype),
        w_in[:, :, o_f:o_r],
        w_in[:, :, o_r:o_g], jnp.zeros((depth, d, RWKV_COLS_PAD - RWKV_COLS), w_in.dtype),
        w_in[:, :, o_g:]], axis=2))
    mu = jnp.concatenate([rwkv_mu.astype(F32), jnp.zeros((depth, RWKV_COLS_PAD - RWKV_COLS), F32)],
                         axis=1).reshape(depth, 1, -1)
    inproj_layered = (
        rows(mix_norm), w_pad, mu, rows(rwkv_w0_f), rows(rwkv_w0_b), rows(rwkv_a0_f), rows(rwkv_a0_b),
        jnp.concatenate([_pad_rows(rwkv_w2_f, 0, LANES), _pad_rows(rwkv_w2_b, 32, LANES)], axis=2),
        jnp.concatenate([_pad_rows(rwkv_a2_f, 64, LANES), _pad_rows(rwkv_a2_b, 96, LANES)], axis=2),
        _pad_rows(rwkv_g2, 0, LANES), rows(rwkv_k_k), rows(rwkv_k_a), rows(rwkv_r_k))
    gla_layered = (_pad_rows(gla_up_f, 0, LANES), rows(gla_bias_f), _pad_rows(gla_up_b, GLA_RANK, LANES),
                   rows(gla_bias_b))
    merge_layered = (rows(gla_norm), rows(rwkv_ln_g), rows(rwkv_ln_b))
    proj_layered = (bf16(proj_gla), bf16(proj_fnet), bf16(proj_rwkv), bf16(w_out))

    x2 = x.reshape(t, d)
    for l in range(depth):
        at = lambda arrays: tuple((a, l) for a in arrays)
        x2 = _ffn(x2, at(ffn1) + (fnorm,), final=False, tm=tm)

        z_gla, z_fnet, z_gate, op_f, dec_f, kb_f, op_b, dec_b, kb_b, tok = _inproj(x2, at(inproj_layered) + (trif_r, trib_r, sel_r, gsum), seq=s)

        y_fnet = _fnet(z_fnet.reshape(bsz, s, FNET_W), fnet_tables)

        gla_of, gla_ob, r_yf, r_yb = _recurrences(
            z_gla.reshape(bsz, s, GLA_COLS), at(gla_layered) + (trif_r, trib_r, sel_r),
            (op_f,) * 4 + (tok, dec_f, kb_f), (op_b,) * 4 + (tok, dec_b, kb_b), bsz, s)

        flat = lambda a: a.reshape(t, a.shape[-1])
        x2 = _merge(x2, flat(gla_of), flat(gla_ob), z_gla, flat(y_fnet), flat(r_yf), flat(r_yb),
                    tok, tok, z_gate, *at(merge_layered), gavg, *at(proj_layered), tm=tm)

        x2 = _ffn(x2, at(ffn2) + (fnorm,), final=(l == depth - 1), tm=tm)
    return x2.reshape(bsz, s, d)
```

```python
import functools

import jax
import jax.numpy as jnp
from jax import lax
from jax.experimental import pallas as pl
from jax.experimental.pallas import tpu as pltpu

F32 = jnp.float32
BF16 = jnp.bfloat16

NORM_EPS = 1e-6
RWKV_LN_EPS = 64e-5
GLA_TAU = 16.0

GLA_HEADS, GLA_DK, GLA_DV, GLA_RANK = 4, 64, 128, 16
GLA_QK, GLA_V = GLA_HEADS * GLA_DK, GLA_HEADS * GLA_DV
FNET_GROUPS, FNET_GC = 4, 128
FNET_W = FNET_GROUPS * FNET_GC
RWKV_HEADS, RWKV_N = 8, 64
RWKV_W = RWKV_HEADS * RWKV_N
RWKV_LOWRANK = 128
RWKV_GATE_RANK = 96
RWKV_COLS = 3 * RWKV_W + RWKV_LOWRANK + RWKV_GATE_RANK

LANES = 128
SUBLANES = 8
CHUNK = 64

GLA_COLS_RAW = 2 * GLA_QK + 2 * GLA_V + 2 * GLA_RANK
GLA_COLS = -(-GLA_COLS_RAW // LANES) * LANES
RWKV_COLS_PAD = -(-RWKV_COLS // LANES) * LANES
GATE_COLS = 3 * 1024

VMEM_V7X = 64 * 1024 * 1024
VMEM_LIMIT = VMEM_V7X - 8 * 1024 * 1024


def _mm(a, b):
    return jnp.dot(a.astype(BF16), b.astype(BF16), preferred_element_type=F32)


def _mm_nt(a, b):
    return lax.dot_general(a.astype(BF16), b.astype(BF16), (((1,), (1,)), ((), ())),
                           preferred_element_type=F32)


def _mm_tn(a, b):
    return lax.dot_general(a.astype(BF16), b.astype(BF16), (((0,), (0,)), ((), ())),
                           preferred_element_type=F32)


def _split(x, n):
    pieces = []
    for _ in range(n - 1):
        p = x.astype(BF16)
        pieces.append(p)
        x = x - p.astype(F32)
    pieces.append(x.astype(BF16))
    return pieces


def _group_sums(a, w, n):
    parts = _split(a, n)
    tiles = []
    for c in range(0, a.shape[1], LANES):
        out = None
        for p in parts:
            d = jnp.dot(p[:, c:c + LANES], w, preferred_element_type=F32)
            out = d if out is None else out + d
        tiles.append(out)
    return jnp.concatenate(tiles, axis=1)


def _mm_parts(w, parts):
    out = None
    for p in parts:
        d = jnp.dot(w, p, preferred_element_type=F32)
        out = d if out is None else out + d
    return out


def _sigmoid(x):
    return 0.5 * jnp.tanh(0.5 * x) + 0.5


def _log_sigmoid(x):
    return jnp.minimum(x, 0.0) - jnp.log(1.0 + jnp.exp(-jnp.abs(x)))


def _rms(x, g):
    return x * lax.rsqrt(jnp.mean(x * x, axis=-1, keepdims=True) + NORM_EPS) * g


def _const_spec(shape):
    nd = len(shape)
    return pl.BlockSpec(shape, lambda *_: (0,) * nd, pipeline_mode=pl.Buffered(1))


def _resident(operands):
    specs, args = [], []
    for a in operands:
        if isinstance(a, tuple):
            arr, layer = a
            nd = arr.ndim - 1
            specs.append(pl.BlockSpec((None,) + arr.shape[1:], lambda *_, layer=layer, nd=nd: (layer,) + (0,) * nd,
                                      pipeline_mode=pl.Buffered(1)))
            args.append(arr)
        else:
            specs.append(_const_spec(a.shape))
            args.append(a)
    return specs, args


def _params(sem):
    return pltpu.CompilerParams(dimension_semantics=sem, vmem_limit_bytes=VMEM_LIMIT)


def _ffn_stages(rows, x_ref, g_ref, wg_ref, wu_ref, wd_ref, fg_ref, o_ref, final):
    x = x_ref[rows, :]
    h = _rms(x, g_ref[...]).astype(BF16)
    yield
    gate = jnp.dot(h, wg_ref[...], preferred_element_type=F32)
    up = jnp.dot(h, wu_ref[...], preferred_element_type=F32)
    yield
    act = (gate * _sigmoid(gate) * up).astype(BF16)
    y = x + 0.5 * jnp.dot(act, wd_ref[...], preferred_element_type=F32)
    if final:
        y = _rms(y, fg_ref[...])
    o_ref[rows, :] = y
    yield


def _ffn_body(x_ref, g_ref, wg_ref, wu_ref, wd_ref, fg_ref, o_ref, *, final):
    half = x_ref.shape[0] // 2
    args = (x_ref, g_ref, wg_ref, wu_ref, wd_ref, fg_ref, o_ref, final)
    _weave(_ffn_stages(slice(0, half), *args), _ffn_stages(slice(half, 2 * half), *args))


def _ffn(x, consts, *, final, tm):
    t, d = x.shape
    row = pl.BlockSpec((tm, d), lambda i: (i, 0))
    specs, args = _resident(consts)
    return pl.pallas_call(
        functools.partial(_ffn_body, final=final),
        grid=(t // tm,),
        in_specs=[row] + specs,
        out_specs=row,
        out_shape=jax.ShapeDtypeStruct((t, d), F32),
        compiler_params=_params(("parallel",)),
        name="ffn",
    )(x, *args)


SEQ_BLOCK = 256
BLOCK_NCH = SEQ_BLOCK // CHUNK
REC_BLOCK = 512
REC_NCH = REC_BLOCK // CHUNK
E_NEG_HALF = 2.718281828459045 ** -0.5


def _rwkv_feature_stages(u, prev_row, next_row, prm, dir_refs, v_ref, bonus_ref, rg_ref):
    R, W = SEQ_BLOCK, RWKV_W
    (mu, w0f, w0b, a0f, a0b, w2, a2, g2, k_k, k_a, r_k, trif, trib, sel, gsum) = prm
    row = lax.broadcasted_iota(jnp.int32, (8, 1), 0)
    prev = pltpu.roll(u, 1, 0)
    nxt = pltpu.roll(u, R - 1, 0)
    prev = jnp.concatenate([jnp.where(row == 0, prev_row, prev[0:8]), prev[8:]], axis=0)
    nxt = jnp.concatenate([nxt[:R - 8], jnp.where(row == 7, next_row, nxt[R - 8:])], axis=0)
    u = u * (1.0 - mu) + (prev + nxt) * (0.5 * mu)
    yield
    r, k, v = u[:, 0:W], u[:, W:2 * W], u[:, 2 * W:3 * W]
    sm = u[:, 3 * W:3 * W + LANES]
    gd = u[:, 3 * W + LANES:3 * W + 2 * LANES]
    v_ref[...] = v.astype(BF16)
    zw = _mm(jnp.tanh(sm), w2)
    za = _mm(sm, a2)
    rg_ref[...] = _mm(_sigmoid(gd), g2).astype(rg_ref.dtype)
    kk = k * k_k
    kap = kk * lax.rsqrt(_group_sums(kk * kk, gsum, 1) + 1e-12)
    yield
    dirs = ((True, w0f, a0f, trif, slice(0, W)), (False, w0b, a0b, trib, slice(W, 2 * W)))
    lw, kd, beta, inc, tot = [], [], [], [], []
    for fwd, w0, a0, tri, sl in dirs:
        lw.append((-E_NEG_HALF) * _sigmoid(w0 + zw[:, sl]))
        a = _sigmoid(a0 + za[:, sl])
        kd.append(k * (1.0 + (a - 1.0) * k_a))
        beta.append(kap * a)
        yield
    bonus_ref[...] = (_group_sums(r * kd[0] * r_k, gsum, 1) * v).astype(bonus_ref.dtype)
    for d, (fwd, w0, a0, tri, sl) in enumerate(dirs):
        lw_parts = _split(lw[d], 2)
        inc.append(_mm_parts(tri, lw_parts))
        tot.append(_mm_parts(sel, lw_parts))
        yield
    for d, (fwd, w0, a0, tri, sl) in enumerate(dirs):
        ka_ref, rh_ref, kt_ref, bt_ref, dec_ref, kb_ref = dir_refs[d]
        e_exc = jnp.exp(inc[d] - lw[d])
        e_ninc = jnp.exp(-inc[d])
        ka_ref[...] = (kap * e_exc).astype(BF16)
        rh_ref[...] = (r * (jnp.exp(inc[d]) if fwd else e_exc)).astype(BF16)
        dec_ref[...] = jnp.exp(tot[d])
        yield
        k_t, b_t = kd[d] * e_ninc, beta[d] * e_ninc
        kt_ref[...] = k_t.astype(BF16)
        bt_ref[...] = b_t.astype(BF16)
        yield
        for j in range(BLOCK_NCH):
            rows = slice(j * CHUNK, (j + 1) * CHUNK)
            kb_ref[j * W:(j + 1) * W, :] = jnp.concatenate([k_t[rows], b_t[rows]], axis=0).T.astype(BF16)
        yield


def _column_dots(h, w_ref, pieces, width):
    for off, cols, ref in pieces:
        for a in range(0, cols, width):
            b = min(a + width, cols)
            ref[:, a:b] = jnp.dot(h, w_ref[:, off + a:off + b], preferred_element_type=F32).astype(ref.dtype)
            yield


INPROJ_BLOCKS = 2
INPROJ_SLAB = 512


def _inproj_body(x_ref, xp_ref, xn_ref, g_ref, w_ref,
                 mu_ref, w0f_ref, w0b_ref, a0f_ref, a0b_ref, w2_ref, a2_ref, g2_ref, kk_ref, ka_ref, rk_ref,
                 trif_ref, trib_ref, sel_ref, gsum_ref,
                 gla_ref, fnet_ref, gate_ref,
                 opf_ref, decf_ref, kbf_ref, opb_ref, decb_ref, kbb_ref, tok_ref, *, blocks_per_seq):
    R, W = SEQ_BLOCK, RWKV_W
    c0, c1, c2 = GLA_COLS, GLA_COLS + FNET_W, GLA_COLS + FNET_W + RWKV_COLS_PAD
    g = g_ref[...]
    norm = lambda rows: _rms(rows, g).astype(BF16)
    rwkv_cols = lambda hh: jnp.dot(hh, w_ref[:, c1:c2], preferred_element_type=F32)
    z_halo = rwkv_cols(norm(jnp.concatenate([xp_ref[...], xn_ref[...], x_ref[R - 8:R + 8, :]], axis=0)))
    pos = (pl.program_id(0) * INPROJ_BLOCKS) % blocks_per_seq
    edge_rows = ((jnp.where(pos > 0, z_halo[7:8, :], 0.0), z_halo[24:25, :]),
                 (z_halo[23:24, :], jnp.where(pos + 1 < blocks_per_seq - 1, z_halo[8:9, :], 0.0)))
    prm = tuple(ref[...] for ref in (mu_ref, w0f_ref, w0b_ref, a0f_ref, a0b_ref, w2_ref, a2_ref, g2_ref,
                                     kk_ref, ka_ref, rk_ref, trif_ref, trib_ref, sel_ref, gsum_ref))

    def rows_of(ref, blk):
        n = ref.shape[0] // INPROJ_BLOCKS
        return ref.at[pl.ds(blk * n, n)]

    def cols_of(ref, k):
        return ref.at[:, pl.ds(k * W, W)]

    h, u = {}, {}
    h[0] = norm(x_ref[0:R, :])
    u[0] = rwkv_cols(h[0])

    def elementwise():
        for blk in range(INPROJ_BLOCKS):
            dir_refs = tuple(tuple(cols_of(rows_of(op_ref, blk), k) for k in range(4))
                             + (rows_of(dec_ref, blk), rows_of(kb_ref, blk))
                             for op_ref, dec_ref, kb_ref in ((opf_ref, decf_ref, kbf_ref), (opb_ref, decb_ref, kbb_ref)))
            tok = rows_of(tok_ref, blk)
            yield from _rwkv_feature_stages(u[blk], *edge_rows[blk], prm, dir_refs,
                                            cols_of(tok, 0), cols_of(tok, 1), cols_of(tok, 2))

    def matmuls():
        h[1] = norm(x_ref[R:2 * R, :])
        u[1] = rwkv_cols(h[1])
        yield
        for blk in range(INPROJ_BLOCKS):
            yield from _column_dots(h[blk], w_ref, ((c2, GATE_COLS, rows_of(gate_ref, blk)),
                                                    (0, c0, rows_of(gla_ref, blk)),
                                                    (c0, FNET_W, rows_of(fnet_ref, blk))), INPROJ_SLAB)

    _weave(elementwise(), matmuls())


def _inproj(x, consts, *, seq):
    t, d = x.shape
    specs, args = _resident(consts)
    nblk = INPROJ_BLOCKS
    tm = nblk * SEQ_BLOCK
    assert (seq // SEQ_BLOCK) % nblk == 0
    sub = SUBLANES
    per = tm // sub
    row = lambda c: pl.BlockSpec((tm, c), lambda i: (i, 0))
    prev = pl.BlockSpec((sub, d), lambda i: (jnp.maximum(i * per - 1, 0), 0))
    nxt = pl.BlockSpec((sub, d), lambda i: (jnp.minimum((i + 1) * per, t // sub - 1), 0))
    dec = pl.BlockSpec((nblk * BLOCK_NCH * 8, RWKV_W), lambda i: (i, 0))
    op = lambda k: jax.ShapeDtypeStruct((t, k * RWKV_W), BF16)
    dec_shape = jax.ShapeDtypeStruct((t // CHUNK * 8, RWKV_W), F32)
    cm = pl.BlockSpec((nblk * BLOCK_NCH * RWKV_W, LANES), lambda i: (i, 0))
    cm_shape = jax.ShapeDtypeStruct((t // CHUNK * RWKV_W, LANES), BF16)
    return pl.pallas_call(
        functools.partial(_inproj_body, blocks_per_seq=seq // SEQ_BLOCK),
        grid=(t // tm,),
        in_specs=[row(d), prev, nxt] + specs,
        out_specs=[row(GLA_COLS), row(FNET_W), row(GATE_COLS)]
                  + [row(4 * RWKV_W), dec, cm] * 2 + [row(3 * RWKV_W)],
        out_shape=[jax.ShapeDtypeStruct((t, GLA_COLS), BF16),
                   jax.ShapeDtypeStruct((t, FNET_W), BF16),
                   jax.ShapeDtypeStruct((t, GATE_COLS), BF16)]
                  + [op(4), dec_shape, cm_shape] * 2 + [op(3)],
        compiler_params=_params(("parallel",)),
        name="inproj",
    )(x, x, x, *args)


def _woven(*gens, shares=None):
    live = list(zip(gens, shares or (1,) * len(gens)))
    while live:
        for entry in list(live):
            g, n = entry
            for _ in range(n):
                try:
                    next(g)
                except StopIteration:
                    live.remove(entry)
                    break
                yield


def _weave(*gens, shares=None):
    for _ in _woven(*gens, shares=shares):
        pass


def _lane_masks():
    lane = lax.broadcasted_iota(jnp.int32, (CHUNK, LANES), 1)
    return lane < (LANES // 2), lane >= (LANES // 2)


def _masked_stack(x, lo, hi):
    return jnp.concatenate([jnp.where(lo, x, 0.0), jnp.where(hi, x, 0.0)], axis=0)


def _pair_masks(fwd, inclusive):
    n = 2 * CHUNK
    i = lax.broadcasted_iota(jnp.int32, (n, n), 0)
    j = lax.broadcasted_iota(jnp.int32, (n, n), 1)
    same = (i < CHUNK) == (j < CHUNK)
    strict = (j < i) if fwd else (j > i)
    if inclusive:
        return same & (strict | (i == j))
    return same & strict


GLA_PAIRS = GLA_HEADS // 2


def _gla_prep_stages(z_ref, up, bias, tri, sel, d, qd_ref, kd_ref, v_ref, dec_ref):
    L = CHUNK
    z = z_ref[...].astype(F32)
    q = z[:, 0:GLA_QK] * (GLA_DK ** -0.5)
    k = z[:, GLA_QK:2 * GLA_QK]
    v = z[:, 2 * GLA_QK:2 * GLA_QK + GLA_V]
    dn = z[:, GLA_COLS - LANES:GLA_COLS]
    for j in range(REC_NCH):
        v_ref[d, j] = v[j * L:(j + 1) * L]
    la = _log_sigmoid(_mm(dn, up) + bias) * (1.0 / GLA_TAU)
    yield
    la_parts = _split(la, 2)
    b = _mm_parts(tri, la_parts)
    dec_ref[d] = jnp.exp(_mm_parts(sel, la_parts))
    yield
    for ref, val in ((qd_ref, q * jnp.exp(b)), (kd_ref, k * jnp.exp(-b))):
        for j in range(REC_NCH):
            ref[d, j] = val[j * L:(j + 1) * L]
        yield


def _gla_step_stages(d, qd_ref, kd_ref, v_ref, dec_ref, st_ref, o_ref, masks):
    L = CHUNK
    lo, hi, m_read = masks
    units = [(j, p) for j in range(REC_NCH) for p in range(GLA_PAIRS)]
    chunk = lambda j: j if d == 0 else REC_NCH - 1 - j
    q_ms, kd_ms, v_st, sc, inc, o_intra = {}, {}, {}, {}, {}, {}
    for n, u in enumerate(units):
        j, p = u
        c = chunk(j)
        sl = pl.ds(p * LANES, LANES)
        q_ms[u] = _masked_stack(qd_ref[d, c, :, sl], lo, hi)
        kd_ms[u] = _masked_stack(kd_ref[d, c, :, sl], lo, hi)
        v_st[u] = jnp.concatenate([v_ref[d, c, :, pl.ds((2 * p) * GLA_DV, GLA_DV)],
                                   v_ref[d, c, :, pl.ds((2 * p + 1) * GLA_DV, GLA_DV)]], axis=0)
        sc[u] = jnp.where(m_read[d], _mm_nt(q_ms[u], kd_ms[u]), 0.0)
        inc[u] = _mm_tn(v_st[u], kd_ms[u])
        if n % 2:
            yield
    for n, u in enumerate(units):
        o_intra[u] = _mm(sc[u], v_st[u])
        if n % 2:
            yield
    for p in range(GLA_PAIRS):
        sl = pl.ds(p * LANES, LANES)
        st = st_ref[d, p]
        for j in range(REC_NCH):
            u = (j, p)
            c = chunk(j)
            o = o_intra[u] + _mm_nt(q_ms[u], st)
            o_ref[pl.ds(c * L, L), pl.ds((2 * p) * GLA_DV, GLA_DV)] = o[0:L].astype(BF16)
            o_ref[pl.ds(c * L, L), pl.ds((2 * p + 1) * GLA_DV, GLA_DV)] = o[L:2 * L].astype(BF16)
            st = (st + inc[u]) * dec_ref[d, pl.ds(c * 8, 1), sl]
            if j % 2:
                yield
        st_ref[d, p] = st


def _gla_program(zf_ref, zb_ref, upf_ref, bf_ref, upb_ref, bb_ref, trif_ref, trib_ref, sel_ref,
                 of_ref, ob_ref, qd_ref, kd_ref, v_ref, dec_ref, st_ref):
    scr = (qd_ref, kd_ref, v_ref, dec_ref)
    lo, hi = _lane_masks()
    masks = (lo, hi, (_pair_masks(True, inclusive=True), _pair_masks(False, inclusive=False)))
    yield from _gla_prep_stages(zf_ref, upf_ref[...], bf_ref[...], trif_ref[...], sel_ref[...], 0, *scr)
    yield from _woven(_gla_prep_stages(zb_ref, upb_ref[...], bb_ref[...], trib_ref[...], sel_ref[...], 1, *scr),
                      _gla_step_stages(0, *scr, st_ref, of_ref, masks))
    yield from _gla_step_stages(1, *scr, st_ref, ob_ref, masks)


def _fnet_body(z_ref, cc_ref, sc_ref, cp_ref, sp_ref, alt_ref, rev_ref, o_ref, a_ref, b_ref, m_ref, *, rows):
    s = z_ref.shape[0]
    h = s // 2
    z = z_ref[...]
    a_ref[...] = jnp.dot(z, cc_ref[...], preferred_element_type=F32).astype(BF16)
    b_ref[...] = jnp.dot(z, sc_ref[...], preferred_element_type=F32).astype(BF16)
    for i in range(h // rows):
        r = slice(i * rows, (i + 1) * rows)
        p = jnp.dot(cp_ref[r, :], a_ref[...], preferred_element_type=F32)
        q = jnp.dot(sp_ref[r, :], b_ref[...], preferred_element_type=F32)
        o_ref[r, :] = (p - q).astype(o_ref.dtype)
        m_ref[r, :] = (p + q).astype(BF16)
    for i in range(h // rows):
        r = slice(i * rows, (i + 1) * rows)
        o_ref[h + i * rows:h + (i + 1) * rows, :] = jnp.dot(
            rev_ref[r, :], m_ref[...], preferred_element_type=F32).astype(o_ref.dtype)
    mid = jnp.dot(alt_ref[...], a_ref[...], preferred_element_type=F32)
    o_ref[h:h + 1, :] = mid[0:1].astype(o_ref.dtype)


def _fnet(z, tables):
    bsz, s, w = z.shape
    rows = min(s // 2, 512)
    blk = lambda: pl.BlockSpec((None, s, w), lambda b: (b, 0, 0))
    return pl.pallas_call(
        functools.partial(_fnet_body, rows=rows),
        grid=(bsz,),
        in_specs=[blk()] + [_const_spec(a.shape) for a in tables],
        out_specs=blk(),
        out_shape=jax.ShapeDtypeStruct((bsz, s, w), BF16),
        scratch_shapes=[pltpu.VMEM((s, w), BF16), pltpu.VMEM((s, w), BF16), pltpu.VMEM((s // 2, w), BF16)],
        compiler_params=_params(("parallel",)),
        name="fnet",
    )(z, *tables)


RWKV_BLOCK = REC_BLOCK
RWKV_NCH = REC_NCH
RWKV_PAIRS = RWKV_HEADS // 2
SOLVE_GROUP = 2


def _chunk(ref, j, sl):
    return ref[pl.ds(j * CHUNK, CHUNK), sl]


def _rwkv_solve_stages(units, ops, tinv_ref, pkb_ref, mkv_ref, masks):
    L = CHUNK
    n2 = 2 * L
    lo, hi, eye, m_strict, m_read = masks
    top = (lax.broadcasted_iota(jnp.int32, (2 * n2, n2), 0) & (n2 - 1)) < L
    t_inv, pw, s_kb, v_ms, m_k = {}, {}, {}, {}, {}
    for n, u in enumerate(units):
        d, j, p = u
        sl = pl.ds(p * LANES, LANES)
        kaph_ref, rh_ref, _, _, v_ref, _, kbcm_ref = ops[d]
        xs = jnp.concatenate([_masked_stack(_chunk(kaph_ref, j, sl), lo, hi),
                              _masked_stack(_chunk(rh_ref, j, sl), lo, hi)], axis=0)
        s_kb[u] = _mm(xs, kbcm_ref[pl.ds(j * RWKV_W + p * LANES, LANES), :])
        v_ms[u] = _masked_stack(_chunk(v_ref, j, sl), lo, hi)
        if n % 2:
            yield
    for n, u in enumerate(units):
        d, j, p = u
        s_sw = pltpu.roll(s_kb[u], L, 1)
        vs_k = jnp.where(top, s_kb[u], s_sw)
        vs_b = jnp.where(top, s_sw, s_kb[u])
        m_k[u] = jnp.where(m_strict[d], vs_k[0:n2], 0.0).astype(BF16)
        nmat = jnp.where(m_strict[d], -vs_b[0:n2], 0.0)
        p_k = jnp.where(m_read[d], vs_k[n2:2 * n2], 0.0)
        p_b = jnp.where(m_read[d], -vs_b[n2:2 * n2], 0.0)
        pkb_ref[d, j, p] = jnp.concatenate([p_k, p_b], axis=1).astype(BF16)
        t_inv[u] = eye + nmat
        pw[u] = nmat.astype(BF16)
        if n % 2:
            yield
    for n, u in enumerate(units):
        d, j, p = u
        mkv_ref[d, j, p] = _mm(m_k[u], v_ms[u])
        if n % 2:
            yield
    for n, u in enumerate(units):
        pw[u] = jnp.dot(pw[u], pw[u], preferred_element_type=F32).astype(BF16)
        if n % 2:
            yield
    for _ in range(4):
        for n, u in enumerate(units):
            both = jnp.dot(jnp.concatenate([t_inv[u].astype(BF16), pw[u]], axis=0), pw[u],
                           preferred_element_type=F32)
            t_inv[u] = t_inv[u] + both[0:n2]
            pw[u] = both[n2:2 * n2].astype(BF16)
            if n % 2:
                yield
    for n, u in enumerate(units):
        d, j, p = u
        tinv_ref[d, j, p] = (t_inv[u] + jnp.dot(t_inv[u].astype(BF16), pw[u],
                                                preferred_element_type=F32)).astype(BF16)
        if n % 2:
            yield


def _rwkv_scan_stages(steps, ops, tinv_ref, pkb_ref, mkv_ref, st_ref, y_refs, masks):
    L = CHUNK
    n2 = 2 * L
    lo, hi = masks[0], masks[1]
    units = [(d, p) for d in range(2) for p in range(RWKV_PAIRS)]
    for j in steps:
        cj = {0: j, 1: RWKV_NCH - 1 - j}
        v_ms, xh, uu = {}, {}, {}
        for (d, p) in units:
            sl = pl.ds(p * LANES, LANES)
            xs = jnp.concatenate([_masked_stack(_chunk(ops[d][0], cj[d], sl), lo, hi),
                                  _masked_stack(_chunk(ops[d][1], cj[d], sl), lo, hi)], axis=0)
            xh[d, p] = _mm_nt(xs, st_ref[d, p])
            yield
        for (d, p) in units:
            sl = pl.ds(p * LANES, LANES)
            v_ms[d, p] = _masked_stack(_chunk(ops[d][4], cj[d], sl), lo, hi)
            uu[d, p] = _mm(tinv_ref[d, cj[d], p], xh[d, p][0:n2] + mkv_ref[d, cj[d], p])
            yield
        for (d, p) in units:
            sl = pl.ds(p * LANES, LANES)
            y = xh[d, p][n2:2 * n2] + _mm(pkb_ref[d, cj[d], p], jnp.concatenate([v_ms[d, p], uu[d, p]], axis=0))
            y_refs[d][pl.ds(cj[d] * L, L), sl] = (y[0:L] + y[L:n2]).astype(BF16)
            kb = jnp.concatenate([_masked_stack(_chunk(ops[d][2], cj[d], sl), lo, hi),
                                  _masked_stack(_chunk(ops[d][3], cj[d], sl), lo, hi)], axis=0)
            dec = ops[d][5][pl.ds(cj[d] * 8, 1), sl]
            st_ref[d, p] = (st_ref[d, p] + _mm_tn(jnp.concatenate([v_ms[d, p], -uu[d, p]], axis=0), kb)) * dec
            yield


def _rwkv_program(kaf_ref, rhf_ref, ktf_ref, btf_ref, vf_ref, decf_ref, kbf_ref,
                  kab_ref, rhb_ref, ktb_ref, btb_ref, vb_ref, decb_ref, kbb_ref,
                  yf_ref, yb_ref, tinv_ref, pkb_ref, mkv_ref, st_ref):
    ops =((kaf_ref, rhf_ref, ktf_ref, btf_ref, vf_ref, decf_ref, kbf_ref),
           (kab_ref, rhb_ref, ktb_ref, btb_ref, vb_ref, decb_ref, kbb_ref))
    lo, hi = _lane_masks()
    n2 = 2 * CHUNK
    ii = lax.broadcasted_iota(jnp.int32, (n2, n2), 0)
    jj = lax.broadcasted_iota(jnp.int32, (n2, n2), 1)
    eye = (ii == jj).astype(F32)
    m_strict = (_pair_masks(True, inclusive=False), _pair_masks(False, inclusive=False))
    m_read = (_pair_masks(True, inclusive=True), _pair_masks(False, inclusive=False))
    masks = (lo, hi, eye, m_strict, m_read)

    def solve(steps):
        units = [(d, j if d == 0 else RWKV_NCH - 1 - j, p)
                 for j in steps for d in range(2) for p in range(RWKV_PAIRS)]
        return _rwkv_solve_stages(units, ops, tinv_ref, pkb_ref, mkv_ref, masks)

    def scan(steps):
        return _rwkv_scan_stages(steps, ops, tinv_ref, pkb_ref, mkv_ref, st_ref, (yf_ref, yb_ref), masks)

    groups = [list(range(j, j + SOLVE_GROUP)) for j in range(0, RWKV_NCH, SOLVE_GROUP)]
    yield from solve(groups[0])
    for done, ahead in zip(groups, groups[1:]):
        yield from _woven(solve(ahead), scan(done))
    yield from scan(groups[-1])


N_GLA_IN, N_RWKV_IN = 9, 14


def _recurrences_body(*refs):
    a, b = N_GLA_IN, N_GLA_IN + N_RWKV_IN
    gla_refs = refs[:a] + refs[b:b + 2] + refs[b + 4:b + 9]
    rwkv_refs = refs[a:b] + refs[b + 2:b + 4] + refs[b + 9:]

    @pl.when(pl.program_id(1) == 0)
    def _():
        for st_ref in (gla_refs[-1], rwkv_refs[-1]):
            st_ref[...] = jnp.zeros_like(st_ref)

    _weave(_rwkv_program(*rwkv_refs), _gla_program(*gla_refs), shares=(7, 1))


def _recurrences(z_gla, gla_consts, fwd_ops, bwd_ops, bsz, s):
    R = RWKV_BLOCK
    nb = s // R
    fc = lambda i: i
    bc = lambda i: nb - 1 - i
    blk = lambda cm, k=0: pl.BlockSpec((None, R, RWKV_W), lambda b, i: (b, cm(i), k))
    dec = lambda cm: pl.BlockSpec((None, RWKV_NCH * 8, RWKV_W), lambda b, i: (b, cm(i), 0))
    cm = lambda cm_: pl.BlockSpec((None, RWKV_NCH * RWKV_W, LANES), lambda b, i: (b, cm_(i), 0))
    seq = lambda a: a.reshape(bsz, -1, a.shape[-1])
    zspec = lambda cm_: pl.BlockSpec((None, R, GLA_COLS), lambda b, i: (b, cm_(i), 0))
    gla_consts, gla_args = _resident(gla_consts)
    assert 2 + len(gla_args) == N_GLA_IN and len(fwd_ops) + len(bwd_ops) == N_RWKV_IN
    out = lambda w: jax.ShapeDtypeStruct((bsz, s, w), BF16)
    per_chunk = lambda rows, w: pltpu.VMEM((2, REC_NCH, rows, w), F32)
    gla_scratch = [per_chunk(CHUNK, GLA_QK), per_chunk(CHUNK, GLA_QK), per_chunk(CHUNK, GLA_V),
                   pltpu.VMEM((2, REC_NCH * 8, GLA_QK), F32),
                   pltpu.VMEM((2, GLA_PAIRS, GLA_DV, LANES), F32)]
    unit = (2, RWKV_NCH, RWKV_PAIRS)
    rwkv_scratch = [pltpu.VMEM(unit + (LANES, LANES), BF16),
                    pltpu.VMEM(unit + (LANES, 2 * LANES), BF16),
                    pltpu.VMEM(unit + (LANES, LANES), F32),
                    pltpu.VMEM((2, RWKV_PAIRS, LANES, LANES), F32)]
    return pl.pallas_call(
        _recurrences_body,
        grid=(bsz, nb),
        in_specs=[zspec(fc), zspec(bc)] + gla_consts
                 + [blk(fc, k) for k in (0, 1, 2, 3, 0)] + [dec(fc), cm(fc)]
                 + [blk(bc, k) for k in (0, 1, 2, 3, 0)] + [dec(bc), cm(bc)],
        out_specs=[pl.BlockSpec((None, R, GLA_V), lambda b, i: (b, fc(i), 0)),
                   pl.BlockSpec((None, R, GLA_V), lambda b, i: (b, bc(i), 0)), blk(fc), blk(bc)],
        out_shape=[out(GLA_V)] * 2 + [out(RWKV_W)] * 2,
        scratch_shapes=gla_scratch + rwkv_scratch,
        compiler_params=_params(("parallel", "arbitrary")),
        name="recurrences",
    )(z_gla, z_gla, *gla_args, *[seq(a) for a in fwd_ops], *[seq(a) for a in bwd_ops])


def _merge_body(x_ref, gof_ref, gob_ref, gr_ref, fn_ref, ryf_ref, ryb_ref, rbon_ref, rg_ref, gate_ref,
                gn_ref, lng_ref, lnb_ref, gavg_ref, pg_ref, pf_ref, pr_ref, wo_ref, o_ref):
    o = gof_ref[...].astype(F32) + gob_ref[...].astype(F32)
    parts = []
    for h in range(GLA_HEADS):
        oh = o[:, h * GLA_DV:(h + 1) * GLA_DV]
        parts.append(oh * lax.rsqrt(jnp.mean(oh * oh, axis=-1, keepdims=True) + NORM_EPS))
    rg = gr_ref[...].astype(F32)
    y_a = jnp.concatenate(parts, axis=1) * gn_ref[...] * (rg * _sigmoid(rg))
    y = ryf_ref[...].astype(F32) + ryb_ref[...].astype(F32)
    mean = _group_sums(y, gavg_ref[...], 2)
    yc = y - mean
    var = _group_sums(yc * yc, gavg_ref[...], 1)
    y_c = (yc * lax.rsqrt(var + RWKV_LN_EPS) * lng_ref[...] + lnb_ref[...] + rbon_ref[...].astype(F32)) * rg_ref[...].astype(F32)
    d = x_ref.shape[1]
    gate = gate_ref[...].astype(F32)
    merged = (_sigmoid(gate[:, 0:d]) * _mm(y_a, pg_ref[...])
              + _sigmoid(gate[:, d:2 * d]) * _mm(fn_ref[...], pf_ref[...])
              + _sigmoid(gate[:, 2 * d:3 * d]) * _mm(y_c, pr_ref[...]))
    o_ref[...] = x_ref[...] + _mm(merged, wo_ref[...])


def _merge(x, gla_of, gla_ob, z_gla, y_fnet, r_yf, r_yb, r_bonus, r_g, z_gate,
           gn, lng, lnb, gavg, pg, pf, pr, wo, *, tm):
    t, d = x.shape
    row = lambda c: pl.BlockSpec((tm, c), lambda i: (i, 0))
    r_col = (2 * GLA_QK + GLA_V) // GLA_V
    gr_spec = pl.BlockSpec((tm, GLA_V), lambda i: (i, r_col))
    tok_col = lambda k: pl.BlockSpec((tm, RWKV_W), lambda i: (i, k))
    specs, args = _resident((gn, lng, lnb, gavg, pg, pf, pr, wo))
    return pl.pallas_call(
        _merge_body,
        grid=(t // tm,),
        in_specs=[row(d), row(GLA_V), row(GLA_V), gr_spec, row(FNET_W), row(RWKV_W), row(RWKV_W),
                  tok_col(1), tok_col(2), row(GATE_COLS)] + specs,
        out_specs=row(d),
        out_shape=jax.ShapeDtypeStruct((t, d), F32),
        compiler_params=_params(("parallel",)),
        name="merge",
    )(x, gla_of, gla_ob, z_gla, y_fnet, r_yf, r_yb, r_bonus, r_g, z_gate, *args)


def _dft_tables(s):
    def cos_sin(n, rows, cols):
        ang = (2.0 * jnp.pi / n) * ((jnp.arange(rows, dtype=jnp.int32)[:, None] * cols[None, :]) % n).astype(F32)
        return jnp.cos(ang), jnp.sin(ang)

    def dft(n, rows):
        step = 64 if n % 64 == 0 and n > 64 else n
        c_lo, s_lo = cos_sin(n, rows, jnp.arange(step, dtype=jnp.int32))
        c_hi, s_hi = cos_sin(n, rows, jnp.arange(n // step, dtype=jnp.int32) * step)
        cos = c_hi[:, :, None] * c_lo[:, None, :] - s_hi[:, :, None] * s_lo[:, None, :]
        sin = s_hi[:, :, None] * c_lo[:, None, :] + c_hi[:, :, None] * s_lo[:, None, :]
        return cos.reshape(rows, n) * (n ** -0.5), sin.reshape(rows, n) * (n ** -0.5)

    cg, sg = dft(FNET_GC, FNET_GC)
    eye = jnp.eye(FNET_GROUPS, dtype=F32)
    cc = jnp.kron(eye, cg).astype(BF16)
    sc = jnp.kron(eye, sg).astype(BF16)
    h = s // 2
    cp, sp = dft(s, h)
    m = jnp.arange(s)
    alt = jnp.broadcast_to(jnp.where(m % 2 == 0, 1.0, -1.0) * (s ** -0.5), (8, s))
    r = jnp.arange(h)
    rev = (r[None, :] == h - r[:, None]).astype(BF16)
    return cc, sc, cp.astype(BF16), sp.astype(BF16), alt.astype(BF16), rev


def _tri(fwd, n):
    i = jnp.arange(n)
    same = (i[None, :] // CHUNK) == (i[:, None] // CHUNK)
    m = (i[None, :] <= i[:, None]) if fwd else (i[None, :] >= i[:, None])
    return (m & same).astype(BF16)


def _chunk_sel(n):
    return (jnp.arange(n // CHUNK * 8)[:, None] // 8 == jnp.arange(n)[None, :] // CHUNK).astype(BF16)


def _pad_rows(w, offset, total):
    return jnp.zeros((w.shape[0], total, w.shape[2]), F32).at[:, offset:offset + w.shape[1]].set(w.astype(F32))


def kernel(x, ffn1_norm, ffn1_gate, ffn1_up, ffn1_down, mix_norm, w_in, gla_up_f, gla_bias_f, gla_up_b, gla_bias_b, gla_norm, rwkv_mu, rwkv_w0_f, rwkv_w2_f, rwkv_w0_b, rwkv_w2_b, rwkv_a0_f, rwkv_a2_f, rwkv_a0_b, rwkv_a2_b, rwkv_g2, rwkv_k_k, rwkv_k_a, rwkv_r_k, rwkv_ln_g, rwkv_ln_b, proj_gla, proj_fnet, proj_rwkv, w_out, ffn2_norm, ffn2_gate, ffn2_up, ffn2_down, final_norm):
    bsz, s, d = x.shape
    depth = w_in.shape[0]
    t = bsz * s
    tm = 512 if t % 512 == 0 else 256
    assert s % REC_BLOCK == 0 and t % tm == 0 and d * 3 == GATE_COLS

    fnet_tables = _dft_tables(s)
    trif_r, trib_r, sel_r = _tri(True, SEQ_BLOCK), _tri(False, SEQ_BLOCK), _chunk_sel(SEQ_BLOCK)
    gla_tables = (_tri(True, REC_BLOCK), _tri(False, REC_BLOCK), _chunk_sel(REC_BLOCK))
    head_of = jnp.arange(LANES) // RWKV_N
    same_head = head_of[:, None] == head_of[None, :]
    gsum = same_head.astype(BF16)
    gavg = (same_head.astype(F32) * (1.0 / RWKV_N)).astype(BF16)
    rows = lambda a: a.astype(F32).reshape(depth, 1, -1)
    bf16 = lambda a: a.astype(BF16)
    fnorm = final_norm.astype(F32).reshape(1, -1)
    ffn1 = (rows(ffn1_norm), bf16(ffn1_gate), bf16(ffn1_up), bf16(ffn1_down))
    ffn2 = (rows(ffn2_norm), bf16(ffn2_gate), bf16(ffn2_up), bf16(ffn2_down))
    o_f, o_r, o_g = GLA_COLS_RAW, GLA_COLS_RAW + FNET_W, GLA_COLS_RAW + FNET_W + RWKV_COLS
    w_pad = bf16(jnp.concatenate([
        w_in[:, :, :o_f], jnp.zeros((depth, d, GLA_COLS - GLA_COLS_RAW), w_in.dtype),
        w_in[:, :, o_f:o_r],
        w_in[:, :, o_r:o_g], jnp.zeros((depth, d, RWKV_COLS_PAD - RWKV_COLS), w_in.dtype),
        w_in[:, :, o_g:]], axis=2))
    mu = jnp.concatenate([rwkv_mu.astype(F32), jnp.zeros((depth, RWKV_COLS_PAD - RWKV_COLS), F32)],
                         axis=1).reshape(depth, 1, -1)
    inproj_layered = (
        rows(mix_norm), w_pad, mu, rows(rwkv_w0_f), rows(rwkv_w0_b), rows(rwkv_a0_f), rows(rwkv_a0_b),
        jnp.concatenate([_pad_rows(rwkv_w2_f, 0, LANES), _pad_rows(rwkv_w2_b, 32, LANES)], axis=2),
        jnp.concatenate([_pad_rows(rwkv_a2_f, 64, LANES), _pad_rows(rwkv_a2_b, 96, LANES)], axis=2),
        _pad_rows(rwkv_g2, 0, LANES), rows(rwkv_k_k), rows(rwkv_k_a), rows(rwkv_r_k))
    gla_layered = (_pad_rows(gla_up_f, 0, LANES), rows(gla_bias_f), _pad_rows(gla_up_b, GLA_RANK, LANES),
                   rows(gla_bias_b))
    merge_layered = (rows(gla_norm), rows(rwkv_ln_g), rows(rwkv_ln_b))
    proj_layered = (bf16(proj_gla), bf16(proj_fnet), bf16(proj_rwkv), bf16(w_out))

    x2 = x.reshape(t, d)
    for l in range(depth):
        at = lambda arrays: tuple((a, l) for a in arrays)
        x2 = _ffn(x2, at(ffn1) + (fnorm,), final=False, tm=tm)

        z_gla, z_fnet, z_gate, op_f, dec_f, kb_f, op_b, dec_b, kb_b, tok = _inproj(x2, at(inproj_layered) + (trif_r, trib_r, sel_r, gsum), seq=s)

        y_fnet = _fnet(z_fnet.reshape(bsz, s, FNET_W), fnet_tables)

        gla_of, gla_ob, r_yf, r_yb = _recurrences(
            z_gla.reshape(bsz, s, GLA_COLS), at(gla_layered) + gla_tables,
            (op_f,) * 4 + (tok, dec_f, kb_f), (op_b,) * 4 + (tok, dec_b, kb_b), bsz, s)

        flat = lambda a: a.reshape(t, a.shape[-1])
        x2 = _merge(x2, flat(gla_of), flat(gla_ob), z_gla, flat(y_fnet), flat(r_yf), flat(r_yb),
                    tok, tok, z_gate, *at(merge_layered), gavg, *at(proj_layered), tm=tm)

        x2 = _ffn(x2, at(ffn2) + (fnorm,), final=(l == depth - 1), tm=tm)
    return x2.reshape(bsz, s, d)
```

```python
import functools

import jax
import jax.numpy as jnp
from jax import lax
from jax.experimental import pallas as pl
from jax.experimental.pallas import tpu as pltpu

F32 = jnp.float32
BF16 = jnp.bfloat16

NORM_EPS = 1e-6
RWKV_LN_EPS = 64e-5
GLA_TAU = 16.0

GLA_HEADS, GLA_DK, GLA_DV, GLA_RANK = 4, 64, 128, 16
GLA_QK, GLA_V = GLA_HEADS * GLA_DK, GLA_HEADS * GLA_DV
FNET_GROUPS, FNET_GC = 4, 128
FNET_W = FNET_GROUPS * FNET_GC
RWKV_HEADS, RWKV_N = 8, 64
RWKV_W = RWKV_HEADS * RWKV_N
RWKV_RANK = 32
RWKV_LOWRANK = 4 * RWKV_RANK
RWKV_GATE_RANK = 96
RWKV_COLS = 3 * RWKV_W + RWKV_LOWRANK + RWKV_GATE_RANK

LANES = 128
SUBLANES = 8
CHUNK = 64

GLA_COLS_RAW = 2 * GLA_QK + 2 * GLA_V + 2 * GLA_RANK
GLA_COLS = -(-GLA_COLS_RAW // LANES) * LANES
RWKV_COLS_PAD = -(-RWKV_COLS // LANES) * LANES
GATE_COLS = 3 * 1024

VMEM_V7X = 64 * 1024 * 1024
VMEM_LIMIT = VMEM_V7X - 8 * 1024 * 1024


def _mm(a, b):
    return jnp.dot(a.astype(BF16), b.astype(BF16), preferred_element_type=F32)


def _mm_nt(a, b):
    return lax.dot_general(a.astype(BF16), b.astype(BF16), (((1,), (1,)), ((), ())),
                           preferred_element_type=F32)


def _mm_tn(a, b):
    return lax.dot_general(a.astype(BF16), b.astype(BF16), (((0,), (0,)), ((), ())),
                           preferred_element_type=F32)


def _split(x, n):
    pieces = []
    for _ in range(n - 1):
        p = x.astype(BF16)
        pieces.append(p)
        x = x - p.astype(F32)
    pieces.append(x.astype(BF16))
    return pieces


def _group_sums(a, w, n):
    parts = _split(a, n)
    tiles = []
    for c in range(0, a.shape[1], LANES):
        out = None
        for p in parts:
            d = jnp.dot(p[:, c:c + LANES], w, preferred_element_type=F32)
            out = d if out is None else out + d
        tiles.append(out)
    return jnp.concatenate(tiles, axis=1)


def _mm_parts(w, parts):
    out = None
    for p in parts:
        d = jnp.dot(w, p, preferred_element_type=F32)
        out = d if out is None else out + d
    return out


def _sigmoid(x):
    return 0.5 * jnp.tanh(0.5 * x) + 0.5


def _log_sigmoid(x):
    return jnp.minimum(x, 0.0) - jnp.log(1.0 + jnp.exp(-jnp.abs(x)))


def _rms(x, g):
    return x * lax.rsqrt(jnp.mean(x * x, axis=-1, keepdims=True) + NORM_EPS) * g


def _const_spec(shape):
    nd = len(shape)
    return pl.BlockSpec(shape, lambda *_: (0,) * nd, pipeline_mode=pl.Buffered(1))


def _resident(operands):
    specs, args = [], []
    for a in operands:
        if isinstance(a, tuple):
            arr, layer = a
            nd = arr.ndim - 1
            specs.append(pl.BlockSpec((None,) + arr.shape[1:], lambda *_, layer=layer, nd=nd: (layer,) + (0,) * nd,
                                      pipeline_mode=pl.Buffered(1)))
            args.append(arr)
        else:
            specs.append(_const_spec(a.shape))
            args.append(a)
    return specs, args


def _params(sem):
    return pltpu.CompilerParams(dimension_semantics=sem, vmem_limit_bytes=VMEM_LIMIT)


def _ffn_stages(rows, x_ref, g_ref, wg_ref, wu_ref, wd_ref, fg_ref, o_ref, final):
    x = x_ref[rows, :]
    h = _rms(x, g_ref[...]).astype(BF16)
    yield
    gate = jnp.dot(h, wg_ref[...], preferred_element_type=F32)
    up = jnp.dot(h, wu_ref[...], preferred_element_type=F32)
    yield
    act = (gate * _sigmoid(gate) * up).astype(BF16)
    y = x + 0.5 * jnp.dot(act, wd_ref[...], preferred_element_type=F32)
    if final:
        y = _rms(y, fg_ref[...])
    o_ref[rows, :] = y
    yield


def _ffn_body(x_ref, g_ref, wg_ref, wu_ref, wd_ref, fg_ref, o_ref, *, final):
    half = x_ref.shape[0] // 2
    args = (x_ref, g_ref, wg_ref, wu_ref, wd_ref, fg_ref, o_ref, final)
    _weave(_ffn_stages(slice(0, half), *args), _ffn_stages(slice(half, 2 * half), *args))


def _ffn(x, consts, *, final, tm):
    t, d = x.shape
    row = pl.BlockSpec((tm, d), lambda i: (i, 0))
    specs, args = _resident(consts)
    return pl.pallas_call(
        functools.partial(_ffn_body, final=final),
        grid=(t // tm,),
        in_specs=[row] + specs,
        out_specs=row,
        out_shape=jax.ShapeDtypeStruct((t, d), F32),
        compiler_params=_params(("parallel",)),
        name="ffn",
    )(x, *args)


SEQ_BLOCK = 256
BLOCK_NCH = SEQ_BLOCK // CHUNK
REC_BLOCK = 512
REC_NCH = REC_BLOCK // CHUNK
E_NEG_HALF = 2.718281828459045 ** -0.5


def _rwkv_feature_stages(u, prev_row, next_row, prm, dir_refs, v_ref, bonus_ref, rg_ref):
    R, W = SEQ_BLOCK, RWKV_W
    (mu, w0f, w0b, a0f, a0b, w2, a2, g2, k_k, k_a, r_k, trif, trib, sel, gsum) = prm
    row = lax.broadcasted_iota(jnp.int32, (8, 1), 0)
    prev = pltpu.roll(u, 1, 0)
    nxt = pltpu.roll(u, R - 1, 0)
    prev = jnp.concatenate([jnp.where(row == 0, prev_row, prev[0:8]), prev[8:]], axis=0)
    nxt = jnp.concatenate([nxt[:R - 8], jnp.where(row == 7, next_row, nxt[R - 8:])], axis=0)
    u = u * (1.0 - mu) + (prev + nxt) * (0.5 * mu)
    yield
    r, k, v = u[:, 0:W], u[:, W:2 * W], u[:, 2 * W:3 * W]
    sm = u[:, 3 * W:3 * W + LANES]
    gd = u[:, 3 * W + LANES:3 * W + 2 * LANES]
    v_ref[...] = v.astype(BF16)
    zw = _mm(jnp.tanh(sm), w2)
    za = _mm(sm, a2)
    rg_ref[...] = _mm(_sigmoid(gd), g2).astype(rg_ref.dtype)
    kk = k * k_k
    kap = kk * lax.rsqrt(_group_sums(kk * kk, gsum, 1) + 1e-12)
    yield
    dirs = ((True, w0f, a0f, trif, slice(0, W)), (False, w0b, a0b, trib, slice(W, 2 * W)))
    lw, kd, beta, inc, tot = [], [], [], [], []
    for fwd, w0, a0, tri, sl in dirs:
        lw.append((-E_NEG_HALF) * _sigmoid(w0 + zw[:, sl]))
        a = _sigmoid(a0 + za[:, sl])
        kd.append(k * (1.0 + (a - 1.0) * k_a))
        beta.append(kap * a)
        yield
    bonus_ref[...] = (_group_sums(r * kd[0] * r_k, gsum, 1) * v).astype(bonus_ref.dtype)
    for d, (fwd, w0, a0, tri, sl) in enumerate(dirs):
        lw_parts = _split(lw[d], 2)
        inc.append(_mm_parts(tri, lw_parts))
        tot.append(_mm_parts(sel, lw_parts))
        yield
    for d, (fwd, w0, a0, tri, sl) in enumerate(dirs):
        ka_ref, rh_ref, kt_ref, bt_ref, dec_ref, kb_ref = dir_refs[d]
        e_exc = jnp.exp(inc[d] - lw[d])
        e_ninc = jnp.exp(-inc[d])
        ka_ref[...] = (kap * e_exc).astype(BF16)
        rh_ref[...] = (r * (jnp.exp(inc[d]) if fwd else e_exc)).astype(BF16)
        dec_ref[...] = jnp.exp(tot[d])
        yield
        k_t, b_t = kd[d] * e_ninc, beta[d] * e_ninc
        kt_ref[...] = k_t.astype(BF16)
        bt_ref[...] = b_t.astype(BF16)
        yield
        for j in range(BLOCK_NCH):
            rows = slice(j * CHUNK, (j + 1) * CHUNK)
            kb_ref[j * W:(j + 1) * W, :] = jnp.concatenate([k_t[rows], b_t[rows]], axis=0).T.astype(BF16)
        yield


def _column_dots(h, w_ref, pieces, width):
    for off, cols, ref in pieces:
        for a in range(0, cols, width):
            b = min(a + width, cols)
            ref[:, a:b] = jnp.dot(h, w_ref[:, off + a:off + b], preferred_element_type=F32).astype(ref.dtype)
            yield


INPROJ_BLOCKS = 2
INPROJ_SLAB = 512


def _inproj_body(x_ref, xp_ref, xn_ref, g_ref, w_ref,
                 mu_ref, w0f_ref, w0b_ref, a0f_ref, a0b_ref, w2_ref, a2_ref, g2_ref, kk_ref, ka_ref, rk_ref,
                 trif_ref, trib_ref, sel_ref, gsum_ref,
                 gla_ref, fnet_ref, gate_ref,
                 opf_ref, decf_ref, kbf_ref, opb_ref, decb_ref, kbb_ref, tok_ref, *, blocks_per_seq):
    R, W = SEQ_BLOCK, RWKV_W
    c0, c1, c2 = GLA_COLS, GLA_COLS + FNET_W, GLA_COLS + FNET_W + RWKV_COLS_PAD
    g = g_ref[...]
    norm = lambda rows: _rms(rows, g).astype(BF16)
    rwkv_cols = lambda hh: jnp.dot(hh, w_ref[:, c1:c2], preferred_element_type=F32)
    S8 = SUBLANES
    z_halo = rwkv_cols(norm(jnp.concatenate([xp_ref[...], xn_ref[...], x_ref[R - S8:R + S8, :]], axis=0)))
    pos = (pl.program_id(0) * INPROJ_BLOCKS) % blocks_per_seq
    last_before, first_after = z_halo[S8 - 1:S8, :], z_halo[S8:S8 + 1, :]
    last_of_a, first_of_b = z_halo[3 * S8 - 1:3 * S8, :], z_halo[3 * S8:3 * S8 + 1, :]
    edge_rows = ((jnp.where(pos > 0, last_before, 0.0), first_of_b),
                 (last_of_a, jnp.where(pos + 1 < blocks_per_seq - 1, first_after, 0.0)))
    prm = tuple(ref[...] for ref in (mu_ref, w0f_ref, w0b_ref, a0f_ref, a0b_ref, w2_ref, a2_ref, g2_ref,
                                     kk_ref, ka_ref, rk_ref, trif_ref, trib_ref, sel_ref, gsum_ref))

    def rows_of(ref, blk):
        n = ref.shape[0] // INPROJ_BLOCKS
        return ref.at[pl.ds(blk * n, n)]

    def cols_of(ref, k):
        return ref.at[:, pl.ds(k * W, W)]

    h, u = {}, {}
    h[0] = norm(x_ref[0:R, :])
    u[0] = rwkv_cols(h[0])

    def elementwise():
        for blk in range(INPROJ_BLOCKS):
            dir_refs = tuple(tuple(cols_of(rows_of(op_ref, blk), k) for k in range(4))
                             + (rows_of(dec_ref, blk), rows_of(kb_ref, blk))
                             for op_ref, dec_ref, kb_ref in ((opf_ref, decf_ref, kbf_ref), (opb_ref, decb_ref, kbb_ref)))
            tok = rows_of(tok_ref, blk)
            yield from _rwkv_feature_stages(u[blk], *edge_rows[blk], prm, dir_refs,
                                            cols_of(tok, 0), cols_of(tok, 1), cols_of(tok, 2))

    def matmuls():
        h[1] = norm(x_ref[R:2 * R, :])
        u[1] = rwkv_cols(h[1])
        yield
        for blk in range(INPROJ_BLOCKS):
            yield from _column_dots(h[blk], w_ref, ((c2, GATE_COLS, rows_of(gate_ref, blk)),
                                                    (0, c0, rows_of(gla_ref, blk)),
                                                    (c0, FNET_W, rows_of(fnet_ref, blk))), INPROJ_SLAB)

    _weave(elementwise(), matmuls())


def _inproj(x, consts, *, seq):
    t, d = x.shape
    specs, args = _resident(consts)
    nblk = INPROJ_BLOCKS
    tm = nblk * SEQ_BLOCK
    assert (seq // SEQ_BLOCK) % nblk == 0
    sub = SUBLANES
    per = tm // sub
    row = lambda c: pl.BlockSpec((tm, c), lambda i: (i, 0))
    prev = pl.BlockSpec((sub, d), lambda i: (jnp.maximum(i * per - 1, 0), 0))
    nxt = pl.BlockSpec((sub, d), lambda i: (jnp.minimum((i + 1) * per, t // sub - 1), 0))
    dec = pl.BlockSpec((nblk * BLOCK_NCH * 8, RWKV_W), lambda i: (i, 0))
    op = lambda k: jax.ShapeDtypeStruct((t, k * RWKV_W), BF16)
    dec_shape = jax.ShapeDtypeStruct((t // CHUNK * 8, RWKV_W), F32)
    cm = pl.BlockSpec((nblk * BLOCK_NCH * RWKV_W, LANES), lambda i: (i, 0))
    cm_shape = jax.ShapeDtypeStruct((t // CHUNK * RWKV_W, LANES), BF16)
    return pl.pallas_call(
        functools.partial(_inproj_body, blocks_per_seq=seq // SEQ_BLOCK),
        grid=(t // tm,),
        in_specs=[row(d), prev, nxt] + specs,
        out_specs=[row(GLA_COLS), row(FNET_W), row(GATE_COLS)]
                  + [row(4 * RWKV_W), dec, cm] * 2 + [row(3 * RWKV_W)],
        out_shape=[jax.ShapeDtypeStruct((t, GLA_COLS), BF16),
                   jax.ShapeDtypeStruct((t, FNET_W), BF16),
                   jax.ShapeDtypeStruct((t, GATE_COLS), BF16)]
                  + [op(4), dec_shape, cm_shape] * 2 + [op(3)],
        compiler_params=_params(("parallel",)),
        name="inproj",
    )(x, x, x, *args)


def _woven(*gens, shares=None):
    live = list(zip(gens, shares or (1,) * len(gens)))
    while live:
        for entry in list(live):
            g, n = entry
            for _ in range(n):
                try:
                    next(g)
                except StopIteration:
                    live.remove(entry)
                    break
                yield


def _weave(*gens, shares=None):
    for _ in _woven(*gens, shares=shares):
        pass


def _lane_masks():
    lane = lax.broadcasted_iota(jnp.int32, (CHUNK, LANES), 1)
    return lane < (LANES // 2), lane >= (LANES // 2)


def _masked_stack(x, lo, hi):
    return jnp.concatenate([jnp.where(lo, x, 0.0), jnp.where(hi, x, 0.0)], axis=0)


def _pair_masks(fwd, inclusive):
    n = 2 * CHUNK
    i = lax.broadcasted_iota(jnp.int32, (n, n), 0)
    j = lax.broadcasted_iota(jnp.int32, (n, n), 1)
    same = (i < CHUNK) == (j < CHUNK)
    strict = (j < i) if fwd else (j > i)
    if inclusive:
        return same & (strict | (i == j))
    return same & strict


GLA_PAIRS = GLA_HEADS // 2


def _gla_prep_stages(z_ref, up, bias, tri, sel, d, qd_ref, kd_ref, v_ref, dec_ref):
    L = CHUNK
    z = z_ref[...].astype(F32)
    q = z[:, 0:GLA_QK] * (GLA_DK ** -0.5)
    k = z[:, GLA_QK:2 * GLA_QK]
    v = z[:, 2 * GLA_QK:2 * GLA_QK + GLA_V]
    dn = z[:, GLA_COLS - LANES:GLA_COLS]
    for j in range(REC_NCH):
        v_ref[d, j] = v[j * L:(j + 1) * L]
    la = _log_sigmoid(_mm(dn, up) + bias) * (1.0 / GLA_TAU)
    yield
    la_parts = _split(la, 2)
    b = _mm_parts(tri, la_parts)
    dec_ref[d] = jnp.exp(_mm_parts(sel, la_parts))
    yield
    for ref, val in ((qd_ref, q * jnp.exp(b)), (kd_ref, k * jnp.exp(-b))):
        for j in range(REC_NCH):
            ref[d, j] = val[j * L:(j + 1) * L]
        yield


def _gla_step_stages(d, qd_ref, kd_ref, v_ref, dec_ref, st_ref, o_ref, masks):
    L = CHUNK
    lo, hi, m_read = masks
    units = [(j, p) for j in range(REC_NCH) for p in range(GLA_PAIRS)]
    chunk = lambda j: j if d == 0 else REC_NCH - 1 - j
    q_ms, kd_ms, v_st, sc, inc, o_intra = {}, {}, {}, {}, {}, {}
    for n, u in enumerate(units):
        j, p = u
        c = chunk(j)
        sl = pl.ds(p * LANES, LANES)
        q_ms[u] = _masked_stack(qd_ref[d, c, :, sl], lo, hi)
        kd_ms[u] = _masked_stack(kd_ref[d, c, :, sl], lo, hi)
        v_st[u] = jnp.concatenate([v_ref[d, c, :, pl.ds((2 * p) * GLA_DV, GLA_DV)],
                                   v_ref[d, c, :, pl.ds((2 * p + 1) * GLA_DV, GLA_DV)]], axis=0)
        sc[u] = jnp.where(m_read[d], _mm_nt(q_ms[u], kd_ms[u]), 0.0)
        inc[u] = _mm_tn(v_st[u], kd_ms[u])
        if n % 2:
            yield
    for n, u in enumerate(units):
        o_intra[u] = _mm(sc[u], v_st[u])
        if n % 2:
            yield
    for p in range(GLA_PAIRS):
        sl = pl.ds(p * LANES, LANES)
        st = st_ref[d, p]
        for j in range(REC_NCH):
            u = (j, p)
            c = chunk(j)
            o = o_intra[u] + _mm_nt(q_ms[u], st)
            o_ref[pl.ds(c * L, L), pl.ds((2 * p) * GLA_DV, GLA_DV)] = o[0:L].astype(BF16)
            o_ref[pl.ds(c * L, L), pl.ds((2 * p + 1) * GLA_DV, GLA_DV)] = o[L:2 * L].astype(BF16)
            st = (st + inc[u]) * dec_ref[d, pl.ds(c * 8, 1), sl]
            if j % 2:
                yield
        st_ref[d, p] = st


def _gla_program(zf_ref, zb_ref, upf_ref, bf_ref, upb_ref, bb_ref, trif_ref, trib_ref, sel_ref,
                 of_ref, ob_ref, qd_ref, kd_ref, v_ref, dec_ref, st_ref):
    scr = (qd_ref, kd_ref, v_ref, dec_ref)
    lo, hi = _lane_masks()
    masks = (lo, hi, (_pair_masks(True, inclusive=True), _pair_masks(False, inclusive=False)))
    yield from _gla_prep_stages(zf_ref, upf_ref[...], bf_ref[...], trif_ref[...], sel_ref[...], 0, *scr)
    yield from _woven(_gla_prep_stages(zb_ref, upb_ref[...], bb_ref[...], trib_ref[...], sel_ref[...], 1, *scr),
                      _gla_step_stages(0, *scr, st_ref, of_ref, masks))
    yield from _gla_step_stages(1, *scr, st_ref, ob_ref, masks)


def _fnet_body(z_ref, cc_ref, sc_ref, cp_ref, sp_ref, alt_ref, rev_ref, o_ref, a_ref, b_ref, m_ref, *, rows):
    s = z_ref.shape[0]
    h = s // 2
    z = z_ref[...]
    a_ref[...] = jnp.dot(z, cc_ref[...], preferred_element_type=F32).astype(BF16)
    b_ref[...] = jnp.dot(z, sc_ref[...], preferred_element_type=F32).astype(BF16)
    for i in range(h // rows):
        r = slice(i * rows, (i + 1) * rows)
        p = jnp.dot(cp_ref[r, :], a_ref[...], preferred_element_type=F32)
        q = jnp.dot(sp_ref[r, :], b_ref[...], preferred_element_type=F32)
        o_ref[r, :] = (p - q).astype(o_ref.dtype)
        m_ref[r, :] = (p + q).astype(BF16)
    for i in range(h // rows):
        r = slice(i * rows, (i + 1) * rows)
        o_ref[h + i * rows:h + (i + 1) * rows, :] = jnp.dot(
            rev_ref[r, :], m_ref[...], preferred_element_type=F32).astype(o_ref.dtype)
    mid = jnp.dot(alt_ref[...], a_ref[...], preferred_element_type=F32)
    o_ref[h:h + 1, :] = mid[0:1].astype(o_ref.dtype)


def _fnet(z, tables):
    bsz, s, w = z.shape
    rows = min(s // 2, 512)
    blk = lambda: pl.BlockSpec((None, s, w), lambda b: (b, 0, 0))
    return pl.pallas_call(
        functools.partial(_fnet_body, rows=rows),
        grid=(bsz,),
        in_specs=[blk()] + [_const_spec(a.shape) for a in tables],
        out_specs=blk(),
        out_shape=jax.ShapeDtypeStruct((bsz, s, w), BF16),
        scratch_shapes=[pltpu.VMEM((s, w), BF16), pltpu.VMEM((s, w), BF16), pltpu.VMEM((s // 2, w), BF16)],
        compiler_params=_params(("parallel",)),
        name="fnet",
    )(z, *tables)


RWKV_BLOCK = REC_BLOCK
RWKV_NCH = REC_NCH
RWKV_PAIRS = RWKV_HEADS // 2
SOLVE_GROUP = 2


def _chunk(ref, j, sl):
    return ref[pl.ds(j * CHUNK, CHUNK), sl]


def _rwkv_solve_stages(units, ops, tinv_ref, pkb_ref, mkv_ref, masks):
    L = CHUNK
    n2 = 2 * L
    lo, hi, eye, m_strict, m_read = masks
    top = (lax.broadcasted_iota(jnp.int32, (2 * n2, n2), 0) & (n2 - 1)) < L
    t_inv, pw, s_kb, v_ms, m_k = {}, {}, {}, {}, {}
    for n, u in enumerate(units):
        d, j, p = u
        sl = pl.ds(p * LANES, LANES)
        kaph_ref, rh_ref, _, _, v_ref, _, kbcm_ref = ops[d]
        xs = jnp.concatenate([_masked_stack(_chunk(kaph_ref, j, sl), lo, hi),
                              _masked_stack(_chunk(rh_ref, j, sl), lo, hi)], axis=0)
        s_kb[u] = _mm(xs, kbcm_ref[pl.ds(j * RWKV_W + p * LANES, LANES), :])
        v_ms[u] = _masked_stack(_chunk(v_ref, j, sl), lo, hi)
        if n % 2:
            yield
    for n, u in enumerate(units):
        d, j, p = u
        s_sw = pltpu.roll(s_kb[u], L, 1)
        vs_k = jnp.where(top, s_kb[u], s_sw)
        vs_b = jnp.where(top, s_sw, s_kb[u])
        m_k[u] = jnp.where(m_strict[d], vs_k[0:n2], 0.0).astype(BF16)
        nmat = jnp.where(m_strict[d], -vs_b[0:n2], 0.0)
        p_k = jnp.where(m_read[d], vs_k[n2:2 * n2], 0.0)
        p_b = jnp.where(m_read[d], -vs_b[n2:2 * n2], 0.0)
        pkb_ref[d, j, p] = jnp.concatenate([p_k, p_b], axis=1).astype(BF16)
        t_inv[u] = eye + nmat
        pw[u] = nmat.astype(BF16)
        if n % 2:
            yield
    for n, u in enumerate(units):
        d, j, p = u
        mkv_ref[d, j, p] = _mm(m_k[u], v_ms[u])
        if n % 2:
            yield
    for n, u in enumerate(units):
        pw[u] = jnp.dot(pw[u], pw[u], preferred_element_type=F32).astype(BF16)
        if n % 2:
            yield
    for _ in range(4):
        for n, u in enumerate(units):
            both = jnp.dot(jnp.concatenate([t_inv[u].astype(BF16), pw[u]], axis=0), pw[u],
                           preferred_element_type=F32)
            t_inv[u] = t_inv[u] + both[0:n2]
            pw[u] = both[n2:2 * n2].astype(BF16)
            if n % 2:
                yield
    for n, u in enumerate(units):
        d, j, p = u
        tinv_ref[d, j, p] = (t_inv[u] + jnp.dot(t_inv[u].astype(BF16), pw[u],
                                                preferred_element_type=F32)).astype(BF16)
        if n % 2:
            yield


def _rwkv_scan_stages(steps, ops, tinv_ref, pkb_ref, mkv_ref, st_ref, y_refs, masks):
    L = CHUNK
    n2 = 2 * L
    lo, hi = masks[0], masks[1]
    units = [(d, p) for d in range(2) for p in range(RWKV_PAIRS)]
    for j in steps:
        cj = {0: j, 1: RWKV_NCH - 1 - j}
        v_ms, xh, uu = {}, {}, {}
        for (d, p) in units:
            sl = pl.ds(p * LANES, LANES)
            xs = jnp.concatenate([_masked_stack(_chunk(ops[d][0], cj[d], sl), lo, hi),
                                  _masked_stack(_chunk(ops[d][1], cj[d], sl), lo, hi)], axis=0)
            xh[d, p] = _mm_nt(xs, st_ref[d, p])
            yield
        for (d, p) in units:
            sl = pl.ds(p * LANES, LANES)
            v_ms[d, p] = _masked_stack(_chunk(ops[d][4], cj[d], sl), lo, hi)
            uu[d, p] = _mm(tinv_ref[d, cj[d], p], xh[d, p][0:n2] + mkv_ref[d, cj[d], p])
            yield
        for (d, p) in units:
            sl = pl.ds(p * LANES, LANES)
            y = xh[d, p][n2:2 * n2] + _mm(pkb_ref[d, cj[d], p], jnp.concatenate([v_ms[d, p], uu[d, p]], axis=0))
            y_refs[d][pl.ds(cj[d] * L, L), sl] = (y[0:L] + y[L:n2]).astype(BF16)
            kb = jnp.concatenate([_masked_stack(_chunk(ops[d][2], cj[d], sl), lo, hi),
                                  _masked_stack(_chunk(ops[d][3], cj[d], sl), lo, hi)], axis=0)
            dec = ops[d][5][pl.ds(cj[d] * 8, 1), sl]
            st_ref[d, p] = (st_ref[d, p] + _mm_tn(jnp.concatenate([v_ms[d, p], -uu[d, p]], axis=0), kb)) * dec
            yield


def _rwkv_program(kaf_ref, rhf_ref, ktf_ref, btf_ref, vf_ref, decf_ref, kbf_ref,
                  kab_ref, rhb_ref, ktb_ref, btb_ref, vb_ref, decb_ref, kbb_ref,
                  yf_ref, yb_ref, tinv_ref, pkb_ref, mkv_ref, st_ref):
    ops =((kaf_ref, rhf_ref, ktf_ref, btf_ref, vf_ref, decf_ref, kbf_ref),
           (kab_ref, rhb_ref, ktb_ref, btb_ref, vb_ref, decb_ref, kbb_ref))
    lo, hi = _lane_masks()
    n2 = 2 * CHUNK
    ii = lax.broadcasted_iota(jnp.int32, (n2, n2), 0)
    jj = lax.broadcasted_iota(jnp.int32, (n2, n2), 1)
    eye = (ii == jj).astype(F32)
    m_strict = (_pair_masks(True, inclusive=False), _pair_masks(False, inclusive=False))
    m_read = (_pair_masks(True, inclusive=True), _pair_masks(False, inclusive=False))
    masks = (lo, hi, eye, m_strict, m_read)

    def solve(steps):
        units = [(d, j if d == 0 else RWKV_NCH - 1 - j, p)
                 for j in steps for d in range(2) for p in range(RWKV_PAIRS)]
        return _rwkv_solve_stages(units, ops, tinv_ref, pkb_ref, mkv_ref, masks)

    def scan(steps):
        return _rwkv_scan_stages(steps, ops, tinv_ref, pkb_ref, mkv_ref, st_ref, (yf_ref, yb_ref), masks)

    groups = [list(range(j, j + SOLVE_GROUP)) for j in range(0, RWKV_NCH, SOLVE_GROUP)]
    yield from solve(groups[0])
    for done, ahead in zip(groups, groups[1:]):
        yield from _woven(solve(ahead), scan(done))
    yield from scan(groups[-1])


N_GLA_IN, N_RWKV_IN = 9, 14


def _recurrences_body(*refs):
    a, b = N_GLA_IN, N_GLA_IN + N_RWKV_IN
    gla_refs = refs[:a] + refs[b:b + 2] + refs[b + 4:b + 9]
    rwkv_refs = refs[a:b] + refs[b + 2:b + 4] + refs[b + 9:]

    @pl.when(pl.program_id(1) == 0)
    def _():
        for st_ref in (gla_refs[-1], rwkv_refs[-1]):
            st_ref[...] = jnp.zeros_like(st_ref)

    _weave(_rwkv_program(*rwkv_refs), _gla_program(*gla_refs), shares=(7, 1))


def _recurrences(z_gla, gla_consts, fwd_ops, bwd_ops, bsz, s):
    R = RWKV_BLOCK
    nb = s // R
    fc = lambda i: i
    bc = lambda i: nb - 1 - i
    blk = lambda cm, k=0: pl.BlockSpec((None, R, RWKV_W), lambda b, i: (b, cm(i), k))
    dec = lambda cm: pl.BlockSpec((None, RWKV_NCH * 8, RWKV_W), lambda b, i: (b, cm(i), 0))
    cm = lambda cm_: pl.BlockSpec((None, RWKV_NCH * RWKV_W, LANES), lambda b, i: (b, cm_(i), 0))
    seq = lambda a: a.reshape(bsz, -1, a.shape[-1])
    zspec = lambda cm_: pl.BlockSpec((None, R, GLA_COLS), lambda b, i: (b, cm_(i), 0))
    gla_consts, gla_args = _resident(gla_consts)
    assert 2 + len(gla_args) == N_GLA_IN and len(fwd_ops) + len(bwd_ops) == N_RWKV_IN
    out = lambda w: jax.ShapeDtypeStruct((bsz, s, w), BF16)
    per_chunk = lambda rows, w: pltpu.VMEM((2, REC_NCH, rows, w), F32)
    gla_scratch = [per_chunk(CHUNK, GLA_QK), per_chunk(CHUNK, GLA_QK), per_chunk(CHUNK, GLA_V),
                   pltpu.VMEM((2, REC_NCH * 8, GLA_QK), F32),
                   pltpu.VMEM((2, GLA_PAIRS, GLA_DV, LANES), F32)]
    unit = (2, RWKV_NCH, RWKV_PAIRS)
    rwkv_scratch = [pltpu.VMEM(unit + (LANES, LANES), BF16),
                    pltpu.VMEM(unit + (LANES, 2 * LANES), BF16),
                    pltpu.VMEM(unit + (LANES, LANES), F32),
                    pltpu.VMEM((2, RWKV_PAIRS, LANES, LANES), F32)]
    return pl.pallas_call(
        _recurrences_body,
        grid=(bsz, nb),
        in_specs=[zspec(fc), zspec(bc)] + gla_consts
                 + [blk(fc, k) for k in (0, 1, 2, 3, 0)] + [dec(fc), cm(fc)]
                 + [blk(bc, k) for k in (0, 1, 2, 3, 0)] + [dec(bc), cm(bc)],
        out_specs=[pl.BlockSpec((None, R, GLA_V), lambda b, i: (b, fc(i), 0)),
                   pl.BlockSpec((None, R, GLA_V), lambda b, i: (b, bc(i), 0)), blk(fc), blk(bc)],
        out_shape=[out(GLA_V)] * 2 + [out(RWKV_W)] * 2,
        scratch_shapes=gla_scratch + rwkv_scratch,
        compiler_params=_params(("parallel", "arbitrary")),
        name="recurrences",
    )(z_gla, z_gla, *gla_args, *[seq(a) for a in fwd_ops], *[seq(a) for a in bwd_ops])


def _merge_body(x_ref, gof_ref, gob_ref, gr_ref, fn_ref, ryf_ref, ryb_ref, rbon_ref, rg_ref, gate_ref,
                gn_ref, lng_ref, lnb_ref, gavg_ref, pg_ref, pf_ref, pr_ref, wo_ref, o_ref):
    o = gof_ref[...].astype(F32) + gob_ref[...].astype(F32)
    parts = []
    for h in range(GLA_HEADS):
        oh = o[:, h * GLA_DV:(h + 1) * GLA_DV]
        parts.append(oh * lax.rsqrt(jnp.mean(oh * oh, axis=-1, keepdims=True) + NORM_EPS))
    rg = gr_ref[...].astype(F32)
    y_a = jnp.concatenate(parts, axis=1) * gn_ref[...] * (rg * _sigmoid(rg))
    y = ryf_ref[...].astype(F32) + ryb_ref[...].astype(F32)
    mean = _group_sums(y, gavg_ref[...], 2)
    yc = y - mean
    var = _group_sums(yc * yc, gavg_ref[...], 1)
    y_c = (yc * lax.rsqrt(var + RWKV_LN_EPS) * lng_ref[...] + lnb_ref[...] + rbon_ref[...].astype(F32)) * rg_ref[...].astype(F32)
    d = x_ref.shape[1]
    gate = gate_ref[...].astype(F32)
    merged = (_sigmoid(gate[:, 0:d]) * _mm(y_a, pg_ref[...])
              + _sigmoid(gate[:, d:2 * d]) * _mm(fn_ref[...], pf_ref[...])
              + _sigmoid(gate[:, 2 * d:3 * d]) * _mm(y_c, pr_ref[...]))
    o_ref[...] = x_ref[...] + _mm(merged, wo_ref[...])


def _merge(x, gla_of, gla_ob, z_gla, y_fnet, r_yf, r_yb, r_bonus, r_g, z_gate,
           gn, lng, lnb, gavg, pg, pf, pr, wo, *, tm):
    t, d = x.shape
    row = lambda c: pl.BlockSpec((tm, c), lambda i: (i, 0))
    r_col = (2 * GLA_QK + GLA_V) // GLA_V
    gr_spec = pl.BlockSpec((tm, GLA_V), lambda i: (i, r_col))
    tok_col = lambda k: pl.BlockSpec((tm, RWKV_W), lambda i: (i, k))
    specs, args = _resident((gn, lng, lnb, gavg, pg, pf, pr, wo))
    return pl.pallas_call(
        _merge_body,
        grid=(t // tm,),
        in_specs=[row(d), row(GLA_V), row(GLA_V), gr_spec, row(FNET_W), row(RWKV_W), row(RWKV_W),
                  tok_col(1), tok_col(2), row(GATE_COLS)] + specs,
        out_specs=row(d),
        out_shape=jax.ShapeDtypeStruct((t, d), F32),
        compiler_params=_params(("parallel",)),
        name="merge",
    )(x, gla_of, gla_ob, z_gla, y_fnet, r_yf, r_yb, r_bonus, r_g, z_gate, *args)


def _dft_tables(s):
    def cos_sin(n, rows, cols):
        ang = (2.0 * jnp.pi / n) * ((jnp.arange(rows, dtype=jnp.int32)[:, None] * cols[None, :]) % n).astype(F32)
        return jnp.cos(ang), jnp.sin(ang)

    def dft(n, rows):
        step = 64 if n % 64 == 0 and n > 64 else n
        c_lo, s_lo = cos_sin(n, rows, jnp.arange(step, dtype=jnp.int32))
        c_hi, s_hi = cos_sin(n, rows, jnp.arange(n // step, dtype=jnp.int32) * step)
        cos = c_hi[:, :, None] * c_lo[:, None, :] - s_hi[:, :, None] * s_lo[:, None, :]
        sin = s_hi[:, :, None] * c_lo[:, None, :] + c_hi[:, :, None] * s_lo[:, None, :]
        return cos.reshape(rows, n) * (n ** -0.5), sin.reshape(rows, n) * (n ** -0.5)

    cg, sg = dft(FNET_GC, FNET_GC)
    eye = jnp.eye(FNET_GROUPS, dtype=F32)
    cc = jnp.kron(eye, cg).astype(BF16)
    sc = jnp.kron(eye, sg).astype(BF16)
    h = s // 2
    cp, sp = dft(s, h)
    m = jnp.arange(s)
    alt = jnp.broadcast_to(jnp.where(m % 2 == 0, 1.0, -1.0) * (s ** -0.5), (8, s))
    r = jnp.arange(h)
    rev = (r[None, :] == h - r[:, None]).astype(BF16)
    return cc, sc, cp.astype(BF16), sp.astype(BF16), alt.astype(BF16), rev


def _tri(fwd, n):
    i = jnp.arange(n)
    same = (i[None, :] // CHUNK) == (i[:, None] // CHUNK)
    m = (i[None, :] <= i[:, None]) if fwd else (i[None, :] >= i[:, None])
    return (m & same).astype(BF16)


def _chunk_sel(n):
    return (jnp.arange(n // CHUNK * 8)[:, None] // 8 == jnp.arange(n)[None, :] // CHUNK).astype(BF16)


def _pad_rows(w, offset, total):
    return jnp.zeros((w.shape[0], total, w.shape[2]), F32).at[:, offset:offset + w.shape[1]].set(w.astype(F32))


def kernel(x, ffn1_norm, ffn1_gate, ffn1_up, ffn1_down, mix_norm, w_in, gla_up_f, gla_bias_f, gla_up_b, gla_bias_b, gla_norm, rwkv_mu, rwkv_w0_f, rwkv_w2_f, rwkv_w0_b, rwkv_w2_b, rwkv_a0_f, rwkv_a2_f, rwkv_a0_b, rwkv_a2_b, rwkv_g2, rwkv_k_k, rwkv_k_a, rwkv_r_k, rwkv_ln_g, rwkv_ln_b, proj_gla, proj_fnet, proj_rwkv, w_out, ffn2_norm, ffn2_gate, ffn2_up, ffn2_down, final_norm):
    bsz, s, d = x.shape
    depth = w_in.shape[0]
    t = bsz * s
    tm = 512 if t % 512 == 0 else 256
    assert s % REC_BLOCK == 0 and t % tm == 0 and d * 3 == GATE_COLS

    fnet_tables = _dft_tables(s)
    trif_r, trib_r, sel_r = _tri(True, SEQ_BLOCK), _tri(False, SEQ_BLOCK), _chunk_sel(SEQ_BLOCK)
    gla_tables = (_tri(True, REC_BLOCK), _tri(False, REC_BLOCK), _chunk_sel(REC_BLOCK))
    head_of = jnp.arange(LANES) // RWKV_N
    same_head = head_of[:, None] == head_of[None, :]
    gsum = same_head.astype(BF16)
    gavg = (same_head.astype(F32) * (1.0 / RWKV_N)).astype(BF16)
    rows = lambda a: a.astype(F32).reshape(depth, 1, -1)
    bf16 = lambda a: a.astype(BF16)
    fnorm = final_norm.astype(F32).reshape(1, -1)
    ffn1 = (rows(ffn1_norm), bf16(ffn1_gate), bf16(ffn1_up), bf16(ffn1_down))
    ffn2 = (rows(ffn2_norm), bf16(ffn2_gate), bf16(ffn2_up), bf16(ffn2_down))
    o_f, o_r, o_g = GLA_COLS_RAW, GLA_COLS_RAW + FNET_W, GLA_COLS_RAW + FNET_W + RWKV_COLS
    w_pad = bf16(jnp.concatenate([
        w_in[:, :, :o_f], jnp.zeros((depth, d, GLA_COLS - GLA_COLS_RAW), w_in.dtype),
        w_in[:, :, o_f:o_r],
        w_in[:, :, o_r:o_g], jnp.zeros((depth, d, RWKV_COLS_PAD - RWKV_COLS), w_in.dtype),
        w_in[:, :, o_g:]], axis=2))
    mu = jnp.concatenate([rwkv_mu.astype(F32), jnp.zeros((depth, RWKV_COLS_PAD - RWKV_COLS), F32)],
                         axis=1).reshape(depth, 1, -1)
    inproj_layered = (
        rows(mix_norm), w_pad, mu, rows(rwkv_w0_f), rows(rwkv_w0_b), rows(rwkv_a0_f), rows(rwkv_a0_b),
        jnp.concatenate([_pad_rows(rwkv_w2_f, 0, LANES), _pad_rows(rwkv_w2_b, RWKV_RANK, LANES)], axis=2),
        jnp.concatenate([_pad_rows(rwkv_a2_f, 2 * RWKV_RANK, LANES), _pad_rows(rwkv_a2_b, 3 * RWKV_RANK, LANES)],
                        axis=2),
        _pad_rows(rwkv_g2, 0, LANES), rows(rwkv_k_k), rows(rwkv_k_a), rows(rwkv_r_k))
    gla_layered = (_pad_rows(gla_up_f, 0, LANES), rows(gla_bias_f), _pad_rows(gla_up_b, GLA_RANK, LANES),
                   rows(gla_bias_b))
    merge_layered = (rows(gla_norm), rows(rwkv_ln_g), rows(rwkv_ln_b))
    proj_layered = (bf16(proj_gla), bf16(proj_fnet), bf16(proj_rwkv), bf16(w_out))

    x2 = x.reshape(t, d)
    for l in range(depth):
        at = lambda arrays: tuple((a, l) for a in arrays)
        x2 = _ffn(x2, at(ffn1) + (fnorm,), final=False, tm=tm)

        z_gla, z_fnet, z_gate, op_f, dec_f, kb_f, op_b, dec_b, kb_b, tok = _inproj(
            x2, at(inproj_layered) + (trif_r, trib_r, sel_r, gsum), seq=s)

        y_fnet = _fnet(z_fnet.reshape(bsz, s, FNET_W), fnet_tables)

        gla_of, gla_ob, r_yf, r_yb = _recurrences(
            z_gla.reshape(bsz, s, GLA_COLS), at(gla_layered) + gla_tables,
            (op_f,) * 4 + (tok, dec_f, kb_f), (op_b,) * 4 + (tok, dec_b, kb_b), bsz, s)

        flat = lambda a: a.reshape(t, a.shape[-1])
        x2 = _merge(x2, flat(gla_of), flat(gla_ob), z_gla, flat(y_fnet), flat(r_yf), flat(r_yb),
                    tok, tok, z_gate, *at(merge_layered), gavg, *at(proj_layered), tm=tm)

        x2 = _ffn(x2, at(ffn2) + (fnorm,), final=(l == depth - 1), tm=tm)
    return x2.reshape(bsz, s, d)
```

```python
import functools

import jax
import jax.numpy as jnp
from jax import lax
from jax.experimental import pallas as pl
from jax.experimental.pallas import tpu as pltpu

F32 = jnp.float32
BF16 = jnp.bfloat16

NORM_EPS = 1e-6
RWKV_LN_EPS = 64e-5
GLA_TAU = 16.0

GLA_HEADS, GLA_DK, GLA_DV, GLA_RANK = 4, 64, 128, 16
GLA_QK, GLA_V = GLA_HEADS * GLA_DK, GLA_HEADS * GLA_DV
FNET_GROUPS, FNET_GC = 4, 128
FNET_W = FNET_GROUPS * FNET_GC
RWKV_HEADS, RWKV_N = 8, 64
RWKV_W = RWKV_HEADS * RWKV_N
RWKV_RANK = 32
RWKV_LOWRANK = 4 * RWKV_RANK
RWKV_GATE_RANK = 96
RWKV_COLS = 3 * RWKV_W + RWKV_LOWRANK + RWKV_GATE_RANK

LANES = 128
SUBLANES = 8
CHUNK = 64

GLA_COLS_RAW = 2 * GLA_QK + 2 * GLA_V + 2 * GLA_RANK
GLA_COLS = -(-GLA_COLS_RAW // LANES) * LANES
RWKV_COLS_PAD = -(-RWKV_COLS // LANES) * LANES
GATE_COLS = 3 * 1024

STREAM_BUFFERS = 3
VMEM_V7X = 64 * 1024 * 1024
VMEM_LIMIT = VMEM_V7X - 8 * 1024 * 1024


def _mm(a, b):
    return jnp.dot(a.astype(BF16), b.astype(BF16), preferred_element_type=F32)


def _mm_nt(a, b):
    return lax.dot_general(a.astype(BF16), b.astype(BF16), (((1,), (1,)), ((), ())),
                           preferred_element_type=F32)


def _mm_tn(a, b):
    return lax.dot_general(a.astype(BF16), b.astype(BF16), (((0,), (0,)), ((), ())),
                           preferred_element_type=F32)


def _split(x, n):
    pieces = []
    for _ in range(n - 1):
        p = x.astype(BF16)
        pieces.append(p)
        x = x - p.astype(F32)
    pieces.append(x.astype(BF16))
    return pieces


def _group_sums(a, w, n):
    parts = _split(a, n)
    tiles = []
    for c in range(0, a.shape[1], LANES):
        out = None
        for p in parts:
            d = jnp.dot(p[:, c:c + LANES], w, preferred_element_type=F32)
            out = d if out is None else out + d
        tiles.append(out)
    return jnp.concatenate(tiles, axis=1)


def _mm_parts(w, parts):
    out = None
    for p in parts:
        d = jnp.dot(w, p, preferred_element_type=F32)
        out = d if out is None else out + d
    return out


def _sigmoid(x):
    return 0.5 * jnp.tanh(0.5 * x) + 0.5


def _log_sigmoid(x):
    return jnp.minimum(x, 0.0) - jnp.log(1.0 + jnp.exp(-jnp.abs(x)))


def _rms(x, g):
    return x * lax.rsqrt(jnp.mean(x * x, axis=-1, keepdims=True) + NORM_EPS) * g


def _const_spec(shape):
    nd = len(shape)
    return pl.BlockSpec(shape, lambda *_: (0,) * nd, pipeline_mode=pl.Buffered(1))


def _resident(operands):
    specs, args = [], []
    for a in operands:
        if isinstance(a, tuple):
            arr, layer = a
            nd = arr.ndim - 1
            specs.append(pl.BlockSpec((None,) + arr.shape[1:], lambda *_, layer=layer, nd=nd: (layer,) + (0,) * nd,
                                      pipeline_mode=pl.Buffered(1)))
            args.append(arr)
        else:
            specs.append(_const_spec(a.shape))
            args.append(a)
    return specs, args


def _params(sem):
    return pltpu.CompilerParams(dimension_semantics=sem, vmem_limit_bytes=VMEM_LIMIT)


def _ffn_stages(rows, x_ref, g_ref, wg_ref, wu_ref, wd_ref, fg_ref, o_ref, final):
    x = x_ref[rows, :]
    h = _rms(x, g_ref[...]).astype(BF16)
    yield
    gate = jnp.dot(h, wg_ref[...], preferred_element_type=F32)
    up = jnp.dot(h, wu_ref[...], preferred_element_type=F32)
    yield
    act = (gate * _sigmoid(gate) * up).astype(BF16)
    y = x + 0.5 * jnp.dot(act, wd_ref[...], preferred_element_type=F32)
    if final:
        y = _rms(y, fg_ref[...])
    o_ref[rows, :] = y
    yield


def _ffn_body(x_ref, g_ref, wg_ref, wu_ref, wd_ref, fg_ref, o_ref, *, final):
    half = x_ref.shape[0] // 2
    args = (x_ref, g_ref, wg_ref, wu_ref, wd_ref, fg_ref, o_ref, final)
    _weave(_ffn_stages(slice(0, half), *args), _ffn_stages(slice(half, 2 * half), *args))


def _ffn(x, consts, *, final, tm):
    t, d = x.shape
    row = pl.BlockSpec((tm, d), lambda i: (i, 0))
    specs, args = _resident(consts)
    return pl.pallas_call(
        functools.partial(_ffn_body, final=final),
        grid=(t // tm,),
        in_specs=[row] + specs,
        out_specs=row,
        out_shape=jax.ShapeDtypeStruct((t, d), F32),
        compiler_params=_params(("parallel",)),
        name="ffn",
    )(x, *args)


SEQ_BLOCK = 256
BLOCK_NCH = SEQ_BLOCK // CHUNK
REC_BLOCK = 512
REC_NCH = REC_BLOCK // CHUNK
E_NEG_HALF = 2.718281828459045 ** -0.5


def _rwkv_feature_stages(u, prev_row, next_row, prm, dir_refs, v_ref, bonus_ref, rg_ref):
    R, W = SEQ_BLOCK, RWKV_W
    (mu, w0f, w0b, a0f, a0b, w2, a2, g2, k_k, k_a, r_k, trif, trib, sel, gsum) = prm
    row = lax.broadcasted_iota(jnp.int32, (8, 1), 0)
    prev = pltpu.roll(u, 1, 0)
    nxt = pltpu.roll(u, R - 1, 0)
    prev = jnp.concatenate([jnp.where(row == 0, prev_row, prev[0:8]), prev[8:]], axis=0)
    nxt = jnp.concatenate([nxt[:R - 8], jnp.where(row == 7, next_row, nxt[R - 8:])], axis=0)
    u = u * (1.0 - mu) + (prev + nxt) * (0.5 * mu)
    yield
    r, k, v = u[:, 0:W], u[:, W:2 * W], u[:, 2 * W:3 * W]
    sm = u[:, 3 * W:3 * W + LANES]
    gd = u[:, 3 * W + LANES:3 * W + 2 * LANES]
    v_ref[...] = v.astype(BF16)
    zw = _mm(jnp.tanh(sm), w2)
    za = _mm(sm, a2)
    rg_ref[...] = _mm(_sigmoid(gd), g2).astype(rg_ref.dtype)
    kk = k * k_k
    kap = kk * lax.rsqrt(_group_sums(kk * kk, gsum, 1) + 1e-12)
    yield
    dirs = ((True, w0f, a0f, trif, slice(0, W)), (False, w0b, a0b, trib, slice(W, 2 * W)))
    lw, kd, beta, inc, tot = [], [], [], [], []
    for fwd, w0, a0, tri, sl in dirs:
        lw.append((-E_NEG_HALF) * _sigmoid(w0 + zw[:, sl]))
        a = _sigmoid(a0 + za[:, sl])
        kd.append(k * (1.0 + (a - 1.0) * k_a))
        beta.append(kap * a)
        yield
    bonus_ref[...] = (_group_sums(r * kd[0] * r_k, gsum, 1) * v).astype(bonus_ref.dtype)
    for d, (fwd, w0, a0, tri, sl) in enumerate(dirs):
        lw_parts = _split(lw[d], 2)
        inc.append(_mm_parts(tri, lw_parts))
        tot.append(_mm_parts(sel, lw_parts))
        yield
    for d, (fwd, w0, a0, tri, sl) in enumerate(dirs):
        ka_ref, rh_ref, kt_ref, bt_ref, dec_ref, kb_ref = dir_refs[d]
        e_exc = jnp.exp(inc[d] - lw[d])
        e_ninc = jnp.exp(-inc[d])
        ka_ref[...] = (kap * e_exc).astype(BF16)
        rh_ref[...] = (r * (jnp.exp(inc[d]) if fwd else e_exc)).astype(BF16)
        dec_ref[...] = jnp.exp(tot[d])
        yield
        k_t, b_t = kd[d] * e_ninc, beta[d] * e_ninc
        kt_ref[...] = k_t.astype(BF16)
        bt_ref[...] = b_t.astype(BF16)
        yield
        for j in range(BLOCK_NCH):
            rows = slice(j * CHUNK, (j + 1) * CHUNK)
            kb_ref[j * W:(j + 1) * W, :] = jnp.concatenate([k_t[rows], b_t[rows]], axis=0).T.astype(BF16)
        yield


def _column_dots(h, w_ref, pieces, width):
    for off, cols, ref in pieces:
        for a in range(0, cols, width):
            b = min(a + width, cols)
            ref[:, a:b] = jnp.dot(h, w_ref[:, off + a:off + b], preferred_element_type=F32).astype(ref.dtype)
            yield


INPROJ_BLOCKS = 2
INPROJ_SLAB = 512


def _inproj_body(x_ref, xp_ref, xn_ref, g_ref, w_ref,
                 mu_ref, w0f_ref, w0b_ref, a0f_ref, a0b_ref, w2_ref, a2_ref, g2_ref, kk_ref, ka_ref, rk_ref,
                 trif_ref, trib_ref, sel_ref, gsum_ref,
                 gla_ref, fnet_ref, gate_ref,
                 opf_ref, decf_ref, kbf_ref, opb_ref, decb_ref, kbb_ref, tok_ref, *, blocks_per_seq):
    R, W = SEQ_BLOCK, RWKV_W
    c0, c1, c2 = GLA_COLS, GLA_COLS + FNET_W, GLA_COLS + FNET_W + RWKV_COLS_PAD
    g = g_ref[...]
    norm = lambda rows: _rms(rows, g).astype(BF16)
    rwkv_cols = lambda hh: jnp.dot(hh, w_ref[:, c1:c2], preferred_element_type=F32)
    S8 = SUBLANES
    z_halo = rwkv_cols(norm(jnp.concatenate([xp_ref[...], xn_ref[...], x_ref[R - S8:R + S8, :]], axis=0)))
    pos = (pl.program_id(0) * INPROJ_BLOCKS) % blocks_per_seq
    last_before, first_after = z_halo[S8 - 1:S8, :], z_halo[S8:S8 + 1, :]
    last_of_a, first_of_b = z_halo[3 * S8 - 1:3 * S8, :], z_halo[3 * S8:3 * S8 + 1, :]
    edge_rows = ((jnp.where(pos > 0, last_before, 0.0), first_of_b),
                 (last_of_a, jnp.where(pos + 1 < blocks_per_seq - 1, first_after, 0.0)))
    prm = tuple(ref[...] for ref in (mu_ref, w0f_ref, w0b_ref, a0f_ref, a0b_ref, w2_ref, a2_ref, g2_ref,
                                     kk_ref, ka_ref, rk_ref, trif_ref, trib_ref, sel_ref, gsum_ref))

    def rows_of(ref, blk):
        n = ref.shape[0] // INPROJ_BLOCKS
        return ref.at[pl.ds(blk * n, n)]

    def cols_of(ref, k):
        return ref.at[:, pl.ds(k * W, W)]

    h, u = {}, {}
    h[0] = norm(x_ref[0:R, :])
    u[0] = rwkv_cols(h[0])

    def elementwise():
        for blk in range(INPROJ_BLOCKS):
            dir_refs = tuple(tuple(cols_of(rows_of(op_ref, blk), k) for k in range(4))
                             + (rows_of(dec_ref, blk), rows_of(kb_ref, blk))
                             for op_ref, dec_ref, kb_ref in ((opf_ref, decf_ref, kbf_ref), (opb_ref, decb_ref, kbb_ref)))
            tok = rows_of(tok_ref, blk)
            yield from _rwkv_feature_stages(u[blk], *edge_rows[blk], prm, dir_refs,
                                            cols_of(tok, 0), cols_of(tok, 1), cols_of(tok, 2))

    def matmuls():
        h[1] = norm(x_ref[R:2 * R, :])
        u[1] = rwkv_cols(h[1])
        yield
        for blk in range(INPROJ_BLOCKS):
            yield from _column_dots(h[blk], w_ref, ((c2, GATE_COLS, rows_of(gate_ref, blk)),
                                                    (0, c0, rows_of(gla_ref, blk)),
                                                    (c0, FNET_W, rows_of(fnet_ref, blk))), INPROJ_SLAB)

    _weave(elementwise(), matmuls())


def _inproj(x, consts, *, seq):
    t, d = x.shape
    specs, args = _resident(consts)
    nblk = INPROJ_BLOCKS
    tm = nblk * SEQ_BLOCK
    assert (seq // SEQ_BLOCK) % nblk == 0
    sub = SUBLANES
    per = tm // sub
    row = lambda c: pl.BlockSpec((tm, c), lambda i: (i, 0))
    prev = pl.BlockSpec((sub, d), lambda i: (jnp.maximum(i * per - 1, 0), 0))
    nxt = pl.BlockSpec((sub, d), lambda i: (jnp.minimum((i + 1) * per, t // sub - 1), 0))
    dec = pl.BlockSpec((nblk * BLOCK_NCH * 8, RWKV_W), lambda i: (i, 0))
    op = lambda k: jax.ShapeDtypeStruct((t, k * RWKV_W), BF16)
    dec_shape = jax.ShapeDtypeStruct((t // CHUNK * 8, RWKV_W), F32)
    cm = pl.BlockSpec((nblk * BLOCK_NCH * RWKV_W, LANES), lambda i: (i, 0))
    cm_shape = jax.ShapeDtypeStruct((t // CHUNK * RWKV_W, LANES), BF16)
    return pl.pallas_call(
        functools.partial(_inproj_body, blocks_per_seq=seq // SEQ_BLOCK),
        grid=(t // tm,),
        in_specs=[row(d), prev, nxt] + specs,
        out_specs=[row(GLA_COLS), row(FNET_W), row(GATE_COLS)]
                  + [row(4 * RWKV_W), dec, cm] * 2 + [row(3 * RWKV_W)],
        out_shape=[jax.ShapeDtypeStruct((t, GLA_COLS), BF16),
                   jax.ShapeDtypeStruct((t, FNET_W), BF16),
                   jax.ShapeDtypeStruct((t, GATE_COLS), BF16)]
                  + [op(4), dec_shape, cm_shape] * 2 + [op(3)],
        compiler_params=_params(("parallel",)),
        name="inproj",
    )(x, x, x, *args)


def _woven(*gens, shares=None):
    live = list(zip(gens, shares or (1,) * len(gens)))
    while live:
        for entry in list(live):
            g, n = entry
            for _ in range(n):
                try:
                    next(g)
                except StopIteration:
                    live.remove(entry)
                    break
                yield


def _weave(*gens, shares=None):
    for _ in _woven(*gens, shares=shares):
        pass


def _lane_masks():
    lane = lax.broadcasted_iota(jnp.int32, (CHUNK, LANES), 1)
    return lane < (LANES // 2), lane >= (LANES // 2)


def _masked_stack(x, lo, hi):
    return jnp.concatenate([jnp.where(lo, x, 0.0), jnp.where(hi, x, 0.0)], axis=0)


def _pair_masks(fwd, inclusive):
    n = 2 * CHUNK
    i = lax.broadcasted_iota(jnp.int32, (n, n), 0)
    j = lax.broadcasted_iota(jnp.int32, (n, n), 1)
    same = (i < CHUNK) == (j < CHUNK)
    strict = (j < i) if fwd else (j > i)
    if inclusive:
        return same & (strict | (i == j))
    return same & strict


GLA_PAIRS = GLA_HEADS // 2


def _gla_prep_stages(z_ref, up, bias, tri, sel, d, qd_ref, kd_ref, v_ref, dec_ref):
    L = CHUNK
    z = z_ref[...].astype(F32)
    q = z[:, 0:GLA_QK] * (GLA_DK ** -0.5)
    k = z[:, GLA_QK:2 * GLA_QK]
    v = z[:, 2 * GLA_QK:2 * GLA_QK + GLA_V]
    dn = z[:, GLA_COLS - LANES:GLA_COLS]
    for j in range(REC_NCH):
        v_ref[d, j] = v[j * L:(j + 1) * L]
    la = _log_sigmoid(_mm(dn, up) + bias) * (1.0 / GLA_TAU)
    yield
    la_parts = _split(la, 2)
    b = _mm_parts(tri, la_parts)
    dec_ref[d] = jnp.exp(_mm_parts(sel, la_parts))
    yield
    for ref, val in ((qd_ref, q * jnp.exp(b)), (kd_ref, k * jnp.exp(-b))):
        for j in range(REC_NCH):
            ref[d, j] = val[j * L:(j + 1) * L]
        yield


def _gla_step_stages(d, qd_ref, kd_ref, v_ref, dec_ref, st_ref, o_ref, masks):
    L = CHUNK
    lo, hi, m_read = masks
    units = [(j, p) for j in range(REC_NCH) for p in range(GLA_PAIRS)]
    chunk = lambda j: j if d == 0 else REC_NCH - 1 - j
    q_ms, kd_ms, v_st, sc, inc, o_intra = {}, {}, {}, {}, {}, {}
    for n, u in enumerate(units):
        j, p = u
        c = chunk(j)
        sl = pl.ds(p * LANES, LANES)
        q_ms[u] = _masked_stack(qd_ref[d, c, :, sl], lo, hi)
        kd_ms[u] = _masked_stack(kd_ref[d, c, :, sl], lo, hi)
        v_st[u] = jnp.concatenate([v_ref[d, c, :, pl.ds((2 * p) * GLA_DV, GLA_DV)],
                                   v_ref[d, c, :, pl.ds((2 * p + 1) * GLA_DV, GLA_DV)]], axis=0)
        sc[u] = jnp.where(m_read[d], _mm_nt(q_ms[u], kd_ms[u]), 0.0)
        inc[u] = _mm_tn(v_st[u], kd_ms[u])
        if n % 2:
            yield
    for n, u in enumerate(units):
        o_intra[u] = _mm(sc[u], v_st[u])
        if n % 2:
            yield
    for p in range(GLA_PAIRS):
        sl = pl.ds(p * LANES, LANES)
        st = st_ref[d, p]
        for j in range(REC_NCH):
            u = (j, p)
            c = chunk(j)
            o = o_intra[u] + _mm_nt(q_ms[u], st)
            o_ref[pl.ds(c * L, L), pl.ds((2 * p) * GLA_DV, GLA_DV)] = o[0:L].astype(BF16)
            o_ref[pl.ds(c * L, L), pl.ds((2 * p + 1) * GLA_DV, GLA_DV)] = o[L:2 * L].astype(BF16)
            st = (st + inc[u]) * dec_ref[d, pl.ds(c * 8, 1), sl]
            if j % 2:
                yield
        st_ref[d, p] = st


def _gla_program(zf_ref, zb_ref, upf_ref, bf_ref, upb_ref, bb_ref, trif_ref, trib_ref, sel_ref,
                 of_ref, ob_ref, qd_ref, kd_ref, v_ref, dec_ref, st_ref):
    scr = (qd_ref, kd_ref, v_ref, dec_ref)
    lo, hi = _lane_masks()
    masks = (lo, hi, (_pair_masks(True, inclusive=True), _pair_masks(False, inclusive=False)))
    yield from _gla_prep_stages(zf_ref, upf_ref[...], bf_ref[...], trif_ref[...], sel_ref[...], 0, *scr)
    yield from _woven(_gla_prep_stages(zb_ref, upb_ref[...], bb_ref[...], trib_ref[...], sel_ref[...], 1, *scr),
                      _gla_step_stages(0, *scr, st_ref, of_ref, masks))
    yield from _gla_step_stages(1, *scr, st_ref, ob_ref, masks)


def _fnet_body(z_ref, cc_ref, sc_ref, cp_ref, sp_ref, alt_ref, rev_ref, o_ref, a_ref, b_ref, m_ref, *, rows):
    s = z_ref.shape[0]
    h = s // 2
    z = z_ref[...]
    a_ref[...] = jnp.dot(z, cc_ref[...], preferred_element_type=F32).astype(BF16)
    b_ref[...] = jnp.dot(z, sc_ref[...], preferred_element_type=F32).astype(BF16)
    for i in range(h // rows):
        r = slice(i * rows, (i + 1) * rows)
        p = jnp.dot(cp_ref[r, :], a_ref[...], preferred_element_type=F32)
        q = jnp.dot(sp_ref[r, :], b_ref[...], preferred_element_type=F32)
        o_ref[r, :] = (p - q).astype(o_ref.dtype)
        m_ref[r, :] = (p + q).astype(BF16)
    for i in range(h // rows):
        r = slice(i * rows, (i + 1) * rows)
        o_ref[h + i * rows:h + (i + 1) * rows, :] = jnp.dot(
            rev_ref[r, :], m_ref[...], preferred_element_type=F32).astype(o_ref.dtype)
    mid = jnp.dot(alt_ref[...], a_ref[...], preferred_element_type=F32)
    o_ref[h:h + 1, :] = mid[0:1].astype(o_ref.dtype)


def _fnet(z, tables):
    bsz, s, w = z.shape
    rows = min(s // 2, 512)
    blk = lambda: pl.BlockSpec((None, s, w), lambda b: (b, 0, 0))
    return pl.pallas_call(
        functools.partial(_fnet_body, rows=rows),
        grid=(bsz,),
        in_specs=[blk()] + [_const_spec(a.shape) for a in tables],
        out_specs=blk(),
        out_shape=jax.ShapeDtypeStruct((bsz, s, w), BF16),
        scratch_shapes=[pltpu.VMEM((s, w), BF16), pltpu.VMEM((s, w), BF16), pltpu.VMEM((s // 2, w), BF16)],
        compiler_params=_params(("parallel",)),
        name="fnet",
    )(z, *tables)


RWKV_BLOCK = REC_BLOCK
RWKV_NCH = REC_NCH
RWKV_PAIRS = RWKV_HEADS // 2
SOLVE_GROUP = 2


def _chunk(ref, j, sl):
    return ref[pl.ds(j * CHUNK, CHUNK), sl]


def _rwkv_solve_stages(units, ops, tinv_ref, pkb_ref, mkv_ref, masks):
    L = CHUNK
    n2 = 2 * L
    lo, hi, eye, m_strict, m_read = masks
    top = (lax.broadcasted_iota(jnp.int32, (2 * n2, n2), 0) & (n2 - 1)) < L
    t_inv, pw, s_kb, v_ms, m_k = {}, {}, {}, {}, {}
    for n, u in enumerate(units):
        d, j, p = u
        sl = pl.ds(p * LANES, LANES)
        kaph_ref, rh_ref, _, _, v_ref, _, kbcm_ref = ops[d]
        xs = jnp.concatenate([_masked_stack(_chunk(kaph_ref, j, sl), lo, hi),
                              _masked_stack(_chunk(rh_ref, j, sl), lo, hi)], axis=0)
        s_kb[u] = _mm(xs, kbcm_ref[pl.ds(j * RWKV_W + p * LANES, LANES), :])
        v_ms[u] = _masked_stack(_chunk(v_ref, j, sl), lo, hi)
        if n % 2:
            yield
    for n, u in enumerate(units):
        d, j, p = u
        s_sw = pltpu.roll(s_kb[u], L, 1)
        vs_k = jnp.where(top, s_kb[u], s_sw)
        vs_b = jnp.where(top, s_sw, s_kb[u])
        m_k[u] = jnp.where(m_strict[d], vs_k[0:n2], 0.0).astype(BF16)
        nmat = jnp.where(m_strict[d], -vs_b[0:n2], 0.0)
        p_k = jnp.where(m_read[d], vs_k[n2:2 * n2], 0.0)
        p_b = jnp.where(m_read[d], -vs_b[n2:2 * n2], 0.0)
        pkb_ref[d, j, p] = jnp.concatenate([p_k, p_b], axis=1).astype(BF16)
        t_inv[u] = eye + nmat
        pw[u] = nmat.astype(BF16)
        if n % 2:
            yield
    for n, u in enumerate(units):
        d, j, p = u
        mkv_ref[d, j, p] = _mm(m_k[u], v_ms[u])
        if n % 2:
            yield
    for n, u in enumerate(units):
        pw[u] = jnp.dot(pw[u], pw[u], preferred_element_type=F32).astype(BF16)
        if n % 2:
            yield
    for _ in range(4):
        for n, u in enumerate(units):
            both = jnp.dot(jnp.concatenate([t_inv[u].astype(BF16), pw[u]], axis=0), pw[u],
                           preferred_element_type=F32)
            t_inv[u] = t_inv[u] + both[0:n2]
            pw[u] = both[n2:2 * n2].astype(BF16)
            if n % 2:
                yield
    for n, u in enumerate(units):
        d, j, p = u
        tinv_ref[d, j, p] = (t_inv[u] + jnp.dot(t_inv[u].astype(BF16), pw[u],
                                                preferred_element_type=F32)).astype(BF16)
        if n % 2:
            yield


def _rwkv_scan_stages(steps, ops, tinv_ref, pkb_ref, mkv_ref, st_ref, y_refs, masks):
    L = CHUNK
    n2 = 2 * L
    lo, hi = masks[0], masks[1]
    units = [(d, p) for d in range(2) for p in range(RWKV_PAIRS)]
    for j in steps:
        cj = {0: j, 1: RWKV_NCH - 1 - j}
        v_ms, xh, uu = {}, {}, {}
        for (d, p) in units:
            sl = pl.ds(p * LANES, LANES)
            xs = jnp.concatenate([_masked_stack(_chunk(ops[d][0], cj[d], sl), lo, hi),
                                  _masked_stack(_chunk(ops[d][1], cj[d], sl), lo, hi)], axis=0)
            xh[d, p] = _mm_nt(xs, st_ref[d, p])
            yield
        for (d, p) in units:
            sl = pl.ds(p * LANES, LANES)
            v_ms[d, p] = _masked_stack(_chunk(ops[d][4], cj[d], sl), lo, hi)
            uu[d, p] = _mm(tinv_ref[d, cj[d], p], xh[d, p][0:n2] + mkv_ref[d, cj[d], p])
            yield
        for (d, p) in units:
            sl = pl.ds(p * LANES, LANES)
            y = xh[d, p][n2:2 * n2] + _mm(pkb_ref[d, cj[d], p], jnp.concatenate([v_ms[d, p], uu[d, p]], axis=0))
            y_refs[d][pl.ds(cj[d] * L, L), sl] = (y[0:L] + y[L:n2]).astype(BF16)
            kb = jnp.concatenate([_masked_stack(_chunk(ops[d][2], cj[d], sl), lo, hi),
                                  _masked_stack(_chunk(ops[d][3], cj[d], sl), lo, hi)], axis=0)
            dec = ops[d][5][pl.ds(cj[d] * 8, 1), sl]
            st_ref[d, p] = (st_ref[d, p] + _mm_tn(jnp.concatenate([v_ms[d, p], -uu[d, p]], axis=0), kb)) * dec
            yield


def _rwkv_program(kaf_ref, rhf_ref, ktf_ref, btf_ref, vf_ref, decf_ref, kbf_ref,
                  kab_ref, rhb_ref, ktb_ref, btb_ref, vb_ref, decb_ref, kbb_ref,
                  yf_ref, yb_ref, tinv_ref, pkb_ref, mkv_ref, st_ref):
    ops =((kaf_ref, rhf_ref, ktf_ref, btf_ref, vf_ref, decf_ref, kbf_ref),
           (kab_ref, rhb_ref, ktb_ref, btb_ref, vb_ref, decb_ref, kbb_ref))
    lo, hi = _lane_masks()
    n2 = 2 * CHUNK
    ii = lax.broadcasted_iota(jnp.int32, (n2, n2), 0)
    jj = lax.broadcasted_iota(jnp.int32, (n2, n2), 1)
    eye = (ii == jj).astype(F32)
    m_strict = (_pair_masks(True, inclusive=False), _pair_masks(False, inclusive=False))
    m_read = (_pair_masks(True, inclusive=True), _pair_masks(False, inclusive=False))
    masks = (lo, hi, eye, m_strict, m_read)

    def solve(steps):
        units = [(d, j if d == 0 else RWKV_NCH - 1 - j, p)
                 for j in steps for d in range(2) for p in range(RWKV_PAIRS)]
        return _rwkv_solve_stages(units, ops, tinv_ref, pkb_ref, mkv_ref, masks)

    def scan(steps):
        return _rwkv_scan_stages(steps, ops, tinv_ref, pkb_ref, mkv_ref, st_ref, (yf_ref, yb_ref), masks)

    groups = [list(range(j, j + SOLVE_GROUP)) for j in range(0, RWKV_NCH, SOLVE_GROUP)]
    yield from solve(groups[0])
    for done, ahead in zip(groups, groups[1:]):
        yield from _woven(solve(ahead), scan(done))
    yield from scan(groups[-1])


N_GLA_IN, N_RWKV_IN = 9, 14


def _recurrences_body(*refs):
    a, b = N_GLA_IN, N_GLA_IN + N_RWKV_IN
    gla_refs = refs[:a] + refs[b:b + 2] + refs[b + 4:b + 9]
    rwkv_refs = refs[a:b] + refs[b + 2:b + 4] + refs[b + 9:]

    @pl.when(pl.program_id(1) == 0)
    def _():
        for st_ref in (gla_refs[-1], rwkv_refs[-1]):
            st_ref[...] = jnp.zeros_like(st_ref)

    _weave(_rwkv_program(*rwkv_refs), _gla_program(*gla_refs), shares=(7, 1))


def _recurrences(z_gla, gla_consts, fwd_ops, bwd_ops, bsz, s):
    R = RWKV_BLOCK
    nb = s // R
    fc = lambda i: i
    bc = lambda i: nb - 1 - i
    blk = lambda cm, k=0: pl.BlockSpec((None, R, RWKV_W), lambda b, i: (b, cm(i), k))
    dec = lambda cm: pl.BlockSpec((None, RWKV_NCH * 8, RWKV_W), lambda b, i: (b, cm(i), 0))
    cm = lambda cm_: pl.BlockSpec((None, RWKV_NCH * RWKV_W, LANES), lambda b, i: (b, cm_(i), 0))
    seq = lambda a: a.reshape(bsz, -1, a.shape[-1])
    zspec = lambda cm_: pl.BlockSpec((None, R, GLA_COLS), lambda b, i: (b, cm_(i), 0))
    gla_consts, gla_args = _resident(gla_consts)
    assert 2 + len(gla_args) == N_GLA_IN and len(fwd_ops) + len(bwd_ops) == N_RWKV_IN
    out = lambda w: jax.ShapeDtypeStruct((bsz, s, w), BF16)
    per_chunk = lambda rows, w: pltpu.VMEM((2, REC_NCH, rows, w), F32)
    gla_scratch = [per_chunk(CHUNK, GLA_QK), per_chunk(CHUNK, GLA_QK), per_chunk(CHUNK, GLA_V),
                   pltpu.VMEM((2, REC_NCH * 8, GLA_QK), F32),
                   pltpu.VMEM((2, GLA_PAIRS, GLA_DV, LANES), F32)]
    unit = (2, RWKV_NCH, RWKV_PAIRS)
    rwkv_scratch = [pltpu.VMEM(unit + (LANES, LANES), BF16),
                    pltpu.VMEM(unit + (LANES, 2 * LANES), BF16),
                    pltpu.VMEM(unit + (LANES, LANES), F32),
                    pltpu.VMEM((2, RWKV_PAIRS, LANES, LANES), F32)]
    return pl.pallas_call(
        _recurrences_body,
        grid=(bsz, nb),
        in_specs=[zspec(fc), zspec(bc)] + gla_consts
                 + [blk(fc, k) for k in (0, 1, 2, 3, 0)] + [dec(fc), cm(fc)]
                 + [blk(bc, k) for k in (0, 1, 2, 3, 0)] + [dec(bc), cm(bc)],
        out_specs=[pl.BlockSpec((None, R, GLA_V), lambda b, i: (b, fc(i), 0)),
                   pl.BlockSpec((None, R, GLA_V), lambda b, i: (b, bc(i), 0)), blk(fc), blk(bc)],
        out_shape=[out(GLA_V)] * 2 + [out(RWKV_W)] * 2,
        scratch_shapes=gla_scratch + rwkv_scratch,
        compiler_params=_params(("parallel", "arbitrary")),
        name="recurrences",
    )(z_gla, z_gla, *gla_args, *[seq(a) for a in fwd_ops], *[seq(a) for a in bwd_ops])


def _merge_body(x_ref, gof_ref, gob_ref, gr_ref, fn_ref, ryf_ref, ryb_ref, rbon_ref, rg_ref, gate_ref,
                gn_ref, lng_ref, lnb_ref, gavg_ref, pg_ref, pf_ref, pr_ref, wo_ref, o_ref):
    o = gof_ref[...].astype(F32) + gob_ref[...].astype(F32)
    parts = []
    for h in range(GLA_HEADS):
        oh = o[:, h * GLA_DV:(h + 1) * GLA_DV]
        parts.append(oh * lax.rsqrt(jnp.mean(oh * oh, axis=-1, keepdims=True) + NORM_EPS))
    rg = gr_ref[...].astype(F32)
    y_a = jnp.concatenate(parts, axis=1) * gn_ref[...] * (rg * _sigmoid(rg))
    y = ryf_ref[...].astype(F32) + ryb_ref[...].astype(F32)
    mean = _group_sums(y, gavg_ref[...], 2)
    yc = y - mean
    var = _group_sums(yc * yc, gavg_ref[...], 1)
    y_c = (yc * lax.rsqrt(var + RWKV_LN_EPS) * lng_ref[...] + lnb_ref[...] + rbon_ref[...].astype(F32)) * rg_ref[...].astype(F32)
    d = x_ref.shape[1]
    gate = gate_ref[...].astype(F32)
    merged = (_sigmoid(gate[:, 0:d]) * _mm(y_a, pg_ref[...])
              + _sigmoid(gate[:, d:2 * d]) * _mm(fn_ref[...], pf_ref[...])
              + _sigmoid(gate[:, 2 * d:3 * d]) * _mm(y_c, pr_ref[...]))
    o_ref[...] = x_ref[...] + _mm(merged, wo_ref[...])


def _merge(x, gla_of, gla_ob, z_gla, y_fnet, r_yf, r_yb, r_bonus, r_g, z_gate,
           gn, lng, lnb, gavg, pg, pf, pr, wo, *, tm):
    t, d = x.shape
    deep = pl.Buffered(STREAM_BUFFERS)
    row = lambda c, mode=None: pl.BlockSpec((tm, c), lambda i: (i, 0), pipeline_mode=mode)
    r_col = (2 * GLA_QK + GLA_V) // GLA_V
    gr_spec = pl.BlockSpec((tm, GLA_V), lambda i: (i, r_col))
    tok_col = lambda k: pl.BlockSpec((tm, RWKV_W), lambda i: (i, k))
    streamed = (x, gla_of, gla_ob, z_gla, y_fnet, r_yf, r_yb, r_bonus, r_g, z_gate)
    consts = (gn, lng, lnb, gavg, pg, pf, pr, wo)
    layers = [c[1] if isinstance(c, tuple) else None for c in consts]
    const_arrays = [c[0] if isinstance(c, tuple) else c for c in consts]
    n_s = len(streamed)

    def outer(*refs):
        s_refs, c_refs, o_ref = refs[:n_s], refs[n_s:n_s + len(consts)], refs[-1]
        c_views = [c if l is None else c.at[l] for c, l in zip(c_refs, layers)]

        def inner(*blocks):
            _merge_body(*blocks[:n_s], *c_views, blocks[n_s])

        pltpu.emit_pipeline(
            inner, grid=(t // tm,),
            in_specs=[row(d, deep), row(GLA_V), row(GLA_V), gr_spec, row(FNET_W), row(RWKV_W), row(RWKV_W),
                      tok_col(1), tok_col(2), row(GATE_COLS, deep)],
            out_specs=[row(d)])(*s_refs, o_ref)

    return pl.pallas_call(
        outer,
        in_specs=[pl.BlockSpec(memory_space=pl.ANY)] * n_s + [pl.BlockSpec(memory_space=pltpu.VMEM)] * len(consts),
        out_specs=pl.BlockSpec(memory_space=pl.ANY),
        out_shape=jax.ShapeDtypeStruct((t, d), F32),
        compiler_params=pltpu.CompilerParams(vmem_limit_bytes=VMEM_LIMIT),
        name="merge",
    )(*streamed, *const_arrays)


def _dft_tables(s):
    def cos_sin(n, rows, cols):
        ang = (2.0 * jnp.pi / n) * ((jnp.arange(rows, dtype=jnp.int32)[:, None] * cols[None, :]) % n).astype(F32)
        return jnp.cos(ang), jnp.sin(ang)

    def dft(n, rows):
        step = 64 if n % 64 == 0 and n > 64 else n
        c_lo, s_lo = cos_sin(n, rows, jnp.arange(step, dtype=jnp.int32))
        c_hi, s_hi = cos_sin(n, rows, jnp.arange(n // step, dtype=jnp.int32) * step)
        cos = c_hi[:, :, None] * c_lo[:, None, :] - s_hi[:, :, None] * s_lo[:, None, :]
        sin = s_hi[:, :, None] * c_lo[:, None, :] + c_hi[:, :, None] * s_lo[:, None, :]
        return cos.reshape(rows, n) * (n ** -0.5), sin.reshape(rows, n) * (n ** -0.5)

    cg, sg = dft(FNET_GC, FNET_GC)
    eye = jnp.eye(FNET_GROUPS, dtype=F32)
    cc = jnp.kron(eye, cg).astype(BF16)
    sc = jnp.kron(eye, sg).astype(BF16)
    h = s // 2
    cp, sp = dft(s, h)
    m = jnp.arange(s)
    alt = jnp.broadcast_to(jnp.where(m % 2 == 0, 1.0, -1.0) * (s ** -0.5), (8, s))
    r = jnp.arange(h)
    rev = (r[None, :] == h - r[:, None]).astype(BF16)
    return cc, sc, cp.astype(BF16), sp.astype(BF16), alt.astype(BF16), rev


def _tri(fwd, n):
    i = jnp.arange(n)
    same = (i[None, :] // CHUNK) == (i[:, None] // CHUNK)
    m = (i[None, :] <= i[:, None]) if fwd else (i[None, :] >= i[:, None])
    return (m & same).astype(BF16)


def _chunk_sel(n):
    return (jnp.arange(n // CHUNK * 8)[:, None] // 8 == jnp.arange(n)[None, :] // CHUNK).astype(BF16)


def _pad_rows(w, offset, total):
    return jnp.zeros((w.shape[0], total, w.shape[2]), F32).at[:, offset:offset + w.shape[1]].set(w.astype(F32))


def kernel(x, ffn1_norm, ffn1_gate, ffn1_up, ffn1_down, mix_norm, w_in, gla_up_f, gla_bias_f, gla_up_b, gla_bias_b, gla_norm, rwkv_mu, rwkv_w0_f, rwkv_w2_f, rwkv_w0_b, rwkv_w2_b, rwkv_a0_f, rwkv_a2_f, rwkv_a0_b, rwkv_a2_b, rwkv_g2, rwkv_k_k, rwkv_k_a, rwkv_r_k, rwkv_ln_g, rwkv_ln_b, proj_gla, proj_fnet, proj_rwkv, w_out, ffn2_norm, ffn2_gate, ffn2_up, ffn2_down, final_norm):
    bsz, s, d = x.shape
    depth = w_in.shape[0]
    t = bsz * s
    tm = 512 if t % 512 == 0 else 256
    assert s % REC_BLOCK == 0 and t % tm == 0 and d * 3 == GATE_COLS

    fnet_tables = _dft_tables(s)
    trif_r, trib_r, sel_r = _tri(True, SEQ_BLOCK), _tri(False, SEQ_BLOCK), _chunk_sel(SEQ_BLOCK)
    gla_tables = (_tri(True, REC_BLOCK), _tri(False, REC_BLOCK), _chunk_sel(REC_BLOCK))
    head_of = jnp.arange(LANES) // RWKV_N
    same_head = head_of[:, None] == head_of[None, :]
    gsum = same_head.astype(BF16)
    gavg = (same_head.astype(F32) * (1.0 / RWKV_N)).astype(BF16)
    rows = lambda a: a.astype(F32).reshape(depth, 1, -1)
    bf16 = lambda a: a.astype(BF16)
    fnorm = final_norm.astype(F32).reshape(1, -1)
    ffn1 = (rows(ffn1_norm), bf16(ffn1_gate), bf16(ffn1_up), bf16(ffn1_down))
    ffn2 = (rows(ffn2_norm), bf16(ffn2_gate), bf16(ffn2_up), bf16(ffn2_down))
    o_f, o_r, o_g = GLA_COLS_RAW, GLA_COLS_RAW + FNET_W, GLA_COLS_RAW + FNET_W + RWKV_COLS
    w_pad = bf16(jnp.concatenate([
        w_in[:, :, :o_f], jnp.zeros((depth, d, GLA_COLS - GLA_COLS_RAW), w_in.dtype),
        w_in[:, :, o_f:o_r],
        w_in[:, :, o_r:o_g], jnp.zeros((depth, d, RWKV_COLS_PAD - RWKV_COLS), w_in.dtype),
        w_in[:, :, o_g:]], axis=2))
    mu = jnp.concatenate([rwkv_mu.astype(F32), jnp.zeros((depth, RWKV_COLS_PAD - RWKV_COLS), F32)],
                         axis=1).reshape(depth, 1, -1)
    inproj_layered = (
        rows(mix_norm), w_pad, mu, rows(rwkv_w0_f), rows(rwkv_w0_b), rows(rwkv_a0_f), rows(rwkv_a0_b),
        jnp.concatenate([_pad_rows(rwkv_w2_f, 0, LANES), _pad_rows(rwkv_w2_b, RWKV_RANK, LANES)], axis=2),
        jnp.concatenate([_pad_rows(rwkv_a2_f, 2 * RWKV_RANK, LANES), _pad_rows(rwkv_a2_b, 3 * RWKV_RANK, LANES)],
                        axis=2),
        _pad_rows(rwkv_g2, 0, LANES), rows(rwkv_k_k), rows(rwkv_k_a), rows(rwkv_r_k))
    gla_layered = (_pad_rows(gla_up_f, 0, LANES), rows(gla_bias_f), _pad_rows(gla_up_b, GLA_RANK, LANES),
                   rows(gla_bias_b))
    merge_layered = (rows(gla_norm), rows(rwkv_ln_g), rows(rwkv_ln_b))
    proj_layered = (bf16(proj_gla), bf16(proj_fnet), bf16(proj_rwkv), bf16(w_out))

    x2 = x.reshape(t, d)
    for l in range(depth):
        at = lambda arrays: tuple((a, l) for a in arrays)
        x2 = _ffn(x2, at(ffn1) + (fnorm,), final=False, tm=tm)

        z_gla, z_fnet, z_gate, op_f, dec_f, kb_f, op_b, dec_b, kb_b, tok = _inproj(
            x2, at(inproj_layered) + (trif_r, trib_r, sel_r, gsum), seq=s)

        y_fnet = _fnet(z_fnet.reshape(bsz, s, FNET_W), fnet_tables)

        gla_of, gla_ob, r_yf, r_yb = _recurrences(
            z_gla.reshape(bsz, s, GLA_COLS), at(gla_layered) + gla_tables,
            (op_f,) * 4 + (tok, dec_f, kb_f), (op_b,) * 4 + (tok, dec_b, kb_b), bsz, s)

        flat = lambda a: a.reshape(t, a.shape[-1])
        x2 = _merge(x2, flat(gla_of), flat(gla_ob), z_gla, flat(y_fnet), flat(r_yf), flat(r_yb),
                    tok, tok, z_gate, *at(merge_layered), gavg, *at(proj_layered), tm=tm)

        x2 = _ffn(x2, at(ffn2) + (fnorm,), final=(l == depth - 1), tm=tm)
    return x2.reshape(bsz, s, d)
```
